```python
import math
import jax, jax.numpy as jnp
from jax import lax
import numpy as np

D_MODEL = 1024
BATCH = 8
SEQ = 2048
DEPTH = 1
DEC_BATCH = 128
DEC_SEQ = 4
PAST_LEN = 16384
PAGE_SIZE = 128

D_SSM = D_MODEL // 2
SSM_GROUP = 16
N_SSM_GROUPS = D_SSM // SSM_GROUP
SSM_STATE = 64
DT_MIN = 1e-3
DT_MAX = 1e-1
D_POOL = D_MODEL - D_SSM
POOL_WINDOWS = (2, 4, 8, 16)
N_POOL_GROUPS = len(POOL_WINDOWS)
POOL_GROUP = D_POOL // N_POOL_GROUPS
POOL_OUT = D_MODEL // N_POOL_GROUPS
POOL_HIST = max(POOL_WINDOWS) - 1
D_FF = -(-8 * D_MODEL // (3 * 256)) * 256
N_MOD = 6
EPS = 1e-6

kernel_name = "s5_pool_gated_hybrid_decode_step"


def rmsnorm(x, g):
    xf = x.astype(jnp.float32)
    y = xf * lax.rsqrt(jnp.mean(xf * xf, axis=-1, keepdims=True) + EPS)
    return (y * g.astype(jnp.float32)).astype(x.dtype)


def modulate(h, shift, scale):
    return h * (1.0 + scale[:, None, :]) + shift[:, None, :]


def cmul(ar, ai, br, bi):
    return ar * br - ai * bi, ar * bi + ai * br


def ssm_combine(e1, e2):
    a1r, a1i, b1r, b1i = e1
    a2r, a2i, b2r, b2i = e2
    ar, ai = cmul(a1r, a1i, a2r, a2i)
    br, bi = cmul(a2r, a2i, b1r, b1i)
    return ar, ai, br + b2r, bi + b2i


def s5_mixer(u, h0_re, h0_im, lam_re, lam_im, log_dt, b_re, b_im, c_re, c_im, d_skip):
    bsz, t_len, _ = u.shape
    uf = u.astype(jnp.float32)
    ug = uf.reshape(bsz, t_len, N_SSM_GROUPS, SSM_GROUP)
    lr = lam_re.astype(jnp.float32)
    li = lam_im.astype(jnp.float32)
    dt = jnp.exp(log_dt.astype(jnp.float32))[:, None]
    mag = jnp.exp(lr * dt)
    abar_r = mag * jnp.cos(li * dt)
    abar_i = mag * jnp.sin(li * dt)
    den = lr * lr + li * li
    nr = abar_r - 1.0
    ni = abar_i
    f_r = (nr * lr + ni * li) / den
    f_i = (ni * lr - nr * li) / den
    bb_r, bb_i = cmul(f_r[..., None], f_i[..., None],
                      b_re.astype(jnp.float32), b_im.astype(jnp.float32))
    bu_r = jnp.einsum('btgc,gpc->btgp', ug, bb_r)
    bu_i = jnp.einsum('btgc,gpc->btgp', ug, bb_i)
    h0r, h0i = cmul(abar_r, abar_i, h0_re.astype(jnp.float32), h0_im.astype(jnp.float32))
    bu_r = bu_r.at[:, 0].add(h0r)
    bu_i = bu_i.at[:, 0].add(h0i)
    a_r = jnp.broadcast_to(abar_r, (1, t_len, N_SSM_GROUPS, SSM_STATE))
    a_i = jnp.broadcast_to(abar_i, (1, t_len, N_SSM_GROUPS, SSM_STATE))
    _, _, s_r, s_i = lax.associative_scan(ssm_combine, (a_r, a_i, bu_r, bu_i), axis=1)
    y = (jnp.einsum('btgp,gcp->btgc', s_r, c_re.astype(jnp.float32))
         - jnp.einsum('btgp,gcp->btgc', s_i, c_im.astype(jnp.float32)))
    y = y.reshape(bsz, t_len, D_SSM) + d_skip.astype(jnp.float32) * uf
    return y, s_r[:, -1], s_i[:, -1]


def pool_mixer(u, buf, pos, w_pool, pool_scale):
    bsz, t_len, _ = u.shape
    uf = u.astype(jnp.float32)
    ext = jnp.concatenate([buf.astype(jnp.float32), uf], axis=1)
    cs = jnp.cumsum(jnp.pad(ext, ((0, 0), (1, 0), (0, 0))), axis=1)
    top = cs[:, POOL_HIST + 1:]
    groups = []
    for k, w in enumerate(POOL_WINDOWS):
        sl = slice(k * POOL_GROUP, (k + 1) * POOL_GROUP)
        lo = cs[:, POOL_HIST + 1 - w:POOL_HIST + 1 - w + t_len, sl]
        count = jnp.minimum(pos + 1, w).astype(jnp.float32)[None, :, None]
        groups.append((top[..., sl] - lo) / count)
    pooled = jnp.concatenate(groups, axis=-1) - uf
    z = jnp.einsum('btgc,gco->btgo',
                   pooled.reshape(bsz, t_len, N_POOL_GROUPS, POOL_GROUP),
                   w_pool.astype(jnp.float32)).reshape(bsz, t_len, D_MODEL)
    z = z * pool_scale.astype(jnp.float32)
    new_buf = ext[:, -POOL_HIST:].astype(u.dtype)
    return z, new_buf


def hybrid_layer(x, c, pos, h0_re, h0_im, pool_buf,
                 norm1_g, norm2_g, w_ada, b_ada, w_in,
                 ssm_lam_re, ssm_lam_im, ssm_log_dt, ssm_b_re, ssm_b_im, ssm_c_re, ssm_c_im, ssm_d,
                 w_glu, w_pool, pool_scale, w_out, w_ffn_in, w_ffn_out):
    bsz = c.shape[0]
    mod = (jax.nn.silu(c.astype(jnp.float32)) @ w_ada + b_ada).reshape(bsz, N_MOD, D_MODEL)
    shift1, scale1, gate1 = mod[:, 0], mod[:, 1], mod[:, 2]
    shift2, scale2, gate2 = mod[:, 3], mod[:, 4], mod[:, 5]

    h = modulate(rmsnorm(x, norm1_g), shift1, scale1)
    proj = h @ w_in
    u_ssm, u_pool, g_ssm, g_pool = jnp.split(
        proj, [D_SSM, D_SSM + D_POOL, D_SSM + D_POOL + D_MODEL], axis=-1)
    y_ssm, h_re, h_im = s5_mixer(u_ssm, h0_re, h0_im, ssm_lam_re, ssm_lam_im, ssm_log_dt,
                                 ssm_b_re, ssm_b_im, ssm_c_re, ssm_c_im, ssm_d)
    glu_a, glu_b = jnp.split(jax.nn.gelu(y_ssm) @ w_glu, 2, axis=-1)
    br_ssm = glu_a * jax.nn.sigmoid(glu_b)
    br_pool, new_buf = pool_mixer(u_pool, pool_buf, pos, w_pool, pool_scale)
    merged = jax.nn.sigmoid(g_ssm) * br_ssm + jax.nn.sigmoid(g_pool) * br_pool
    x = x + gate1[:, None, :] * (merged @ w_out)

    h2 = modulate(rmsnorm(x, norm2_g), shift2, scale2)
    f_a, f_b = jnp.split(h2 @ w_ffn_in, 2, axis=-1)
    x = x + gate2[:, None, :] * ((jax.nn.silu(f_a) * f_b) @ w_ffn_out)
    return x, h_re, h_im, new_buf


def trunk(x, c, pos, st_re, st_im, st_pool, normf_g, params):
    (norm1_g, norm2_g, w_ada, b_ada, w_in, ssm_lam_re, ssm_lam_im, ssm_log_dt, ssm_b_re, ssm_b_im,
     ssm_c_re, ssm_c_im, ssm_d, w_glu, w_pool, pool_scale, w_out, w_ffn_in, w_ffn_out) = params
    out_re, out_im, out_pool = [], [], []
    for l in range(DEPTH):
        x, h_re, h_im, nb = hybrid_layer(
            x, c, pos, st_re[l], st_im[l], st_pool[l],
            norm1_g[l], norm2_g[l], w_ada[l], b_ada[l], w_in[l],
            ssm_lam_re[l], ssm_lam_im[l], ssm_log_dt[l], ssm_b_re[l], ssm_b_im[l],
            ssm_c_re[l], ssm_c_im[l], ssm_d[l], w_glu[l], w_pool[l], pool_scale[l],
            w_out[l], w_ffn_in[l], w_ffn_out[l])
        out_re.append(h_re)
        out_im.append(h_im)
        out_pool.append(nb)
    y = rmsnorm(x, normf_g)
    return y, jnp.stack(out_re), jnp.stack(out_im), jnp.stack(out_pool)


def setup_inputs(seed: int = 0) -> dict:
    key = jax.random.key(seed)
    ks = jax.random.split(key, 32)
    f32 = jnp.float32
    G, P, C = N_SSM_GROUPS, SSM_STATE, SSM_GROUP
    nrm = lambda k, shape, s: jax.random.normal(k, shape, f32) * s
    n_idx = jnp.arange(P, dtype=f32)
    return {
        "x_prompt": nrm(ks[0], (BATCH, SEQ, D_MODEL), 1.0),
        "x_sample": nrm(ks[1], (DEC_BATCH, DEC_SEQ, D_MODEL), 1.0),
        "c_prompt": nrm(ks[2], (BATCH, D_MODEL), 1.0),
        "c_sample": nrm(ks[3], (DEC_BATCH, D_MODEL), 1.0),
        "state_ssm_re": nrm(ks[4], (DEPTH, DEC_BATCH, G, P), 0.3),
        "state_ssm_im": nrm(ks[5], (DEPTH, DEC_BATCH, G, P), 0.3),
        "state_pool": nrm(ks[6], (DEPTH, DEC_BATCH, POOL_HIST, D_POOL), 1.0),
        "norm1_g": 1.0 + nrm(ks[7], (DEPTH, D_MODEL), 0.05),
        "norm2_g": 1.0 + nrm(ks[8], (DEPTH, D_MODEL), 0.05),
        "normf_g": 1.0 + nrm(ks[9], (D_MODEL,), 0.05),
        "w_ada": nrm(ks[10], (DEPTH, D_MODEL, N_MOD * D_MODEL), 0.5 * D_MODEL ** -0.5),
        "b_ada": nrm(ks[11], (DEPTH, N_MOD * D_MODEL), 0.02),
        "w_in": nrm(ks[12], (DEPTH, D_MODEL, D_SSM + D_POOL + 2 * D_MODEL), D_MODEL ** -0.5),
        "ssm_lam_re": -0.5 + nrm(ks[13], (DEPTH, G, P), 0.01),
        "ssm_lam_im": jnp.pi * n_idx + nrm(ks[14], (DEPTH, G, P), 0.01),
        "ssm_log_dt": jax.random.uniform(ks[15], (DEPTH, G), f32,
                                         minval=math.log(DT_MIN), maxval=math.log(DT_MAX)),
        "ssm_b_re": nrm(ks[16], (DEPTH, G, P, C), (2 * C) ** -0.5),
        "ssm_b_im": nrm(ks[17], (DEPTH, G, P, C), (2 * C) ** -0.5),
        "ssm_c_re": nrm(ks[18], (DEPTH, G, C, P), P ** -0.5),
        "ssm_c_im": nrm(ks[19], (DEPTH, G, C, P), P ** -0.5),
        "ssm_d": nrm(ks[20], (DEPTH, D_SSM), 1.0),
        "w_glu": nrm(ks[21], (DEPTH, D_SSM, 2 * D_MODEL), D_SSM ** -0.5),
        "w_pool": nrm(ks[22], (DEPTH, N_POOL_GROUPS, POOL_GROUP, POOL_OUT), POOL_GROUP ** -0.5),
        "pool_scale": 1.0 + nrm(ks[23], (DEPTH, D_MODEL), 0.1),
        "w_out": nrm(ks[24], (DEPTH, D_MODEL, D_MODEL), D_MODEL ** -0.5),
        "w_ffn_in": nrm(ks[25], (DEPTH, D_MODEL, 2 * D_FF), D_MODEL ** -0.5),
        "w_ffn_out": nrm(ks[26], (DEPTH, D_FF, D_MODEL), D_FF ** -0.5),
    }


def reference(x_prompt, x_sample, c_prompt, c_sample, state_ssm_re, state_ssm_im, state_pool,
              norm1_g, norm2_g, normf_g, w_ada, b_ada, w_in,
              ssm_lam_re, ssm_lam_im, ssm_log_dt, ssm_b_re, ssm_b_im, ssm_c_re, ssm_c_im, ssm_d,
              w_glu, w_pool, pool_scale, w_out, w_ffn_in, w_ffn_out):
    params = (norm1_g, norm2_g, w_ada, b_ada, w_in, ssm_lam_re, ssm_lam_im, ssm_log_dt,
              ssm_b_re, ssm_b_im, ssm_c_re, ssm_c_im, ssm_d, w_glu, w_pool, pool_scale,
              w_out, w_ffn_in, w_ffn_out)
    n_prompt, t_prompt = x_prompt.shape[0], x_prompt.shape[1]
    zero_ssm = jnp.zeros((DEPTH, n_prompt, N_SSM_GROUPS, SSM_STATE), jnp.float32)
    zero_pool = jnp.zeros((DEPTH, n_prompt, POOL_HIST, D_POOL), x_prompt.dtype)
    pos_prompt = jnp.arange(t_prompt, dtype=jnp.int32)
    y_prompt, p_ssm_re, p_ssm_im, p_pool = trunk(
        x_prompt, c_prompt, pos_prompt, zero_ssm, zero_ssm, zero_pool, normf_g, params)
    pos_sample = PAST_LEN + jnp.arange(x_sample.shape[1], dtype=jnp.int32)
    y_sample, s_ssm_re, s_ssm_im, s_pool = trunk(
        x_sample, c_sample, pos_sample, state_ssm_re, state_ssm_im, state_pool, normf_g, params)
    return (y_prompt, y_sample, p_ssm_re, p_ssm_im, p_pool, s_ssm_re, s_ssm_im, s_pool)
```

```python
import functools

import jax
import jax.numpy as jnp
from jax import lax
from jax.experimental import pallas as pl
from jax.experimental.pallas import tpu as pltpu

D_MODEL = 1024
D_SSM = 512
SSM_GROUP = 16
N_SSM_GROUPS = 32
SSM_STATE = 64
N_STATE = N_SSM_GROUPS * SSM_STATE
D_POOL = 512
POOL_WINDOWS = (2, 4, 8, 16)
POOL_GROUP = 128
POOL_OUT = 256
POOL_HIST = 15
D_FF = 2816
N_MOD = 6
EPS = 1e-6
PAST_LEN = 16384

SUBLANES = 8
MXU_DIM = 256
TILE_ROWS = 512
SCAN_LANES = 512
GROUPS_PER_KTILE = MXU_DIM // SSM_GROUP
N_B_TILES = D_SSM // MXU_DIM
B_TILE_STATES = GROUPS_PER_KTILE * SSM_STATE
C_PIECE_GROUPS = 8
C_PIECE_STATES = C_PIECE_GROUPS * SSM_STATE
C_PIECE_OUT = C_PIECE_GROUPS * SSM_GROUP
N_C_PIECES = N_SSM_GROUPS // C_PIECE_GROUPS
VMEM_LIMIT = 56 * 1024 * 1024

BF16 = jnp.bfloat16
F32 = jnp.float32


def _dot(a, b):
    return jnp.dot(a, b, preferred_element_type=F32)


def _rmsnorm(x, g):
    return x * lax.rsqrt(jnp.mean(x * x, axis=-1, keepdims=True) + EPS) * g


def _modulate(h, shift, scale, tt, nb):
    h3 = h.reshape(tt, nb, h.shape[-1])
    return (h3 * (1.0 + scale)[None] + shift[None]).reshape(h.shape)


def _rows_gate(v, gate, tt, nb):
    return (v.reshape(tt, nb, v.shape[-1]) * gate[None]).reshape(v.shape)


def _ada_kernel(c_ref, w_ref, b_ref, o_ref):
    c = c_ref[...]
    s = (c * jax.nn.sigmoid(c)).astype(BF16)
    o_ref[...] = _dot(s, w_ref[...].astype(BF16)) + b_ref[...]


def _ada(c_all, w_ada, b_ada):
    n = c_all.shape[0]
    nc = 4
    cw = N_MOD * D_MODEL // nc
    return pl.pallas_call(
        _ada_kernel,
        grid=(nc,),
        in_specs=[pl.BlockSpec((n, D_MODEL), lambda j: (0, 0)),
                  pl.BlockSpec((D_MODEL, cw), lambda j: (0, j)),
                  pl.BlockSpec((1, cw), lambda j: (0, j))],
        out_specs=pl.BlockSpec((n, cw), lambda j: (0, j)),
        out_shape=jax.ShapeDtypeStruct((n, N_MOD * D_MODEL), F32),
        compiler_params=pltpu.CompilerParams(dimension_semantics=("arbitrary",),
                                             vmem_limit_bytes=VMEM_LIMIT),
        name="ada_mod",
    )(c_all, w_ada, b_ada.reshape(1, -1))


def _ssm_prep_kernel(lr_ref, li_ref, ldt_ref, br_ref, bi_ref, ar_ref, ai_ref, bbr_ref, bbi_ref):
    lr = lr_ref[...]
    li = li_ref[...]
    dt = jnp.exp(ldt_ref[...])
    mag = jnp.exp(lr * dt)
    abar_r = mag * jnp.cos(li * dt)
    abar_i = mag * jnp.sin(li * dt)
    den = lr * lr + li * li
    nr = abar_r - 1.0
    ni = abar_i
    f_r = (nr * lr + ni * li) / den
    f_i = (ni * lr - nr * li) / den
    b_r = br_ref[...]
    b_i = bi_ref[...]
    ar_ref[...] = abar_r
    ai_ref[...] = abar_i
    bbr_ref[...] = f_r * b_r - f_i * b_i
    bbi_ref[...] = f_r * b_i + f_i * b_r


def _ssm_prep(lam_re, lam_im, log_dt, b_re_gcp, b_im_gcp):
    g, p, c = N_SSM_GROUPS, SSM_STATE, SSM_GROUP
    return pl.pallas_call(
        _ssm_prep_kernel,
        out_shape=(jax.ShapeDtypeStruct((g, 1, p), F32), jax.ShapeDtypeStruct((g, 1, p), F32),
                   jax.ShapeDtypeStruct((g, c, p), F32), jax.ShapeDtypeStruct((g, c, p), F32)),
        name="ssm_prep",
    )(lam_re.reshape(g, 1, p), lam_im.reshape(g, 1, p), log_dt.reshape(g, 1, 1), b_re_gcp, b_im_gcp)


def _scan_tile(bur_ref, bui_ref, st_r, st_i, abr_ref, abi_ref, nb, tt):
    for lc in range(N_STATE // SCAN_LANES):
        lanes = slice(lc * SCAN_LANES, (lc + 1) * SCAN_LANES)
        ar = jnp.broadcast_to(abr_ref[:, lanes], (SUBLANES, SCAN_LANES))
        ai = jnp.broadcast_to(abi_ref[:, lanes], (SUBLANES, SCAN_LANES))

        def seq_block(rb, carry, lanes=lanes, ar=ar, ai=ai):
            r0 = pl.multiple_of(rb * SUBLANES, SUBLANES)
            sr = st_r[pl.ds(r0, SUBLANES), lanes]
            si = st_i[pl.ds(r0, SUBLANES), lanes]

            def step(t, s):
                sr, si = s
                row = pl.multiple_of(t * nb + r0, SUBLANES)
                nr = ar * sr - ai * si + bur_ref[pl.ds(row, SUBLANES), lanes]
                ni = ar * si + ai * sr + bui_ref[pl.ds(row, SUBLANES), lanes]
                bur_ref[pl.ds(row, SUBLANES), lanes] = nr
                bui_ref[pl.ds(row, SUBLANES), lanes] = ni
                return nr, ni

            sr, si = lax.fori_loop(0, tt, step, (sr, si), unroll=min(tt, 8))
            st_r[pl.ds(r0, SUBLANES), lanes] = sr
            st_i[pl.ds(r0, SUBLANES), lanes] = si
            return carry

        lax.fori_loop(0, nb // SUBLANES, seq_block, 0)


def _mixer_kernel(x_ref, mod_ref, h0r_ref, h0i_ref, pool0_ref, n1_ref, win_ref, bk_ref, cr_ref, ci_ref,
                  abr_ref, abi_ref, d_ref, wglu_ref, wpool_ref, pscale_ref, wout_ref,
                  xo_ref, sr_out_ref, si_out_ref, pool_out_ref,
                  bur_ref, bui_ref, st_r, st_i, hist_ref, *, nb, tt, n_steps, pos0):
    i = pl.program_id(0)
    m = tt * nb
    hist_rows = POOL_HIST * nb

    @pl.when(i == 0)
    def _():
        st_r[...] = h0r_ref[...]
        st_i[...] = h0i_ref[...]
        hist_ref[0:hist_rows, :] = pool0_ref[...]

    x = x_ref[...]
    shift1 = mod_ref[:, 0 * D_MODEL:1 * D_MODEL]
    scale1 = mod_ref[:, 1 * D_MODEL:2 * D_MODEL]
    gate1 = mod_ref[:, 2 * D_MODEL:3 * D_MODEL]
    hb = _modulate(_rmsnorm(x, n1_ref[...]), shift1, scale1, tt, nb).astype(BF16)

    u_ssm = _dot(hb, win_ref[:, 0:D_SSM])
    for k in range(N_B_TILES):
        res = _dot(u_ssm[:, k * MXU_DIM:(k + 1) * MXU_DIM].astype(BF16), bk_ref[k])
        bur_ref[:, k * B_TILE_STATES:(k + 1) * B_TILE_STATES] = res[:, :B_TILE_STATES]
        bui_ref[:, k * B_TILE_STATES:(k + 1) * B_TILE_STATES] = res[:, B_TILE_STATES:]
    _scan_tile(bur_ref, bui_ref, st_r, st_i, abr_ref, abi_ref, nb, tt)
    y_pieces = []
    for j in range(N_C_PIECES):
        sl = slice(j * C_PIECE_STATES, (j + 1) * C_PIECE_STATES)
        y_pieces.append(_dot(bur_ref[:, sl].astype(BF16), cr_ref[j]) + _dot(bui_ref[:, sl].astype(BF16), ci_ref[j]))
    y_ssm = jnp.concatenate(y_pieces, axis=-1) + d_ref[...] * u_ssm
    glu = _dot(jax.nn.gelu(y_ssm).astype(BF16), wglu_ref[...])
    br_ssm = glu[:, :D_MODEL] * jax.nn.sigmoid(glu[:, D_MODEL:])

    u_pool = _dot(hb, win_ref[:, D_SSM:D_SSM + D_POOL])
    hist_ref[hist_rows:hist_rows + m, :] = u_pool
    t_abs = pos0 + i * tt + lax.broadcasted_iota(jnp.int32, (tt, nb, POOL_GROUP), 0).reshape(m, POOL_GROUP)
    z_pieces = []
    for k, w in enumerate(POOL_WINDOWS):
        cols = slice(k * POOL_GROUP, (k + 1) * POOL_GROUP)
        acc = u_pool[:, cols]
        for j in range(1, w):
            acc = acc + hist_ref[hist_rows - j * nb:hist_rows - j * nb + m, cols]
        count = jnp.minimum(t_abs + 1, w).astype(F32)
        pooled = acc / count - u_pool[:, cols]
        z_pieces.append(_dot(pooled.astype(BF16), wpool_ref[k]))
    br_pool = jnp.concatenate(z_pieces, axis=-1) * pscale_ref[...]
    new_hist = hist_ref[m:m + hist_rows, :]
    if n_steps > 1:
        hist_ref[0:hist_rows, :] = new_hist

    g_ssm = _dot(hb, win_ref[:, D_SSM + D_POOL:D_SSM + D_POOL + D_MODEL])
    g_pool = _dot(hb, win_ref[:, D_SSM + D_POOL + D_MODEL:])
    merged = jax.nn.sigmoid(g_ssm) * br_ssm + jax.nn.sigmoid(g_pool) * br_pool
    xo_ref[...] = x + _rows_gate(_dot(merged.astype(BF16), wout_ref[...]), gate1, tt, nb)

    @pl.when(i == n_steps - 1)
    def _():
        sr_out_ref[...] = st_r[...]
        si_out_ref[...] = st_i[...]
        pool_out_ref[...] = new_hist


def _const_spec(shape):
    nd = len(shape)
    return pl.BlockSpec(shape, lambda i, _nd=nd: (0,) * _nd, pipeline_mode=pl.Buffered(1))


def _mixer(x_rows, mod, h0r, h0i, pool0, n1, win, bk, cr, ci, abr, abi, d, wglu, wpool, pscale, wout,
           *, nb, tt, pos0):
    rows = x_rows.shape[0]
    m = tt * nb
    n_steps = rows // m
    hist_rows = POOL_HIST * nb
    consts = (mod, h0r, h0i, pool0, n1, win, bk, cr, ci, abr, abi, d, wglu, wpool, pscale, wout)
    kern = functools.partial(_mixer_kernel, nb=nb, tt=tt, n_steps=n_steps, pos0=pos0)
    return pl.pallas_call(
        kern,
        grid=(n_steps,),
        in_specs=[pl.BlockSpec((m, D_MODEL), lambda i: (i, 0))] + [_const_spec(a.shape) for a in consts],
        out_specs=(pl.BlockSpec((m, D_MODEL), lambda i: (i, 0)),
                   pl.BlockSpec((nb, N_STATE), lambda i: (0, 0)),
                   pl.BlockSpec((nb, N_STATE), lambda i: (0, 0)),
                   pl.BlockSpec((hist_rows, D_POOL), lambda i: (0, 0))),
        out_shape=(jax.ShapeDtypeStruct((rows, D_MODEL), F32),
                   jax.ShapeDtypeStruct((nb, N_STATE), F32),
                   jax.ShapeDtypeStruct((nb, N_STATE), F32),
                   jax.ShapeDtypeStruct((hist_rows, D_POOL), F32)),
        scratch_shapes=[pltpu.VMEM((m, N_STATE), F32), pltpu.VMEM((m, N_STATE), F32),
                        pltpu.VMEM((nb, N_STATE), F32), pltpu.VMEM((nb, N_STATE), F32),
                        pltpu.VMEM((hist_rows + m, D_POOL), F32)],
        compiler_params=pltpu.CompilerParams(dimension_semantics=("arbitrary",),
                                             vmem_limit_bytes=VMEM_LIMIT),
        name=f"mixer_nb{nb}",
    )(x_rows, *consts)


def _ffn_kernel(x_ref, mod_ref, n2_ref, nf_ref, wfi_ref, wfo_ref, o_ref, *, nb, tt):
    x = x_ref[...]
    shift2 = mod_ref[:, 3 * D_MODEL:4 * D_MODEL]
    scale2 = mod_ref[:, 4 * D_MODEL:5 * D_MODEL]
    gate2 = mod_ref[:, 5 * D_MODEL:6 * D_MODEL]
    hb = _modulate(_rmsnorm(x, n2_ref[...]), shift2, scale2, tt, nb).astype(BF16)
    f_a = _dot(hb, wfi_ref[:, :D_FF])
    f_b = _dot(hb, wfi_ref[:, D_FF:])
    act = (f_a * jax.nn.sigmoid(f_a) * f_b).astype(BF16)
    x2 = x + _rows_gate(_dot(act, wfo_ref[...]), gate2, tt, nb)
    o_ref[...] = _rmsnorm(x2, nf_ref[...])


def _ffn(x_rows, mod, n2, nf, wfi, wfo, *, nb, tt):
    rows = x_rows.shape[0]
    m = tt * nb
    consts = (mod, n2, nf, wfi, wfo)
    return pl.pallas_call(
        functools.partial(_ffn_kernel, nb=nb, tt=tt),
        grid=(rows // m,),
        in_specs=[pl.BlockSpec((m, D_MODEL), lambda i: (i, 0))] + [_const_spec(a.shape) for a in consts],
        out_specs=pl.BlockSpec((m, D_MODEL), lambda i: (i, 0)),
        out_shape=jax.ShapeDtypeStruct((rows, D_MODEL), F32),
        compiler_params=pltpu.CompilerParams(dimension_semantics=("arbitrary",),
                                             vmem_limit_bytes=VMEM_LIMIT),
        name=f"ffn_nb{nb}",
    )(x_rows, *consts)


def _block_diag_b(bb_r, bb_i):
    eye = jnp.eye(GROUPS_PER_KTILE, dtype=F32)

    def one(bb):
        b4 = bb.reshape(N_B_TILES, GROUPS_PER_KTILE, SSM_GROUP, SSM_STATE)
        return jnp.einsum('kgcp,gh->kgchp', b4, eye).reshape(N_B_TILES, MXU_DIM, B_TILE_STATES)

    return jnp.concatenate([one(bb_r), one(bb_i)], axis=-1).astype(BF16)


def _block_diag_c(c_gcp):
    eye = jnp.eye(C_PIECE_GROUPS, dtype=F32)
    c4 = c_gcp.reshape(N_C_PIECES, C_PIECE_GROUPS, SSM_GROUP, SSM_STATE)
    return jnp.einsum('jgcp,gh->jgphc', c4, eye).reshape(N_C_PIECES, C_PIECE_STATES, C_PIECE_OUT).astype(BF16)


def _to_rows(x):
    b, t, d = x.shape
    return jnp.transpose(x, (1, 0, 2)).reshape(t * b, d)


def _from_rows(y, b, t):
    return jnp.transpose(y.reshape(t, b, -1), (1, 0, 2))


def _trunk(x, mod, h0r, h0i, pool0, pos0, p):
    nb, t_len, _ = x.shape
    tt = min(t_len, TILE_ROWS // nb)
    x1, s_r, s_i, new_hist = _mixer(
        _to_rows(x), mod, h0r.reshape(nb, N_STATE), h0i.reshape(nb, N_STATE), _to_rows(pool0),
        p["n1"], p["win"], p["bk"], p["cr"], p["ci"], p["abr"], p["abi"], p["d"], p["wglu"], p["wpool"],
        p["pscale"], p["wout"], nb=nb, tt=tt, pos0=pos0)
    y = _ffn(x1, mod, p["n2"], p["nf"], p["wfi"], p["wfo"], nb=nb, tt=tt)
    shape_s = (1, nb, N_SSM_GROUPS, SSM_STATE)
    return (_from_rows(y, nb, t_len), s_r.reshape(shape_s), s_i.reshape(shape_s),
            _from_rows(new_hist, nb, POOL_HIST)[None])


def kernel(x_prompt, x_sample, c_prompt, c_sample, state_ssm_re, state_ssm_im, state_pool, norm1_g, norm2_g, normf_g, w_ada, b_ada, w_in, ssm_lam_re, ssm_lam_im, ssm_log_dt, ssm_b_re, ssm_b_im, ssm_c_re, ssm_c_im, ssm_d, w_glu, w_pool, pool_scale, w_out, w_ffn_in, w_ffn_out):
    n_prompt = x_prompt.shape[0]
    mod = _ada(jnp.concatenate([c_prompt, c_sample], axis=0), w_ada[0], b_ada[0])
    abar_r, abar_i, bb_r, bb_i = _ssm_prep(
        ssm_lam_re[0], ssm_lam_im[0], ssm_log_dt[0],
        jnp.transpose(ssm_b_re[0], (0, 2, 1)), jnp.transpose(ssm_b_im[0], (0, 2, 1)))
    p = dict(
        n1=norm1_g[0].reshape(1, -1), n2=norm2_g[0].reshape(1, -1), nf=normf_g.reshape(1, -1),
        win=w_in[0].astype(BF16), bk=_block_diag_b(bb_r, bb_i),
        cr=_block_diag_c(ssm_c_re[0]), ci=_block_diag_c(-ssm_c_im[0]),
        abr=abar_r.reshape(1, N_STATE), abi=abar_i.reshape(1, N_STATE), d=ssm_d[0].reshape(1, -1),
        wglu=w_glu[0].astype(BF16), wpool=w_pool[0].astype(BF16), pscale=pool_scale[0].reshape(1, -1),
        wout=w_out[0].astype(BF16), wfi=w_ffn_in[0].astype(BF16), wfo=w_ffn_out[0].astype(BF16))
    zero_state = jnp.zeros((n_prompt, N_SSM_GROUPS, SSM_STATE), F32)
    zero_pool = jnp.zeros((n_prompt, POOL_HIST, D_POOL), F32)
    y_p, p_re, p_im, p_pool = _trunk(x_prompt, mod[:n_prompt], zero_state, zero_state, zero_pool, 0, p)
    y_s, s_re, s_im, s_pool = _trunk(x_sample, mod[n_prompt:], state_ssm_re[0], state_ssm_im[0],
                                     state_pool[0], PAST_LEN, p)
    return (y_p, y_s, p_re, p_im, p_pool, s_re, s_im, s_pool)
```

```python
import functools

import jax
import jax.numpy as jnp
from jax import lax
from jax.experimental import pallas as pl
from jax.experimental.pallas import tpu as pltpu

D_MODEL = 1024
D_SSM = 512
SSM_GROUP = 16
N_SSM_GROUPS = 32
SSM_STATE = 64
N_STATE = N_SSM_GROUPS * SSM_STATE
D_POOL = 512
POOL_WINDOWS = (2, 4, 8, 16)
POOL_GROUP = 128
POOL_OUT = 256
POOL_HIST = 15
D_FF = 2816
N_MOD = 6
EPS = 1e-6
PAST_LEN = 16384

SUBLANES = 8
LANES = 128
MXU_DIM = 256
TILE_ROWS = 512
SCAN_LANES = 512
HIST_PAD = 16
GROUPS_PER_KTILE = MXU_DIM // SSM_GROUP
N_B_TILES = D_SSM // MXU_DIM
B_TILE_STATES = GROUPS_PER_KTILE * SSM_STATE
C_PIECE_GROUPS = 8
C_PIECE_STATES = C_PIECE_GROUPS * SSM_STATE
C_PIECE_OUT = C_PIECE_GROUPS * SSM_GROUP
N_C_PIECES = N_SSM_GROUPS // C_PIECE_GROUPS
VMEM_LIMIT = 56 * 1024 * 1024

BF16 = jnp.bfloat16
F32 = jnp.float32


def _dot(a, b):
    return jnp.dot(a, b, preferred_element_type=F32)


def _rmsnorm(x, g):
    return x * lax.rsqrt(jnp.mean(x * x, axis=-1, keepdims=True) + EPS) * g


def _tile3(v, tt, nb, time_major):
    lead, inner = (tt, nb) if time_major else (nb, tt)
    return v.reshape(lead, inner, v.shape[-1])


def _modulate(h, shift, scale, tt, nb, time_major):
    return (_tile3(h, tt, nb, time_major) * (1.0 + scale) + shift).reshape(h.shape)


def _rows_gate(v, gate, tt, nb, time_major):
    return (_tile3(v, tt, nb, time_major) * gate).reshape(v.shape)


def _ada_kernel(c_ref, w_ref, b_ref, o_ref):
    c = c_ref[...]
    s = (c * jax.nn.sigmoid(c)).astype(BF16)
    o_ref[...] = _dot(s, w_ref[...].astype(BF16)) + b_ref[...]


def _ada(c_all, w_ada, b_ada):
    n = c_all.shape[0]
    nc = 4
    cw = N_MOD * D_MODEL // nc
    return pl.pallas_call(
        _ada_kernel,
        grid=(nc,),
        in_specs=[pl.BlockSpec((n, D_MODEL), lambda j: (0, 0)),
                  pl.BlockSpec((D_MODEL, cw), lambda j: (0, j)),
                  pl.BlockSpec((1, cw), lambda j: (0, j))],
        out_specs=pl.BlockSpec((n, cw), lambda j: (0, j)),
        out_shape=jax.ShapeDtypeStruct((n, N_MOD * D_MODEL), F32),
        compiler_params=pltpu.CompilerParams(dimension_semantics=("arbitrary",),
                                             vmem_limit_bytes=VMEM_LIMIT),
        name="ada_mod",
    )(c_all, w_ada, b_ada.reshape(1, -1))


def _ssm_prep_kernel(lr_ref, li_ref, ldt_ref, br_ref, bi_ref, ar_ref, ai_ref, bbr_ref, bbi_ref):
    lr = lr_ref[...]
    li = li_ref[...]
    dt = jnp.exp(ldt_ref[...])
    mag = jnp.exp(lr * dt)
    abar_r = mag * jnp.cos(li * dt)
    abar_i = mag * jnp.sin(li * dt)
    den = lr * lr + li * li
    nr = abar_r - 1.0
    ni = abar_i
    f_r = (nr * lr + ni * li) / den
    f_i = (ni * lr - nr * li) / den
    b_r = br_ref[...]
    b_i = bi_ref[...]
    ar_ref[...] = abar_r
    ai_ref[...] = abar_i
    bbr_ref[...] = f_r * b_r - f_i * b_i
    bbi_ref[...] = f_r * b_i + f_i * b_r


def _ssm_prep(lam_re, lam_im, log_dt, b_re_gcp, b_im_gcp):
    g, p, c = N_SSM_GROUPS, SSM_STATE, SSM_GROUP
    return pl.pallas_call(
        _ssm_prep_kernel,
        out_shape=(jax.ShapeDtypeStruct((g, 1, p), F32), jax.ShapeDtypeStruct((g, 1, p), F32),
                   jax.ShapeDtypeStruct((g, c, p), F32), jax.ShapeDtypeStruct((g, c, p), F32)),
        name="ssm_prep",
    )(lam_re.reshape(g, 1, p), lam_im.reshape(g, 1, p), log_dt.reshape(g, 1, 1), b_re_gcp, b_im_gcp)


def _scan_tile(bur_ref, bui_ref, st_r, st_i, abr_ref, abi_ref, nb, tt):
    for lc in range(N_STATE // SCAN_LANES):
        lanes = slice(lc * SCAN_LANES, (lc + 1) * SCAN_LANES)
        ar = jnp.broadcast_to(abr_ref[:, lanes], (SUBLANES, SCAN_LANES))
        ai = jnp.broadcast_to(abi_ref[:, lanes], (SUBLANES, SCAN_LANES))

        def seq_block(rb, carry, lanes=lanes, ar=ar, ai=ai):
            r0 = pl.multiple_of(rb * SUBLANES, SUBLANES)
            sr = st_r[pl.ds(r0, SUBLANES), lanes]
            si = st_i[pl.ds(r0, SUBLANES), lanes]

            def step(t, s):
                sr, si = s
                row = pl.multiple_of(t * nb + r0, SUBLANES)
                nr = ar * sr - ai * si + bur_ref[pl.ds(row, SUBLANES), lanes]
                ni = ar * si + ai * sr + bui_ref[pl.ds(row, SUBLANES), lanes]
                bur_ref[pl.ds(row, SUBLANES), lanes] = nr
                bui_ref[pl.ds(row, SUBLANES), lanes] = ni
                return nr, ni

            sr, si = lax.fori_loop(0, tt, step, (sr, si), unroll=min(tt, 8))
            st_r[pl.ds(r0, SUBLANES), lanes] = sr
            st_i[pl.ds(r0, SUBLANES), lanes] = si
            return carry

        lax.fori_loop(0, nb // SUBLANES, seq_block, 0)


def _to_time_major(v, perm_ref, nb, tt):
    for c in range(v.shape[-1] // LANES):
        for b in range(nb):
            perm_ref.at[c][pl.ds(b, tt, stride=nb), :] = v[b * tt:(b + 1) * tt, c * LANES:(c + 1) * LANES]
    return jnp.concatenate([perm_ref[c] for c in range(v.shape[-1] // LANES)], axis=-1)


def _to_seq_major(v, perm_ref, nb, tt):
    n_slabs = v.shape[-1] // LANES
    for c in range(n_slabs):
        perm_ref[c] = v[:, c * LANES:(c + 1) * LANES]
    return jnp.concatenate(
        [jnp.concatenate([perm_ref.at[c][pl.ds(b, tt, stride=nb), :] for b in range(nb)], axis=0)
         for c in range(n_slabs)], axis=-1)


def _pool_time_major(u_pool, hist_ref, i, nb, tt, pos0):
    m = tt * nb
    hist_rows = POOL_HIST * nb
    hist_ref[hist_rows:hist_rows + m, :] = u_pool
    t_abs = pos0 + i * tt + lax.broadcasted_iota(jnp.int32, (tt, nb, POOL_GROUP), 0).reshape(m, POOL_GROUP)
    pooled = []
    for k, w in enumerate(POOL_WINDOWS):
        cols = slice(k * POOL_GROUP, (k + 1) * POOL_GROUP)
        acc = u_pool[:, cols]
        for j in range(1, w):
            acc = acc + hist_ref[hist_rows - j * nb:hist_rows - j * nb + m, cols]
        count = jnp.minimum(t_abs + 1, w).astype(F32)
        pooled.append(acc / count - u_pool[:, cols])
    return pooled


def _pool_seq_major(u_pool, hist_ref, i, nb, tt, pos0):
    m = tt * nb
    hist_ref[:, HIST_PAD:HIST_PAD + tt, :] = u_pool.reshape(nb, tt, D_POOL)
    t_abs = pos0 + i * tt + lax.broadcasted_iota(jnp.int32, (nb, tt, POOL_GROUP), 1)
    pooled = []
    for k, w in enumerate(POOL_WINDOWS):
        cols = slice(k * POOL_GROUP, (k + 1) * POOL_GROUP)
        cur = hist_ref[:, HIST_PAD:HIST_PAD + tt, cols]
        acc = cur
        for j in range(1, w):
            acc = acc + hist_ref[:, HIST_PAD - j:HIST_PAD - j + tt, cols]
        count = jnp.minimum(t_abs + 1, w).astype(F32)
        pooled.append((acc / count - cur).reshape(m, POOL_GROUP))
    return pooled


def _mixer_kernel(x_ref, mod_ref, h0r_ref, h0i_ref, pool0_ref, n1_ref, win_ref, bk_ref, cr_ref, ci_ref,
                  abr_ref, abi_ref, d_ref, wglu_ref, wpool_ref, pscale_ref, wout_ref,
                  xo_ref, sr_out_ref, si_out_ref, pool_out_ref,
                  bur_ref, bui_ref, st_r, st_i, hist_ref, perm_ref, *, nb, tt, n_steps, pos0, time_major):
    i = pl.program_id(0)
    m = tt * nb
    hist_rows = POOL_HIST * nb

    @pl.when(i == 0)
    def _():
        st_r[...] = h0r_ref[...]
        st_i[...] = h0i_ref[...]
        if time_major:
            hist_ref[0:hist_rows, :] = pool0_ref[...]
        else:
            hist_ref[:, 0:HIST_PAD, :] = pool0_ref[...]

    x = x_ref[...].reshape(m, D_MODEL)
    shift1 = mod_ref[:, :, 0 * D_MODEL:1 * D_MODEL]
    scale1 = mod_ref[:, :, 1 * D_MODEL:2 * D_MODEL]
    gate1 = mod_ref[:, :, 2 * D_MODEL:3 * D_MODEL]
    hb = _modulate(_rmsnorm(x, n1_ref[...]), shift1, scale1, tt, nb, time_major).astype(BF16)

    u_ssm = _dot(hb, win_ref[:, 0:D_SSM])
    u_tm = u_ssm if time_major else _to_time_major(u_ssm, perm_ref, nb, tt)
    for k in range(N_B_TILES):
        res = _dot(u_tm[:, k * MXU_DIM:(k + 1) * MXU_DIM].astype(BF16), bk_ref[k])
        bur_ref[:, k * B_TILE_STATES:(k + 1) * B_TILE_STATES] = res[:, :B_TILE_STATES]
        bui_ref[:, k * B_TILE_STATES:(k + 1) * B_TILE_STATES] = res[:, B_TILE_STATES:]
    _scan_tile(bur_ref, bui_ref, st_r, st_i, abr_ref, abi_ref, nb, tt)
    y_pieces = []
    for j in range(N_C_PIECES):
        sl = slice(j * C_PIECE_STATES, (j + 1) * C_PIECE_STATES)
        y_pieces.append(_dot(bur_ref[:, sl].astype(BF16), cr_ref[j]) + _dot(bui_ref[:, sl].astype(BF16), ci_ref[j]))
    y_cs = jnp.concatenate(y_pieces, axis=-1)
    if not time_major:
        y_cs = _to_seq_major(y_cs, perm_ref, nb, tt)
    y_ssm = y_cs + d_ref[...] * u_ssm
    glu = _dot(jax.nn.gelu(y_ssm).astype(BF16), wglu_ref[...])
    br_ssm = glu[:, :D_MODEL] * jax.nn.sigmoid(glu[:, D_MODEL:])

    u_pool = _dot(hb, win_ref[:, D_SSM:D_SSM + D_POOL])
    if time_major:
        pooled = _pool_time_major(u_pool, hist_ref, i, nb, tt, pos0)
        new_buf_rows = (slice(m, m + hist_rows), slice(None))
        if n_steps > 1:
            hist_ref[0:hist_rows, :] = hist_ref[new_buf_rows]
    else:
        pooled = _pool_seq_major(u_pool, hist_ref, i, nb, tt, pos0)
        new_buf_rows = (slice(None), slice(tt + HIST_PAD - POOL_HIST, tt + HIST_PAD), slice(None))
        if n_steps > 1:
            hist_ref[:, 0:HIST_PAD, :] = hist_ref[:, tt:tt + HIST_PAD, :]
    z_pieces = [_dot(pooled[k].astype(BF16), wpool_ref[k]) for k in range(len(POOL_WINDOWS))]
    br_pool = jnp.concatenate(z_pieces, axis=-1) * pscale_ref[...]

    g_ssm = _dot(hb, win_ref[:, D_SSM + D_POOL:D_SSM + D_POOL + D_MODEL])
    g_pool = _dot(hb, win_ref[:, D_SSM + D_POOL + D_MODEL:])
    merged = jax.nn.sigmoid(g_ssm) * br_ssm + jax.nn.sigmoid(g_pool) * br_pool
    x1 = x + _rows_gate(_dot(merged.astype(BF16), wout_ref[...]), gate1, tt, nb, time_major)
    xo_ref[...] = x1.reshape(xo_ref.shape)

    @pl.when(i == n_steps - 1)
    def _():
        sr_out_ref[...] = st_r[...]
        si_out_ref[...] = st_i[...]
        pool_out_ref[...] = hist_ref[new_buf_rows]


def _const_spec(shape):
    nd = len(shape)
    return pl.BlockSpec(shape, lambda i, _nd=nd: (0,) * _nd, pipeline_mode=pl.Buffered(1))


def _row_spec(nb, tt, time_major):
    if time_major:
        return pl.BlockSpec((tt * nb, D_MODEL), lambda i: (i, 0))
    return pl.BlockSpec((nb, tt, D_MODEL), lambda i: (0, i, 0))


def _mixer(x, mod, h0r, h0i, pool0, n1, win, bk, cr, ci, abr, abi, d, wglu, wpool, pscale, wout,
           *, nb, tt, n_steps, pos0, time_major):
    m = tt * nb
    consts = (mod, h0r, h0i, pool0, n1, win, bk, cr, ci, abr, abi, d, wglu, wpool, pscale, wout)
    if time_major:
        hist_shape = (POOL_HIST * nb + m, D_POOL)
        pool_out_shape = (POOL_HIST * nb, D_POOL)
    else:
        hist_shape = (nb, HIST_PAD + tt, D_POOL)
        pool_out_shape = (nb, POOL_HIST, D_POOL)
    kern = functools.partial(_mixer_kernel, nb=nb, tt=tt, n_steps=n_steps, pos0=pos0, time_major=time_major)
    return pl.pallas_call(
        kern,
        grid=(n_steps,),
        in_specs=[_row_spec(nb, tt, time_major)] + [_const_spec(a.shape) for a in consts],
        out_specs=(_row_spec(nb, tt, time_major),
                   pl.BlockSpec((nb, N_STATE), lambda i: (0, 0)),
                   pl.BlockSpec((nb, N_STATE), lambda i: (0, 0)),
                   pl.BlockSpec(pool_out_shape, lambda i: (0,) * len(pool_out_shape))),
        out_shape=(jax.ShapeDtypeStruct(x.shape, F32),
                   jax.ShapeDtypeStruct((nb, N_STATE), F32),
                   jax.ShapeDtypeStruct((nb, N_STATE), F32),
                   jax.ShapeDtypeStruct(pool_out_shape, F32)),
        scratch_shapes=[pltpu.VMEM((m, N_STATE), F32), pltpu.VMEM((m, N_STATE), F32),
                        pltpu.VMEM((nb, N_STATE), F32), pltpu.VMEM((nb, N_STATE), F32),
                        pltpu.VMEM(hist_shape, F32),
                        pltpu.VMEM((D_SSM // LANES, m, LANES), F32)],
        compiler_params=pltpu.CompilerParams(dimension_semantics=("arbitrary",),
                                             vmem_limit_bytes=VMEM_LIMIT),
        name=f"mixer_nb{nb}",
    )(x, *consts)


def _ffn_kernel(x_ref, mod_ref, n2_ref, nf_ref, wfi_ref, wfo_ref, o_ref, *, nb, tt, time_major):
    x = x_ref[...].reshape(tt * nb, D_MODEL)
    shift2 = mod_ref[:, :, 3 * D_MODEL:4 * D_MODEL]
    scale2 = mod_ref[:, :, 4 * D_MODEL:5 * D_MODEL]
    gate2 = mod_ref[:, :, 5 * D_MODEL:6 * D_MODEL]
    hb = _modulate(_rmsnorm(x, n2_ref[...]), shift2, scale2, tt, nb, time_major).astype(BF16)
    f_a = _dot(hb, wfi_ref[:, :D_FF])
    f_b = _dot(hb, wfi_ref[:, D_FF:])
    act = (f_a * jax.nn.sigmoid(f_a) * f_b).astype(BF16)
    x2 = x + _rows_gate(_dot(act, wfo_ref[...]), gate2, tt, nb, time_major)
    o_ref[...] = _rmsnorm(x2, nf_ref[...]).reshape(o_ref.shape)


def _ffn(x, mod, n2, nf, wfi, wfo, *, nb, tt, n_steps, time_major):
    consts = (mod, n2, nf, wfi, wfo)
    return pl.pallas_call(
        functools.partial(_ffn_kernel, nb=nb, tt=tt, time_major=time_major),
        grid=(n_steps,),
        in_specs=[_row_spec(nb, tt, time_major)] + [_const_spec(a.shape) for a in consts],
        out_specs=_row_spec(nb, tt, time_major),
        out_shape=jax.ShapeDtypeStruct(x.shape, F32),
        compiler_params=pltpu.CompilerParams(dimension_semantics=("arbitrary",),
                                             vmem_limit_bytes=VMEM_LIMIT),
        name=f"ffn_nb{nb}",
    )(x, *consts)


def _block_diag_b(bb_r, bb_i):
    eye = jnp.eye(GROUPS_PER_KTILE, dtype=F32)

    def one(bb):
        b4 = bb.reshape(N_B_TILES, GROUPS_PER_KTILE, SSM_GROUP, SSM_STATE)
        return jnp.einsum('kgcp,gh->kgchp', b4, eye).reshape(N_B_TILES, MXU_DIM, B_TILE_STATES)

    return jnp.concatenate([one(bb_r), one(bb_i)], axis=-1).astype(BF16)


def _block_diag_c(c_gcp):
    eye = jnp.eye(C_PIECE_GROUPS, dtype=F32)
    c4 = c_gcp.reshape(N_C_PIECES, C_PIECE_GROUPS, SSM_GROUP, SSM_STATE)
    return jnp.einsum('jgcp,gh->jgphc', c4, eye).reshape(N_C_PIECES, C_PIECE_STATES, C_PIECE_OUT).astype(BF16)


def _trunk(x, mod, h0r, h0i, pool0, pos0, p, *, time_major):
    nb, t_len, _ = x.shape
    tt = min(t_len, TILE_ROWS // nb)
    n_steps = t_len // tt
    if time_major:
        rows = lambda a: jnp.transpose(a, (1, 0, 2)).reshape(a.shape[1] * nb, a.shape[2])
        unrows = lambda a, t: jnp.transpose(a.reshape(t, nb, a.shape[-1]), (1, 0, 2))
        x_in, pool_in, mod3 = rows(x), rows(pool0), mod[None]
    else:
        assert nb == SUBLANES
        x_in, mod3 = x, mod[:, None, :]
        pool_in = jnp.pad(pool0, ((0, 0), (HIST_PAD - POOL_HIST, 0), (0, 0)))
    x1, s_r, s_i, new_buf = _mixer(
        x_in, mod3, h0r.reshape(nb, N_STATE), h0i.reshape(nb, N_STATE), pool_in,
        p["n1"], p["win"], p["bk"], p["cr"], p["ci"], p["abr"], p["abi"], p["d"], p["wglu"], p["wpool"],
        p["pscale"], p["wout"], nb=nb, tt=tt, n_steps=n_steps, pos0=pos0, time_major=time_major)
    y = _ffn(x1, mod3, p["n2"], p["nf"], p["wfi"], p["wfo"], nb=nb, tt=tt, n_steps=n_steps, time_major=time_major)
    if time_major:
        y, new_buf = unrows(y, t_len), unrows(new_buf, POOL_HIST)
    shape_s = (1, nb, N_SSM_GROUPS, SSM_STATE)
    return y, s_r.reshape(shape_s), s_i.reshape(shape_s), new_buf[None]


def kernel(x_prompt, x_sample, c_prompt, c_sample, state_ssm_re, state_ssm_im, state_pool, norm1_g, norm2_g, normf_g, w_ada, b_ada, w_in, ssm_lam_re, ssm_lam_im, ssm_log_dt, ssm_b_re, ssm_b_im, ssm_c_re, ssm_c_im, ssm_d, w_glu, w_pool, pool_scale, w_out, w_ffn_in, w_ffn_out):
    n_prompt = x_prompt.shape[0]
    mod = _ada(jnp.concatenate([c_prompt, c_sample], axis=0), w_ada[0], b_ada[0])
    abar_r, abar_i, bb_r, bb_i = _ssm_prep(
        ssm_lam_re[0], ssm_lam_im[0], ssm_log_dt[0],
        jnp.transpose(ssm_b_re[0], (0, 2, 1)), jnp.transpose(ssm_b_im[0], (0, 2, 1)))
    p = dict(
        n1=norm1_g[0].reshape(1, -1), n2=norm2_g[0].reshape(1, -1), nf=normf_g.reshape(1, -1),
        win=w_in[0].astype(BF16), bk=_block_diag_b(bb_r, bb_i),
        cr=_block_diag_c(ssm_c_re[0]), ci=_block_diag_c(-ssm_c_im[0]),
        abr=abar_r.reshape(1, N_STATE), abi=abar_i.reshape(1, N_STATE), d=ssm_d[0].reshape(1, -1),
        wglu=w_glu[0].astype(BF16), wpool=w_pool[0].astype(BF16), pscale=pool_scale[0].reshape(1, -1),
        wout=w_out[0].astype(BF16), wfi=w_ffn_in[0].astype(BF16), wfo=w_ffn_out[0].astype(BF16))
    zero_state = jnp.zeros((n_prompt, N_SSM_GROUPS, SSM_STATE), F32)
    zero_pool = jnp.zeros((n_prompt, POOL_HIST, D_POOL), F32)
    y_p, p_re, p_im, p_pool = _trunk(x_prompt, mod[:n_prompt], zero_state, zero_state, zero_pool, 0, p,
                                     time_major=False)
    y_s, s_re, s_im, s_pool = _trunk(x_sample, mod[n_prompt:], state_ssm_re[0], state_ssm_im[0],
                                     state_pool[0], PAST_LEN, p, time_major=True)
    return (y_p, y_s, p_re, p_im, p_pool, s_re, s_im, s_pool)
```

```python
import functools

import jax
import jax.numpy as jnp
from jax import lax
from jax.experimental import pallas as pl
from jax.experimental.pallas import tpu as pltpu

D_MODEL = 1024
D_SSM = 512
SSM_GROUP = 16
N_SSM_GROUPS = 32
SSM_STATE = 64
N_STATE = N_SSM_GROUPS * SSM_STATE
D_POOL = 512
POOL_WINDOWS = (2, 4, 8, 16)
POOL_GROUP = 128
POOL_OUT = 256
POOL_HIST = 15
D_FF = 2816
N_MOD = 6
EPS = 1e-6
PAST_LEN = 16384

SUBLANES = 8
LANES = 128
MXU_DIM = 256
TILE_ROWS = 512
SCAN_LANES = 512
HIST_PAD = 16
N_TAIL_SPLITS = 2
GROUPS_PER_KTILE = MXU_DIM // SSM_GROUP
N_B_TILES = D_SSM // MXU_DIM
B_TILE_STATES = GROUPS_PER_KTILE * SSM_STATE
C_PIECE_GROUPS = 8
C_PIECE_STATES = C_PIECE_GROUPS * SSM_STATE
C_PIECE_OUT = C_PIECE_GROUPS * SSM_GROUP
N_C_PIECES = N_SSM_GROUPS // C_PIECE_GROUPS
VMEM_LIMIT = 60 * 1024 * 1024

BF16 = jnp.bfloat16
F32 = jnp.float32


def _dot(a, b):
    return jnp.dot(a, b, preferred_element_type=F32)


def _rmsnorm(x, g):
    return x * lax.rsqrt(jnp.mean(x * x, axis=-1, keepdims=True) + EPS) * g


def _tile3(v, tt, nb, time_major):
    lead, inner = (tt, nb) if time_major else (nb, tt)
    return v.reshape(lead, inner, v.shape[-1])


def _modulate(h, shift, scale, tt, nb, time_major):
    return (_tile3(h, tt, nb, time_major) * (1.0 + scale) + shift).reshape(h.shape)


def _rows_gate(v, gate, tt, nb, time_major):
    return (_tile3(v, tt, nb, time_major) * gate).reshape(v.shape)


def _ada_kernel(c_ref, w_ref, b_ref, o_ref):
    c = c_ref[...]
    s = (c * jax.nn.sigmoid(c)).astype(BF16)
    o_ref[...] = _dot(s, w_ref[...].astype(BF16)) + b_ref[...]


def _ada(c_all, w_ada, b_ada):
    n = c_all.shape[0]
    nc = 4
    cw = N_MOD * D_MODEL // nc
    return pl.pallas_call(
        _ada_kernel,
        grid=(nc,),
        in_specs=[pl.BlockSpec((n, D_MODEL), lambda j: (0, 0)),
                  pl.BlockSpec((D_MODEL, cw), lambda j: (0, j)),
                  pl.BlockSpec((1, cw), lambda j: (0, j))],
        out_specs=pl.BlockSpec((n, cw), lambda j: (0, j)),
        out_shape=jax.ShapeDtypeStruct((n, N_MOD * D_MODEL), F32),
        compiler_params=pltpu.CompilerParams(dimension_semantics=("arbitrary",),
                                             vmem_limit_bytes=VMEM_LIMIT),
        name="ada_mod",
    )(c_all, w_ada, b_ada.reshape(1, -1))


def _ssm_prep_kernel(lr_ref, li_ref, ldt_ref, br_ref, bi_ref, ar_ref, ai_ref, bbr_ref, bbi_ref):
    lr = lr_ref[...]
    li = li_ref[...]
    dt = jnp.exp(ldt_ref[...])
    mag = jnp.exp(lr * dt)
    abar_r = mag * jnp.cos(li * dt)
    abar_i = mag * jnp.sin(li * dt)
    den = lr * lr + li * li
    nr = abar_r - 1.0
    ni = abar_i
    f_r = (nr * lr + ni * li) / den
    f_i = (ni * lr - nr * li) / den
    b_r = br_ref[...]
    b_i = bi_ref[...]
    ar_ref[...] = abar_r
    ai_ref[...] = abar_i
    bbr_ref[...] = f_r * b_r - f_i * b_i
    bbi_ref[...] = f_r * b_i + f_i * b_r


def _ssm_prep(lam_re, lam_im, log_dt, b_re_gcp, b_im_gcp):
    g, p, c = N_SSM_GROUPS, SSM_STATE, SSM_GROUP
    return pl.pallas_call(
        _ssm_prep_kernel,
        out_shape=(jax.ShapeDtypeStruct((g, 1, p), F32), jax.ShapeDtypeStruct((g, 1, p), F32),
                   jax.ShapeDtypeStruct((g, c, p), F32), jax.ShapeDtypeStruct((g, c, p), F32)),
        name="ssm_prep",
    )(lam_re.reshape(g, 1, p), lam_im.reshape(g, 1, p), log_dt.reshape(g, 1, 1), b_re_gcp, b_im_gcp)


def _scan_tile(bur_ref, bui_ref, st_r, st_i, abr_ref, abi_ref, nb, tt):
    for lc in range(N_STATE // SCAN_LANES):
        lanes = slice(lc * SCAN_LANES, (lc + 1) * SCAN_LANES)
        ar = jnp.broadcast_to(abr_ref[:, lanes], (SUBLANES, SCAN_LANES))
        ai = jnp.broadcast_to(abi_ref[:, lanes], (SUBLANES, SCAN_LANES))

        def seq_block(rb, carry, lanes=lanes, ar=ar, ai=ai):
            r0 = pl.multiple_of(rb * SUBLANES, SUBLANES)
            sr = st_r[pl.ds(r0, SUBLANES), lanes]
            si = st_i[pl.ds(r0, SUBLANES), lanes]

            def step(t, s):
                sr, si = s
                row = pl.multiple_of(t * nb + r0, SUBLANES)
                nr = ar * sr - ai * si + bur_ref[pl.ds(row, SUBLANES), lanes]
                ni = ar * si + ai * sr + bui_ref[pl.ds(row, SUBLANES), lanes]
                bur_ref[pl.ds(row, SUBLANES), lanes] = nr
                bui_ref[pl.ds(row, SUBLANES), lanes] = ni
                return nr, ni

            sr, si = lax.fori_loop(0, tt, step, (sr, si), unroll=True)
            st_r[pl.ds(r0, SUBLANES), lanes] = sr
            st_i[pl.ds(r0, SUBLANES), lanes] = si
            return carry

        if nb == SUBLANES:
            seq_block(0, 0)
        else:
            lax.fori_loop(0, nb // SUBLANES, seq_block, 0)


def _to_time_major(v, perm_ref, nb, tt):
    for c in range(v.shape[-1] // LANES):
        for b in range(nb):
            perm_ref.at[c][pl.ds(b, tt, stride=nb), :] = v[b * tt:(b + 1) * tt, c * LANES:(c + 1) * LANES]
    return jnp.concatenate([perm_ref[c] for c in range(v.shape[-1] // LANES)], axis=-1)


def _to_seq_major(v, perm_ref, nb, tt):
    n_slabs = v.shape[-1] // LANES
    for c in range(n_slabs):
        perm_ref[c] = v[:, c * LANES:(c + 1) * LANES]
    return jnp.concatenate(
        [jnp.concatenate([perm_ref.at[c][pl.ds(b, tt, stride=nb), :] for b in range(nb)], axis=0)
         for c in range(n_slabs)], axis=-1)


def _pool_time_major(u_pool, hist_ref, i, nb, tt, pos0):
    m = tt * nb
    hist_rows = POOL_HIST * nb
    hist_ref[hist_rows:hist_rows + m, :] = u_pool
    t_abs = pos0 + i * tt + lax.broadcasted_iota(jnp.int32, (tt, nb, POOL_GROUP), 0).reshape(m, POOL_GROUP)
    pooled = []
    for k, w in enumerate(POOL_WINDOWS):
        cols = slice(k * POOL_GROUP, (k + 1) * POOL_GROUP)
        acc = u_pool[:, cols]
        for j in range(1, w):
            acc = acc + hist_ref[hist_rows - j * nb:hist_rows - j * nb + m, cols]
        count = jnp.minimum(t_abs + 1, w).astype(F32)
        pooled.append(acc / count - u_pool[:, cols])
    return pooled


def _pool_seq_major(u_pool, hist_ref, i, nb, tt, pos0):
    m = tt * nb
    hist_ref[:, HIST_PAD:HIST_PAD + tt, :] = u_pool.reshape(nb, tt, D_POOL)
    t_abs = pos0 + i * tt + lax.broadcasted_iota(jnp.int32, (nb, tt, POOL_GROUP), 1)
    pooled = []
    for k, w in enumerate(POOL_WINDOWS):
        cols = slice(k * POOL_GROUP, (k + 1) * POOL_GROUP)
        cur = hist_ref[:, HIST_PAD:HIST_PAD + tt, cols]
        acc = cur
        for j in range(1, w):
            acc = acc + hist_ref[:, HIST_PAD - j:HIST_PAD - j + tt, cols]
        count = jnp.minimum(t_abs + 1, w).astype(F32)
        pooled.append((acc / count - cur).reshape(m, POOL_GROUP))
    return pooled


def _mixer_kernel(x_ref, mod_ref, h0r_ref, h0i_ref, pool0_ref, n1_ref, win_ref, bk_ref, cr_ref, ci_ref,
                  abr_ref, abi_ref, d_ref, wglu_ref, wpool_ref, pscale_ref, wout_ref,
                  xo_ref, sr_out_ref, si_out_ref, pool_out_ref,
                  bur_ref, bui_ref, st_r, st_i, hist_ref, perm_ref, *, nb, tt, n_steps, pos0, time_major):
    i = pl.program_id(0)
    m = tt * nb
    hist_rows = POOL_HIST * nb

    @pl.when(i == 0)
    def _():
        st_r[...] = h0r_ref[...]
        st_i[...] = h0i_ref[...]
        if time_major:
            hist_ref[0:hist_rows, :] = pool0_ref[...]
        else:
            hist_ref[:, 0:HIST_PAD, :] = pool0_ref[...]

    x = x_ref[...].reshape(m, D_MODEL)
    shift1 = mod_ref[:, :, 0 * D_MODEL:1 * D_MODEL]
    scale1 = mod_ref[:, :, 1 * D_MODEL:2 * D_MODEL]
    gate1 = mod_ref[:, :, 2 * D_MODEL:3 * D_MODEL]
    hb = _modulate(_rmsnorm(x, n1_ref[...]), shift1, scale1, tt, nb, time_major).astype(BF16)

    u_ssm = _dot(hb, win_ref[:, 0:D_SSM])
    u_pool = _dot(hb, win_ref[:, D_SSM:D_SSM + D_POOL])
    u_tm = u_ssm if time_major else _to_time_major(u_ssm, perm_ref, nb, tt)
    for k in range(N_B_TILES):
        res = _dot(u_tm[:, k * MXU_DIM:(k + 1) * MXU_DIM].astype(BF16), bk_ref[k])
        bur_ref[:, k * B_TILE_STATES:(k + 1) * B_TILE_STATES] = res[:, :B_TILE_STATES]
        bui_ref[:, k * B_TILE_STATES:(k + 1) * B_TILE_STATES] = res[:, B_TILE_STATES:]
    g_ssm = _dot(hb, win_ref[:, D_SSM + D_POOL:D_SSM + D_POOL + D_MODEL])
    g_pool = _dot(hb, win_ref[:, D_SSM + D_POOL + D_MODEL:])

    _scan_tile(bur_ref, bui_ref, st_r, st_i, abr_ref, abi_ref, nb, tt)
    if time_major:
        pooled = _pool_time_major(u_pool, hist_ref, i, nb, tt, pos0)
        new_buf_rows = (slice(m, m + hist_rows), slice(None))
        if n_steps > 1:
            hist_ref[0:hist_rows, :] = hist_ref[new_buf_rows]
    else:
        pooled = _pool_seq_major(u_pool, hist_ref, i, nb, tt, pos0)
        new_buf_rows = (slice(None), slice(tt + HIST_PAD - POOL_HIST, tt + HIST_PAD), slice(None))
        if n_steps > 1:
            hist_ref[:, 0:HIST_PAD, :] = hist_ref[:, tt:tt + HIST_PAD, :]
    z_pieces = [_dot(pooled[k].astype(BF16), wpool_ref[k]) for k in range(len(POOL_WINDOWS))]
    br_pool = jnp.concatenate(z_pieces, axis=-1) * pscale_ref[...]

    y_pieces = []
    for j in range(N_C_PIECES):
        sl = slice(j * C_PIECE_STATES, (j + 1) * C_PIECE_STATES)
        y_pieces.append(_dot(bur_ref[:, sl].astype(BF16), cr_ref[j]) + _dot(bui_ref[:, sl].astype(BF16), ci_ref[j]))
    y_cs = jnp.concatenate(y_pieces, axis=-1)
    if not time_major:
        y_cs = _to_seq_major(y_cs, perm_ref, nb, tt)

    for h in range(N_TAIL_SPLITS):
        rows = slice(h * m // N_TAIL_SPLITS, (h + 1) * m // N_TAIL_SPLITS)
        y_ssm = y_cs[rows] + d_ref[...] * u_ssm[rows]
        glu = _dot(jax.nn.gelu(y_ssm).astype(BF16), wglu_ref[...])
        br_ssm = glu[:, :D_MODEL] * jax.nn.sigmoid(glu[:, D_MODEL:])
        merged = jax.nn.sigmoid(g_ssm[rows]) * br_ssm + jax.nn.sigmoid(g_pool[rows]) * br_pool[rows]
        o = _dot(merged.astype(BF16), wout_ref[...])
        if time_major:
            xo_ref[rows, :] = x[rows] + _rows_gate(o, gate1, tt // N_TAIL_SPLITS, nb, True)
        else:
            nb_h = nb // N_TAIL_SPLITS
            seqs = slice(h * nb_h, (h + 1) * nb_h)
            xo_ref[seqs] = (x[rows] + _rows_gate(o, gate1[seqs], tt, nb_h, False)).reshape(nb_h, tt, D_MODEL)

    @pl.when(i == n_steps - 1)
    def _():
        sr_out_ref[...] = st_r[...]
        si_out_ref[...] = st_i[...]
        pool_out_ref[...] = hist_ref[new_buf_rows]


def _const_spec(shape):
    nd = len(shape)
    return pl.BlockSpec(shape, lambda i, _nd=nd: (0,) * _nd, pipeline_mode=pl.Buffered(1))


def _row_spec(nb, tt, time_major):
    if time_major:
        return pl.BlockSpec((tt * nb, D_MODEL), lambda i: (i, 0))
    return pl.BlockSpec((nb, tt, D_MODEL), lambda i: (0, i, 0))


def _mixer(x, mod, h0r, h0i, pool0, n1, win, bk, cr, ci, abr, abi, d, wglu, wpool, pscale, wout,
           *, nb, tt, n_steps, pos0, time_major):
    m = tt * nb
    consts = (mod, h0r, h0i, pool0, n1, win, bk, cr, ci, abr, abi, d, wglu, wpool, pscale, wout)
    if time_major:
        hist_shape = (POOL_HIST * nb + m, D_POOL)
        pool_out_shape = (POOL_HIST * nb, D_POOL)
    else:
        hist_shape = (nb, HIST_PAD + tt, D_POOL)
        pool_out_shape = (nb, POOL_HIST, D_POOL)
    kern = functools.partial(_mixer_kernel, nb=nb, tt=tt, n_steps=n_steps, pos0=pos0, time_major=time_major)
    return pl.pallas_call(
        kern,
        grid=(n_steps,),
        in_specs=[_row_spec(nb, tt, time_major)] + [_const_spec(a.shape) for a in consts],
        out_specs=(_row_spec(nb, tt, time_major),
                   pl.BlockSpec((nb, N_STATE), lambda i: (0, 0)),
                   pl.BlockSpec((nb, N_STATE), lambda i: (0, 0)),
                   pl.BlockSpec(pool_out_shape, lambda i: (0,) * len(pool_out_shape))),
        out_shape=(jax.ShapeDtypeStruct(x.shape, F32),
                   jax.ShapeDtypeStruct((nb, N_STATE), F32),
                   jax.ShapeDtypeStruct((nb, N_STATE), F32),
                   jax.ShapeDtypeStruct(pool_out_shape, F32)),
        scratch_shapes=[pltpu.VMEM((m, N_STATE), F32), pltpu.VMEM((m, N_STATE), F32),
                        pltpu.VMEM((nb, N_STATE), F32), pltpu.VMEM((nb, N_STATE), F32),
                        pltpu.VMEM(hist_shape, F32),
                        pltpu.VMEM((D_SSM // LANES, m, LANES), F32)],
        compiler_params=pltpu.CompilerParams(dimension_semantics=("arbitrary",),
                                             vmem_limit_bytes=VMEM_LIMIT),
        name=f"mixer_nb{nb}",
    )(x, *consts)


def _ffn_kernel(x_ref, mod_ref, n2_ref, nf_ref, wfi_ref, wfo_ref, o_ref, *, nb, tt, time_major):
    x = x_ref[...].reshape(tt * nb, D_MODEL)
    shift2 = mod_ref[:, :, 3 * D_MODEL:4 * D_MODEL]
    scale2 = mod_ref[:, :, 4 * D_MODEL:5 * D_MODEL]
    gate2 = mod_ref[:, :, 5 * D_MODEL:6 * D_MODEL]
    hb = _modulate(_rmsnorm(x, n2_ref[...]), shift2, scale2, tt, nb, time_major).astype(BF16)
    f_a = _dot(hb, wfi_ref[:, :D_FF])
    f_b = _dot(hb, wfi_ref[:, D_FF:])
    act = (f_a * jax.nn.sigmoid(f_a) * f_b).astype(BF16)
    x2 = x + _rows_gate(_dot(act, wfo_ref[...]), gate2, tt, nb, time_major)
    o_ref[...] = _rmsnorm(x2, nf_ref[...]).reshape(o_ref.shape)


def _ffn(x, mod, n2, nf, wfi, wfo, *, nb, tt, n_steps, time_major):
    consts = (mod, n2, nf, wfi, wfo)
    return pl.pallas_call(
        functools.partial(_ffn_kernel, nb=nb, tt=tt, time_major=time_major),
        grid=(n_steps,),
        in_specs=[_row_spec(nb, tt, time_major)] + [_const_spec(a.shape) for a in consts],
        out_specs=_row_spec(nb, tt, time_major),
        out_shape=jax.ShapeDtypeStruct(x.shape, F32),
        compiler_params=pltpu.CompilerParams(dimension_semantics=("arbitrary",),
                                             vmem_limit_bytes=VMEM_LIMIT),
        name=f"ffn_nb{nb}",
    )(x, *consts)


def _block_diag_b(bb_r, bb_i):
    eye = jnp.eye(GROUPS_PER_KTILE, dtype=F32)

    def one(bb):
        b4 = bb.reshape(N_B_TILES, GROUPS_PER_KTILE, SSM_GROUP, SSM_STATE)
        return jnp.einsum('kgcp,gh->kgchp', b4, eye).reshape(N_B_TILES, MXU_DIM, B_TILE_STATES)

    return jnp.concatenate([one(bb_r), one(bb_i)], axis=-1).astype(BF16)


def _block_diag_c(c_gcp):
    eye = jnp.eye(C_PIECE_GROUPS, dtype=F32)
    c4 = c_gcp.reshape(N_C_PIECES, C_PIECE_GROUPS, SSM_GROUP, SSM_STATE)
    return jnp.einsum('jgcp,gh->jgphc', c4, eye).reshape(N_C_PIECES, C_PIECE_STATES, C_PIECE_OUT).astype(BF16)


def _trunk(x, mod, h0r, h0i, pool0, pos0, p, *, time_major):
    nb, t_len, _ = x.shape
    tt = min(t_len, TILE_ROWS // nb)
    n_steps = t_len // tt
    if time_major:
        rows = lambda a: jnp.transpose(a, (1, 0, 2)).reshape(a.shape[1] * nb, a.shape[2])
        unrows = lambda a, t: jnp.transpose(a.reshape(t, nb, a.shape[-1]), (1, 0, 2))
        x_in, pool_in, mod3 = rows(x), rows(pool0), mod[None]
    else:
        assert nb == SUBLANES
        x_in, mod3 = x, mod[:, None, :]
        pool_in = jnp.pad(pool0, ((0, 0), (HIST_PAD - POOL_HIST, 0), (0, 0)))
    x1, s_r, s_i, new_buf = _mixer(
        x_in, mod3, h0r.reshape(nb, N_STATE), h0i.reshape(nb, N_STATE), pool_in,
        p["n1"], p["win"], p["bk"], p["cr"], p["ci"], p["abr"], p["abi"], p["d"], p["wglu"], p["wpool"],
        p["pscale"], p["wout"], nb=nb, tt=tt, n_steps=n_steps, pos0=pos0, time_major=time_major)
    y = _ffn(x1, mod3, p["n2"], p["nf"], p["wfi"], p["wfo"], nb=nb, tt=tt, n_steps=n_steps, time_major=time_major)
    if time_major:
        y, new_buf = unrows(y, t_len), unrows(new_buf, POOL_HIST)
    shape_s = (1, nb, N_SSM_GROUPS, SSM_STATE)
    return y, s_r.reshape(shape_s), s_i.reshape(shape_s), new_buf[None]


def kernel(x_prompt, x_sample, c_prompt, c_sample, state_ssm_re, state_ssm_im, state_pool, norm1_g, norm2_g, normf_g, w_ada, b_ada, w_in, ssm_lam_re, ssm_lam_im, ssm_log_dt, ssm_b_re, ssm_b_im, ssm_c_re, ssm_c_im, ssm_d, w_glu, w_pool, pool_scale, w_out, w_ffn_in, w_ffn_out):
    n_prompt = x_prompt.shape[0]
    mod = _ada(jnp.concatenate([c_prompt, c_sample], axis=0), w_ada[0], b_ada[0])
    abar_r, abar_i, bb_r, bb_i = _ssm_prep(
        ssm_lam_re[0], ssm_lam_im[0], ssm_log_dt[0],
        jnp.transpose(ssm_b_re[0], (0, 2, 1)), jnp.transpose(ssm_b_im[0], (0, 2, 1)))
    p = dict(
        n1=norm1_g[0].reshape(1, -1), n2=norm2_g[0].reshape(1, -1), nf=normf_g.reshape(1, -1),
        win=w_in[0].astype(BF16), bk=_block_diag_b(bb_r, bb_i),
        cr=_block_diag_c(ssm_c_re[0]), ci=_block_diag_c(-ssm_c_im[0]),
        abr=abar_r.reshape(1, N_STATE), abi=abar_i.reshape(1, N_STATE), d=ssm_d[0].reshape(1, -1),
        wglu=w_glu[0].astype(BF16), wpool=w_pool[0].astype(BF16), pscale=pool_scale[0].reshape(1, -1),
        wout=w_out[0].astype(BF16), wfi=w_ffn_in[0].astype(BF16), wfo=w_ffn_out[0].astype(BF16))
    zero_state = jnp.zeros((n_prompt, N_SSM_GROUPS, SSM_STATE), F32)
    zero_pool = jnp.zeros((n_prompt, POOL_HIST, D_POOL), F32)
    y_p, p_re, p_im, p_pool = _trunk(x_prompt, mod[:n_prompt], zero_state, zero_state, zero_pool, 0, p,
                                     time_major=False)
    y_s, s_re, s_im, s_pool = _trunk(x_sample, mod[n_prompt:], state_ssm_re[0], state_ssm_im[0],
                                     state_pool[0], PAST_LEN, p, time_major=True)
    return (y_p, y_s, p_re, p_im, p_pool, s_re, s_im, s_pool)
```

```python
import functools

import jax
import jax.numpy as jnp
from jax import lax
from jax.experimental import pallas as pl
from jax.experimental.pallas import tpu as pltpu

D_MODEL = 1024
D_SSM = 512
SSM_GROUP = 16
N_SSM_GROUPS = 32
SSM_STATE = 64
N_STATE = N_SSM_GROUPS * SSM_STATE
D_POOL = 512
POOL_WINDOWS = (2, 4, 8, 16)
POOL_GROUP = 128
POOL_OUT = 256
POOL_HIST = 15
D_FF = 2816
N_MOD = 6
EPS = 1e-6
PAST_LEN = 16384

SUBLANES = 8
BF16_SUBLANES = 16
LANES = 128
MXU_DIM = 256
TILE_ROWS = 512
SCAN_LANES = 512
HIST_PAD = 16
N_TAIL_SPLITS = 2
ADA_COL_BLOCKS = 4
GROUPS_PER_KTILE = MXU_DIM // SSM_GROUP
N_B_TILES = D_SSM // MXU_DIM
B_TILE_STATES = GROUPS_PER_KTILE * SSM_STATE
C_PIECE_GROUPS = 8
C_PIECE_STATES = C_PIECE_GROUPS * SSM_STATE
C_PIECE_OUT = C_PIECE_GROUPS * SSM_GROUP
N_C_PIECES = N_SSM_GROUPS // C_PIECE_GROUPS
VMEM_LIMIT = 60 * 1024 * 1024

BF16 = jnp.bfloat16
F32 = jnp.float32


def _dot(a, b):
    return jnp.dot(a, b, preferred_element_type=F32)


def _rmsnorm(x, g):
    return x * lax.rsqrt(jnp.mean(x * x, axis=-1, keepdims=True) + EPS) * g


def _tile3(v, tt, nb, time_major):
    lead, inner = (tt, nb) if time_major else (nb, tt)
    return v.reshape(lead, inner, v.shape[-1])


def _modulate(h, shift, scale, tt, nb, time_major):
    return (_tile3(h, tt, nb, time_major) * (1.0 + scale) + shift).reshape(h.shape)


def _rows_gate(v, gate, tt, nb, time_major):
    return (_tile3(v, tt, nb, time_major) * gate).reshape(v.shape)


def _ada_kernel(cp_ref, cs_ref, w_ref, b_ref, op_ref, os_ref):
    w = w_ref[...].astype(BF16)
    for c_ref, o_ref in ((cp_ref, op_ref), (cs_ref, os_ref)):
        c = c_ref[...]
        o_ref[...] = _dot((c * jax.nn.sigmoid(c)).astype(BF16), w) + b_ref[...]


def _ada(c_prompt, c_sample, w_ada, b_ada):
    n_p, n_s = c_prompt.shape[0], c_sample.shape[0]
    cw = N_MOD * D_MODEL // ADA_COL_BLOCKS
    return pl.pallas_call(
        _ada_kernel,
        grid=(ADA_COL_BLOCKS,),
        in_specs=[pl.BlockSpec((n_p, D_MODEL), lambda j: (0, 0)),
                  pl.BlockSpec((n_s, D_MODEL), lambda j: (0, 0)),
                  pl.BlockSpec((D_MODEL, cw), lambda j: (0, j)),
                  pl.BlockSpec((1, cw), lambda j: (0, j))],
        out_specs=(pl.BlockSpec((n_p, cw), lambda j: (0, j)), pl.BlockSpec((n_s, cw), lambda j: (0, j))),
        out_shape=(jax.ShapeDtypeStruct((n_p, N_MOD * D_MODEL), F32),
                   jax.ShapeDtypeStruct((n_s, N_MOD * D_MODEL), F32)),
        compiler_params=pltpu.CompilerParams(dimension_semantics=("arbitrary",),
                                             vmem_limit_bytes=VMEM_LIMIT),
        name="ada_mod",
    )(c_prompt, c_sample, w_ada, b_ada.reshape(1, -1))


def _ssm_prep_kernel(lr_ref, li_ref, ldt_ref, br_ref, bi_ref, ar_ref, ai_ref, bbr_ref, bbi_ref):
    lr = lr_ref[...]
    li = li_ref[...]
    dt = jnp.exp(ldt_ref[...])
    mag = jnp.exp(lr * dt)
    abar_r = mag * jnp.cos(li * dt)
    abar_i = mag * jnp.sin(li * dt)
    den = lr * lr + li * li
    nr = abar_r - 1.0
    ni = abar_i
    f_r = (nr * lr + ni * li) / den
    f_i = (ni * lr - nr * li) / den
    b_r = br_ref[...]
    b_i = bi_ref[...]
    ar_ref[...] = abar_r
    ai_ref[...] = abar_i
    bbr_ref[...] = f_r * b_r - f_i * b_i
    bbi_ref[...] = f_r * b_i + f_i * b_r


def _ssm_prep(lam_re, lam_im, log_dt, b_re_gcp, b_im_gcp):
    g, p, c = N_SSM_GROUPS, SSM_STATE, SSM_GROUP
    return pl.pallas_call(
        _ssm_prep_kernel,
        out_shape=(jax.ShapeDtypeStruct((g, 1, p), F32), jax.ShapeDtypeStruct((g, 1, p), F32),
                   jax.ShapeDtypeStruct((g, c, p), F32), jax.ShapeDtypeStruct((g, c, p), F32)),
        name="ssm_prep",
    )(lam_re.reshape(g, 1, p), lam_im.reshape(g, 1, p), log_dt.reshape(g, 1, 1), b_re_gcp, b_im_gcp)


def _scan_tile(bur_ref, bui_ref, st_r, st_i, abr_ref, abi_ref, nb, tt):
    for lc in range(N_STATE // SCAN_LANES):
        lanes = slice(lc * SCAN_LANES, (lc + 1) * SCAN_LANES)
        ar = jnp.broadcast_to(abr_ref[:, lanes], (SUBLANES, SCAN_LANES))
        ai = jnp.broadcast_to(abi_ref[:, lanes], (SUBLANES, SCAN_LANES))

        def seq_block(rb, carry, lanes=lanes, ar=ar, ai=ai):
            r0 = pl.multiple_of(rb * SUBLANES, SUBLANES)
            sr = st_r[pl.ds(r0, SUBLANES), lanes]
            si = st_i[pl.ds(r0, SUBLANES), lanes]

            def step(t, s):
                sr, si = s
                row = pl.multiple_of(t * nb + r0, SUBLANES)
                nr = ar * sr - ai * si + bur_ref[pl.ds(row, SUBLANES), lanes]
                ni = ar * si + ai * sr + bui_ref[pl.ds(row, SUBLANES), lanes]
                bur_ref[pl.ds(row, SUBLANES), lanes] = nr
                bui_ref[pl.ds(row, SUBLANES), lanes] = ni
                return nr, ni

            sr, si = lax.fori_loop(0, tt, step, (sr, si), unroll=True)
            st_r[pl.ds(r0, SUBLANES), lanes] = sr
            st_i[pl.ds(r0, SUBLANES), lanes] = si
            return carry

        if nb == SUBLANES:
            seq_block(0, 0)
        else:
            lax.fori_loop(0, nb // SUBLANES, seq_block, 0)


def _to_time_major(v, perm_ref, nb, tt):
    for c in range(v.shape[-1] // LANES):
        for b in range(nb):
            perm_ref.at[c][pl.ds(b, tt, stride=nb), :] = v[b * tt:(b + 1) * tt, c * LANES:(c + 1) * LANES]
    return jnp.concatenate([perm_ref[c] for c in range(v.shape[-1] // LANES)], axis=-1)


def _to_seq_major(v, perm_ref, nb, tt):
    n_slabs = v.shape[-1] // LANES
    for c in range(n_slabs):
        perm_ref[c] = v[:, c * LANES:(c + 1) * LANES]
    return jnp.concatenate(
        [jnp.concatenate([perm_ref.at[c][pl.ds(b, tt, stride=nb), :] for b in range(nb)], axis=0)
         for c in range(n_slabs)], axis=-1)


def _pool_time_major(u_pool, hist_ref, i, nb, tt, pos0):
    m = tt * nb
    hist_rows = POOL_HIST * nb
    hist_ref[hist_rows:hist_rows + m, :] = u_pool
    t_abs = pos0 + i * tt + lax.broadcasted_iota(jnp.int32, (tt, nb, POOL_GROUP), 0).reshape(m, POOL_GROUP)
    pooled = []
    for k, w in enumerate(POOL_WINDOWS):
        cols = slice(k * POOL_GROUP, (k + 1) * POOL_GROUP)
        acc = u_pool[:, cols]
        for j in range(1, w):
            acc = acc + hist_ref[hist_rows - j * nb:hist_rows - j * nb + m, cols]
        count = jnp.minimum(t_abs + 1, w).astype(F32)
        pooled.append(acc / count - u_pool[:, cols])
    return pooled


def _pool_seq_major(u_pool, hist_ref, i, nb, tt, pos0):
    m = tt * nb
    hist_ref[:, HIST_PAD:HIST_PAD + tt, :] = u_pool.reshape(nb, tt, D_POOL)
    t_abs = pos0 + i * tt + lax.broadcasted_iota(jnp.int32, (nb, tt, POOL_GROUP), 1)
    pooled = []
    for k, w in enumerate(POOL_WINDOWS):
        cols = slice(k * POOL_GROUP, (k + 1) * POOL_GROUP)
        cur = hist_ref[:, HIST_PAD:HIST_PAD + tt, cols]
        acc = cur
        for j in range(1, w):
            acc = acc + hist_ref[:, HIST_PAD - j:HIST_PAD - j + tt, cols]
        count = jnp.minimum(t_abs + 1, w).astype(F32)
        pooled.append((acc / count - cur).reshape(m, POOL_GROUP))
    return pooled


def _mixer_kernel(x_ref, mod_ref, h0r_ref, h0i_ref, pool0_ref, n1_ref, win_ref, bk_ref, cr_ref, ci_ref,
                  abr_ref, abi_ref, d_ref, wglu_ref, wpool_ref, pscale_ref, wout_ref, *rest,
                  nb, tt, n_steps, pos0, time_major, n_cast):
    cast_src, rest = rest[:n_cast], rest[n_cast:]
    xo_ref, sr_out_ref, si_out_ref, pool_out_ref = rest[:4]
    cast_dst, rest = rest[4:4 + n_cast], rest[4 + n_cast:]
    bur_ref, bui_ref, st_r, st_i, hist_ref, perm_ref = rest
    i = pl.program_id(0)
    for src, dst in zip(cast_src, cast_dst):
        dst[...] = src[...].astype(BF16)
    m = tt * nb
    hist_rows = POOL_HIST * nb

    @pl.when(i == 0)
    def _():
        st_r[...] = h0r_ref[...]
        st_i[...] = h0i_ref[...]
        if time_major:
            hist_ref[0:hist_rows, :] = pool0_ref[...]
        else:
            hist_ref[:, 0:HIST_PAD, :] = pool0_ref[...]

    x = x_ref[...].reshape(m, D_MODEL)
    shift1 = mod_ref[:, :, 0 * D_MODEL:1 * D_MODEL]
    scale1 = mod_ref[:, :, 1 * D_MODEL:2 * D_MODEL]
    gate1 = mod_ref[:, :, 2 * D_MODEL:3 * D_MODEL]
    hb = _modulate(_rmsnorm(x, n1_ref[...]), shift1, scale1, tt, nb, time_major).astype(BF16)

    u_ssm = _dot(hb, win_ref[:, 0:D_SSM])
    u_pool = _dot(hb, win_ref[:, D_SSM:D_SSM + D_POOL])
    u_tm = u_ssm if time_major else _to_time_major(u_ssm, perm_ref, nb, tt)
    for k in range(N_B_TILES):
        res = _dot(u_tm[:, k * MXU_DIM:(k + 1) * MXU_DIM].astype(BF16), bk_ref[k])
        bur_ref[:, k * B_TILE_STATES:(k + 1) * B_TILE_STATES] = res[:, :B_TILE_STATES]
        bui_ref[:, k * B_TILE_STATES:(k + 1) * B_TILE_STATES] = res[:, B_TILE_STATES:]
    g_ssm = _dot(hb, win_ref[:, D_SSM + D_POOL:D_SSM + D_POOL + D_MODEL])
    g_pool = _dot(hb, win_ref[:, D_SSM + D_POOL + D_MODEL:])

    _scan_tile(bur_ref, bui_ref, st_r, st_i, abr_ref, abi_ref, nb, tt)
    if time_major:
        pooled = _pool_time_major(u_pool, hist_ref, i, nb, tt, pos0)
        new_buf_rows = (slice(m, m + hist_rows), slice(None))
        if n_steps > 1:
            hist_ref[0:hist_rows, :] = hist_ref[new_buf_rows]
    else:
        pooled = _pool_seq_major(u_pool, hist_ref, i, nb, tt, pos0)
        new_buf_rows = (slice(None), slice(tt + HIST_PAD - POOL_HIST, tt + HIST_PAD), slice(None))
        if n_steps > 1:
            hist_ref[:, 0:HIST_PAD, :] = hist_ref[:, tt:tt + HIST_PAD, :]
    z_pieces = [_dot(pooled[k].astype(BF16), wpool_ref[k]) for k in range(len(POOL_WINDOWS))]
    br_pool = jnp.concatenate(z_pieces, axis=-1) * pscale_ref[...]

    y_pieces = []
    for j in range(N_C_PIECES):
        sl = slice(j * C_PIECE_STATES, (j + 1) * C_PIECE_STATES)
        y_pieces.append(_dot(bur_ref[:, sl].astype(BF16), cr_ref[j]) + _dot(bui_ref[:, sl].astype(BF16), ci_ref[j]))
    y_cs = jnp.concatenate(y_pieces, axis=-1)
    if not time_major:
        y_cs = _to_seq_major(y_cs, perm_ref, nb, tt)

    for h in range(N_TAIL_SPLITS):
        rows = slice(h * m // N_TAIL_SPLITS, (h + 1) * m // N_TAIL_SPLITS)
        y_ssm = y_cs[rows] + d_ref[...] * u_ssm[rows]
        glu = _dot(jax.nn.gelu(y_ssm).astype(BF16), wglu_ref[...])
        br_ssm = glu[:, :D_MODEL] * jax.nn.sigmoid(glu[:, D_MODEL:])
        merged = jax.nn.sigmoid(g_ssm[rows]) * br_ssm + jax.nn.sigmoid(g_pool[rows]) * br_pool[rows]
        o = _dot(merged.astype(BF16), wout_ref[...])
        if time_major:
            xo_ref[rows, :] = x[rows] + _rows_gate(o, gate1, tt // N_TAIL_SPLITS, nb, True)
        else:
            nb_h = nb // N_TAIL_SPLITS
            seqs = slice(h * nb_h, (h + 1) * nb_h)
            xo_ref[seqs] = (x[rows] + _rows_gate(o, gate1[seqs], tt, nb_h, False)).reshape(nb_h, tt, D_MODEL)

    @pl.when(i == n_steps - 1)
    def _():
        sr_out_ref[...] = st_r[...]
        si_out_ref[...] = st_i[...]
        pool_out_ref[...] = hist_ref[new_buf_rows]


def _const_spec(shape):
    nd = len(shape)
    return pl.BlockSpec(shape, lambda i, _nd=nd: (0,) * _nd, pipeline_mode=pl.Buffered(1))


def _row_spec(nb, tt, time_major):
    if time_major:
        return pl.BlockSpec((tt * nb, D_MODEL), lambda i: (i, 0))
    return pl.BlockSpec((nb, tt, D_MODEL), lambda i: (0, i, 0))


def _mixer(x, mod, h0r, h0i, pool0, n1, win, bk, cr, ci, abr, abi, d, wglu, wpool, pscale, wout,
           *, nb, tt, n_steps, pos0, time_major, cast=()):
    m = tt * nb
    consts = (mod, h0r, h0i, pool0, n1, win, bk, cr, ci, abr, abi, d, wglu, wpool, pscale, wout)
    cast_specs = []
    for a in cast:
        n_blk = max(k for k in range(1, n_steps + 1)
                    if n_steps % k == 0 and a.shape[0] % (k * BF16_SUBLANES) == 0)
        cast_specs.append(pl.BlockSpec((a.shape[0] // n_blk, a.shape[1]),
                                       lambda i, _hold=n_steps // n_blk: (i // _hold, 0)))
    if time_major:
        hist_shape = (POOL_HIST * nb + m, D_POOL)
        pool_out_shape = (POOL_HIST * nb, D_POOL)
    else:
        hist_shape = (nb, HIST_PAD + tt, D_POOL)
        pool_out_shape = (nb, POOL_HIST, D_POOL)
    kern = functools.partial(_mixer_kernel, nb=nb, tt=tt, n_steps=n_steps, pos0=pos0, time_major=time_major,
                             n_cast=len(cast))
    return pl.pallas_call(
        kern,
        grid=(n_steps,),
        in_specs=[_row_spec(nb, tt, time_major)] + [_const_spec(a.shape) for a in consts] + cast_specs,
        out_specs=(_row_spec(nb, tt, time_major),
                   pl.BlockSpec((nb, N_STATE), lambda i: (0, 0)),
                   pl.BlockSpec((nb, N_STATE), lambda i: (0, 0)),
                   pl.BlockSpec(pool_out_shape, lambda i: (0,) * len(pool_out_shape))) + tuple(cast_specs),
        out_shape=(jax.ShapeDtypeStruct(x.shape, F32),
                   jax.ShapeDtypeStruct((nb, N_STATE), F32),
                   jax.ShapeDtypeStruct((nb, N_STATE), F32),
                   jax.ShapeDtypeStruct(pool_out_shape, F32)) + tuple(jax.ShapeDtypeStruct(a.shape, BF16) for a in cast),
        scratch_shapes=[pltpu.VMEM((m, N_STATE), F32), pltpu.VMEM((m, N_STATE), F32),
                        pltpu.VMEM((nb, N_STATE), F32), pltpu.VMEM((nb, N_STATE), F32),
                        pltpu.VMEM(hist_shape, F32),
                        pltpu.VMEM((D_SSM // LANES, m, LANES), F32)],
        compiler_params=pltpu.CompilerParams(dimension_semantics=("arbitrary",),
                                             vmem_limit_bytes=VMEM_LIMIT),
        name=f"mixer_nb{nb}",
    )(x, *consts, *cast)


def _ffn_kernel(x_ref, mod_ref, n2_ref, nf_ref, wfi_ref, wfo_ref, o_ref, *, nb, tt, time_major):
    x = x_ref[...].reshape(tt * nb, D_MODEL)
    shift2 = mod_ref[:, :, 3 * D_MODEL:4 * D_MODEL]
    scale2 = mod_ref[:, :, 4 * D_MODEL:5 * D_MODEL]
    gate2 = mod_ref[:, :, 5 * D_MODEL:6 * D_MODEL]
    hb = _modulate(_rmsnorm(x, n2_ref[...]), shift2, scale2, tt, nb, time_major).astype(BF16)
    f_a = _dot(hb, wfi_ref[:, :D_FF])
    f_b = _dot(hb, wfi_ref[:, D_FF:])
    act = (f_a * jax.nn.sigmoid(f_a) * f_b).astype(BF16)
    x2 = x + _rows_gate(_dot(act, wfo_ref[...]), gate2, tt, nb, time_major)
    o_ref[...] = _rmsnorm(x2, nf_ref[...]).reshape(o_ref.shape)


def _ffn(x, mod, n2, nf, wfi, wfo, *, nb, tt, n_steps, time_major):
    consts = (mod, n2, nf, wfi, wfo)
    return pl.pallas_call(
        functools.partial(_ffn_kernel, nb=nb, tt=tt, time_major=time_major),
        grid=(n_steps,),
        in_specs=[_row_spec(nb, tt, time_major)] + [_const_spec(a.shape) for a in consts],
        out_specs=_row_spec(nb, tt, time_major),
        out_shape=jax.ShapeDtypeStruct(x.shape, F32),
        compiler_params=pltpu.CompilerParams(dimension_semantics=("arbitrary",),
                                             vmem_limit_bytes=VMEM_LIMIT),
        name=f"ffn_nb{nb}",
    )(x, *consts)


def _block_diag_b(bb_r, bb_i):
    eye = jnp.eye(GROUPS_PER_KTILE, dtype=F32)

    def one(bb):
        b4 = bb.reshape(N_B_TILES, GROUPS_PER_KTILE, SSM_GROUP, SSM_STATE)
        return jnp.einsum('kgcp,gh->kgchp', b4, eye).reshape(N_B_TILES, MXU_DIM, B_TILE_STATES)

    return jnp.concatenate([one(bb_r), one(bb_i)], axis=-1).astype(BF16)


def _block_diag_c(c_gcp):
    eye = jnp.eye(C_PIECE_GROUPS, dtype=F32)
    c4 = c_gcp.reshape(N_C_PIECES, C_PIECE_GROUPS, SSM_GROUP, SSM_STATE)
    return jnp.einsum('jgcp,gh->jgphc', c4, eye).reshape(N_C_PIECES, C_PIECE_STATES, C_PIECE_OUT).astype(BF16)


def _trunk(x, mod, h0r, h0i, pool0, pos0, p, *, time_major, ffn_w):
    nb, t_len, _ = x.shape
    tt = min(t_len, TILE_ROWS // nb)
    n_steps = t_len // tt
    if time_major:
        rows = lambda a: jnp.transpose(a, (1, 0, 2)).reshape(a.shape[1] * nb, a.shape[2])
        unrows = lambda a, t: jnp.transpose(a.reshape(t, nb, a.shape[-1]), (1, 0, 2))
        x_in, pool_in, mod3 = rows(x), rows(pool0), mod[None]
    else:
        assert nb == SUBLANES
        x_in, mod3 = x, mod[:, None, :]
        pool_in = jnp.pad(pool0, ((0, 0), (HIST_PAD - POOL_HIST, 0), (0, 0)))
    cast = ffn_w if ffn_w[0].dtype == F32 else ()
    x1, s_r, s_i, new_buf, *cast_out = _mixer(
        x_in, mod3, h0r.reshape(nb, N_STATE), h0i.reshape(nb, N_STATE), pool_in,
        p["n1"], p["win"], p["bk"], p["cr"], p["ci"], p["abr"], p["abi"], p["d"], p["wglu"], p["wpool"],
        p["pscale"], p["wout"], nb=nb, tt=tt, n_steps=n_steps, pos0=pos0, time_major=time_major, cast=cast)
    wfi, wfo = cast_out if cast else ffn_w
    y = _ffn(x1, mod3, p["n2"], p["nf"], wfi, wfo, nb=nb, tt=tt, n_steps=n_steps, time_major=time_major)
    if time_major:
        y, new_buf = unrows(y, t_len), unrows(new_buf, POOL_HIST)
    shape_s = (1, nb, N_SSM_GROUPS, SSM_STATE)
    return (y, s_r.reshape(shape_s), s_i.reshape(shape_s), new_buf[None]), (wfi, wfo)


def kernel(x_prompt, x_sample, c_prompt, c_sample, state_ssm_re, state_ssm_im, state_pool, norm1_g, norm2_g, normf_g, w_ada, b_ada, w_in, ssm_lam_re, ssm_lam_im, ssm_log_dt, ssm_b_re, ssm_b_im, ssm_c_re, ssm_c_im, ssm_d, w_glu, w_pool, pool_scale, w_out, w_ffn_in, w_ffn_out):
    n_prompt = x_prompt.shape[0]
    mod_p, mod_s = _ada(c_prompt, c_sample, w_ada[0], b_ada[0])
    abar_r, abar_i, bb_r, bb_i = _ssm_prep(
        ssm_lam_re[0], ssm_lam_im[0], ssm_log_dt[0],
        jnp.transpose(ssm_b_re[0], (0, 2, 1)), jnp.transpose(ssm_b_im[0], (0, 2, 1)))
    p = dict(
        n1=norm1_g[0].reshape(1, -1), n2=norm2_g[0].reshape(1, -1), nf=normf_g.reshape(1, -1),
        win=w_in[0].astype(BF16), bk=_block_diag_b(bb_r, bb_i),
        cr=_block_diag_c(ssm_c_re[0]), ci=_block_diag_c(-ssm_c_im[0]),
        abr=abar_r.reshape(1, N_STATE), abi=abar_i.reshape(1, N_STATE), d=ssm_d[0].reshape(1, -1),
        wglu=w_glu[0].astype(BF16), wpool=w_pool[0].astype(BF16), pscale=pool_scale[0].reshape(1, -1),
        wout=w_out[0].astype(BF16))
    zero_state = jnp.zeros((n_prompt, N_SSM_GROUPS, SSM_STATE), F32)
    zero_pool = jnp.zeros((n_prompt, POOL_HIST, D_POOL), F32)
    ffn_f32 = (w_ffn_in[0], w_ffn_out[0])
    (y_p, p_re, p_im, p_pool), ffn_bf16 = _trunk(x_prompt, mod_p, zero_state, zero_state, zero_pool, 0, p,
                                                  time_major=False, ffn_w=ffn_f32)
    (y_s, s_re, s_im, s_pool), _ = _trunk(x_sample, mod_s, state_ssm_re[0], state_ssm_im[0], state_pool[0],
                                          PAST_LEN, p, time_major=True, ffn_w=ffn_bf16)
    return (y_p, y_s, p_re, p_im, p_pool, s_re, s_im, s_pool)
```

```python
import functools

import jax
import jax.numpy as jnp
from jax import lax
from jax.experimental import pallas as pl
from jax.experimental.pallas import tpu as pltpu

D_MODEL = 1024
D_SSM = 512
SSM_GROUP = 16
N_SSM_GROUPS = 32
SSM_STATE = 64
N_STATE = N_SSM_GROUPS * SSM_STATE
D_POOL = 512
POOL_WINDOWS = (2, 4, 8, 16)
POOL_GROUP = 128
POOL_OUT = 256
POOL_HIST = 15
D_FF = 2816
N_MOD = 6
EPS = 1e-6
PAST_LEN = 16384

SUBLANES = 8
BF16_SUBLANES = 16
LANES = 128
MXU_DIM = 256
TILE_ROWS = 512
MIXER_SUBTILES = 2
FFN_TILE_ROWS = 1024
SCAN_LANES = 512
HIST_PAD = 16
N_TAIL_SPLITS = 2
N_FFN_SPLITS = 4
ADA_COL_BLOCKS = 4
GROUPS_PER_KTILE = MXU_DIM // SSM_GROUP
N_B_TILES = D_SSM // MXU_DIM
B_TILE_STATES = GROUPS_PER_KTILE * SSM_STATE
C_PIECE_GROUPS = 8
C_PIECE_STATES = C_PIECE_GROUPS * SSM_STATE
C_PIECE_OUT = C_PIECE_GROUPS * SSM_GROUP
N_C_PIECES = N_SSM_GROUPS // C_PIECE_GROUPS
VMEM_LIMIT = 60 * 1024 * 1024

BF16 = jnp.bfloat16
F32 = jnp.float32


def _dot(a, b):
    return jnp.dot(a, b, preferred_element_type=F32)


def _sigmoid(x):
    return 0.5 * jnp.tanh(0.5 * x) + 0.5


def _rmsnorm(x, g):
    return x * lax.rsqrt(jnp.mean(x * x, axis=-1, keepdims=True) + EPS) * g


def _tile3(v, tt, nb, time_major):
    lead, inner = (tt, nb) if time_major else (nb, tt)
    return v.reshape(lead, inner, v.shape[-1])


def _modulate(h, shift, scale, tt, nb, time_major):
    return (_tile3(h, tt, nb, time_major) * (1.0 + scale) + shift).reshape(h.shape)


def _rows_gate(v, gate, tt, nb, time_major):
    return (_tile3(v, tt, nb, time_major) * gate).reshape(v.shape)


def _ada_kernel(cp_ref, cs_ref, w_ref, b_ref, op_ref, os_ref):
    w = w_ref[...].astype(BF16)
    for c_ref, o_ref in ((cp_ref, op_ref), (cs_ref, os_ref)):
        c = c_ref[...]
        o_ref[...] = _dot((c * _sigmoid(c)).astype(BF16), w) + b_ref[...]


def _ada(c_prompt, c_sample, w_ada, b_ada):
    n_p, n_s = c_prompt.shape[0], c_sample.shape[0]
    cw = N_MOD * D_MODEL // ADA_COL_BLOCKS
    return pl.pallas_call(
        _ada_kernel,
        grid=(ADA_COL_BLOCKS,),
        in_specs=[pl.BlockSpec((n_p, D_MODEL), lambda j: (0, 0)),
                  pl.BlockSpec((n_s, D_MODEL), lambda j: (0, 0)),
                  pl.BlockSpec((D_MODEL, cw), lambda j: (0, j)),
                  pl.BlockSpec((1, cw), lambda j: (0, j))],
        out_specs=(pl.BlockSpec((n_p, cw), lambda j: (0, j)), pl.BlockSpec((n_s, cw), lambda j: (0, j))),
        out_shape=(jax.ShapeDtypeStruct((n_p, N_MOD * D_MODEL), F32),
                   jax.ShapeDtypeStruct((n_s, N_MOD * D_MODEL), F32)),
        compiler_params=pltpu.CompilerParams(dimension_semantics=("arbitrary",),
                                             vmem_limit_bytes=VMEM_LIMIT),
        name="ada_mod",
    )(c_prompt, c_sample, w_ada, b_ada.reshape(1, -1))


def _ssm_prep_kernel(lr_ref, li_ref, ldt_ref, br_ref, bi_ref, ar_ref, ai_ref, bbr_ref, bbi_ref):
    lr = lr_ref[...]
    li = li_ref[...]
    dt = jnp.exp(ldt_ref[...])
    mag = jnp.exp(lr * dt)
    abar_r = mag * jnp.cos(li * dt)
    abar_i = mag * jnp.sin(li * dt)
    den = lr * lr + li * li
    nr = abar_r - 1.0
    ni = abar_i
    f_r = (nr * lr + ni * li) / den
    f_i = (ni * lr - nr * li) / den
    b_r = br_ref[...]
    b_i = bi_ref[...]
    ar_ref[...] = abar_r
    ai_ref[...] = abar_i
    bbr_ref[...] = f_r * b_r - f_i * b_i
    bbi_ref[...] = f_r * b_i + f_i * b_r


def _ssm_prep(lam_re, lam_im, log_dt, b_re_gcp, b_im_gcp):
    g, p, c = N_SSM_GROUPS, SSM_STATE, SSM_GROUP
    return pl.pallas_call(
        _ssm_prep_kernel,
        out_shape=(jax.ShapeDtypeStruct((g, 1, p), F32), jax.ShapeDtypeStruct((g, 1, p), F32),
                   jax.ShapeDtypeStruct((g, c, p), F32), jax.ShapeDtypeStruct((g, c, p), F32)),
        name="ssm_prep",
    )(lam_re.reshape(g, 1, p), lam_im.reshape(g, 1, p), log_dt.reshape(g, 1, 1), b_re_gcp, b_im_gcp)


def _scan_tile(bur_ref, bui_ref, st_r, st_i, abr_ref, abi_ref, nb, tt):
    for lc in range(N_STATE // SCAN_LANES):
        lanes = slice(lc * SCAN_LANES, (lc + 1) * SCAN_LANES)
        ar = jnp.broadcast_to(abr_ref[:, lanes], (SUBLANES, SCAN_LANES))
        ai = jnp.broadcast_to(abi_ref[:, lanes], (SUBLANES, SCAN_LANES))

        def seq_block(rb, carry, lanes=lanes, ar=ar, ai=ai):
            r0 = pl.multiple_of(rb * SUBLANES, SUBLANES)
            sr = st_r[pl.ds(r0, SUBLANES), lanes]
            si = st_i[pl.ds(r0, SUBLANES), lanes]

            def step(t, s):
                sr, si = s
                row = pl.multiple_of(t * nb + r0, SUBLANES)
                nr = ar * sr - ai * si + bur_ref[pl.ds(row, SUBLANES), lanes]
                ni = ar * si + ai * sr + bui_ref[pl.ds(row, SUBLANES), lanes]
                bur_ref[pl.ds(row, SUBLANES), lanes] = nr
                bui_ref[pl.ds(row, SUBLANES), lanes] = ni
                return nr, ni

            sr, si = lax.fori_loop(0, tt, step, (sr, si), unroll=True)
            st_r[pl.ds(r0, SUBLANES), lanes] = sr
            st_i[pl.ds(r0, SUBLANES), lanes] = si
            return carry

        if nb == SUBLANES:
            seq_block(0, 0)
        else:
            lax.fori_loop(0, nb // SUBLANES, seq_block, 0)


def _to_time_major(v, perm_ref, nb, tt):
    for c in range(v.shape[-1] // LANES):
        for b in range(nb):
            perm_ref.at[c][pl.ds(b, tt, stride=nb), :] = v[b * tt:(b + 1) * tt, c * LANES:(c + 1) * LANES]
    return jnp.concatenate([perm_ref[c] for c in range(v.shape[-1] // LANES)], axis=-1)


def _to_seq_major(v, perm_ref, nb, tt):
    n_slabs = v.shape[-1] // LANES
    for c in range(n_slabs):
        perm_ref[c] = v[:, c * LANES:(c + 1) * LANES]
    return jnp.concatenate(
        [jnp.concatenate([perm_ref.at[c][pl.ds(b, tt, stride=nb), :] for b in range(nb)], axis=0)
         for c in range(n_slabs)], axis=-1)


def _pool_time_major(u_pool, hist_ref, i, nb, tt, pos0):
    m = tt * nb
    hist_rows = POOL_HIST * nb
    hist_ref[hist_rows:hist_rows + m, :] = u_pool
    t_abs = pos0 + i * tt + lax.broadcasted_iota(jnp.int32, (tt, nb, POOL_GROUP), 0).reshape(m, POOL_GROUP)
    pooled = []
    for k, w in enumerate(POOL_WINDOWS):
        cols = slice(k * POOL_GROUP, (k + 1) * POOL_GROUP)
        acc = u_pool[:, cols]
        for j in range(1, w):
            acc = acc + hist_ref[hist_rows - j * nb:hist_rows - j * nb + m, cols]
        count = jnp.minimum(t_abs + 1, w).astype(F32)
        pooled.append(acc / count - u_pool[:, cols])
    return pooled


def _pool_seq_major(u_pool, hist_ref, i, nb, tt, pos0):
    m = tt * nb
    t_abs = pos0 + i * tt + lax.broadcasted_iota(jnp.int32, (nb, tt, POOL_GROUP), 1)
    pooled = []
    for k, w in enumerate(POOL_WINDOWS):
        cur = u_pool[:, k * POOL_GROUP:(k + 1) * POOL_GROUP].reshape(nb, tt, POOL_GROUP)
        hist_ref[k, :, HIST_PAD:HIST_PAD + tt, :] = cur
        acc = cur
        for j in range(1, w):
            acc = acc + hist_ref[k, :, HIST_PAD - j:HIST_PAD - j + tt, :]
        count = jnp.minimum(t_abs + 1, w).astype(F32)
        pooled.append((acc / count - cur).reshape(m, POOL_GROUP))
    return pooled


def _mixer_kernel(x_ref, mod_ref, h0r_ref, h0i_ref, pool0_ref, n1_ref, win_ref, bk_ref, cr_ref, ci_ref,
                  abr_ref, abi_ref, d_ref, wglu_ref, wpool_ref, pscale_ref, wout_ref, *rest,
                  nb, tt, n_sub, n_steps, pos0, time_major, n_cast):
    cast_src, rest = rest[:n_cast], rest[n_cast:]
    xo_ref, sr_out_ref, si_out_ref, pool_out_ref = rest[:4]
    cast_dst, rest = rest[4:4 + n_cast], rest[4 + n_cast:]
    bur_ref, bui_ref, st_r, st_i, hist_ref, perm_ref = rest
    i = pl.program_id(0)
    for src, dst in zip(cast_src, cast_dst):
        dst[...] = src[...].astype(BF16)
    m = tt * nb
    hist_rows = POOL_HIST * nb

    @pl.when(i == 0)
    def _():
        st_r[...] = h0r_ref[...]
        st_i[...] = h0i_ref[...]
        if time_major:
            hist_ref[0:hist_rows, :] = pool0_ref[...]
        else:
            for k in range(len(POOL_WINDOWS)):
                hist_ref[k, :, 0:HIST_PAD, :] = pool0_ref[:, :, k * POOL_GROUP:(k + 1) * POOL_GROUP]

    for sub in range(n_sub):
        tile = i * n_sub + sub
        t_rows = slice(sub * tt, (sub + 1) * tt)
        x = (x_ref[...] if time_major else x_ref[:, t_rows, :]).reshape(m, D_MODEL)
        shift1 = mod_ref[:, :, 0 * D_MODEL:1 * D_MODEL]
        scale1 = mod_ref[:, :, 1 * D_MODEL:2 * D_MODEL]
        gate1 = mod_ref[:, :, 2 * D_MODEL:3 * D_MODEL]
        hb = _modulate(_rmsnorm(x, n1_ref[...]), shift1, scale1, tt, nb, time_major).astype(BF16)

        u_ssm = _dot(hb, win_ref[:, 0:D_SSM])
        u_pool = _dot(hb, win_ref[:, D_SSM:D_SSM + D_POOL])
        u_tm = u_ssm if time_major else _to_time_major(u_ssm, perm_ref, nb, tt)
        for k in range(N_B_TILES):
            res = _dot(u_tm[:, k * MXU_DIM:(k + 1) * MXU_DIM].astype(BF16), bk_ref[k])
            bur_ref[:, k * B_TILE_STATES:(k + 1) * B_TILE_STATES] = res[:, :B_TILE_STATES]
            bui_ref[:, k * B_TILE_STATES:(k + 1) * B_TILE_STATES] = res[:, B_TILE_STATES:]
        g_ssm = _dot(hb, win_ref[:, D_SSM + D_POOL:D_SSM + D_POOL + D_MODEL])
        g_pool = _dot(hb, win_ref[:, D_SSM + D_POOL + D_MODEL:])

        _scan_tile(bur_ref, bui_ref, st_r, st_i, abr_ref, abi_ref, nb, tt)
        if time_major:
            pooled = _pool_time_major(u_pool, hist_ref, tile, nb, tt, pos0)
            new_buf_rows = (slice(m, m + hist_rows), slice(None))
            if n_steps * n_sub > 1:
                hist_ref[0:hist_rows, :] = hist_ref[new_buf_rows]
        else:
            pooled = _pool_seq_major(u_pool, hist_ref, tile, nb, tt, pos0)
            if n_steps * n_sub > 1:
                hist_ref[:, :, 0:HIST_PAD, :] = hist_ref[:, :, tt:tt + HIST_PAD, :]
        z_pieces = [_dot(pooled[k].astype(BF16), wpool_ref[k]) for k in range(len(POOL_WINDOWS))]
        br_pool = jnp.concatenate(z_pieces, axis=-1) * pscale_ref[...]

        y_pieces = []
        for j in range(N_C_PIECES):
            sl = slice(j * C_PIECE_STATES, (j + 1) * C_PIECE_STATES)
            y_pieces.append(_dot(bur_ref[:, sl].astype(BF16), cr_ref[j])
                            + _dot(bui_ref[:, sl].astype(BF16), ci_ref[j]))
        y_cs = jnp.concatenate(y_pieces, axis=-1)
        if not time_major:
            y_cs = _to_seq_major(y_cs, perm_ref, nb, tt)

        for h in range(N_TAIL_SPLITS):
            rows = slice(h * m // N_TAIL_SPLITS, (h + 1) * m // N_TAIL_SPLITS)
            y_ssm = y_cs[rows] + d_ref[...] * u_ssm[rows]
            glu = _dot(jax.nn.gelu(y_ssm).astype(BF16), wglu_ref[...])
            br_ssm = glu[:, :D_MODEL] * _sigmoid(glu[:, D_MODEL:])
            merged = _sigmoid(g_ssm[rows]) * br_ssm + _sigmoid(g_pool[rows]) * br_pool[rows]
            o = _dot(merged.astype(BF16), wout_ref[...])
            if time_major:
                xo_ref[rows, :] = x[rows] + _rows_gate(o, gate1, tt // N_TAIL_SPLITS, nb, True)
            else:
                nb_h = nb // N_TAIL_SPLITS
                seqs = slice(h * nb_h, (h + 1) * nb_h)
                x1 = x[rows] + _rows_gate(o, gate1[seqs], tt, nb_h, False)
                xo_ref[seqs, t_rows, :] = x1.reshape(nb_h, tt, D_MODEL)

    @pl.when(i == n_steps - 1)
    def _():
        sr_out_ref[...] = st_r[...]
        si_out_ref[...] = st_i[...]
        if time_major:
            pool_out_ref[...] = hist_ref[new_buf_rows]
        else:
            pool_out_ref[...] = jnp.concatenate(
                [hist_ref[k, :, tt + HIST_PAD - POOL_HIST:tt + HIST_PAD, :] for k in range(len(POOL_WINDOWS))],
                axis=-1)


def _const_spec(shape):
    nd = len(shape)
    return pl.BlockSpec(shape, lambda i, _nd=nd: (0,) * _nd, pipeline_mode=pl.Buffered(1))


def _row_spec(nb, tt, time_major):
    if time_major:
        return pl.BlockSpec((tt * nb, D_MODEL), lambda i: (i, 0))
    return pl.BlockSpec((nb, tt, D_MODEL), lambda i: (0, i, 0))


def _mixer(x, mod, h0r, h0i, pool0, n1, win, bk, cr, ci, abr, abi, d, wglu, wpool, pscale, wout,
           *, nb, tt, n_sub, n_steps, pos0, time_major, cast=()):
    m = tt * nb
    consts = (mod, h0r, h0i, pool0, n1, win, bk, cr, ci, abr, abi, d, wglu, wpool, pscale, wout)
    cast_specs = []
    for a in cast:
        n_blk = max(k for k in range(1, n_steps + 1)
                    if n_steps % k == 0 and a.shape[0] % (k * BF16_SUBLANES) == 0)
        cast_specs.append(pl.BlockSpec((a.shape[0] // n_blk, a.shape[1]),
                                       lambda i, _hold=n_steps // n_blk: (i // _hold, 0)))
    if time_major:
        hist_shape = (POOL_HIST * nb + m, D_POOL)
        pool_out_shape = (POOL_HIST * nb, D_POOL)
    else:
        hist_shape = (len(POOL_WINDOWS), nb, HIST_PAD + tt, POOL_GROUP)
        pool_out_shape = (nb, POOL_HIST, D_POOL)
    kern = functools.partial(_mixer_kernel, nb=nb, tt=tt, n_sub=n_sub, n_steps=n_steps, pos0=pos0,
                             time_major=time_major, n_cast=len(cast))
    return pl.pallas_call(
        kern,
        grid=(n_steps,),
        in_specs=[_row_spec(nb, n_sub * tt, time_major)] + [_const_spec(a.shape) for a in consts] + cast_specs,
        out_specs=(_row_spec(nb, n_sub * tt, time_major),
                   pl.BlockSpec((nb, N_STATE), lambda i: (0, 0)),
                   pl.BlockSpec((nb, N_STATE), lambda i: (0, 0)),
                   pl.BlockSpec(pool_out_shape, lambda i: (0,) * len(pool_out_shape))) + tuple(cast_specs),
        out_shape=(jax.ShapeDtypeStruct(x.shape, F32),
                   jax.ShapeDtypeStruct((nb, N_STATE), F32),
                   jax.ShapeDtypeStruct((nb, N_STATE), F32),
                   jax.ShapeDtypeStruct(pool_out_shape, F32)) + tuple(jax.ShapeDtypeStruct(a.shape, BF16) for a in cast),
        scratch_shapes=[pltpu.VMEM((m, N_STATE), F32), pltpu.VMEM((m, N_STATE), F32),
                        pltpu.VMEM((nb, N_STATE), F32), pltpu.VMEM((nb, N_STATE), F32),
                        pltpu.VMEM(hist_shape, F32),
                        pltpu.VMEM((D_SSM // LANES, m, LANES), F32)],
        compiler_params=pltpu.CompilerParams(dimension_semantics=("arbitrary",),
                                             vmem_limit_bytes=VMEM_LIMIT),
        name=f"mixer_nb{nb}",
    )(x, *consts, *cast)


def _ffn_kernel(x_ref, mod_ref, n2_ref, nf_ref, wfi_ref, wfo_ref, o_ref, *, nb, tt, time_major):
    for h in range(N_FFN_SPLITS):
        if time_major:
            tt_h, nb_h = tt // N_FFN_SPLITS, nb
            piece = (slice(h * tt_h * nb, (h + 1) * tt_h * nb), slice(None))
            mod = mod_ref
        else:
            tt_h, nb_h = tt, nb // N_FFN_SPLITS
            piece = (slice(h * nb_h, (h + 1) * nb_h), slice(None), slice(None))
            mod = mod_ref.at[h * nb_h:(h + 1) * nb_h]
        x = x_ref[piece].reshape(tt_h * nb_h, D_MODEL)
        shift2 = mod[:, :, 3 * D_MODEL:4 * D_MODEL]
        scale2 = mod[:, :, 4 * D_MODEL:5 * D_MODEL]
        gate2 = mod[:, :, 5 * D_MODEL:6 * D_MODEL]
        hb = _modulate(_rmsnorm(x, n2_ref[...]), shift2, scale2, tt_h, nb_h, time_major).astype(BF16)
        f_a = _dot(hb, wfi_ref[:, :D_FF])
        f_b = _dot(hb, wfi_ref[:, D_FF:])
        act = (f_a * _sigmoid(f_a) * f_b).astype(BF16)
        x2 = x + _rows_gate(_dot(act, wfo_ref[...]), gate2, tt_h, nb_h, time_major)
        y = _rmsnorm(x2, nf_ref[...])
        o_ref[piece] = y if time_major else y.reshape(nb_h, tt_h, D_MODEL)


def _ffn(x, mod, n2, nf, wfi, wfo, *, nb, tt, n_steps, time_major):
    consts = (mod, n2, nf, wfi, wfo)
    return pl.pallas_call(
        functools.partial(_ffn_kernel, nb=nb, tt=tt, time_major=time_major),
        grid=(n_steps,),
        in_specs=[_row_spec(nb, tt, time_major)] + [_const_spec(a.shape) for a in consts],
        out_specs=_row_spec(nb, tt, time_major),
        out_shape=jax.ShapeDtypeStruct(x.shape, F32),
        compiler_params=pltpu.CompilerParams(dimension_semantics=("arbitrary",),
                                             vmem_limit_bytes=VMEM_LIMIT),
        name=f"ffn_nb{nb}",
    )(x, *consts)


def _block_diag_b(bb_r, bb_i):
    eye = jnp.eye(GROUPS_PER_KTILE, dtype=F32)

    def one(bb):
        b4 = bb.reshape(N_B_TILES, GROUPS_PER_KTILE, SSM_GROUP, SSM_STATE)
        return jnp.einsum('kgcp,gh->kgchp', b4, eye).reshape(N_B_TILES, MXU_DIM, B_TILE_STATES)

    return jnp.concatenate([one(bb_r), one(bb_i)], axis=-1).astype(BF16)


def _block_diag_c(c_gcp):
    eye = jnp.eye(C_PIECE_GROUPS, dtype=F32)
    c4 = c_gcp.reshape(N_C_PIECES, C_PIECE_GROUPS, SSM_GROUP, SSM_STATE)
    return jnp.einsum('jgcp,gh->jgphc', c4, eye).reshape(N_C_PIECES, C_PIECE_STATES, C_PIECE_OUT).astype(BF16)


def _trunk(x, mod, h0r, h0i, pool0, pos0, p, *, time_major, ffn_w):
    nb, t_len, _ = x.shape
    tt = min(t_len, TILE_ROWS // nb)
    n_sub = 1 if time_major else MIXER_SUBTILES
    tt_ffn = min(t_len, FFN_TILE_ROWS // nb)
    if time_major:
        rows = lambda a: jnp.transpose(a, (1, 0, 2)).reshape(a.shape[1] * nb, a.shape[2])
        unrows = lambda a, t: jnp.transpose(a.reshape(t, nb, a.shape[-1]), (1, 0, 2))
        x_in, pool_in, mod3 = rows(x), rows(pool0), mod[None]
    else:
        assert nb == SUBLANES
        x_in, mod3 = x, mod[:, None, :]
        pool_in = jnp.pad(pool0, ((0, 0), (HIST_PAD - POOL_HIST, 0), (0, 0)))
    cast = ffn_w if ffn_w[0].dtype == F32 else ()
    x1, s_r, s_i, new_buf, *cast_out = _mixer(
        x_in, mod3, h0r.reshape(nb, N_STATE), h0i.reshape(nb, N_STATE), pool_in,
        p["n1"], p["win"], p["bk"], p["cr"], p["ci"], p["abr"], p["abi"], p["d"], p["wglu"], p["wpool"],
        p["pscale"], p["wout"], nb=nb, tt=tt, n_sub=n_sub, n_steps=t_len // (tt * n_sub), pos0=pos0,
        time_major=time_major, cast=cast)
    wfi, wfo = cast_out if cast else ffn_w
    y = _ffn(x1, mod3, p["n2"], p["nf"], wfi, wfo, nb=nb, tt=tt_ffn, n_steps=t_len // tt_ffn,
             time_major=time_major)
    if time_major:
        y, new_buf = unrows(y, t_len), unrows(new_buf, POOL_HIST)
    shape_s = (1, nb, N_SSM_GROUPS, SSM_STATE)
    return (y, s_r.reshape(shape_s), s_i.reshape(shape_s), new_buf[None]), (wfi, wfo)


def kernel(x_prompt, x_sample, c_prompt, c_sample, state_ssm_re, state_ssm_im, state_pool, norm1_g, norm2_g, normf_g, w_ada, b_ada, w_in, ssm_lam_re, ssm_lam_im, ssm_log_dt, ssm_b_re, ssm_b_im, ssm_c_re, ssm_c_im, ssm_d, w_glu, w_pool, pool_scale, w_out, w_ffn_in, w_ffn_out):
    n_prompt = x_prompt.shape[0]
    mod_p, mod_s = _ada(c_prompt, c_sample, w_ada[0], b_ada[0])
    abar_r, abar_i, bb_r, bb_i = _ssm_prep(
        ssm_lam_re[0], ssm_lam_im[0], ssm_log_dt[0],
        jnp.transpose(ssm_b_re[0], (0, 2, 1)), jnp.transpose(ssm_b_im[0], (0, 2, 1)))
    p = dict(
        n1=norm1_g[0].reshape(1, -1), n2=norm2_g[0].reshape(1, -1), nf=normf_g.reshape(1, -1),
        win=w_in[0].astype(BF16), bk=_block_diag_b(bb_r, bb_i),
        cr=_block_diag_c(ssm_c_re[0]), ci=_block_diag_c(-ssm_c_im[0]),
        abr=abar_r.reshape(1, N_STATE), abi=abar_i.reshape(1, N_STATE), d=ssm_d[0].reshape(1, -1),
        wglu=w_glu[0].astype(BF16), wpool=w_pool[0].astype(BF16), pscale=pool_scale[0].reshape(1, -1),
        wout=w_out[0].astype(BF16))
    zero_state = jnp.zeros((n_prompt, N_SSM_GROUPS, SSM_STATE), F32)
    zero_pool = jnp.zeros((n_prompt, POOL_HIST, D_POOL), F32)
    ffn_f32 = (w_ffn_in[0], w_ffn_out[0])
    (y_p, p_re, p_im, p_pool), ffn_bf16 = _trunk(x_prompt, mod_p, zero_state, zero_state, zero_pool, 0, p,
                                                  time_major=False, ffn_w=ffn_f32)
    (y_s, s_re, s_im, s_pool), _ = _trunk(x_sample, mod_s, state_ssm_re[0], state_ssm_im[0], state_pool[0],
                                          PAST_LEN, p, time_major=True, ffn_w=ffn_bf16)
    return (y_p, y_s, p_re, p_im, p_pool, s_re, s_im, s_pool)
```

```python
import functools

import jax
import jax.numpy as jnp
from jax import lax
from jax.experimental import pallas as pl
from jax.experimental.pallas import tpu as pltpu

D_MODEL = 1024
D_SSM = 512
SSM_GROUP = 16
N_SSM_GROUPS = 32
SSM_STATE = 64
N_STATE = N_SSM_GROUPS * SSM_STATE
D_POOL = 512
POOL_WINDOWS = (2, 4, 8, 16)
POOL_GROUP = 128
POOL_OUT = 256
POOL_HIST = 15
D_FF = 2816
N_MOD = 6
EPS = 1e-6
PAST_LEN = 16384

SUBLANES = 8
BF16_SUBLANES = 16
LANES = 128
MXU_DIM = 256
TILE_ROWS = 512
MIXER_SUBTILES = 2
FFN_TILE_ROWS = 1024
SCAN_LANES = 512
HIST_PAD = 16
N_TAIL_SPLITS = 2
N_FFN_SPLITS = 4
ADA_COL_BLOCKS = 4
GROUPS_PER_KTILE = MXU_DIM // SSM_GROUP
N_B_TILES = D_SSM // MXU_DIM
B_TILE_STATES = GROUPS_PER_KTILE * SSM_STATE
C_PIECE_GROUPS = 8
C_PIECE_STATES = C_PIECE_GROUPS * SSM_STATE
C_PIECE_OUT = C_PIECE_GROUPS * SSM_GROUP
N_C_PIECES = N_SSM_GROUPS // C_PIECE_GROUPS
VMEM_LIMIT = 60 * 1024 * 1024

BF16 = jnp.bfloat16
F32 = jnp.float32


def _dot(a, b):
    return jnp.dot(a, b, preferred_element_type=F32)


def _sigmoid(x):
    return 0.5 * jnp.tanh(0.5 * x) + 0.5


def _two_sigmoid(half_x):
    return jnp.tanh(half_x) + 1.0


def _rmsnorm(x, g):
    return x * lax.rsqrt(jnp.mean(x * x, axis=-1, keepdims=True) + EPS) * g


def _tile3(v, tt, nb, time_major):
    lead, inner = (tt, nb) if time_major else (nb, tt)
    return v.reshape(lead, inner, v.shape[-1])


def _modulate(h, shift, scale, tt, nb, time_major):
    return (_tile3(h, tt, nb, time_major) * (1.0 + scale) + shift).reshape(h.shape)


def _rows_gate(v, gate, tt, nb, time_major):
    return (_tile3(v, tt, nb, time_major) * gate).reshape(v.shape)


def _ada_kernel(cp_ref, cs_ref, w_ref, b_ref, op_ref, os_ref):
    w = w_ref[...].astype(BF16)
    for c_ref, o_ref in ((cp_ref, op_ref), (cs_ref, os_ref)):
        c = c_ref[...]
        o_ref[...] = _dot((c * _sigmoid(c)).astype(BF16), w) + b_ref[...]


def _ada(c_prompt, c_sample, w_ada, b_ada):
    n_p, n_s = c_prompt.shape[0], c_sample.shape[0]
    cw = N_MOD * D_MODEL // ADA_COL_BLOCKS
    return pl.pallas_call(
        _ada_kernel,
        grid=(ADA_COL_BLOCKS,),
        in_specs=[pl.BlockSpec((n_p, D_MODEL), lambda j: (0, 0)),
                  pl.BlockSpec((n_s, D_MODEL), lambda j: (0, 0)),
                  pl.BlockSpec((D_MODEL, cw), lambda j: (0, j)),
                  pl.BlockSpec((1, cw), lambda j: (0, j))],
        out_specs=(pl.BlockSpec((n_p, cw), lambda j: (0, j)), pl.BlockSpec((n_s, cw), lambda j: (0, j))),
        out_shape=(jax.ShapeDtypeStruct((n_p, N_MOD * D_MODEL), F32),
                   jax.ShapeDtypeStruct((n_s, N_MOD * D_MODEL), F32)),
        compiler_params=pltpu.CompilerParams(dimension_semantics=("arbitrary",),
                                             vmem_limit_bytes=VMEM_LIMIT),
        name="ada_mod",
    )(c_prompt, c_sample, w_ada, b_ada.reshape(1, -1))


def _ssm_prep_kernel(lr_ref, li_ref, ldt_ref, br_ref, bi_ref, ar_ref, ai_ref, bbr_ref, bbi_ref):
    lr = lr_ref[...]
    li = li_ref[...]
    dt = jnp.exp(ldt_ref[...])
    mag = jnp.exp(lr * dt)
    abar_r = mag * jnp.cos(li * dt)
    abar_i = mag * jnp.sin(li * dt)
    den = lr * lr + li * li
    nr = abar_r - 1.0
    ni = abar_i
    f_r = (nr * lr + ni * li) / den
    f_i = (ni * lr - nr * li) / den
    b_r = br_ref[...]
    b_i = bi_ref[...]
    ar_ref[...] = abar_r
    ai_ref[...] = abar_i
    bbr_ref[...] = f_r * b_r - f_i * b_i
    bbi_ref[...] = f_r * b_i + f_i * b_r


def _ssm_prep(lam_re, lam_im, log_dt, b_re_gcp, b_im_gcp):
    g, p, c = N_SSM_GROUPS, SSM_STATE, SSM_GROUP
    return pl.pallas_call(
        _ssm_prep_kernel,
        out_shape=(jax.ShapeDtypeStruct((g, 1, p), F32), jax.ShapeDtypeStruct((g, 1, p), F32),
                   jax.ShapeDtypeStruct((g, c, p), F32), jax.ShapeDtypeStruct((g, c, p), F32)),
        name="ssm_prep",
    )(lam_re.reshape(g, 1, p), lam_im.reshape(g, 1, p), log_dt.reshape(g, 1, 1), b_re_gcp, b_im_gcp)


def _scan_tile(bur_ref, bui_ref, st_r, st_i, abr_ref, abi_ref, nb, tt):
    for lc in range(N_STATE // SCAN_LANES):
        lanes = slice(lc * SCAN_LANES, (lc + 1) * SCAN_LANES)
        ar = jnp.broadcast_to(abr_ref[:, lanes], (SUBLANES, SCAN_LANES))
        ai = jnp.broadcast_to(abi_ref[:, lanes], (SUBLANES, SCAN_LANES))

        def seq_block(rb, carry, lanes=lanes, ar=ar, ai=ai):
            r0 = pl.multiple_of(rb * SUBLANES, SUBLANES)
            sr = st_r[pl.ds(r0, SUBLANES), lanes]
            si = st_i[pl.ds(r0, SUBLANES), lanes]

            def step(t, s):
                sr, si = s
                row = pl.multiple_of(t * nb + r0, SUBLANES)
                nr = ar * sr - ai * si + bur_ref[pl.ds(row, SUBLANES), lanes]
                ni = ar * si + ai * sr + bui_ref[pl.ds(row, SUBLANES), lanes]
                bur_ref[pl.ds(row, SUBLANES), lanes] = nr
                bui_ref[pl.ds(row, SUBLANES), lanes] = ni
                return nr, ni

            sr, si = lax.fori_loop(0, tt, step, (sr, si), unroll=True)
            st_r[pl.ds(r0, SUBLANES), lanes] = sr
            st_i[pl.ds(r0, SUBLANES), lanes] = si
            return carry

        if nb == SUBLANES:
            seq_block(0, 0)
        else:
            lax.fori_loop(0, nb // SUBLANES, seq_block, 0)


def _to_time_major(v, perm_ref, nb, tt):
    for c in range(v.shape[-1] // LANES):
        for b in range(nb):
            perm_ref.at[c][pl.ds(b, tt, stride=nb), :] = v[b * tt:(b + 1) * tt, c * LANES:(c + 1) * LANES]
    return jnp.concatenate([perm_ref[c] for c in range(v.shape[-1] // LANES)], axis=-1)


def _to_seq_major(v, perm_ref, nb, tt):
    n_slabs = v.shape[-1] // LANES
    for c in range(n_slabs):
        perm_ref[c] = v[:, c * LANES:(c + 1) * LANES]
    return jnp.concatenate(
        [jnp.concatenate([perm_ref.at[c][pl.ds(b, tt, stride=nb), :] for b in range(nb)], axis=0)
         for c in range(n_slabs)], axis=-1)


def _pool_time_major(u_pool, hist_ref, i, nb, tt, pos0):
    m = tt * nb
    hist_rows = POOL_HIST * nb
    hist_ref[hist_rows:hist_rows + m, :] = u_pool
    t_abs = pos0 + i * tt + lax.broadcasted_iota(jnp.int32, (tt, nb, POOL_GROUP), 0).reshape(m, POOL_GROUP)
    pooled = []
    for k, w in enumerate(POOL_WINDOWS):
        cols = slice(k * POOL_GROUP, (k + 1) * POOL_GROUP)
        acc = u_pool[:, cols]
        for j in range(1, w):
            acc = acc + hist_ref[hist_rows - j * nb:hist_rows - j * nb + m, cols]
        count = jnp.minimum(t_abs + 1, w).astype(F32)
        pooled.append(acc / count - u_pool[:, cols])
    return pooled


def _pool_seq_major(u_pool, hist_ref, i, nb, tt, pos0):
    m = tt * nb
    t_abs = pos0 + i * tt + lax.broadcasted_iota(jnp.int32, (nb, tt, POOL_GROUP), 1)
    pooled = []
    for k, w in enumerate(POOL_WINDOWS):
        cur = u_pool[:, k * POOL_GROUP:(k + 1) * POOL_GROUP].reshape(nb, tt, POOL_GROUP)
        hist_ref[k, :, HIST_PAD:HIST_PAD + tt, :] = cur
        acc = cur
        for j in range(1, w):
            acc = acc + hist_ref[k, :, HIST_PAD - j:HIST_PAD - j + tt, :]
        count = jnp.minimum(t_abs + 1, w).astype(F32)
        pooled.append((acc / count - cur).reshape(m, POOL_GROUP))
    return pooled


def _mixer_kernel(x_ref, mod_ref, h0r_ref, h0i_ref, pool0_ref, n1_ref, win_ref, bk_ref, cr_ref, ci_ref,
                  abr_ref, abi_ref, d_ref, wglu_ref, wpool_ref, pscale_ref, wout_ref, *rest,
                  nb, tt, n_sub, n_steps, pos0, time_major, cast_halved):
    n_cast = len(cast_halved)
    cast_src, rest = rest[:n_cast], rest[n_cast:]
    xo_ref, sr_out_ref, si_out_ref, pool_out_ref = rest[:4]
    cast_dst, rest = rest[4:4 + n_cast], rest[4 + n_cast:]
    bur_ref, bui_ref, st_r, st_i, hist_ref, perm_ref = rest
    i = pl.program_id(0)
    for src, dst, n_half in zip(cast_src, cast_dst, cast_halved):
        if n_half:
            dst[:, :n_half] = (src[:, :n_half] * 0.5).astype(BF16)
        dst[:, n_half:] = src[:, n_half:].astype(BF16)
    m = tt * nb
    hist_rows = POOL_HIST * nb

    @pl.when(i == 0)
    def _():
        st_r[...] = h0r_ref[...]
        st_i[...] = h0i_ref[...]
        if time_major:
            hist_ref[0:hist_rows, :] = pool0_ref[...]
        else:
            for k in range(len(POOL_WINDOWS)):
                hist_ref[k, :, 0:HIST_PAD, :] = pool0_ref[:, :, k * POOL_GROUP:(k + 1) * POOL_GROUP]

    for sub in range(n_sub):
        tile = i * n_sub + sub
        t_rows = slice(sub * tt, (sub + 1) * tt)
        x = (x_ref[...] if time_major else x_ref[:, t_rows, :]).reshape(m, D_MODEL)
        shift1 = mod_ref[:, :, 0 * D_MODEL:1 * D_MODEL]
        scale1 = mod_ref[:, :, 1 * D_MODEL:2 * D_MODEL]
        gate1 = mod_ref[:, :, 2 * D_MODEL:3 * D_MODEL]
        hb = _modulate(_rmsnorm(x, n1_ref[...]), shift1, scale1, tt, nb, time_major).astype(BF16)

        u_ssm = _dot(hb, win_ref[:, 0:D_SSM])
        u_pool = _dot(hb, win_ref[:, D_SSM:D_SSM + D_POOL])
        u_tm = u_ssm if time_major else _to_time_major(u_ssm, perm_ref, nb, tt)
        for k in range(N_B_TILES):
            res = _dot(u_tm[:, k * MXU_DIM:(k + 1) * MXU_DIM].astype(BF16), bk_ref[k])
            bur_ref[:, k * B_TILE_STATES:(k + 1) * B_TILE_STATES] = res[:, :B_TILE_STATES]
            bui_ref[:, k * B_TILE_STATES:(k + 1) * B_TILE_STATES] = res[:, B_TILE_STATES:]
        g_ssm = _dot(hb, win_ref[:, D_SSM + D_POOL:D_SSM + D_POOL + D_MODEL])
        g_pool = _dot(hb, win_ref[:, D_SSM + D_POOL + D_MODEL:])

        _scan_tile(bur_ref, bui_ref, st_r, st_i, abr_ref, abi_ref, nb, tt)
        if time_major:
            pooled = _pool_time_major(u_pool, hist_ref, tile, nb, tt, pos0)
            new_buf_rows = (slice(m, m + hist_rows), slice(None))
            if n_steps * n_sub > 1:
                hist_ref[0:hist_rows, :] = hist_ref[new_buf_rows]
        else:
            pooled = _pool_seq_major(u_pool, hist_ref, tile, nb, tt, pos0)
            if n_steps * n_sub > 1:
                hist_ref[:, :, 0:HIST_PAD, :] = hist_ref[:, :, tt:tt + HIST_PAD, :]
        z_pieces = [_dot(pooled[k].astype(BF16), wpool_ref[k]) for k in range(len(POOL_WINDOWS))]
        br_pool = jnp.concatenate(z_pieces, axis=-1) * pscale_ref[...]

        y_pieces = []
        for j in range(N_C_PIECES):
            sl = slice(j * C_PIECE_STATES, (j + 1) * C_PIECE_STATES)
            y_pieces.append(_dot(bur_ref[:, sl].astype(BF16), cr_ref[j])
                            + _dot(bui_ref[:, sl].astype(BF16), ci_ref[j]))
        y_cs = jnp.concatenate(y_pieces, axis=-1)
        if not time_major:
            y_cs = _to_seq_major(y_cs, perm_ref, nb, tt)

        for h in range(N_TAIL_SPLITS):
            rows = slice(h * m // N_TAIL_SPLITS, (h + 1) * m // N_TAIL_SPLITS)
            y_ssm = y_cs[rows] + d_ref[...] * u_ssm[rows]
            glu = _dot(jax.nn.gelu(y_ssm).astype(BF16), wglu_ref[...])
            br_ssm = glu[:, :D_MODEL] * _two_sigmoid(glu[:, D_MODEL:])
            merged2 = _two_sigmoid(g_ssm[rows]) * br_ssm + _two_sigmoid(g_pool[rows]) * br_pool[rows]
            o = _dot(merged2.astype(BF16), wout_ref[...])
            if time_major:
                xo_ref[rows, :] = x[rows] + _rows_gate(o, gate1, tt // N_TAIL_SPLITS, nb, True)
            else:
                nb_h = nb // N_TAIL_SPLITS
                seqs = slice(h * nb_h, (h + 1) * nb_h)
                x1 = x[rows] + _rows_gate(o, gate1[seqs], tt, nb_h, False)
                xo_ref[seqs, t_rows, :] = x1.reshape(nb_h, tt, D_MODEL)

    @pl.when(i == n_steps - 1)
    def _():
        sr_out_ref[...] = st_r[...]
        si_out_ref[...] = st_i[...]
        if time_major:
            pool_out_ref[...] = hist_ref[new_buf_rows]
        else:
            pool_out_ref[...] = jnp.concatenate(
                [hist_ref[k, :, tt + HIST_PAD - POOL_HIST:tt + HIST_PAD, :] for k in range(len(POOL_WINDOWS))],
                axis=-1)


def _const_spec(shape):
    nd = len(shape)
    return pl.BlockSpec(shape, lambda i, _nd=nd: (0,) * _nd, pipeline_mode=pl.Buffered(1))


def _row_spec(nb, tt, time_major):
    if time_major:
        return pl.BlockSpec((tt * nb, D_MODEL), lambda i: (i, 0))
    return pl.BlockSpec((nb, tt, D_MODEL), lambda i: (0, i, 0))


def _mixer(x, mod, h0r, h0i, pool0, n1, win, bk, cr, ci, abr, abi, d, wglu, wpool, pscale, wout,
           *, nb, tt, n_sub, n_steps, pos0, time_major, cast=()):
    m = tt * nb
    consts = (mod, h0r, h0i, pool0, n1, win, bk, cr, ci, abr, abi, d, wglu, wpool, pscale, wout)
    cast, cast_halved = tuple(a for a, _ in cast), tuple(n for _, n in cast)
    cast_specs = []
    for a in cast:
        n_blk = max(k for k in range(1, n_steps + 1)
                    if n_steps % k == 0 and a.shape[0] % (k * BF16_SUBLANES) == 0)
        cast_specs.append(pl.BlockSpec((a.shape[0] // n_blk, a.shape[1]),
                                       lambda i, _hold=n_steps // n_blk: (i // _hold, 0)))
    if time_major:
        hist_shape = (POOL_HIST * nb + m, D_POOL)
        pool_out_shape = (POOL_HIST * nb, D_POOL)
    else:
        hist_shape = (len(POOL_WINDOWS), nb, HIST_PAD + tt, POOL_GROUP)
        pool_out_shape = (nb, POOL_HIST, D_POOL)
    kern = functools.partial(_mixer_kernel, nb=nb, tt=tt, n_sub=n_sub, n_steps=n_steps, pos0=pos0,
                             time_major=time_major, cast_halved=cast_halved)
    return pl.pallas_call(
        kern,
        grid=(n_steps,),
        in_specs=[_row_spec(nb, n_sub * tt, time_major)] + [_const_spec(a.shape) for a in consts] + cast_specs,
        out_specs=(_row_spec(nb, n_sub * tt, time_major),
                   pl.BlockSpec((nb, N_STATE), lambda i: (0, 0)),
                   pl.BlockSpec((nb, N_STATE), lambda i: (0, 0)),
                   pl.BlockSpec(pool_out_shape, lambda i: (0,) * len(pool_out_shape))) + tuple(cast_specs),
        out_shape=(jax.ShapeDtypeStruct(x.shape, F32),
                   jax.ShapeDtypeStruct((nb, N_STATE), F32),
                   jax.ShapeDtypeStruct((nb, N_STATE), F32),
                   jax.ShapeDtypeStruct(pool_out_shape, F32)) + tuple(jax.ShapeDtypeStruct(a.shape, BF16) for a in cast),
        scratch_shapes=[pltpu.VMEM((m, N_STATE), F32), pltpu.VMEM((m, N_STATE), F32),
                        pltpu.VMEM((nb, N_STATE), F32), pltpu.VMEM((nb, N_STATE), F32),
                        pltpu.VMEM(hist_shape, F32),
                        pltpu.VMEM((D_SSM // LANES, m, LANES), F32)],
        compiler_params=pltpu.CompilerParams(dimension_semantics=("arbitrary",),
                                             vmem_limit_bytes=VMEM_LIMIT),
        name=f"mixer_nb{nb}",
    )(x, *consts, *cast)


def _ffn_kernel(x_ref, mod_ref, n2_ref, nf_ref, wfi_ref, wfo_ref, o_ref, *, nb, tt, time_major):
    for h in range(N_FFN_SPLITS):
        if time_major:
            tt_h, nb_h = tt // N_FFN_SPLITS, nb
            piece = (slice(h * tt_h * nb, (h + 1) * tt_h * nb), slice(None))
            mod = mod_ref
        else:
            tt_h, nb_h = tt, nb // N_FFN_SPLITS
            piece = (slice(h * nb_h, (h + 1) * nb_h), slice(None), slice(None))
            mod = mod_ref.at[h * nb_h:(h + 1) * nb_h]
        x = x_ref[piece].reshape(tt_h * nb_h, D_MODEL)
        shift2 = mod[:, :, 3 * D_MODEL:4 * D_MODEL]
        scale2 = mod[:, :, 4 * D_MODEL:5 * D_MODEL]
        gate2 = mod[:, :, 5 * D_MODEL:6 * D_MODEL]
        hb = _modulate(_rmsnorm(x, n2_ref[...]), shift2, scale2, tt_h, nb_h, time_major).astype(BF16)
        f_a = _dot(hb, wfi_ref[:, :D_FF])
        f_b = _dot(hb, wfi_ref[:, D_FF:])
        act = (f_a * _two_sigmoid(f_a) * f_b).astype(BF16)
        x2 = x + _rows_gate(_dot(act, wfo_ref[...]), gate2, tt_h, nb_h, time_major)
        y = _rmsnorm(x2, nf_ref[...])
        o_ref[piece] = y if time_major else y.reshape(nb_h, tt_h, D_MODEL)


def _ffn(x, mod, n2, nf, wfi, wfo, *, nb, tt, n_steps, time_major):
    consts = (mod, n2, nf, wfi, wfo)
    return pl.pallas_call(
        functools.partial(_ffn_kernel, nb=nb, tt=tt, time_major=time_major),
        grid=(n_steps,),
        in_specs=[_row_spec(nb, tt, time_major)] + [_const_spec(a.shape) for a in consts],
        out_specs=_row_spec(nb, tt, time_major),
        out_shape=jax.ShapeDtypeStruct(x.shape, F32),
        compiler_params=pltpu.CompilerParams(dimension_semantics=("arbitrary",),
                                             vmem_limit_bytes=VMEM_LIMIT),
        name=f"ffn_nb{nb}",
    )(x, *consts)


def _block_diag_b(bb_r, bb_i):
    eye = jnp.eye(GROUPS_PER_KTILE, dtype=F32)

    def one(bb):
        b4 = bb.reshape(N_B_TILES, GROUPS_PER_KTILE, SSM_GROUP, SSM_STATE)
        return jnp.einsum('kgcp,gh->kgchp', b4, eye).reshape(N_B_TILES, MXU_DIM, B_TILE_STATES)

    return jnp.concatenate([one(bb_r), one(bb_i)], axis=-1).astype(BF16)


def _block_diag_c(c_gcp):
    eye = jnp.eye(C_PIECE_GROUPS, dtype=F32)
    c4 = c_gcp.reshape(N_C_PIECES, C_PIECE_GROUPS, SSM_GROUP, SSM_STATE)
    return jnp.einsum('jgcp,gh->jgphc', c4, eye).reshape(N_C_PIECES, C_PIECE_STATES, C_PIECE_OUT).astype(BF16)


def _trunk(x, mod, h0r, h0i, pool0, pos0, p, *, time_major, ffn_w):
    nb, t_len, _ = x.shape
    tt = min(t_len, TILE_ROWS // nb)
    n_sub = 1 if time_major else MIXER_SUBTILES
    tt_ffn = min(t_len, FFN_TILE_ROWS // nb)
    if time_major:
        rows = lambda a: jnp.transpose(a, (1, 0, 2)).reshape(a.shape[1] * nb, a.shape[2])
        unrows = lambda a, t: jnp.transpose(a.reshape(t, nb, a.shape[-1]), (1, 0, 2))
        x_in, pool_in, mod3 = rows(x), rows(pool0), mod[None]
    else:
        assert nb == SUBLANES
        x_in, mod3 = x, mod[:, None, :]
        pool_in = jnp.pad(pool0, ((0, 0), (HIST_PAD - POOL_HIST, 0), (0, 0)))
    cast = ((ffn_w[0], D_FF), (ffn_w[1], 0)) if ffn_w[0].dtype == F32 else ()
    x1, s_r, s_i, new_buf, *cast_out = _mixer(
        x_in, mod3, h0r.reshape(nb, N_STATE), h0i.reshape(nb, N_STATE), pool_in,
        p["n1"], p["win"], p["bk"], p["cr"], p["ci"], p["abr"], p["abi"], p["d"], p["wglu"], p["wpool"],
        p["pscale"], p["wout"], nb=nb, tt=tt, n_sub=n_sub, n_steps=t_len // (tt * n_sub), pos0=pos0,
        time_major=time_major, cast=cast)
    wfi, wfo = cast_out if cast else ffn_w
    y = _ffn(x1, mod3, p["n2"], p["nf"], wfi, wfo, nb=nb, tt=tt_ffn, n_steps=t_len // tt_ffn,
             time_major=time_major)
    if time_major:
        y, new_buf = unrows(y, t_len), unrows(new_buf, POOL_HIST)
    shape_s = (1, nb, N_SSM_GROUPS, SSM_STATE)
    return (y, s_r.reshape(shape_s), s_i.reshape(shape_s), new_buf[None]), (wfi, wfo)


def kernel(x_prompt, x_sample, c_prompt, c_sample, state_ssm_re, state_ssm_im, state_pool, norm1_g, norm2_g, normf_g, w_ada, b_ada, w_in, ssm_lam_re, ssm_lam_im, ssm_log_dt, ssm_b_re, ssm_b_im, ssm_c_re, ssm_c_im, ssm_d, w_glu, w_pool, pool_scale, w_out, w_ffn_in, w_ffn_out):
    n_prompt = x_prompt.shape[0]
    mod_p, mod_s = _ada(c_prompt, c_sample, w_ada[0], b_ada[0])
    abar_r, abar_i, bb_r, bb_i = _ssm_prep(
        ssm_lam_re[0], ssm_lam_im[0], ssm_log_dt[0],
        jnp.transpose(ssm_b_re[0], (0, 2, 1)), jnp.transpose(ssm_b_im[0], (0, 2, 1)))
    win_scale = jnp.where(jnp.arange(w_in.shape[-1]) < D_SSM + D_POOL, 1.0, 0.5).astype(F32)
    p = dict(
        n1=norm1_g[0].reshape(1, -1), n2=norm2_g[0].reshape(1, -1), nf=normf_g.reshape(1, -1),
        win=(w_in[0] * win_scale).astype(BF16), bk=_block_diag_b(bb_r, bb_i),
        cr=_block_diag_c(ssm_c_re[0]), ci=_block_diag_c(-ssm_c_im[0]),
        abr=abar_r.reshape(1, N_STATE), abi=abar_i.reshape(1, N_STATE), d=ssm_d[0].reshape(1, -1),
        wglu=(w_glu[0] * 0.5).astype(BF16), wpool=w_pool[0].astype(BF16), pscale=pool_scale[0].reshape(1, -1),
        wout=(w_out[0] * 0.5).astype(BF16))
    zero_state = jnp.zeros((n_prompt, N_SSM_GROUPS, SSM_STATE), F32)
    zero_pool = jnp.zeros((n_prompt, POOL_HIST, D_POOL), F32)
    ffn_f32 = (w_ffn_in[0], w_ffn_out[0])
    (y_p, p_re, p_im, p_pool), ffn_bf16 = _trunk(x_prompt, mod_p, zero_state, zero_state, zero_pool, 0, p,
                                                  time_major=False, ffn_w=ffn_f32)
    (y_s, s_re, s_im, s_pool), _ = _trunk(x_sample, mod_s, state_ssm_re[0], state_ssm_im[0], state_pool[0],
                                          PAST_LEN, p, time_major=True, ffn_w=ffn_bf16)
    return (y_p, y_s, p_re, p_im, p_pool, s_re, s_im, s_pool)
```

```python
import functools

import jax
import jax.numpy as jnp
from jax import lax
from jax.experimental import pallas as pl
from jax.experimental.pallas import tpu as pltpu

D_MODEL = 1024
D_SSM = 512
SSM_GROUP = 16
N_SSM_GROUPS = 32
SSM_STATE = 64
N_STATE = N_SSM_GROUPS * SSM_STATE
D_POOL = 512
POOL_WINDOWS = (2, 4, 8, 16)
POOL_GROUP = 128
POOL_OUT = 256
POOL_HIST = 15
D_FF = 2816
N_MOD = 6
EPS = 1e-6
PAST_LEN = 16384

SUBLANES = 8
BF16_SUBLANES = 16
LANES = 128
MXU_DIM = 256
TILE_ROWS = 512
MIXER_SUBTILES = 2
FFN_TILE_ROWS = 1024
SCAN_LANES = 512
HIST_PAD = 16
N_TAIL_SPLITS = 2
N_FFN_SPLITS = 4
ADA_COL_BLOCKS = 8
GROUPS_PER_KTILE = MXU_DIM // SSM_GROUP
N_B_TILES = D_SSM // MXU_DIM
B_TILE_STATES = GROUPS_PER_KTILE * SSM_STATE
C_PIECE_GROUPS = 8
C_PIECE_STATES = C_PIECE_GROUPS * SSM_STATE
C_PIECE_OUT = C_PIECE_GROUPS * SSM_GROUP
N_C_PIECES = N_SSM_GROUPS // C_PIECE_GROUPS
VMEM_LIMIT = 60 * 1024 * 1024

BF16 = jnp.bfloat16
F32 = jnp.float32


def _dot(a, b):
    return jnp.dot(a, b, preferred_element_type=F32)


def _sigmoid(x):
    return 0.5 * jnp.tanh(0.5 * x) + 0.5


def _two_sigmoid(half_x):
    return jnp.tanh(half_x) + 1.0


def _rmsnorm(x, g):
    return x * lax.rsqrt(jnp.mean(x * x, axis=-1, keepdims=True) + EPS) * g


def _tile3(v, tt, nb, time_major):
    lead, inner = (tt, nb) if time_major else (nb, tt)
    return v.reshape(lead, inner, v.shape[-1])


def _modulate(h, shift, scale, tt, nb, time_major):
    return (_tile3(h, tt, nb, time_major) * (1.0 + scale) + shift).reshape(h.shape)


def _rows_gate(v, gate, tt, nb, time_major):
    return (_tile3(v, tt, nb, time_major) * gate).reshape(v.shape)


def _ada_kernel(cp_ref, cs_ref, w_ref, b_ref, op_ref, os_ref):
    w = w_ref[...].astype(BF16)
    for c_ref, o_ref in ((cp_ref, op_ref), (cs_ref, os_ref)):
        c = c_ref[...]
        o_ref[...] = _dot((c * _sigmoid(c)).astype(BF16), w) + b_ref[...]


def _ada(c_prompt, c_sample, w_ada, b_ada):
    n_p, n_s = c_prompt.shape[0], c_sample.shape[0]
    cw = N_MOD * D_MODEL // ADA_COL_BLOCKS
    return pl.pallas_call(
        _ada_kernel,
        grid=(ADA_COL_BLOCKS,),
        in_specs=[pl.BlockSpec((n_p, D_MODEL), lambda j: (0, 0)),
                  pl.BlockSpec((n_s, D_MODEL), lambda j: (0, 0)),
                  pl.BlockSpec((D_MODEL, cw), lambda j: (0, j)),
                  pl.BlockSpec((1, cw), lambda j: (0, j))],
        out_specs=(pl.BlockSpec((n_p, cw), lambda j: (0, j)), pl.BlockSpec((n_s, cw), lambda j: (0, j))),
        out_shape=(jax.ShapeDtypeStruct((n_p, N_MOD * D_MODEL), F32),
                   jax.ShapeDtypeStruct((n_s, N_MOD * D_MODEL), F32)),
        compiler_params=pltpu.CompilerParams(dimension_semantics=("arbitrary",),
                                             vmem_limit_bytes=VMEM_LIMIT),
        name="ada_mod",
    )(c_prompt, c_sample, w_ada, b_ada.reshape(1, -1))


def _ssm_prep_kernel(lr_ref, li_ref, ldt_ref, br_ref, bi_ref, cr_ref, ci_ref,
                     ar_ref, ai_ref, bk_ref, crm_ref, cim_ref):
    lr = lr_ref[...]
    li = li_ref[...]
    dt = jnp.exp(ldt_ref[...])
    mag = jnp.exp(lr * dt)
    abar_r = mag * jnp.cos(li * dt)
    abar_i = mag * jnp.sin(li * dt)
    den = lr * lr + li * li
    nr = abar_r - 1.0
    ni = abar_i
    f_r = (nr * lr + ni * li) / den
    f_i = (ni * lr - nr * li) / den
    bk_ref[...] = jnp.zeros(bk_ref.shape, BF16)
    crm_ref[...] = jnp.zeros(crm_ref.shape, BF16)
    cim_ref[...] = jnp.zeros(cim_ref.shape, BF16)
    for g in range(N_SSM_GROUPS):
        states = slice(g * SSM_STATE, (g + 1) * SSM_STATE)
        ar_ref[:, states] = abar_r[g:g + 1, :]
        ai_ref[:, states] = abar_i[g:g + 1, :]
        b_r = br_ref[g]
        b_i = bi_ref[g]
        frg = jnp.transpose(f_r[g:g + 1, :])
        fig = jnp.transpose(f_i[g:g + 1, :])
        bbr = jnp.transpose(frg * b_r - fig * b_i)
        bbi = jnp.transpose(frg * b_i + fig * b_r)
        k, gl = divmod(g, GROUPS_PER_KTILE)
        rows = slice(gl * SSM_GROUP, (gl + 1) * SSM_GROUP)
        bk_ref[k, rows, gl * SSM_STATE:(gl + 1) * SSM_STATE] = bbr.astype(BF16)
        bk_ref[k, rows, B_TILE_STATES + gl * SSM_STATE:B_TILE_STATES + (gl + 1) * SSM_STATE] = bbi.astype(BF16)
        j, gj = divmod(g, C_PIECE_GROUPS)
        rows = slice(gj * SSM_STATE, (gj + 1) * SSM_STATE)
        cols = slice(gj * SSM_GROUP, (gj + 1) * SSM_GROUP)
        crm_ref[j, rows, cols] = jnp.transpose(cr_ref[g]).astype(BF16)
        cim_ref[j, rows, cols] = (-jnp.transpose(ci_ref[g])).astype(BF16)


def _ssm_prep(lam_re, lam_im, log_dt, b_re, b_im, c_re, c_im):
    return pl.pallas_call(
        _ssm_prep_kernel,
        out_shape=(jax.ShapeDtypeStruct((1, N_STATE), F32), jax.ShapeDtypeStruct((1, N_STATE), F32),
                   jax.ShapeDtypeStruct((N_B_TILES, MXU_DIM, 2 * B_TILE_STATES), BF16),
                   jax.ShapeDtypeStruct((N_C_PIECES, C_PIECE_STATES, C_PIECE_OUT), BF16),
                   jax.ShapeDtypeStruct((N_C_PIECES, C_PIECE_STATES, C_PIECE_OUT), BF16)),
        name="ssm_prep",
    )(lam_re, lam_im, log_dt.reshape(N_SSM_GROUPS, 1), b_re, b_im, c_re, c_im)


def _scan_tile(bur_ref, bui_ref, st_r, st_i, abr_ref, abi_ref, nb, tt):
    for lc in range(N_STATE // SCAN_LANES):
        lanes = slice(lc * SCAN_LANES, (lc + 1) * SCAN_LANES)
        ar = jnp.broadcast_to(abr_ref[:, lanes], (SUBLANES, SCAN_LANES))
        ai = jnp.broadcast_to(abi_ref[:, lanes], (SUBLANES, SCAN_LANES))

        def seq_block(rb, carry, lanes=lanes, ar=ar, ai=ai):
            r0 = pl.multiple_of(rb * SUBLANES, SUBLANES)
            sr = st_r[pl.ds(r0, SUBLANES), lanes]
            si = st_i[pl.ds(r0, SUBLANES), lanes]

            def step(t, s):
                sr, si = s
                row = pl.multiple_of(t * nb + r0, SUBLANES)
                nr = ar * sr - ai * si + bur_ref[pl.ds(row, SUBLANES), lanes]
                ni = ar * si + ai * sr + bui_ref[pl.ds(row, SUBLANES), lanes]
                bur_ref[pl.ds(row, SUBLANES), lanes] = nr
                bui_ref[pl.ds(row, SUBLANES), lanes] = ni
                return nr, ni

            sr, si = lax.fori_loop(0, tt, step, (sr, si), unroll=True)
            st_r[pl.ds(r0, SUBLANES), lanes] = sr
            st_i[pl.ds(r0, SUBLANES), lanes] = si
            return carry

        if nb == SUBLANES:
            seq_block(0, 0)
        else:
            lax.fori_loop(0, nb // SUBLANES, seq_block, 0)


def _to_time_major(v, perm_ref, nb, tt):
    for c in range(v.shape[-1] // LANES):
        for b in range(nb):
            perm_ref.at[c][pl.ds(b, tt, stride=nb), :] = v[b * tt:(b + 1) * tt, c * LANES:(c + 1) * LANES]
    return jnp.concatenate([perm_ref[c] for c in range(v.shape[-1] // LANES)], axis=-1)


def _to_seq_major(v, perm_ref, nb, tt):
    n_slabs = v.shape[-1] // LANES
    for c in range(n_slabs):
        perm_ref[c] = v[:, c * LANES:(c + 1) * LANES]
    return jnp.concatenate(
        [jnp.concatenate([perm_ref.at[c][pl.ds(b, tt, stride=nb), :] for b in range(nb)], axis=0)
         for c in range(n_slabs)], axis=-1)


def _pool_time_major(u_pool, hist_ref, i, nb, tt, pos0):
    m = tt * nb
    hist_rows = POOL_HIST * nb
    hist_ref[hist_rows:hist_rows + m, :] = u_pool
    t_abs = pos0 + i * tt + lax.broadcasted_iota(jnp.int32, (tt, nb, POOL_GROUP), 0).reshape(m, POOL_GROUP)
    pooled = []
    for k, w in enumerate(POOL_WINDOWS):
        cols = slice(k * POOL_GROUP, (k + 1) * POOL_GROUP)
        acc = u_pool[:, cols]
        for j in range(1, w):
            acc = acc + hist_ref[hist_rows - j * nb:hist_rows - j * nb + m, cols]
        count = jnp.minimum(t_abs + 1, w).astype(F32)
        pooled.append(acc / count - u_pool[:, cols])
    return pooled


def _pool_seq_major(u_pool, hist_ref, i, nb, tt, pos0):
    m = tt * nb
    t_abs = pos0 + i * tt + lax.broadcasted_iota(jnp.int32, (nb, tt, POOL_GROUP), 1)
    pooled = []
    for k, w in enumerate(POOL_WINDOWS):
        cur = u_pool[:, k * POOL_GROUP:(k + 1) * POOL_GROUP].reshape(nb, tt, POOL_GROUP)
        hist_ref[k, :, HIST_PAD:HIST_PAD + tt, :] = cur
        acc = cur
        for j in range(1, w):
            acc = acc + hist_ref[k, :, HIST_PAD - j:HIST_PAD - j + tt, :]
        count = jnp.minimum(t_abs + 1, w).astype(F32)
        pooled.append((acc / count - cur).reshape(m, POOL_GROUP))
    return pooled


def _mixer_kernel(x_ref, mod_ref, h0r_ref, h0i_ref, pool0_ref, n1_ref, win_ref, bk_ref, cr_ref, ci_ref,
                  abr_ref, abi_ref, d_ref, wglu_ref, wpool_ref, pscale_ref, wout_ref, *rest,
                  nb, tt, n_sub, n_steps, pos0, time_major, cast_halved):
    n_cast = len(cast_halved)
    cast_src, rest = rest[:n_cast], rest[n_cast:]
    xo_ref, sr_out_ref, si_out_ref, pool_out_ref = rest[:4]
    cast_dst, rest = rest[4:4 + n_cast], rest[4 + n_cast:]
    bur_ref, bui_ref, st_r, st_i, hist_ref, perm_ref = rest
    i = pl.program_id(0)
    for src, dst, n_half in zip(cast_src, cast_dst, cast_halved):
        if n_half:
            dst[:, :n_half] = (src[:, :n_half] * 0.5).astype(BF16)
        dst[:, n_half:] = src[:, n_half:].astype(BF16)
    m = tt * nb
    hist_rows = POOL_HIST * nb

    @pl.when(i == 0)
    def _():
        st_r[...] = h0r_ref[...]
        st_i[...] = h0i_ref[...]
        if time_major:
            hist_ref[0:hist_rows, :] = pool0_ref[...]
        else:
            for k in range(len(POOL_WINDOWS)):
                hist_ref[k, :, 0:HIST_PAD, :] = pool0_ref[:, :, k * POOL_GROUP:(k + 1) * POOL_GROUP]

    for sub in range(n_sub):
        tile = i * n_sub + sub
        t_rows = slice(sub * tt, (sub + 1) * tt)
        x = (x_ref[...] if time_major else x_ref[:, t_rows, :]).reshape(m, D_MODEL)
        shift1 = mod_ref[:, :, 0 * D_MODEL:1 * D_MODEL]
        scale1 = mod_ref[:, :, 1 * D_MODEL:2 * D_MODEL]
        gate1 = mod_ref[:, :, 2 * D_MODEL:3 * D_MODEL]
        hb = _modulate(_rmsnorm(x, n1_ref[...]), shift1, scale1, tt, nb, time_major).astype(BF16)

        u_ssm = _dot(hb, win_ref[:, 0:D_SSM])
        u_pool = _dot(hb, win_ref[:, D_SSM:D_SSM + D_POOL])
        u_tm = u_ssm if time_major else _to_time_major(u_ssm, perm_ref, nb, tt)
        for k in range(N_B_TILES):
            res = _dot(u_tm[:, k * MXU_DIM:(k + 1) * MXU_DIM].astype(BF16), bk_ref[k])
            bur_ref[:, k * B_TILE_STATES:(k + 1) * B_TILE_STATES] = res[:, :B_TILE_STATES]
            bui_ref[:, k * B_TILE_STATES:(k + 1) * B_TILE_STATES] = res[:, B_TILE_STATES:]
        g_ssm = _dot(hb, win_ref[:, D_SSM + D_POOL:D_SSM + D_POOL + D_MODEL])
        g_pool = _dot(hb, win_ref[:, D_SSM + D_POOL + D_MODEL:])

        _scan_tile(bur_ref, bui_ref, st_r, st_i, abr_ref, abi_ref, nb, tt)
        if time_major:
            pooled = _pool_time_major(u_pool, hist_ref, tile, nb, tt, pos0)
            new_buf_rows = (slice(m, m + hist_rows), slice(None))
            if n_steps * n_sub > 1:
                hist_ref[0:hist_rows, :] = hist_ref[new_buf_rows]
        else:
            pooled = _pool_seq_major(u_pool, hist_ref, tile, nb, tt, pos0)
            if n_steps * n_sub > 1:
                hist_ref[:, :, 0:HIST_PAD, :] = hist_ref[:, :, tt:tt + HIST_PAD, :]
        z_pieces = [_dot(pooled[k].astype(BF16), wpool_ref[k]) for k in range(len(POOL_WINDOWS))]
        br_pool = jnp.concatenate(z_pieces, axis=-1) * pscale_ref[...]

        y_pieces = []
        for j in range(N_C_PIECES):
            sl = slice(j * C_PIECE_STATES, (j + 1) * C_PIECE_STATES)
            y_pieces.append(_dot(bur_ref[:, sl].astype(BF16), cr_ref[j])
                            + _dot(bui_ref[:, sl].astype(BF16), ci_ref[j]))
        y_cs = jnp.concatenate(y_pieces, axis=-1)
        if not time_major:
            y_cs = _to_seq_major(y_cs, perm_ref, nb, tt)

        for h in range(N_TAIL_SPLITS):
            rows = slice(h * m // N_TAIL_SPLITS, (h + 1) * m // N_TAIL_SPLITS)
            y_ssm = y_cs[rows] + d_ref[...] * u_ssm[rows]
            glu = _dot(jax.nn.gelu(y_ssm).astype(BF16), wglu_ref[...])
            br_ssm = glu[:, :D_MODEL] * _two_sigmoid(glu[:, D_MODEL:])
            merged2 = _two_sigmoid(g_ssm[rows]) * br_ssm + _two_sigmoid(g_pool[rows]) * br_pool[rows]
            o = _dot(merged2.astype(BF16), wout_ref[...])
            if time_major:
                xo_ref[rows, :] = x[rows] + _rows_gate(o, gate1, tt // N_TAIL_SPLITS, nb, True)
            else:
                nb_h = nb // N_TAIL_SPLITS
                seqs = slice(h * nb_h, (h + 1) * nb_h)
                x1 = x[rows] + _rows_gate(o, gate1[seqs], tt, nb_h, False)
                xo_ref[seqs, t_rows, :] = x1.reshape(nb_h, tt, D_MODEL)

    @pl.when(i == n_steps - 1)
    def _():
        sr_out_ref[...] = st_r[...]
        si_out_ref[...] = st_i[...]
        if time_major:
            pool_out_ref[...] = hist_ref[new_buf_rows]
        else:
            pool_out_ref[...] = jnp.concatenate(
                [hist_ref[k, :, tt + HIST_PAD - POOL_HIST:tt + HIST_PAD, :] for k in range(len(POOL_WINDOWS))],
                axis=-1)


def _const_spec(shape):
    nd = len(shape)
    return pl.BlockSpec(shape, lambda i, _nd=nd: (0,) * _nd, pipeline_mode=pl.Buffered(1))


def _row_spec(nb, tt, time_major):
    if time_major:
        return pl.BlockSpec((tt * nb, D_MODEL), lambda i: (i, 0))
    return pl.BlockSpec((nb, tt, D_MODEL), lambda i: (0, i, 0))


def _mixer(x, mod, h0r, h0i, pool0, n1, win, bk, cr, ci, abr, abi, d, wglu, wpool, pscale, wout,
           *, nb, tt, n_sub, n_steps, pos0, time_major, cast=()):
    m = tt * nb
    consts = (mod, h0r, h0i, pool0, n1, win, bk, cr, ci, abr, abi, d, wglu, wpool, pscale, wout)
    cast, cast_halved = tuple(a for a, _ in cast), tuple(n for _, n in cast)
    cast_specs = []
    for a in cast:
        n_blk = max(k for k in range(1, n_steps + 1)
                    if n_steps % k == 0 and a.shape[0] % (k * BF16_SUBLANES) == 0)
        cast_specs.append(pl.BlockSpec((a.shape[0] // n_blk, a.shape[1]),
                                       lambda i, _hold=n_steps // n_blk: (i // _hold, 0)))
    if time_major:
        hist_shape = (POOL_HIST * nb + m, D_POOL)
        pool_out_shape = (POOL_HIST * nb, D_POOL)
    else:
        hist_shape = (len(POOL_WINDOWS), nb, HIST_PAD + tt, POOL_GROUP)
        pool_out_shape = (nb, POOL_HIST, D_POOL)
    kern = functools.partial(_mixer_kernel, nb=nb, tt=tt, n_sub=n_sub, n_steps=n_steps, pos0=pos0,
                             time_major=time_major, cast_halved=cast_halved)
    return pl.pallas_call(
        kern,
        grid=(n_steps,),
        in_specs=[_row_spec(nb, n_sub * tt, time_major)] + [_const_spec(a.shape) for a in consts] + cast_specs,
        out_specs=(_row_spec(nb, n_sub * tt, time_major),
                   pl.BlockSpec((nb, N_STATE), lambda i: (0, 0)),
                   pl.BlockSpec((nb, N_STATE), lambda i: (0, 0)),
                   pl.BlockSpec(pool_out_shape, lambda i: (0,) * len(pool_out_shape))) + tuple(cast_specs),
        out_shape=(jax.ShapeDtypeStruct(x.shape, F32),
                   jax.ShapeDtypeStruct((nb, N_STATE), F32),
                   jax.ShapeDtypeStruct((nb, N_STATE), F32),
                   jax.ShapeDtypeStruct(pool_out_shape, F32)) + tuple(jax.ShapeDtypeStruct(a.shape, BF16) for a in cast),
        scratch_shapes=[pltpu.VMEM((m, N_STATE), F32), pltpu.VMEM((m, N_STATE), F32),
                        pltpu.VMEM((nb, N_STATE), F32), pltpu.VMEM((nb, N_STATE), F32),
                        pltpu.VMEM(hist_shape, F32),
                        pltpu.VMEM((D_SSM // LANES, m, LANES), F32)],
        compiler_params=pltpu.CompilerParams(dimension_semantics=("arbitrary",),
                                             vmem_limit_bytes=VMEM_LIMIT),
        name=f"mixer_nb{nb}",
    )(x, *consts, *cast)


def _ffn_kernel(x_ref, mod_ref, n2_ref, nf_ref, wfi_ref, wfo_ref, o_ref, *, nb, tt, time_major):
    for h in range(N_FFN_SPLITS):
        if time_major:
            tt_h, nb_h = tt // N_FFN_SPLITS, nb
            piece = (slice(h * tt_h * nb, (h + 1) * tt_h * nb), slice(None))
            mod = mod_ref
        else:
            tt_h, nb_h = tt, nb // N_FFN_SPLITS
            piece = (slice(h * nb_h, (h + 1) * nb_h), slice(None), slice(None))
            mod = mod_ref.at[h * nb_h:(h + 1) * nb_h]
        x = x_ref[piece].reshape(tt_h * nb_h, D_MODEL)
        shift2 = mod[:, :, 3 * D_MODEL:4 * D_MODEL]
        scale2 = mod[:, :, 4 * D_MODEL:5 * D_MODEL]
        gate2 = mod[:, :, 5 * D_MODEL:6 * D_MODEL]
        hb = _modulate(_rmsnorm(x, n2_ref[...]), shift2, scale2, tt_h, nb_h, time_major).astype(BF16)
        f_a = _dot(hb, wfi_ref[:, :D_FF])
        f_b = _dot(hb, wfi_ref[:, D_FF:])
        act = (f_a * _two_sigmoid(f_a) * f_b).astype(BF16)
        x2 = x + _rows_gate(_dot(act, wfo_ref[...]), gate2, tt_h, nb_h, time_major)
        y = _rmsnorm(x2, nf_ref[...])
        o_ref[piece] = y if time_major else y.reshape(nb_h, tt_h, D_MODEL)


def _ffn(x, mod, n2, nf, wfi, wfo, *, nb, tt, n_steps, time_major):
    consts = (mod, n2, nf, wfi, wfo)
    return pl.pallas_call(
        functools.partial(_ffn_kernel, nb=nb, tt=tt, time_major=time_major),
        grid=(n_steps,),
        in_specs=[_row_spec(nb, tt, time_major)] + [_const_spec(a.shape) for a in consts],
        out_specs=_row_spec(nb, tt, time_major),
        out_shape=jax.ShapeDtypeStruct(x.shape, F32),
        compiler_params=pltpu.CompilerParams(dimension_semantics=("arbitrary",),
                                             vmem_limit_bytes=VMEM_LIMIT),
        name=f"ffn_nb{nb}",
    )(x, *consts)


def _trunk(x, mod, h0r, h0i, pool0, pos0, p, *, time_major, ffn_w):
    nb, t_len, _ = x.shape
    tt = min(t_len, TILE_ROWS // nb)
    n_sub = 1 if time_major else MIXER_SUBTILES
    tt_ffn = min(t_len, FFN_TILE_ROWS // nb)
    if time_major:
        rows = lambda a: jnp.transpose(a, (1, 0, 2)).reshape(a.shape[1] * nb, a.shape[2])
        unrows = lambda a, t: jnp.transpose(a.reshape(t, nb, a.shape[-1]), (1, 0, 2))
        x_in, pool_in, mod3 = rows(x), rows(pool0), mod[None]
    else:
        assert nb == SUBLANES
        x_in, mod3 = x, mod[:, None, :]
        pool_in = jnp.pad(pool0, ((0, 0), (HIST_PAD - POOL_HIST, 0), (0, 0)))
    cast = ((ffn_w[0], D_FF), (ffn_w[1], 0)) if ffn_w[0].dtype == F32 else ()
    x1, s_r, s_i, new_buf, *cast_out = _mixer(
        x_in, mod3, h0r.reshape(nb, N_STATE), h0i.reshape(nb, N_STATE), pool_in,
        p["n1"], p["win"], p["bk"], p["cr"], p["ci"], p["abr"], p["abi"], p["d"], p["wglu"], p["wpool"],
        p["pscale"], p["wout"], nb=nb, tt=tt, n_sub=n_sub, n_steps=t_len // (tt * n_sub), pos0=pos0,
        time_major=time_major, cast=cast)
    wfi, wfo = cast_out if cast else ffn_w
    y = _ffn(x1, mod3, p["n2"], p["nf"], wfi, wfo, nb=nb, tt=tt_ffn, n_steps=t_len // tt_ffn,
             time_major=time_major)
    if time_major:
        y, new_buf = unrows(y, t_len), unrows(new_buf, POOL_HIST)
    shape_s = (1, nb, N_SSM_GROUPS, SSM_STATE)
    return (y, s_r.reshape(shape_s), s_i.reshape(shape_s), new_buf[None]), (wfi, wfo)


def kernel(x_prompt, x_sample, c_prompt, c_sample, state_ssm_re, state_ssm_im, state_pool, norm1_g, norm2_g, normf_g, w_ada, b_ada, w_in, ssm_lam_re, ssm_lam_im, ssm_log_dt, ssm_b_re, ssm_b_im, ssm_c_re, ssm_c_im, ssm_d, w_glu, w_pool, pool_scale, w_out, w_ffn_in, w_ffn_out):
    n_prompt = x_prompt.shape[0]
    mod_p, mod_s = _ada(c_prompt, c_sample, w_ada[0], b_ada[0])
    abar_r, abar_i, bk, cr, ci = _ssm_prep(ssm_lam_re[0], ssm_lam_im[0], ssm_log_dt[0], ssm_b_re[0], ssm_b_im[0],
                                           ssm_c_re[0], ssm_c_im[0])
    win_scale = jnp.where(jnp.arange(w_in.shape[-1]) < D_SSM + D_POOL, 1.0, 0.5).astype(F32)
    p = dict(
        n1=norm1_g[0].reshape(1, -1), n2=norm2_g[0].reshape(1, -1), nf=normf_g.reshape(1, -1),
        win=(w_in[0] * win_scale).astype(BF16), bk=bk, cr=cr, ci=ci,
        abr=abar_r, abi=abar_i, d=ssm_d[0].reshape(1, -1),
        wglu=(w_glu[0] * 0.5).astype(BF16), wpool=w_pool[0].astype(BF16), pscale=pool_scale[0].reshape(1, -1),
        wout=(w_out[0] * 0.5).astype(BF16))
    zero_state = jnp.zeros((n_prompt, N_SSM_GROUPS, SSM_STATE), F32)
    zero_pool = jnp.zeros((n_prompt, POOL_HIST, D_POOL), F32)
    ffn_f32 = (w_ffn_in[0], w_ffn_out[0])
    (y_p, p_re, p_im, p_pool), ffn_bf16 = _trunk(x_prompt, mod_p, zero_state, zero_state, zero_pool, 0, p,
                                                  time_major=False, ffn_w=ffn_f32)
    (y_s, s_re, s_im, s_pool), _ = _trunk(x_sample, mod_s, state_ssm_re[0], state_ssm_im[0], state_pool[0],
                                          PAST_LEN, p, time_major=True, ffn_w=ffn_bf16)
    return (y_p, y_s, p_re, p_im, p_pool, s_re, s_im, s_pool)
```

```python
import functools

import jax
import jax.numpy as jnp
from jax import lax
from jax.experimental import pallas as pl
from jax.experimental.pallas import tpu as pltpu

D_MODEL = 1024
D_SSM = 512
SSM_GROUP = 16
N_SSM_GROUPS = 32
SSM_STATE = 64
N_STATE = N_SSM_GROUPS * SSM_STATE
D_POOL = 512
POOL_WINDOWS = (2, 4, 8, 16)
POOL_GROUP = 128
POOL_OUT = 256
POOL_HIST = 15
D_FF = 2816
N_MOD = 6
EPS = 1e-6
PAST_LEN = 16384

SUBLANES = 8
BF16_SUBLANES = 16
LANES = 128
MXU_DIM = 256
TILE_ROWS = 512
MIXER_SUBTILES = 2
FFN_TILE_ROWS = 1024
SCAN_LANES = 512
HIST_PAD = 16
N_TAIL_SPLITS = 2
N_FFN_SPLITS = 4
ADA_COL_BLOCKS = 8
GROUPS_PER_KTILE = MXU_DIM // SSM_GROUP
N_B_TILES = D_SSM // MXU_DIM
B_TILE_STATES = GROUPS_PER_KTILE * SSM_STATE
C_PIECE_GROUPS = 8
C_PIECE_STATES = C_PIECE_GROUPS * SSM_STATE
C_PIECE_OUT = C_PIECE_GROUPS * SSM_GROUP
N_C_PIECES = N_SSM_GROUPS // C_PIECE_GROUPS
VMEM_LIMIT = 60 * 1024 * 1024

BF16 = jnp.bfloat16
F32 = jnp.float32


def _dot(a, b):
    return jnp.dot(a, b, preferred_element_type=F32)


def _sigmoid(x):
    return 0.5 * jnp.tanh(0.5 * x) + 0.5


def _two_sigmoid(half_x):
    return jnp.tanh(half_x) + 1.0


def _rmsnorm(x, g):
    return x * lax.rsqrt(jnp.mean(x * x, axis=-1, keepdims=True) + EPS) * g


def _tile3(v, tt, nb, time_major):
    lead, inner = (tt, nb) if time_major else (nb, tt)
    return v.reshape(lead, inner, v.shape[-1])


def _modulate(h, shift, scale, tt, nb, time_major):
    return (_tile3(h, tt, nb, time_major) * (1.0 + scale) + shift).reshape(h.shape)


def _rows_gate(v, gate, tt, nb, time_major):
    return (_tile3(v, tt, nb, time_major) * gate).reshape(v.shape)


def _ada_kernel(cp_ref, cs_ref, w_ref, b_ref, op_ref, os_ref):
    w = w_ref[...].astype(BF16)
    for c_ref, o_ref in ((cp_ref, op_ref), (cs_ref, os_ref)):
        c = c_ref[...]
        o_ref[...] = _dot((c * _sigmoid(c)).astype(BF16), w) + b_ref[...]


def _ada(c_prompt, c_sample, w_ada, b_ada):
    n_p, n_s = c_prompt.shape[0], c_sample.shape[0]
    cw = N_MOD * D_MODEL // ADA_COL_BLOCKS
    return pl.pallas_call(
        _ada_kernel,
        grid=(ADA_COL_BLOCKS,),
        in_specs=[pl.BlockSpec((n_p, D_MODEL), lambda j: (0, 0)),
                  pl.BlockSpec((n_s, D_MODEL), lambda j: (0, 0)),
                  pl.BlockSpec((D_MODEL, cw), lambda j: (0, j)),
                  pl.BlockSpec((1, cw), lambda j: (0, j))],
        out_specs=(pl.BlockSpec((n_p, cw), lambda j: (0, j)), pl.BlockSpec((n_s, cw), lambda j: (0, j))),
        out_shape=(jax.ShapeDtypeStruct((n_p, N_MOD * D_MODEL), F32),
                   jax.ShapeDtypeStruct((n_s, N_MOD * D_MODEL), F32)),
        compiler_params=pltpu.CompilerParams(dimension_semantics=("arbitrary",),
                                             vmem_limit_bytes=VMEM_LIMIT),
        name="ada_mod",
    )(c_prompt, c_sample, w_ada, b_ada.reshape(1, -1))


def _ssm_prep_kernel(lr_ref, li_ref, ldt_ref, br_ref, bi_ref, cr_ref, ci_ref,
                     ar_ref, ai_ref, bk_ref, crm_ref, cim_ref):
    lr = lr_ref[...]
    li = li_ref[...]
    dt = jnp.exp(ldt_ref[...])
    mag = jnp.exp(lr * dt)
    abar_r = mag * jnp.cos(li * dt)
    abar_i = mag * jnp.sin(li * dt)
    den = lr * lr + li * li
    nr = abar_r - 1.0
    ni = abar_i
    f_r = (nr * lr + ni * li) / den
    f_i = (ni * lr - nr * li) / den
    bk_ref[...] = jnp.zeros(bk_ref.shape, BF16)
    crm_ref[...] = jnp.zeros(crm_ref.shape, BF16)
    cim_ref[...] = jnp.zeros(cim_ref.shape, BF16)
    for g in range(N_SSM_GROUPS):
        states = slice(g * SSM_STATE, (g + 1) * SSM_STATE)
        ar_ref[:, states] = abar_r[g:g + 1, :]
        ai_ref[:, states] = abar_i[g:g + 1, :]
    f_rt = jnp.transpose(f_r)
    f_it = jnp.transpose(f_i)
    for j in range(N_C_PIECES):
        gs = range(j * C_PIECE_GROUPS, (j + 1) * C_PIECE_GROUPS)
        bb_r = jnp.concatenate([f_rt[:, g:g + 1] * br_ref[g] - f_it[:, g:g + 1] * bi_ref[g] for g in gs], axis=1)
        bb_i = jnp.concatenate([f_rt[:, g:g + 1] * bi_ref[g] + f_it[:, g:g + 1] * br_ref[g] for g in gs], axis=1)
        bb_rt = jnp.transpose(bb_r).astype(BF16)
        bb_it = jnp.transpose(bb_i).astype(BF16)
        c_piece = slice(j * C_PIECE_GROUPS, (j + 1) * C_PIECE_GROUPS)
        c_rt = jnp.transpose(cr_ref[c_piece].reshape(C_PIECE_OUT, SSM_STATE)).astype(BF16)
        c_it = (-jnp.transpose(ci_ref[c_piece].reshape(C_PIECE_OUT, SSM_STATE))).astype(BF16)
        for gj, g in enumerate(gs):
            k, gl = divmod(g, GROUPS_PER_KTILE)
            src = slice(gj * SSM_GROUP, (gj + 1) * SSM_GROUP)
            rows = slice(gl * SSM_GROUP, (gl + 1) * SSM_GROUP)
            bk_ref[k, rows, gl * SSM_STATE:(gl + 1) * SSM_STATE] = bb_rt[src, :]
            bk_ref[k, rows, B_TILE_STATES + gl * SSM_STATE:B_TILE_STATES + (gl + 1) * SSM_STATE] = bb_it[src, :]
            crm_ref[j, gj * SSM_STATE:(gj + 1) * SSM_STATE, src] = c_rt[:, src]
            cim_ref[j, gj * SSM_STATE:(gj + 1) * SSM_STATE, src] = c_it[:, src]


def _ssm_prep(lam_re, lam_im, log_dt, b_re, b_im, c_re, c_im):
    return pl.pallas_call(
        _ssm_prep_kernel,
        out_shape=(jax.ShapeDtypeStruct((1, N_STATE), F32), jax.ShapeDtypeStruct((1, N_STATE), F32),
                   jax.ShapeDtypeStruct((N_B_TILES, MXU_DIM, 2 * B_TILE_STATES), BF16),
                   jax.ShapeDtypeStruct((N_C_PIECES, C_PIECE_STATES, C_PIECE_OUT), BF16),
                   jax.ShapeDtypeStruct((N_C_PIECES, C_PIECE_STATES, C_PIECE_OUT), BF16)),
        name="ssm_prep",
    )(lam_re, lam_im, log_dt.reshape(N_SSM_GROUPS, 1), b_re, b_im, c_re, c_im)


def _scan_tile(bur_ref, bui_ref, st_r, st_i, abr_ref, abi_ref, nb, tt):
    for lc in range(N_STATE // SCAN_LANES):
        lanes = slice(lc * SCAN_LANES, (lc + 1) * SCAN_LANES)
        ar = jnp.broadcast_to(abr_ref[:, lanes], (SUBLANES, SCAN_LANES))
        ai = jnp.broadcast_to(abi_ref[:, lanes], (SUBLANES, SCAN_LANES))

        def seq_block(rb, carry, lanes=lanes, ar=ar, ai=ai):
            r0 = pl.multiple_of(rb * SUBLANES, SUBLANES)
            sr = st_r[pl.ds(r0, SUBLANES), lanes]
            si = st_i[pl.ds(r0, SUBLANES), lanes]

            def step(t, s):
                sr, si = s
                row = pl.multiple_of(t * nb + r0, SUBLANES)
                nr = ar * sr - ai * si + bur_ref[pl.ds(row, SUBLANES), lanes]
                ni = ar * si + ai * sr + bui_ref[pl.ds(row, SUBLANES), lanes]
                bur_ref[pl.ds(row, SUBLANES), lanes] = nr
                bui_ref[pl.ds(row, SUBLANES), lanes] = ni
                return nr, ni

            sr, si = lax.fori_loop(0, tt, step, (sr, si), unroll=True)
            st_r[pl.ds(r0, SUBLANES), lanes] = sr
            st_i[pl.ds(r0, SUBLANES), lanes] = si
            return carry

        if nb == SUBLANES:
            seq_block(0, 0)
        else:
            lax.fori_loop(0, nb // SUBLANES, seq_block, 0)


def _to_time_major(v, perm_ref, nb, tt):
    for c in range(v.shape[-1] // LANES):
        for b in range(nb):
            perm_ref.at[c][pl.ds(b, tt, stride=nb), :] = v[b * tt:(b + 1) * tt, c * LANES:(c + 1) * LANES]
    return jnp.concatenate([perm_ref[c] for c in range(v.shape[-1] // LANES)], axis=-1)


def _to_seq_major(v, perm_ref, nb, tt):
    n_slabs = v.shape[-1] // LANES
    for c in range(n_slabs):
        perm_ref[c] = v[:, c * LANES:(c + 1) * LANES]
    return jnp.concatenate(
        [jnp.concatenate([perm_ref.at[c][pl.ds(b, tt, stride=nb), :] for b in range(nb)], axis=0)
         for c in range(n_slabs)], axis=-1)


def _pool_time_major(u_pool, hist_ref, i, nb, tt, pos0):
    m = tt * nb
    hist_rows = POOL_HIST * nb
    hist_ref[hist_rows:hist_rows + m, :] = u_pool
    t_abs = pos0 + i * tt + lax.broadcasted_iota(jnp.int32, (tt, nb, POOL_GROUP), 0).reshape(m, POOL_GROUP)
    pooled = []
    for k, w in enumerate(POOL_WINDOWS):
        cols = slice(k * POOL_GROUP, (k + 1) * POOL_GROUP)
        acc = u_pool[:, cols]
        for j in range(1, w):
            acc = acc + hist_ref[hist_rows - j * nb:hist_rows - j * nb + m, cols]
        count = jnp.minimum(t_abs + 1, w).astype(F32)
        pooled.append(acc / count - u_pool[:, cols])
    return pooled


def _pool_seq_major(u_pool, hist_ref, i, nb, tt, pos0):
    m = tt * nb
    t_abs = pos0 + i * tt + lax.broadcasted_iota(jnp.int32, (nb, tt, POOL_GROUP), 1)
    pooled = []
    for k, w in enumerate(POOL_WINDOWS):
        cur = u_pool[:, k * POOL_GROUP:(k + 1) * POOL_GROUP].reshape(nb, tt, POOL_GROUP)
        hist_ref[k, :, HIST_PAD:HIST_PAD + tt, :] = cur
        acc = cur
        for j in range(1, w):
            acc = acc + hist_ref[k, :, HIST_PAD - j:HIST_PAD - j + tt, :]
        count = jnp.minimum(t_abs + 1, w).astype(F32)
        pooled.append((acc / count - cur).reshape(m, POOL_GROUP))
    return pooled


def _mixer_kernel(x_ref, mod_ref, h0r_ref, h0i_ref, pool0_ref, n1_ref, win_ref, bk_ref, cr_ref, ci_ref,
                  abr_ref, abi_ref, d_ref, wglu_ref, wpool_ref, pscale_ref, wout_ref, *rest,
                  nb, tt, n_sub, n_steps, pos0, time_major, cast_halved):
    n_cast = len(cast_halved)
    cast_src, rest = rest[:n_cast], rest[n_cast:]
    xo_ref, sr_out_ref, si_out_ref, pool_out_ref = rest[:4]
    cast_dst, rest = rest[4:4 + n_cast], rest[4 + n_cast:]
    bur_ref, bui_ref, st_r, st_i, hist_ref, perm_ref = rest
    i = pl.program_id(0)
    for src, dst, n_half in zip(cast_src, cast_dst, cast_halved):
        if n_half:
            dst[:, :n_half] = (src[:, :n_half] * 0.5).astype(BF16)
        dst[:, n_half:] = src[:, n_half:].astype(BF16)
    m = tt * nb
    hist_rows = POOL_HIST * nb

    @pl.when(i == 0)
    def _():
        st_r[...] = h0r_ref[...]
        st_i[...] = h0i_ref[...]
        if time_major:
            hist_ref[0:hist_rows, :] = pool0_ref[...]
        else:
            for k in range(len(POOL_WINDOWS)):
                hist_ref[k, :, 0:HIST_PAD, :] = pool0_ref[:, :, k * POOL_GROUP:(k + 1) * POOL_GROUP]

    for sub in range(n_sub):
        tile = i * n_sub + sub
        t_rows = slice(sub * tt, (sub + 1) * tt)
        x = (x_ref[...] if time_major else x_ref[:, t_rows, :]).reshape(m, D_MODEL)
        shift1 = mod_ref[:, :, 0 * D_MODEL:1 * D_MODEL]
        scale1 = mod_ref[:, :, 1 * D_MODEL:2 * D_MODEL]
        gate1 = mod_ref[:, :, 2 * D_MODEL:3 * D_MODEL]
        hb = _modulate(_rmsnorm(x, n1_ref[...]), shift1, scale1, tt, nb, time_major).astype(BF16)

        u_ssm = _dot(hb, win_ref[:, 0:D_SSM])
        u_pool = _dot(hb, win_ref[:, D_SSM:D_SSM + D_POOL])
        u_tm = u_ssm if time_major else _to_time_major(u_ssm, perm_ref, nb, tt)
        for k in range(N_B_TILES):
            res = _dot(u_tm[:, k * MXU_DIM:(k + 1) * MXU_DIM].astype(BF16), bk_ref[k])
            bur_ref[:, k * B_TILE_STATES:(k + 1) * B_TILE_STATES] = res[:, :B_TILE_STATES]
            bui_ref[:, k * B_TILE_STATES:(k + 1) * B_TILE_STATES] = res[:, B_TILE_STATES:]
        g_ssm = _dot(hb, win_ref[:, D_SSM + D_POOL:D_SSM + D_POOL + D_MODEL])
        g_pool = _dot(hb, win_ref[:, D_SSM + D_POOL + D_MODEL:])

        _scan_tile(bur_ref, bui_ref, st_r, st_i, abr_ref, abi_ref, nb, tt)
        if time_major:
            pooled = _pool_time_major(u_pool, hist_ref, tile, nb, tt, pos0)
            new_buf_rows = (slice(m, m + hist_rows), slice(None))
            if n_steps * n_sub > 1:
                hist_ref[0:hist_rows, :] = hist_ref[new_buf_rows]
        else:
            pooled = _pool_seq_major(u_pool, hist_ref, tile, nb, tt, pos0)
            if n_steps * n_sub > 1:
                hist_ref[:, :, 0:HIST_PAD, :] = hist_ref[:, :, tt:tt + HIST_PAD, :]
        z_pieces = [_dot(pooled[k].astype(BF16), wpool_ref[k]) for k in range(len(POOL_WINDOWS))]
        br_pool = jnp.concatenate(z_pieces, axis=-1) * pscale_ref[...]

        y_pieces = []
        for j in range(N_C_PIECES):
            sl = slice(j * C_PIECE_STATES, (j + 1) * C_PIECE_STATES)
            y_pieces.append(_dot(bur_ref[:, sl].astype(BF16), cr_ref[j])
                            + _dot(bui_ref[:, sl].astype(BF16), ci_ref[j]))
        y_cs = jnp.concatenate(y_pieces, axis=-1)
        if not time_major:
            y_cs = _to_seq_major(y_cs, perm_ref, nb, tt)

        for h in range(N_TAIL_SPLITS):
            rows = slice(h * m // N_TAIL_SPLITS, (h + 1) * m // N_TAIL_SPLITS)
            y_ssm = y_cs[rows] + d_ref[...] * u_ssm[rows]
            glu = _dot(jax.nn.gelu(y_ssm).astype(BF16), wglu_ref[...])
            br_ssm = glu[:, :D_MODEL] * _two_sigmoid(glu[:, D_MODEL:])
            merged2 = _two_sigmoid(g_ssm[rows]) * br_ssm + _two_sigmoid(g_pool[rows]) * br_pool[rows]
            o = _dot(merged2.astype(BF16), wout_ref[...])
            if time_major:
                xo_ref[rows, :] = x[rows] + _rows_gate(o, gate1, tt // N_TAIL_SPLITS, nb, True)
            else:
                nb_h = nb // N_TAIL_SPLITS
                seqs = slice(h * nb_h, (h + 1) * nb_h)
                x1 = x[rows] + _rows_gate(o, gate1[seqs], tt, nb_h, False)
                xo_ref[seqs, t_rows, :] = x1.reshape(nb_h, tt, D_MODEL)

    @pl.when(i == n_steps - 1)
    def _():
        sr_out_ref[...] = st_r[...]
        si_out_ref[...] = st_i[...]
        if time_major:
            pool_out_ref[...] = hist_ref[new_buf_rows]
        else:
            pool_out_ref[...] = jnp.concatenate(
                [hist_ref[k, :, tt + HIST_PAD - POOL_HIST:tt + HIST_PAD, :] for k in range(len(POOL_WINDOWS))],
                axis=-1)


def _const_spec(shape):
    nd = len(shape)
    return pl.BlockSpec(shape, lambda i, _nd=nd: (0,) * _nd, pipeline_mode=pl.Buffered(1))


def _row_spec(nb, tt, time_major):
    if time_major:
        return pl.BlockSpec((tt * nb, D_MODEL), lambda i: (i, 0))
    return pl.BlockSpec((nb, tt, D_MODEL), lambda i: (0, i, 0))


def _mixer(x, mod, h0r, h0i, pool0, n1, win, bk, cr, ci, abr, abi, d, wglu, wpool, pscale, wout,
           *, nb, tt, n_sub, n_steps, pos0, time_major, cast=()):
    m = tt * nb
    consts = (mod, h0r, h0i, pool0, n1, win, bk, cr, ci, abr, abi, d, wglu, wpool, pscale, wout)
    cast, cast_halved = tuple(a for a, _ in cast), tuple(n for _, n in cast)
    cast_specs = []
    for a in cast:
        n_blk = max(k for k in range(1, n_steps + 1)
                    if n_steps % k == 0 and a.shape[0] % (k * BF16_SUBLANES) == 0)
        cast_specs.append(pl.BlockSpec((a.shape[0] // n_blk, a.shape[1]),
                                       lambda i, _hold=n_steps // n_blk: (i // _hold, 0)))
    if time_major:
        hist_shape = (POOL_HIST * nb + m, D_POOL)
        pool_out_shape = (POOL_HIST * nb, D_POOL)
    else:
        hist_shape = (len(POOL_WINDOWS), nb, HIST_PAD + tt, POOL_GROUP)
        pool_out_shape = (nb, POOL_HIST, D_POOL)
    kern = functools.partial(_mixer_kernel, nb=nb, tt=tt, n_sub=n_sub, n_steps=n_steps, pos0=pos0,
                             time_major=time_major, cast_halved=cast_halved)
    return pl.pallas_call(
        kern,
        grid=(n_steps,),
        in_specs=[_row_spec(nb, n_sub * tt, time_major)] + [_const_spec(a.shape) for a in consts] + cast_specs,
        out_specs=(_row_spec(nb, n_sub * tt, time_major),
                   pl.BlockSpec((nb, N_STATE), lambda i: (0, 0)),
                   pl.BlockSpec((nb, N_STATE), lambda i: (0, 0)),
                   pl.BlockSpec(pool_out_shape, lambda i: (0,) * len(pool_out_shape))) + tuple(cast_specs),
        out_shape=(jax.ShapeDtypeStruct(x.shape, F32),
                   jax.ShapeDtypeStruct((nb, N_STATE), F32),
                   jax.ShapeDtypeStruct((nb, N_STATE), F32),
                   jax.ShapeDtypeStruct(pool_out_shape, F32)) + tuple(jax.ShapeDtypeStruct(a.shape, BF16) for a in cast),
        scratch_shapes=[pltpu.VMEM((m, N_STATE), F32), pltpu.VMEM((m, N_STATE), F32),
                        pltpu.VMEM((nb, N_STATE), F32), pltpu.VMEM((nb, N_STATE), F32),
                        pltpu.VMEM(hist_shape, F32),
                        pltpu.VMEM((D_SSM // LANES, m, LANES), F32)],
        compiler_params=pltpu.CompilerParams(dimension_semantics=("arbitrary",),
                                             vmem_limit_bytes=VMEM_LIMIT),
        name=f"mixer_nb{nb}",
    )(x, *consts, *cast)


def _ffn_kernel(x_ref, mod_ref, n2_ref, nf_ref, wfi_ref, wfo_ref, o_ref, *, nb, tt, time_major):
    for h in range(N_FFN_SPLITS):
        if time_major:
            tt_h, nb_h = tt // N_FFN_SPLITS, nb
            piece = (slice(h * tt_h * nb, (h + 1) * tt_h * nb), slice(None))
            mod = mod_ref
        else:
            tt_h, nb_h = tt, nb // N_FFN_SPLITS
            piece = (slice(h * nb_h, (h + 1) * nb_h), slice(None), slice(None))
            mod = mod_ref.at[h * nb_h:(h + 1) * nb_h]
        x = x_ref[piece].reshape(tt_h * nb_h, D_MODEL)
        shift2 = mod[:, :, 3 * D_MODEL:4 * D_MODEL]
        scale2 = mod[:, :, 4 * D_MODEL:5 * D_MODEL]
        gate2 = mod[:, :, 5 * D_MODEL:6 * D_MODEL]
        hb = _modulate(_rmsnorm(x, n2_ref[...]), shift2, scale2, tt_h, nb_h, time_major).astype(BF16)
        f_a = _dot(hb, wfi_ref[:, :D_FF])
        f_b = _dot(hb, wfi_ref[:, D_FF:])
        act = (f_a * _two_sigmoid(f_a) * f_b).astype(BF16)
        x2 = x + _rows_gate(_dot(act, wfo_ref[...]), gate2, tt_h, nb_h, time_major)
        y = _rmsnorm(x2, nf_ref[...])
        o_ref[piece] = y if time_major else y.reshape(nb_h, tt_h, D_MODEL)


def _ffn(x, mod, n2, nf, wfi, wfo, *, nb, tt, n_steps, time_major):
    consts = (mod, n2, nf, wfi, wfo)
    return pl.pallas_call(
        functools.partial(_ffn_kernel, nb=nb, tt=tt, time_major=time_major),
        grid=(n_steps,),
        in_specs=[_row_spec(nb, tt, time_major)] + [_const_spec(a.shape) for a in consts],
        out_specs=_row_spec(nb, tt, time_major),
        out_shape=jax.ShapeDtypeStruct(x.shape, F32),
        compiler_params=pltpu.CompilerParams(dimension_semantics=("arbitrary",),
                                             vmem_limit_bytes=VMEM_LIMIT),
        name=f"ffn_nb{nb}",
    )(x, *consts)


def _trunk(x, mod, h0r, h0i, pool0, pos0, p, *, time_major, ffn_w):
    nb, t_len, _ = x.shape
    tt = min(t_len, TILE_ROWS // nb)
    n_sub = 1 if time_major else MIXER_SUBTILES
    tt_ffn = min(t_len, FFN_TILE_ROWS // nb)
    if time_major:
        rows = lambda a: jnp.transpose(a, (1, 0, 2)).reshape(a.shape[1] * nb, a.shape[2])
        unrows = lambda a, t: jnp.transpose(a.reshape(t, nb, a.shape[-1]), (1, 0, 2))
        x_in, pool_in, mod3 = rows(x), rows(pool0), mod[None]
    else:
        assert nb == SUBLANES
        x_in, mod3 = x, mod[:, None, :]
        pool_in = jnp.pad(pool0, ((0, 0), (HIST_PAD - POOL_HIST, 0), (0, 0)))
    cast = ((ffn_w[0], D_FF), (ffn_w[1], 0)) if ffn_w[0].dtype == F32 else ()
    x1, s_r, s_i, new_buf, *cast_out = _mixer(
        x_in, mod3, h0r.reshape(nb, N_STATE), h0i.reshape(nb, N_STATE), pool_in,
        p["n1"], p["win"], p["bk"], p["cr"], p["ci"], p["abr"], p["abi"], p["d"], p["wglu"], p["wpool"],
        p["pscale"], p["wout"], nb=nb, tt=tt, n_sub=n_sub, n_steps=t_len // (tt * n_sub), pos0=pos0,
        time_major=time_major, cast=cast)
    wfi, wfo = cast_out if cast else ffn_w
    y = _ffn(x1, mod3, p["n2"], p["nf"], wfi, wfo, nb=nb, tt=tt_ffn, n_steps=t_len // tt_ffn,
             time_major=time_major)
    if time_major:
        y, new_buf = unrows(y, t_len), unrows(new_buf, POOL_HIST)
    shape_s = (1, nb, N_SSM_GROUPS, SSM_STATE)
    return (y, s_r.reshape(shape_s), s_i.reshape(shape_s), new_buf[None]), (wfi, wfo)


def kernel(x_prompt, x_sample, c_prompt, c_sample, state_ssm_re, state_ssm_im, state_pool, norm1_g, norm2_g, normf_g, w_ada, b_ada, w_in, ssm_lam_re, ssm_lam_im, ssm_log_dt, ssm_b_re, ssm_b_im, ssm_c_re, ssm_c_im, ssm_d, w_glu, w_pool, pool_scale, w_out, w_ffn_in, w_ffn_out):
    n_prompt = x_prompt.shape[0]
    mod_p, mod_s = _ada(c_prompt, c_sample, w_ada[0], b_ada[0])
    abar_r, abar_i, bk, cr, ci = _ssm_prep(ssm_lam_re[0], ssm_lam_im[0], ssm_log_dt[0], ssm_b_re[0], ssm_b_im[0],
                                           ssm_c_re[0], ssm_c_im[0])
    win_scale = jnp.where(jnp.arange(w_in.shape[-1]) < D_SSM + D_POOL, 1.0, 0.5).astype(F32)
    p = dict(
        n1=norm1_g[0].reshape(1, -1), n2=norm2_g[0].reshape(1, -1), nf=normf_g.reshape(1, -1),
        win=(w_in[0] * win_scale).astype(BF16), bk=bk, cr=cr, ci=ci,
        abr=abar_r, abi=abar_i, d=ssm_d[0].reshape(1, -1),
        wglu=(w_glu[0] * 0.5).astype(BF16), wpool=w_pool[0].astype(BF16), pscale=pool_scale[0].reshape(1, -1),
        wout=(w_out[0] * 0.5).astype(BF16))
    zero_state = jnp.zeros((n_prompt, N_SSM_GROUPS, SSM_STATE), F32)
    zero_pool = jnp.zeros((n_prompt, POOL_HIST, D_POOL), F32)
    ffn_f32 = (w_ffn_in[0], w_ffn_out[0])
    (y_p, p_re, p_im, p_pool), ffn_bf16 = _trunk(x_prompt, mod_p, zero_state, zero_state, zero_pool, 0, p,
                                                  time_major=False, ffn_w=ffn_f32)
    (y_s, s_re, s_im, s_pool), _ = _trunk(x_sample, mod_s, state_ssm_re[0], state_ssm_im[0], state_pool[0],
                                          PAST_LEN, p, time_major=True, ffn_w=ffn_bf16)
    return (y_p, y_s, p_re, p_im, p_pool, s_re, s_im, s_pool)
```

```python
import functools

import jax
import jax.numpy as jnp
from jax import lax
from jax.experimental import pallas as pl
from jax.experimental.pallas import tpu as pltpu

D_MODEL = 1024
D_SSM = 512
SSM_GROUP = 16
N_SSM_GROUPS = 32
SSM_STATE = 64
N_STATE = N_SSM_GROUPS * SSM_STATE
D_POOL = 512
POOL_WINDOWS = (2, 4, 8, 16)
POOL_GROUP = 128
POOL_OUT = 256
POOL_HIST = 15
D_FF = 2816
N_MOD = 6
EPS = 1e-6
PAST_LEN = 16384

SUBLANES = 8
BF16_SUBLANES = 16
LANES = 128
TILE_ROWS = 512
MIXER_SUBTILES = 2
FFN_TILE_ROWS = 1024
HIST_PAD = 16
N_TAIL_SPLITS = 2
N_FFN_SPLITS = 4
ADA_COL_BLOCKS = 8
BLOCK_STEPS = 2
SET_GROUPS = LANES // SSM_GROUP
N_SETS = N_SSM_GROUPS // SET_GROUPS
SET_IN = SET_GROUPS * SSM_GROUP
SET_STATES = SET_GROUPS * SSM_STATE
VMEM_LIMIT = 60 * 1024 * 1024

BF16 = jnp.bfloat16
F32 = jnp.float32


def _dot(a, b):
    return jnp.dot(a, b, preferred_element_type=F32)


def _sigmoid(x):
    return 0.5 * jnp.tanh(0.5 * x) + 0.5


def _two_sigmoid(half_x):
    return jnp.tanh(half_x) + 1.0


def _rmsnorm(x, g):
    return x * lax.rsqrt(jnp.mean(x * x, axis=-1, keepdims=True) + EPS) * g


def _tile3(v, tt, nb, time_major):
    lead, inner = (tt, nb) if time_major else (nb, tt)
    return v.reshape(lead, inner, v.shape[-1])


def _modulate(h, shift, scale, tt, nb, time_major):
    return (_tile3(h, tt, nb, time_major) * (1.0 + scale) + shift).reshape(h.shape)


def _rows_gate(v, gate, tt, nb, time_major):
    return (_tile3(v, tt, nb, time_major) * gate).reshape(v.shape)


def _ada_kernel(cp_ref, cs_ref, w_ref, b_ref, op_ref, os_ref):
    w = w_ref[...].astype(BF16)
    for c_ref, o_ref in ((cp_ref, op_ref), (cs_ref, os_ref)):
        c = c_ref[...]
        o_ref[...] = _dot((c * _sigmoid(c)).astype(BF16), w) + b_ref[...]


def _ada(c_prompt, c_sample, w_ada, b_ada):
    n_p, n_s = c_prompt.shape[0], c_sample.shape[0]
    cw = N_MOD * D_MODEL // ADA_COL_BLOCKS
    return pl.pallas_call(
        _ada_kernel,
        grid=(ADA_COL_BLOCKS,),
        in_specs=[pl.BlockSpec((n_p, D_MODEL), lambda j: (0, 0)),
                  pl.BlockSpec((n_s, D_MODEL), lambda j: (0, 0)),
                  pl.BlockSpec((D_MODEL, cw), lambda j: (0, j)),
                  pl.BlockSpec((1, cw), lambda j: (0, j))],
        out_specs=(pl.BlockSpec((n_p, cw), lambda j: (0, j)), pl.BlockSpec((n_s, cw), lambda j: (0, j))),
        out_shape=(jax.ShapeDtypeStruct((n_p, N_MOD * D_MODEL), F32),
                   jax.ShapeDtypeStruct((n_s, N_MOD * D_MODEL), F32)),
        compiler_params=pltpu.CompilerParams(dimension_semantics=("arbitrary",),
                                             vmem_limit_bytes=VMEM_LIMIT),
        name="ada_mod",
    )(c_prompt, c_sample, w_ada, b_ada.reshape(1, -1))


def _cmul(ar, ai, br, bi):
    return ar * br - ai * bi, ar * bi + ai * br


def _ssm_prep_kernel(lr_ref, li_ref, ldt_ref, br_ref, bi_ref, cr_ref, ci_ref,
                     a2r_ref, a2i_ref, wsi_ref, wso_ref, wt_ref):
    lr = lr_ref[...]
    li = li_ref[...]
    dt = jnp.exp(ldt_ref[...])
    mag = jnp.exp(lr * dt)
    a_r = mag * jnp.cos(li * dt)
    a_i = mag * jnp.sin(li * dt)
    den = lr * lr + li * li
    nr = a_r - 1.0
    ni = a_i
    f_r = (nr * lr + ni * li) / den
    f_i = (ni * lr - nr * li) / den
    a2_r, a2_i = _cmul(a_r, a_i, a_r, a_i)
    for g in range(N_SSM_GROUPS):
        states = slice(g * SSM_STATE, (g + 1) * SSM_STATE)
        a2r_ref[:, states] = a2_r[g:g + 1, :]
        a2i_ref[:, states] = a2_i[g:g + 1, :]
    cols = {name: jnp.transpose(v) for name, v in
            dict(f_r=f_r, f_i=f_i, a_r=a_r, a_i=a_i, a2_r=a2_r, a2_i=a2_i).items()}
    wsi_ref[...] = jnp.zeros(wsi_ref.shape, BF16)
    wso_ref[...] = jnp.zeros(wso_ref.shape, BF16)
    row_group = lax.broadcasted_iota(jnp.int32, (SET_IN, SET_IN), 0) // SSM_GROUP
    col_group = lax.broadcasted_iota(jnp.int32, (SET_IN, SET_IN), 1) // SSM_GROUP
    same_group = row_group == col_group
    for s in range(N_SETS):
        gs = range(s * SET_GROUPS, (s + 1) * SET_GROUPS)

        def per_group(fn, gs=gs):
            return jnp.concatenate([fn(g) for g in gs], axis=1)

        def expand(name):
            return per_group(lambda g: jnp.broadcast_to(cols[name][:, g:g + 1], (SSM_STATE, SSM_GROUP)))

        b_r = per_group(lambda g: br_ref[g])
        b_i = per_group(lambda g: bi_ref[g])
        bb_last = _cmul(expand("f_r"), expand("f_i"), b_r, b_i)
        bb_first = _cmul(expand("a_r"), expand("a_i"), *bb_last)
        c_set = slice(s * SET_GROUPS, (s + 1) * SET_GROUPS)
        c_rt = jnp.transpose(cr_ref[c_set].reshape(SET_IN, SSM_STATE))
        c_it = jnp.transpose(ci_ref[c_set].reshape(SET_IN, SSM_STATE))
        for j, (bb_r, bb_i) in enumerate((bb_first, bb_last)):
            bb_rt = jnp.transpose(bb_r)
            bb_it = jnp.transpose(bb_i)
            k_t = (jnp.dot(bb_rt, c_rt, precision=lax.Precision.HIGHEST, preferred_element_type=F32)
                   - jnp.dot(bb_it, c_it, precision=lax.Precision.HIGHEST, preferred_element_type=F32))
            k_t = jnp.where(same_group, k_t, 0.0).astype(BF16)
            lag = BLOCK_STEPS - 1 - j
            for t in range(lag, BLOCK_STEPS):
                wt_ref[s, (t - lag) * SET_IN:(t - lag + 1) * SET_IN, t * SET_IN:(t + 1) * SET_IN] = k_t
            for gj in range(SET_GROUPS):
                src = slice(gj * SSM_GROUP, (gj + 1) * SSM_GROUP)
                rows = slice(j * SET_IN + gj * SSM_GROUP, j * SET_IN + (gj + 1) * SSM_GROUP)
                wsi_ref[s, rows, gj * SSM_STATE:(gj + 1) * SSM_STATE] = bb_rt[src, :].astype(BF16)
                wsi_ref[s, rows, SET_STATES + gj * SSM_STATE:SET_STATES + (gj + 1) * SSM_STATE] = (
                    bb_it[src, :].astype(BF16))
        wt_ref[s, SET_IN:2 * SET_IN, 0:SET_IN] = jnp.zeros((SET_IN, SET_IN), BF16)
        for t, (pr, pi) in enumerate((("a_r", "a_i"), ("a2_r", "a2_i"))):
            p_r, p_i = expand(pr), expand(pi)
            from_sr = (c_rt * p_r - c_it * p_i).astype(BF16)
            from_si = (-c_rt * p_i - c_it * p_r).astype(BF16)
            for gj in range(SET_GROUPS):
                src = slice(gj * SSM_GROUP, (gj + 1) * SSM_GROUP)
                out = slice(t * SET_IN + gj * SSM_GROUP, t * SET_IN + (gj + 1) * SSM_GROUP)
                wso_ref[s, gj * SSM_STATE:(gj + 1) * SSM_STATE, out] = from_sr[:, src]
                wso_ref[s, SET_STATES + gj * SSM_STATE:SET_STATES + (gj + 1) * SSM_STATE, out] = from_si[:, src]


def _ssm_prep(lam_re, lam_im, log_dt, b_re, b_im, c_re, c_im):
    assert BLOCK_STEPS == 2
    return pl.pallas_call(
        _ssm_prep_kernel,
        out_shape=(jax.ShapeDtypeStruct((1, N_STATE), F32), jax.ShapeDtypeStruct((1, N_STATE), F32),
                   jax.ShapeDtypeStruct((N_SETS, BLOCK_STEPS * SET_IN, 2 * SET_STATES), BF16),
                   jax.ShapeDtypeStruct((N_SETS, 2 * SET_STATES, BLOCK_STEPS * SET_IN), BF16),
                   jax.ShapeDtypeStruct((N_SETS, BLOCK_STEPS * SET_IN, BLOCK_STEPS * SET_IN), BF16)),
        name="ssm_prep",
    )(lam_re, lam_im, log_dt.reshape(N_SSM_GROUPS, 1), b_re, b_im, c_re, c_im)


def _block_scan(z_ref, st_r, st_i, a2r_ref, a2i_ref, nb, n_blocks):
    for s in range(N_SETS):
        lanes = slice(s * SET_STATES, (s + 1) * SET_STATES)
        re_lanes = slice(2 * s * SET_STATES, (2 * s + 1) * SET_STATES)
        im_lanes = slice((2 * s + 1) * SET_STATES, (2 * s + 2) * SET_STATES)
        ar = jnp.broadcast_to(a2r_ref[:, lanes], (SUBLANES, SET_STATES))
        ai = jnp.broadcast_to(a2i_ref[:, lanes], (SUBLANES, SET_STATES))

        def seq_block(rb, carry, lanes=lanes, re_lanes=re_lanes, im_lanes=im_lanes, ar=ar, ai=ai):
            r0 = pl.multiple_of(rb * SUBLANES, SUBLANES)
            sr = st_r[pl.ds(r0, SUBLANES), lanes]
            si = st_i[pl.ds(r0, SUBLANES), lanes]

            def step(k, state):
                sr, si = state
                rows = pl.ds(pl.multiple_of(k * nb + r0, SUBLANES), SUBLANES)
                zr = z_ref[rows, re_lanes]
                zi = z_ref[rows, im_lanes]
                z_ref[rows, re_lanes] = sr
                z_ref[rows, im_lanes] = si
                return ar * sr - ai * si + zr, ar * si + ai * sr + zi

            sr, si = lax.fori_loop(0, n_blocks, step, (sr, si), unroll=True)
            st_r[pl.ds(r0, SUBLANES), lanes] = sr
            st_i[pl.ds(r0, SUBLANES), lanes] = si
            return carry

        if nb == SUBLANES:
            seq_block(0, 0)
        else:
            lax.fori_loop(0, nb // SUBLANES, seq_block, 0)


def _to_time_major(v, perm_ref, nb, tt):
    for c in range(v.shape[-1] // LANES):
        for b in range(nb):
            perm_ref.at[c][pl.ds(b, tt, stride=nb), :] = v[b * tt:(b + 1) * tt, c * LANES:(c + 1) * LANES]
    return jnp.concatenate([perm_ref[c] for c in range(v.shape[-1] // LANES)], axis=-1)


def _to_seq_major(v, perm_ref, nb, tt):
    n_slabs = v.shape[-1] // LANES
    for c in range(n_slabs):
        perm_ref[c] = v[:, c * LANES:(c + 1) * LANES]
    return jnp.concatenate(
        [jnp.concatenate([perm_ref.at[c][pl.ds(b, tt, stride=nb), :] for b in range(nb)], axis=0)
         for c in range(n_slabs)], axis=-1)


def _pool_time_major(u_pool, hist_ref, i, nb, tt, pos0):
    m = tt * nb
    hist_rows = POOL_HIST * nb
    hist_ref[hist_rows:hist_rows + m, :] = u_pool
    t_abs = pos0 + i * tt + lax.broadcasted_iota(jnp.int32, (tt, nb, POOL_GROUP), 0).reshape(m, POOL_GROUP)
    pooled = []
    for k, w in enumerate(POOL_WINDOWS):
        cols = slice(k * POOL_GROUP, (k + 1) * POOL_GROUP)
        acc = u_pool[:, cols]
        for j in range(1, w):
            acc = acc + hist_ref[hist_rows - j * nb:hist_rows - j * nb + m, cols]
        count = jnp.minimum(t_abs + 1, w).astype(F32)
        pooled.append(acc / count - u_pool[:, cols])
    return pooled


def _pool_seq_major(u_pool, hist_ref, i, nb, tt, pos0):
    m = tt * nb
    t_abs = pos0 + i * tt + lax.broadcasted_iota(jnp.int32, (nb, tt, POOL_GROUP), 1)
    pooled = []
    for k, w in enumerate(POOL_WINDOWS):
        cur = u_pool[:, k * POOL_GROUP:(k + 1) * POOL_GROUP].reshape(nb, tt, POOL_GROUP)
        hist_ref[k, :, HIST_PAD:HIST_PAD + tt, :] = cur
        acc = cur
        for j in range(1, w):
            acc = acc + hist_ref[k, :, HIST_PAD - j:HIST_PAD - j + tt, :]
        count = jnp.minimum(t_abs + 1, w).astype(F32)
        pooled.append((acc / count - cur).reshape(m, POOL_GROUP))
    return pooled


def _mixer_kernel(x_ref, mod_ref, h0r_ref, h0i_ref, pool0_ref, n1_ref, win_ref, wsi_ref, wso_ref, wt_ref,
                  a2r_ref, a2i_ref, d_ref, wglu_ref, wpool_ref, pscale_ref, wout_ref, *rest,
                  nb, tt, n_sub, n_steps, pos0, time_major, cast_halved):
    n_cast = len(cast_halved)
    cast_src, rest = rest[:n_cast], rest[n_cast:]
    xo_ref, sr_out_ref, si_out_ref, pool_out_ref = rest[:4]
    cast_dst, rest = rest[4:4 + n_cast], rest[4 + n_cast:]
    z_ref, st_r, st_i, hist_ref, perm_ref = rest
    n_blocks = tt // BLOCK_STEPS
    mb = n_blocks * nb
    i = pl.program_id(0)
    for src, dst, n_half in zip(cast_src, cast_dst, cast_halved):
        if n_half:
            dst[:, :n_half] = (src[:, :n_half] * 0.5).astype(BF16)
        dst[:, n_half:] = src[:, n_half:].astype(BF16)
    m = tt * nb
    hist_rows = POOL_HIST * nb

    @pl.when(i == 0)
    def _():
        st_r[...] = h0r_ref[...]
        st_i[...] = h0i_ref[...]
        if time_major:
            hist_ref[0:hist_rows, :] = pool0_ref[...]
        else:
            for k in range(len(POOL_WINDOWS)):
                hist_ref[k, :, 0:HIST_PAD, :] = pool0_ref[:, :, k * POOL_GROUP:(k + 1) * POOL_GROUP]

    for sub in range(n_sub):
        tile = i * n_sub + sub
        t_rows = slice(sub * tt, (sub + 1) * tt)
        x = (x_ref[...] if time_major else x_ref[:, t_rows, :]).reshape(m, D_MODEL)
        shift1 = mod_ref[:, :, 0 * D_MODEL:1 * D_MODEL]
        scale1 = mod_ref[:, :, 1 * D_MODEL:2 * D_MODEL]
        gate1 = mod_ref[:, :, 2 * D_MODEL:3 * D_MODEL]
        hb = _modulate(_rmsnorm(x, n1_ref[...]), shift1, scale1, tt, nb, time_major).astype(BF16)

        u_ssm = _dot(hb, win_ref[:, 0:D_SSM])
        u_pool = _dot(hb, win_ref[:, D_SSM:D_SSM + D_POOL])
        u_tm = u_ssm if time_major else _to_time_major(u_ssm, perm_ref, nb, tt)
        u_blocks = u_tm.reshape(n_blocks, BLOCK_STEPS, nb, D_SSM)
        u_steps = [u_blocks[:, j].reshape(mb, D_SSM).astype(BF16) for j in range(BLOCK_STEPS)]
        y_sets = []
        for s in range(N_SETS):
            chans = slice(s * SET_IN, (s + 1) * SET_IN)
            lhs = jnp.concatenate([u[:, chans] for u in u_steps], axis=1)
            z_ref[:, 2 * s * SET_STATES:2 * (s + 1) * SET_STATES] = _dot(lhs, wsi_ref[s])
            y_sets.append(_dot(lhs, wt_ref[s]))
        g_ssm = _dot(hb, win_ref[:, D_SSM + D_POOL:D_SSM + D_POOL + D_MODEL])
        g_pool = _dot(hb, win_ref[:, D_SSM + D_POOL + D_MODEL:])

        _block_scan(z_ref, st_r, st_i, a2r_ref, a2i_ref, nb, n_blocks)
        if time_major:
            pooled = _pool_time_major(u_pool, hist_ref, tile, nb, tt, pos0)
            new_buf_rows = (slice(m, m + hist_rows), slice(None))
            if n_steps * n_sub > 1:
                hist_ref[0:hist_rows, :] = hist_ref[new_buf_rows]
        else:
            pooled = _pool_seq_major(u_pool, hist_ref, tile, nb, tt, pos0)
            if n_steps * n_sub > 1:
                hist_ref[:, :, 0:HIST_PAD, :] = hist_ref[:, :, tt:tt + HIST_PAD, :]
        z_pieces = [_dot(pooled[k].astype(BF16), wpool_ref[k]) for k in range(len(POOL_WINDOWS))]
        br_pool = jnp.concatenate(z_pieces, axis=-1) * pscale_ref[...]

        for s in range(N_SETS):
            s_in = z_ref[:, 2 * s * SET_STATES:2 * (s + 1) * SET_STATES].astype(BF16)
            y_sets[s] = y_sets[s] + _dot(s_in, wso_ref[s])
        y_steps = [jnp.concatenate([y[:, j * SET_IN:(j + 1) * SET_IN] for y in y_sets], axis=1).reshape(n_blocks, nb, D_SSM)
                   for j in range(BLOCK_STEPS)]
        y_cs = jnp.stack(y_steps, axis=1).reshape(m, D_SSM)
        if not time_major:
            y_cs = _to_seq_major(y_cs, perm_ref, nb, tt)

        for h in range(N_TAIL_SPLITS):
            rows = slice(h * m // N_TAIL_SPLITS, (h + 1) * m // N_TAIL_SPLITS)
            y_ssm = y_cs[rows] + d_ref[...] * u_ssm[rows]
            glu = _dot(jax.nn.gelu(y_ssm).astype(BF16), wglu_ref[...])
            br_ssm = glu[:, :D_MODEL] * _two_sigmoid(glu[:, D_MODEL:])
            merged2 = _two_sigmoid(g_ssm[rows]) * br_ssm + _two_sigmoid(g_pool[rows]) * br_pool[rows]
            o = _dot(merged2.astype(BF16), wout_ref[...])
            if time_major:
                xo_ref[rows, :] = x[rows] + _rows_gate(o, gate1, tt // N_TAIL_SPLITS, nb, True)
            else:
                nb_h = nb // N_TAIL_SPLITS
                seqs = slice(h * nb_h, (h + 1) * nb_h)
                x1 = x[rows] + _rows_gate(o, gate1[seqs], tt, nb_h, False)
                xo_ref[seqs, t_rows, :] = x1.reshape(nb_h, tt, D_MODEL)

    @pl.when(i == n_steps - 1)
    def _():
        sr_out_ref[...] = st_r[...]
        si_out_ref[...] = st_i[...]
        if time_major:
            pool_out_ref[...] = hist_ref[new_buf_rows]
        else:
            pool_out_ref[...] = jnp.concatenate(
                [hist_ref[k, :, tt + HIST_PAD - POOL_HIST:tt + HIST_PAD, :] for k in range(len(POOL_WINDOWS))],
                axis=-1)


def _const_spec(shape):
    nd = len(shape)
    return pl.BlockSpec(shape, lambda i, _nd=nd: (0,) * _nd, pipeline_mode=pl.Buffered(1))


def _row_spec(nb, tt, time_major):
    if time_major:
        return pl.BlockSpec((tt * nb, D_MODEL), lambda i: (i, 0))
    return pl.BlockSpec((nb, tt, D_MODEL), lambda i: (0, i, 0))


def _mixer(x, mod, h0r, h0i, pool0, n1, win, wsi, wso, wt, a2r, a2i, d, wglu, wpool, pscale, wout,
           *, nb, tt, n_sub, n_steps, pos0, time_major, cast=()):
    m = tt * nb
    consts = (mod, h0r, h0i, pool0, n1, win, wsi, wso, wt, a2r, a2i, d, wglu, wpool, pscale, wout)
    cast, cast_halved = tuple(a for a, _ in cast), tuple(n for _, n in cast)
    cast_specs = []
    for a in cast:
        n_blk = max(k for k in range(1, n_steps + 1)
                    if n_steps % k == 0 and a.shape[0] % (k * BF16_SUBLANES) == 0)
        cast_specs.append(pl.BlockSpec((a.shape[0] // n_blk, a.shape[1]),
                                       lambda i, _hold=n_steps // n_blk: (i // _hold, 0)))
    if time_major:
        hist_shape = (POOL_HIST * nb + m, D_POOL)
        pool_out_shape = (POOL_HIST * nb, D_POOL)
    else:
        hist_shape = (len(POOL_WINDOWS), nb, HIST_PAD + tt, POOL_GROUP)
        pool_out_shape = (nb, POOL_HIST, D_POOL)
    kern = functools.partial(_mixer_kernel, nb=nb, tt=tt, n_sub=n_sub, n_steps=n_steps, pos0=pos0,
                             time_major=time_major, cast_halved=cast_halved)
    return pl.pallas_call(
        kern,
        grid=(n_steps,),
        in_specs=[_row_spec(nb, n_sub * tt, time_major)] + [_const_spec(a.shape) for a in consts] + cast_specs,
        out_specs=(_row_spec(nb, n_sub * tt, time_major),
                   pl.BlockSpec((nb, N_STATE), lambda i: (0, 0)),
                   pl.BlockSpec((nb, N_STATE), lambda i: (0, 0)),
                   pl.BlockSpec(pool_out_shape, lambda i: (0,) * len(pool_out_shape))) + tuple(cast_specs),
        out_shape=(jax.ShapeDtypeStruct(x.shape, F32),
                   jax.ShapeDtypeStruct((nb, N_STATE), F32),
                   jax.ShapeDtypeStruct((nb, N_STATE), F32),
                   jax.ShapeDtypeStruct(pool_out_shape, F32)) + tuple(jax.ShapeDtypeStruct(a.shape, BF16) for a in cast),
        scratch_shapes=[pltpu.VMEM((m // BLOCK_STEPS, 2 * N_STATE), F32),
                        pltpu.VMEM((nb, N_STATE), F32), pltpu.VMEM((nb, N_STATE), F32),
                        pltpu.VMEM(hist_shape, F32),
                        pltpu.VMEM((D_SSM // LANES, m, LANES), F32)],
        compiler_params=pltpu.CompilerParams(dimension_semantics=("arbitrary",),
                                             vmem_limit_bytes=VMEM_LIMIT),
        name=f"mixer_nb{nb}",
    )(x, *consts, *cast)


def _ffn_kernel(x_ref, mod_ref, n2_ref, nf_ref, wfi_ref, wfo_ref, o_ref, *, nb, tt, time_major):
    for h in range(N_FFN_SPLITS):
        if time_major:
            tt_h, nb_h = tt // N_FFN_SPLITS, nb
            piece = (slice(h * tt_h * nb, (h + 1) * tt_h * nb), slice(None))
            mod = mod_ref
        else:
            tt_h, nb_h = tt, nb // N_FFN_SPLITS
            piece = (slice(h * nb_h, (h + 1) * nb_h), slice(None), slice(None))
            mod = mod_ref.at[h * nb_h:(h + 1) * nb_h]
        x = x_ref[piece].reshape(tt_h * nb_h, D_MODEL)
        shift2 = mod[:, :, 3 * D_MODEL:4 * D_MODEL]
        scale2 = mod[:, :, 4 * D_MODEL:5 * D_MODEL]
        gate2 = mod[:, :, 5 * D_MODEL:6 * D_MODEL]
        hb = _modulate(_rmsnorm(x, n2_ref[...]), shift2, scale2, tt_h, nb_h, time_major).astype(BF16)
        f_a = _dot(hb, wfi_ref[:, :D_FF])
        f_b = _dot(hb, wfi_ref[:, D_FF:])
        act = (f_a * _two_sigmoid(f_a) * f_b).astype(BF16)
        x2 = x + _rows_gate(_dot(act, wfo_ref[...]), gate2, tt_h, nb_h, time_major)
        y = _rmsnorm(x2, nf_ref[...])
        o_ref[piece] = y if time_major else y.reshape(nb_h, tt_h, D_MODEL)


def _ffn(x, mod, n2, nf, wfi, wfo, *, nb, tt, n_steps, time_major):
    consts = (mod, n2, nf, wfi, wfo)
    return pl.pallas_call(
        functools.partial(_ffn_kernel, nb=nb, tt=tt, time_major=time_major),
        grid=(n_steps,),
        in_specs=[_row_spec(nb, tt, time_major)] + [_const_spec(a.shape) for a in consts],
        out_specs=_row_spec(nb, tt, time_major),
        out_shape=jax.ShapeDtypeStruct(x.shape, F32),
        compiler_params=pltpu.CompilerParams(dimension_semantics=("arbitrary",),
                                             vmem_limit_bytes=VMEM_LIMIT),
        name=f"ffn_nb{nb}",
    )(x, *consts)


def _trunk(x, mod, h0r, h0i, pool0, pos0, p, *, time_major, ffn_w):
    nb, t_len, _ = x.shape
    tt = min(t_len, TILE_ROWS // nb)
    n_sub = 1 if time_major else MIXER_SUBTILES
    tt_ffn = min(t_len, FFN_TILE_ROWS // nb)
    if time_major:
        rows = lambda a: jnp.transpose(a, (1, 0, 2)).reshape(a.shape[1] * nb, a.shape[2])
        unrows = lambda a, t: jnp.transpose(a.reshape(t, nb, a.shape[-1]), (1, 0, 2))
        x_in, pool_in, mod3 = rows(x), rows(pool0), mod[None]
    else:
        assert nb == SUBLANES
        x_in, mod3 = x, mod[:, None, :]
        pool_in = jnp.pad(pool0, ((0, 0), (HIST_PAD - POOL_HIST, 0), (0, 0)))
    cast = ((ffn_w[0], D_FF), (ffn_w[1], 0)) if ffn_w[0].dtype == F32 else ()
    x1, s_r, s_i, new_buf, *cast_out = _mixer(
        x_in, mod3, h0r.reshape(nb, N_STATE), h0i.reshape(nb, N_STATE), pool_in,
        p["n1"], p["win"], p["wsi"], p["wso"], p["wt"], p["a2r"], p["a2i"], p["d"], p["wglu"], p["wpool"],
        p["pscale"], p["wout"], nb=nb, tt=tt, n_sub=n_sub, n_steps=t_len // (tt * n_sub), pos0=pos0,
        time_major=time_major, cast=cast)
    wfi, wfo = cast_out if cast else ffn_w
    y = _ffn(x1, mod3, p["n2"], p["nf"], wfi, wfo, nb=nb, tt=tt_ffn, n_steps=t_len // tt_ffn,
             time_major=time_major)
    if time_major:
        y, new_buf = unrows(y, t_len), unrows(new_buf, POOL_HIST)
    shape_s = (1, nb, N_SSM_GROUPS, SSM_STATE)
    return (y, s_r.reshape(shape_s), s_i.reshape(shape_s), new_buf[None]), (wfi, wfo)


def kernel(x_prompt, x_sample, c_prompt, c_sample, state_ssm_re, state_ssm_im, state_pool, norm1_g, norm2_g, normf_g, w_ada, b_ada, w_in, ssm_lam_re, ssm_lam_im, ssm_log_dt, ssm_b_re, ssm_b_im, ssm_c_re, ssm_c_im, ssm_d, w_glu, w_pool, pool_scale, w_out, w_ffn_in, w_ffn_out):
    n_prompt = x_prompt.shape[0]
    mod_p, mod_s = _ada(c_prompt, c_sample, w_ada[0], b_ada[0])
    a2r, a2i, wsi, wso, wt = _ssm_prep(ssm_lam_re[0], ssm_lam_im[0], ssm_log_dt[0], ssm_b_re[0], ssm_b_im[0],
                                           ssm_c_re[0], ssm_c_im[0])
    win_scale = jnp.where(jnp.arange(w_in.shape[-1]) < D_SSM + D_POOL, 1.0, 0.5).astype(F32)
    p = dict(
        n1=norm1_g[0].reshape(1, -1), n2=norm2_g[0].reshape(1, -1), nf=normf_g.reshape(1, -1),
        win=(w_in[0] * win_scale).astype(BF16), wsi=wsi, wso=wso, wt=wt, a2r=a2r, a2i=a2i,
        d=ssm_d[0].reshape(1, -1),
        wglu=(w_glu[0] * 0.5).astype(BF16), wpool=w_pool[0].astype(BF16), pscale=pool_scale[0].reshape(1, -1),
        wout=(w_out[0] * 0.5).astype(BF16))
    zero_state = jnp.zeros((n_prompt, N_SSM_GROUPS, SSM_STATE), F32)
    zero_pool = jnp.zeros((n_prompt, POOL_HIST, D_POOL), F32)
    ffn_f32 = (w_ffn_in[0], w_ffn_out[0])
    (y_p, p_re, p_im, p_pool), ffn_bf16 = _trunk(x_prompt, mod_p, zero_state, zero_state, zero_pool, 0, p,
                                                  time_major=False, ffn_w=ffn_f32)
    (y_s, s_re, s_im, s_pool), _ = _trunk(x_sample, mod_s, state_ssm_re[0], state_ssm_im[0], state_pool[0],
                                          PAST_LEN, p, time_major=True, ffn_w=ffn_bf16)
    return (y_p, y_s, p_re, p_im, p_pool, s_re, s_im, s_pool)
```

```python
import functools

import jax
import jax.numpy as jnp
from jax import lax
from jax.experimental import pallas as pl
from jax.experimental.pallas import tpu as pltpu

D_MODEL = 1024
D_SSM = 512
SSM_GROUP = 16
N_SSM_GROUPS = 32
SSM_STATE = 64
N_STATE = N_SSM_GROUPS * SSM_STATE
D_POOL = 512
POOL_WINDOWS = (2, 4, 8, 16)
POOL_GROUP = 128
POOL_OUT = 256
POOL_HIST = 15
D_FF = 2816
N_MOD = 6
EPS = 1e-6
PAST_LEN = 16384

SUBLANES = 8
BF16_SUBLANES = 16
LANES = 128
TILE_ROWS = 512
MIXER_SUBTILES = 2
FFN_TILE_ROWS = 1024
HIST_PAD = 16
N_TAIL_SPLITS = 2
N_FFN_SPLITS = 4
ADA_ROW_BLOCKS = 4
BLOCK_STEPS = 2
SET_GROUPS = LANES // SSM_GROUP
N_SETS = N_SSM_GROUPS // SET_GROUPS
SET_IN = SET_GROUPS * SSM_GROUP
SET_STATES = SET_GROUPS * SSM_STATE
VMEM_LIMIT = 60 * 1024 * 1024

BF16 = jnp.bfloat16
F32 = jnp.float32


def _dot(a, b):
    return jnp.dot(a, b, preferred_element_type=F32)


def _sigmoid(x):
    return 0.5 * jnp.tanh(0.5 * x) + 0.5


def _two_sigmoid(half_x):
    return jnp.tanh(half_x) + 1.0


def _rmsnorm(x, g):
    return x * lax.rsqrt(jnp.mean(x * x, axis=-1, keepdims=True) + EPS) * g


def _tile3(v, tt, nb, time_major):
    lead, inner = (tt, nb) if time_major else (nb, tt)
    return v.reshape(lead, inner, v.shape[-1])


def _modulate(h, shift, scale, tt, nb, time_major):
    return (_tile3(h, tt, nb, time_major) * (1.0 + scale) + shift).reshape(h.shape)


def _rows_gate(v, gate, tt, nb, time_major):
    return (_tile3(v, tt, nb, time_major) * gate).reshape(v.shape)


def _cast_specs(arrays, n_steps):
    specs = []
    for a in arrays:
        n_blk = max(k for k in range(1, n_steps + 1)
                    if n_steps % k == 0 and a.shape[0] % (k * BF16_SUBLANES) == 0)
        specs.append(pl.BlockSpec((a.shape[0] // n_blk, a.shape[1]),
                                  lambda i, _hold=n_steps // n_blk: (i // _hold, 0)))
    return specs


def _cast_blocks(srcs, dsts, halved):
    for src, dst, (lo, hi) in zip(srcs, dsts, halved):
        if lo > 0:
            dst[:, :lo] = src[:, :lo].astype(BF16)
        if hi > lo:
            dst[:, lo:hi] = (src[:, lo:hi] * 0.5).astype(BF16)
        if hi < src.shape[-1]:
            dst[:, hi:] = src[:, hi:].astype(BF16)


def _ada_kernel(cp_ref, cs_ref, w_ref, b_ref, *rest, halved):
    n_cast = len(halved)
    cast_src, (op_ref, os_ref), cast_dst = rest[:n_cast], rest[n_cast:n_cast + 2], rest[n_cast + 2:]
    _cast_blocks(cast_src, cast_dst, halved)
    k = pl.program_id(0)
    w = w_ref[...].astype(BF16)
    for c_ref, o_ref in ((cp_ref, op_ref), (cs_ref, os_ref)):
        c = c_ref[...]
        part = _dot((c * _sigmoid(c)).astype(BF16), w)

        @pl.when(k == 0)
        def _(o_ref=o_ref, part=part):
            o_ref[...] = b_ref[...] + part

        @pl.when(k > 0)
        def _(o_ref=o_ref, part=part):
            o_ref[...] += part


def _ada(c_prompt, c_sample, w_ada, b_ada, cast):
    n_p, n_s = c_prompt.shape[0], c_sample.shape[0]
    kb = D_MODEL // ADA_ROW_BLOCKS
    arrays, halved = tuple(a for a, _ in cast), tuple(h for _, h in cast)
    cast_specs = _cast_specs(arrays, ADA_ROW_BLOCKS)
    n_out = N_MOD * D_MODEL
    return pl.pallas_call(
        functools.partial(_ada_kernel, halved=halved),
        grid=(ADA_ROW_BLOCKS,),
        in_specs=[pl.BlockSpec((n_p, kb), lambda k: (0, k)),
                  pl.BlockSpec((n_s, kb), lambda k: (0, k)),
                  pl.BlockSpec((kb, n_out), lambda k: (k, 0)),
                  pl.BlockSpec((1, n_out), lambda k: (0, 0))] + cast_specs,
        out_specs=[pl.BlockSpec((n_p, n_out), lambda k: (0, 0)), pl.BlockSpec((n_s, n_out), lambda k: (0, 0))]
        + cast_specs,
        out_shape=[jax.ShapeDtypeStruct((n_p, n_out), F32), jax.ShapeDtypeStruct((n_s, n_out), F32)]
        + [jax.ShapeDtypeStruct(a.shape, BF16) for a in arrays],
        compiler_params=pltpu.CompilerParams(dimension_semantics=("arbitrary",),
                                             vmem_limit_bytes=VMEM_LIMIT),
        name="ada_mod",
    )(c_prompt, c_sample, w_ada, b_ada.reshape(1, -1), *arrays)


def _cmul(ar, ai, br, bi):
    return ar * br - ai * bi, ar * bi + ai * br


def _ssm_prep_kernel(lr_ref, li_ref, ldt_ref, br_ref, bi_ref, cr_ref, ci_ref,
                     a2r_ref, a2i_ref, wsi_ref, wso_ref, wt_ref):
    lr = lr_ref[...]
    li = li_ref[...]
    dt = jnp.exp(ldt_ref[...])
    mag = jnp.exp(lr * dt)
    a_r = mag * jnp.cos(li * dt)
    a_i = mag * jnp.sin(li * dt)
    den = lr * lr + li * li
    nr = a_r - 1.0
    ni = a_i
    f_r = (nr * lr + ni * li) / den
    f_i = (ni * lr - nr * li) / den
    a2_r, a2_i = _cmul(a_r, a_i, a_r, a_i)
    for g in range(N_SSM_GROUPS):
        states = slice(g * SSM_STATE, (g + 1) * SSM_STATE)
        a2r_ref[:, states] = a2_r[g:g + 1, :]
        a2i_ref[:, states] = a2_i[g:g + 1, :]
    cols = {name: jnp.transpose(v) for name, v in
            dict(f_r=f_r, f_i=f_i, a_r=a_r, a_i=a_i, a2_r=a2_r, a2_i=a2_i).items()}
    wsi_ref[...] = jnp.zeros(wsi_ref.shape, BF16)
    wso_ref[...] = jnp.zeros(wso_ref.shape, BF16)
    row_group = lax.broadcasted_iota(jnp.int32, (SET_IN, SET_IN), 0) // SSM_GROUP
    col_group = lax.broadcasted_iota(jnp.int32, (SET_IN, SET_IN), 1) // SSM_GROUP
    same_group = row_group == col_group
    for s in range(N_SETS):
        gs = range(s * SET_GROUPS, (s + 1) * SET_GROUPS)

        def per_group(fn, gs=gs):
            return jnp.concatenate([fn(g) for g in gs], axis=1)

        def expand(name):
            return per_group(lambda g: jnp.broadcast_to(cols[name][:, g:g + 1], (SSM_STATE, SSM_GROUP)))

        b_r = per_group(lambda g: br_ref[g])
        b_i = per_group(lambda g: bi_ref[g])
        bb_last = _cmul(expand("f_r"), expand("f_i"), b_r, b_i)
        bb_first = _cmul(expand("a_r"), expand("a_i"), *bb_last)
        c_set = slice(s * SET_GROUPS, (s + 1) * SET_GROUPS)
        c_rt = jnp.transpose(cr_ref[c_set].reshape(SET_IN, SSM_STATE))
        c_it = jnp.transpose(ci_ref[c_set].reshape(SET_IN, SSM_STATE))
        for j, (bb_r, bb_i) in enumerate((bb_first, bb_last)):
            bb_rt = jnp.transpose(bb_r)
            bb_it = jnp.transpose(bb_i)
            k_t = (jnp.dot(bb_rt, c_rt, precision=lax.Precision.HIGHEST, preferred_element_type=F32)
                   - jnp.dot(bb_it, c_it, precision=lax.Precision.HIGHEST, preferred_element_type=F32))
            k_t = jnp.where(same_group, k_t, 0.0).astype(BF16)
            lag = BLOCK_STEPS - 1 - j
            for t in range(lag, BLOCK_STEPS):
                wt_ref[s, (t - lag) * SET_IN:(t - lag + 1) * SET_IN, t * SET_IN:(t + 1) * SET_IN] = k_t
            for gj in range(SET_GROUPS):
                src = slice(gj * SSM_GROUP, (gj + 1) * SSM_GROUP)
                rows = slice(j * SET_IN + gj * SSM_GROUP, j * SET_IN + (gj + 1) * SSM_GROUP)
                wsi_ref[s, rows, gj * SSM_STATE:(gj + 1) * SSM_STATE] = bb_rt[src, :].astype(BF16)
                wsi_ref[s, rows, SET_STATES + gj * SSM_STATE:SET_STATES + (gj + 1) * SSM_STATE] = (
                    bb_it[src, :].astype(BF16))
        wt_ref[s, SET_IN:2 * SET_IN, 0:SET_IN] = jnp.zeros((SET_IN, SET_IN), BF16)
        for t, (pr, pi) in enumerate((("a_r", "a_i"), ("a2_r", "a2_i"))):
            p_r, p_i = expand(pr), expand(pi)
            from_sr = (c_rt * p_r - c_it * p_i).astype(BF16)
            from_si = (-c_rt * p_i - c_it * p_r).astype(BF16)
            for gj in range(SET_GROUPS):
                src = slice(gj * SSM_GROUP, (gj + 1) * SSM_GROUP)
                out = slice(t * SET_IN + gj * SSM_GROUP, t * SET_IN + (gj + 1) * SSM_GROUP)
                wso_ref[s, gj * SSM_STATE:(gj + 1) * SSM_STATE, out] = from_sr[:, src]
                wso_ref[s, SET_STATES + gj * SSM_STATE:SET_STATES + (gj + 1) * SSM_STATE, out] = from_si[:, src]


def _ssm_prep(lam_re, lam_im, log_dt, b_re, b_im, c_re, c_im):
    assert BLOCK_STEPS == 2
    return pl.pallas_call(
        _ssm_prep_kernel,
        out_shape=(jax.ShapeDtypeStruct((1, N_STATE), F32), jax.ShapeDtypeStruct((1, N_STATE), F32),
                   jax.ShapeDtypeStruct((N_SETS, BLOCK_STEPS * SET_IN, 2 * SET_STATES), BF16),
                   jax.ShapeDtypeStruct((N_SETS, 2 * SET_STATES, BLOCK_STEPS * SET_IN), BF16),
                   jax.ShapeDtypeStruct((N_SETS, BLOCK_STEPS * SET_IN, BLOCK_STEPS * SET_IN), BF16)),
        name="ssm_prep",
    )(lam_re, lam_im, log_dt.reshape(N_SSM_GROUPS, 1), b_re, b_im, c_re, c_im)


def _block_scan(z_ref, st_r, st_i, a2r_ref, a2i_ref, nb, n_blocks):
    for s in range(N_SETS):
        lanes = slice(s * SET_STATES, (s + 1) * SET_STATES)
        re_lanes = slice(2 * s * SET_STATES, (2 * s + 1) * SET_STATES)
        im_lanes = slice((2 * s + 1) * SET_STATES, (2 * s + 2) * SET_STATES)
        ar = jnp.broadcast_to(a2r_ref[:, lanes], (SUBLANES, SET_STATES))
        ai = jnp.broadcast_to(a2i_ref[:, lanes], (SUBLANES, SET_STATES))

        def seq_block(rb, carry, lanes=lanes, re_lanes=re_lanes, im_lanes=im_lanes, ar=ar, ai=ai):
            r0 = pl.multiple_of(rb * SUBLANES, SUBLANES)
            sr = st_r[pl.ds(r0, SUBLANES), lanes]
            si = st_i[pl.ds(r0, SUBLANES), lanes]

            def step(k, state):
                sr, si = state
                rows = pl.ds(pl.multiple_of(k * nb + r0, SUBLANES), SUBLANES)
                zr = z_ref[rows, re_lanes]
                zi = z_ref[rows, im_lanes]
                z_ref[rows, re_lanes] = sr
                z_ref[rows, im_lanes] = si
                return ar * sr - ai * si + zr, ar * si + ai * sr + zi

            sr, si = lax.fori_loop(0, n_blocks, step, (sr, si), unroll=True)
            st_r[pl.ds(r0, SUBLANES), lanes] = sr
            st_i[pl.ds(r0, SUBLANES), lanes] = si
            return carry

        if nb == SUBLANES:
            seq_block(0, 0)
        else:
            lax.fori_loop(0, nb // SUBLANES, seq_block, 0)


def _to_time_major(v, perm_ref, nb, tt):
    for c in range(v.shape[-1] // LANES):
        for b in range(nb):
            perm_ref.at[c][pl.ds(b, tt, stride=nb), :] = v[b * tt:(b + 1) * tt, c * LANES:(c + 1) * LANES]
    return jnp.concatenate([perm_ref[c] for c in range(v.shape[-1] // LANES)], axis=-1)


def _to_seq_major(v, perm_ref, nb, tt):
    n_slabs = v.shape[-1] // LANES
    for c in range(n_slabs):
        perm_ref[c] = v[:, c * LANES:(c + 1) * LANES]
    return jnp.concatenate(
        [jnp.concatenate([perm_ref.at[c][pl.ds(b, tt, stride=nb), :] for b in range(nb)], axis=0)
         for c in range(n_slabs)], axis=-1)


def _pool_time_major(u_pool, hist_ref, i, nb, tt, pos0):
    m = tt * nb
    hist_rows = POOL_HIST * nb
    hist_ref[hist_rows:hist_rows + m, :] = u_pool
    t_abs = pos0 + i * tt + lax.broadcasted_iota(jnp.int32, (tt, nb, POOL_GROUP), 0).reshape(m, POOL_GROUP)
    pooled = []
    for k, w in enumerate(POOL_WINDOWS):
        cols = slice(k * POOL_GROUP, (k + 1) * POOL_GROUP)
        acc = u_pool[:, cols]
        for j in range(1, w):
            acc = acc + hist_ref[hist_rows - j * nb:hist_rows - j * nb + m, cols]
        count = jnp.minimum(t_abs + 1, w).astype(F32)
        pooled.append(acc / count - u_pool[:, cols])
    return pooled


def _pool_seq_major(u_pool, hist_ref, i, nb, tt, pos0):
    m = tt * nb
    t_abs = pos0 + i * tt + lax.broadcasted_iota(jnp.int32, (nb, tt, POOL_GROUP), 1)
    pooled = []
    for k, w in enumerate(POOL_WINDOWS):
        cur = u_pool[:, k * POOL_GROUP:(k + 1) * POOL_GROUP].reshape(nb, tt, POOL_GROUP)
        hist_ref[k, :, HIST_PAD:HIST_PAD + tt, :] = cur
        acc = cur
        for j in range(1, w):
            acc = acc + hist_ref[k, :, HIST_PAD - j:HIST_PAD - j + tt, :]
        count = jnp.minimum(t_abs + 1, w).astype(F32)
        pooled.append((acc / count - cur).reshape(m, POOL_GROUP))
    return pooled


def _mixer_kernel(x_ref, mod_ref, h0r_ref, h0i_ref, pool0_ref, n1_ref, win_ref, wsi_ref, wso_ref, wt_ref,
                  a2r_ref, a2i_ref, d_ref, wglu_ref, wpool_ref, pscale_ref, wout_ref, *rest,
                  nb, tt, n_sub, n_steps, pos0, time_major, cast_halved):
    n_cast = len(cast_halved)
    cast_src, rest = rest[:n_cast], rest[n_cast:]
    xo_ref, sr_out_ref, si_out_ref, pool_out_ref = rest[:4]
    cast_dst, rest = rest[4:4 + n_cast], rest[4 + n_cast:]
    z_ref, st_r, st_i, hist_ref, perm_ref = rest
    n_blocks = tt // BLOCK_STEPS
    mb = n_blocks * nb
    i = pl.program_id(0)
    _cast_blocks(cast_src, cast_dst, cast_halved)
    m = tt * nb
    hist_rows = POOL_HIST * nb

    @pl.when(i == 0)
    def _():
        st_r[...] = h0r_ref[...]
        st_i[...] = h0i_ref[...]
        if time_major:
            hist_ref[0:hist_rows, :] = pool0_ref[...]
        else:
            for k in range(len(POOL_WINDOWS)):
                hist_ref[k, :, 0:HIST_PAD, :] = pool0_ref[:, :, k * POOL_GROUP:(k + 1) * POOL_GROUP]

    for sub in range(n_sub):
        tile = i * n_sub + sub
        t_rows = slice(sub * tt, (sub + 1) * tt)
        x = (x_ref[...] if time_major else x_ref[:, t_rows, :]).reshape(m, D_MODEL)
        shift1 = mod_ref[:, :, 0 * D_MODEL:1 * D_MODEL]
        scale1 = mod_ref[:, :, 1 * D_MODEL:2 * D_MODEL]
        gate1 = mod_ref[:, :, 2 * D_MODEL:3 * D_MODEL]
        hb = _modulate(_rmsnorm(x, n1_ref[...]), shift1, scale1, tt, nb, time_major).astype(BF16)

        u_ssm = _dot(hb, win_ref[:, 0:D_SSM])
        u_pool = _dot(hb, win_ref[:, D_SSM:D_SSM + D_POOL])
        u_tm = u_ssm if time_major else _to_time_major(u_ssm, perm_ref, nb, tt)
        u_blocks = u_tm.reshape(n_blocks, BLOCK_STEPS, nb, D_SSM)
        u_steps = [u_blocks[:, j].reshape(mb, D_SSM).astype(BF16) for j in range(BLOCK_STEPS)]
        y_sets = []
        for s in range(N_SETS):
            chans = slice(s * SET_IN, (s + 1) * SET_IN)
            lhs = jnp.concatenate([u[:, chans] for u in u_steps], axis=1)
            z_ref[:, 2 * s * SET_STATES:2 * (s + 1) * SET_STATES] = _dot(lhs, wsi_ref[s])
            y_sets.append(_dot(lhs, wt_ref[s]))
        g_ssm = _dot(hb, win_ref[:, D_SSM + D_POOL:D_SSM + D_POOL + D_MODEL])
        g_pool = _dot(hb, win_ref[:, D_SSM + D_POOL + D_MODEL:])

        _block_scan(z_ref, st_r, st_i, a2r_ref, a2i_ref, nb, n_blocks)
        if time_major:
            pooled = _pool_time_major(u_pool, hist_ref, tile, nb, tt, pos0)
            new_buf_rows = (slice(m, m + hist_rows), slice(None))
            if n_steps * n_sub > 1:
                hist_ref[0:hist_rows, :] = hist_ref[new_buf_rows]
        else:
            pooled = _pool_seq_major(u_pool, hist_ref, tile, nb, tt, pos0)
            if n_steps * n_sub > 1:
                hist_ref[:, :, 0:HIST_PAD, :] = hist_ref[:, :, tt:tt + HIST_PAD, :]
        z_pieces = [_dot(pooled[k].astype(BF16), wpool_ref[k]) for k in range(len(POOL_WINDOWS))]
        br_pool = jnp.concatenate(z_pieces, axis=-1) * pscale_ref[...]

        for s in range(N_SETS):
            s_in = z_ref[:, 2 * s * SET_STATES:2 * (s + 1) * SET_STATES].astype(BF16)
            y_sets[s] = y_sets[s] + _dot(s_in, wso_ref[s])
        y_steps = [jnp.concatenate([y[:, j * SET_IN:(j + 1) * SET_IN] for y in y_sets], axis=1).reshape(n_blocks, nb, D_SSM)
                   for j in range(BLOCK_STEPS)]
        y_cs = jnp.stack(y_steps, axis=1).reshape(m, D_SSM)
        if not time_major:
            y_cs = _to_seq_major(y_cs, perm_ref, nb, tt)

        for h in range(N_TAIL_SPLITS):
            rows = slice(h * m // N_TAIL_SPLITS, (h + 1) * m // N_TAIL_SPLITS)
            y_ssm = y_cs[rows] + d_ref[...] * u_ssm[rows]
            glu = _dot(jax.nn.gelu(y_ssm).astype(BF16), wglu_ref[...])
            br_ssm = glu[:, :D_MODEL] * _two_sigmoid(glu[:, D_MODEL:])
            merged2 = _two_sigmoid(g_ssm[rows]) * br_ssm + _two_sigmoid(g_pool[rows]) * br_pool[rows]
            o = _dot(merged2.astype(BF16), wout_ref[...])
            if time_major:
                xo_ref[rows, :] = x[rows] + _rows_gate(o, gate1, tt // N_TAIL_SPLITS, nb, True)
            else:
                nb_h = nb // N_TAIL_SPLITS
                seqs = slice(h * nb_h, (h + 1) * nb_h)
                x1 = x[rows] + _rows_gate(o, gate1[seqs], tt, nb_h, False)
                xo_ref[seqs, t_rows, :] = x1.reshape(nb_h, tt, D_MODEL)

    @pl.when(i == n_steps - 1)
    def _():
        sr_out_ref[...] = st_r[...]
        si_out_ref[...] = st_i[...]
        if time_major:
            pool_out_ref[...] = hist_ref[new_buf_rows]
        else:
            pool_out_ref[...] = jnp.concatenate(
                [hist_ref[k, :, tt + HIST_PAD - POOL_HIST:tt + HIST_PAD, :] for k in range(len(POOL_WINDOWS))],
                axis=-1)


def _const_spec(shape):
    nd = len(shape)
    return pl.BlockSpec(shape, lambda i, _nd=nd: (0,) * _nd, pipeline_mode=pl.Buffered(1))


def _row_spec(nb, tt, time_major):
    if time_major:
        return pl.BlockSpec((tt * nb, D_MODEL), lambda i: (i, 0))
    return pl.BlockSpec((nb, tt, D_MODEL), lambda i: (0, i, 0))


def _mixer(x, mod, h0r, h0i, pool0, n1, win, wsi, wso, wt, a2r, a2i, d, wglu, wpool, pscale, wout,
           *, nb, tt, n_sub, n_steps, pos0, time_major, cast=()):
    m = tt * nb
    consts = (mod, h0r, h0i, pool0, n1, win, wsi, wso, wt, a2r, a2i, d, wglu, wpool, pscale, wout)
    cast, cast_halved = tuple(a for a, _ in cast), tuple(n for _, n in cast)
    cast_specs = _cast_specs(cast, n_steps)
    if time_major:
        hist_shape = (POOL_HIST * nb + m, D_POOL)
        pool_out_shape = (POOL_HIST * nb, D_POOL)
    else:
        hist_shape = (len(POOL_WINDOWS), nb, HIST_PAD + tt, POOL_GROUP)
        pool_out_shape = (nb, POOL_HIST, D_POOL)
    kern = functools.partial(_mixer_kernel, nb=nb, tt=tt, n_sub=n_sub, n_steps=n_steps, pos0=pos0,
                             time_major=time_major, cast_halved=cast_halved)
    return pl.pallas_call(
        kern,
        grid=(n_steps,),
        in_specs=[_row_spec(nb, n_sub * tt, time_major)] + [_const_spec(a.shape) for a in consts] + cast_specs,
        out_specs=(_row_spec(nb, n_sub * tt, time_major),
                   pl.BlockSpec((nb, N_STATE), lambda i: (0, 0)),
                   pl.BlockSpec((nb, N_STATE), lambda i: (0, 0)),
                   pl.BlockSpec(pool_out_shape, lambda i: (0,) * len(pool_out_shape))) + tuple(cast_specs),
        out_shape=(jax.ShapeDtypeStruct(x.shape, F32),
                   jax.ShapeDtypeStruct((nb, N_STATE), F32),
                   jax.ShapeDtypeStruct((nb, N_STATE), F32),
                   jax.ShapeDtypeStruct(pool_out_shape, F32)) + tuple(jax.ShapeDtypeStruct(a.shape, BF16) for a in cast),
        scratch_shapes=[pltpu.VMEM((m // BLOCK_STEPS, 2 * N_STATE), F32),
                        pltpu.VMEM((nb, N_STATE), F32), pltpu.VMEM((nb, N_STATE), F32),
                        pltpu.VMEM(hist_shape, F32),
                        pltpu.VMEM((D_SSM // LANES, m, LANES), F32)],
        compiler_params=pltpu.CompilerParams(dimension_semantics=("arbitrary",),
                                             vmem_limit_bytes=VMEM_LIMIT),
        name=f"mixer_nb{nb}",
    )(x, *consts, *cast)


def _ffn_kernel(x_ref, mod_ref, n2_ref, nf_ref, wfi_ref, wfo_ref, o_ref, *, nb, tt, time_major):
    for h in range(N_FFN_SPLITS):
        if time_major:
            tt_h, nb_h = tt // N_FFN_SPLITS, nb
            piece = (slice(h * tt_h * nb, (h + 1) * tt_h * nb), slice(None))
            mod = mod_ref
        else:
            tt_h, nb_h = tt, nb // N_FFN_SPLITS
            piece = (slice(h * nb_h, (h + 1) * nb_h), slice(None), slice(None))
            mod = mod_ref.at[h * nb_h:(h + 1) * nb_h]
        x = x_ref[piece].reshape(tt_h * nb_h, D_MODEL)
        shift2 = mod[:, :, 3 * D_MODEL:4 * D_MODEL]
        scale2 = mod[:, :, 4 * D_MODEL:5 * D_MODEL]
        gate2 = mod[:, :, 5 * D_MODEL:6 * D_MODEL]
        hb = _modulate(_rmsnorm(x, n2_ref[...]), shift2, scale2, tt_h, nb_h, time_major).astype(BF16)
        f_a = _dot(hb, wfi_ref[:, :D_FF])
        f_b = _dot(hb, wfi_ref[:, D_FF:])
        act = (f_a * _two_sigmoid(f_a) * f_b).astype(BF16)
        x2 = x + _rows_gate(_dot(act, wfo_ref[...]), gate2, tt_h, nb_h, time_major)
        y = _rmsnorm(x2, nf_ref[...])
        o_ref[piece] = y if time_major else y.reshape(nb_h, tt_h, D_MODEL)


def _ffn(x, mod, n2, nf, wfi, wfo, *, nb, tt, n_steps, time_major):
    consts = (mod, n2, nf, wfi, wfo)
    return pl.pallas_call(
        functools.partial(_ffn_kernel, nb=nb, tt=tt, time_major=time_major),
        grid=(n_steps,),
        in_specs=[_row_spec(nb, tt, time_major)] + [_const_spec(a.shape) for a in consts],
        out_specs=_row_spec(nb, tt, time_major),
        out_shape=jax.ShapeDtypeStruct(x.shape, F32),
        compiler_params=pltpu.CompilerParams(dimension_semantics=("arbitrary",),
                                             vmem_limit_bytes=VMEM_LIMIT),
        name=f"ffn_nb{nb}",
    )(x, *consts)


def _trunk(x, mod, h0r, h0i, pool0, pos0, p, *, time_major, ffn_w):
    nb, t_len, _ = x.shape
    tt = min(t_len, TILE_ROWS // nb)
    n_sub = 1 if time_major else MIXER_SUBTILES
    tt_ffn = min(t_len, FFN_TILE_ROWS // nb)
    if time_major:
        rows = lambda a: jnp.transpose(a, (1, 0, 2)).reshape(a.shape[1] * nb, a.shape[2])
        unrows = lambda a, t: jnp.transpose(a.reshape(t, nb, a.shape[-1]), (1, 0, 2))
        x_in, pool_in, mod3 = rows(x), rows(pool0), mod[None]
    else:
        assert nb == SUBLANES
        x_in, mod3 = x, mod[:, None, :]
        pool_in = jnp.pad(pool0, ((0, 0), (HIST_PAD - POOL_HIST, 0), (0, 0)))
    cast = ((ffn_w[0], (0, D_FF)), (ffn_w[1], (0, 0))) if ffn_w[0].dtype == F32 else ()
    x1, s_r, s_i, new_buf, *cast_out = _mixer(
        x_in, mod3, h0r.reshape(nb, N_STATE), h0i.reshape(nb, N_STATE), pool_in,
        p["n1"], p["win"], p["wsi"], p["wso"], p["wt"], p["a2r"], p["a2i"], p["d"], p["wglu"], p["wpool"],
        p["pscale"], p["wout"], nb=nb, tt=tt, n_sub=n_sub, n_steps=t_len // (tt * n_sub), pos0=pos0,
        time_major=time_major, cast=cast)
    wfi, wfo = cast_out if cast else ffn_w
    y = _ffn(x1, mod3, p["n2"], p["nf"], wfi, wfo, nb=nb, tt=tt_ffn, n_steps=t_len // tt_ffn,
             time_major=time_major)
    if time_major:
        y, new_buf = unrows(y, t_len), unrows(new_buf, POOL_HIST)
    shape_s = (1, nb, N_SSM_GROUPS, SSM_STATE)
    return (y, s_r.reshape(shape_s), s_i.reshape(shape_s), new_buf[None]), (wfi, wfo)


def kernel(x_prompt, x_sample, c_prompt, c_sample, state_ssm_re, state_ssm_im, state_pool, norm1_g, norm2_g, normf_g, w_ada, b_ada, w_in, ssm_lam_re, ssm_lam_im, ssm_log_dt, ssm_b_re, ssm_b_im, ssm_c_re, ssm_c_im, ssm_d, w_glu, w_pool, pool_scale, w_out, w_ffn_in, w_ffn_out):
    n_prompt = x_prompt.shape[0]
    mod_p, mod_s, win, wglu, wout, wpool = _ada(
        c_prompt, c_sample, w_ada[0], b_ada[0],
        cast=((w_in[0], (D_SSM + D_POOL, w_in.shape[-1])), (w_glu[0], (0, 2 * D_MODEL)),
              (w_out[0], (0, D_MODEL)), (w_pool[0].reshape(D_POOL, POOL_OUT), (0, 0))))
    a2r, a2i, wsi, wso, wt = _ssm_prep(ssm_lam_re[0], ssm_lam_im[0], ssm_log_dt[0], ssm_b_re[0], ssm_b_im[0],
                                       ssm_c_re[0], ssm_c_im[0])
    p = dict(
        n1=norm1_g[0].reshape(1, -1), n2=norm2_g[0].reshape(1, -1), nf=normf_g.reshape(1, -1),
        win=win, wsi=wsi, wso=wso, wt=wt, a2r=a2r, a2i=a2i, d=ssm_d[0].reshape(1, -1),
        wglu=wglu, wpool=wpool.reshape(len(POOL_WINDOWS), POOL_GROUP, POOL_OUT), pscale=pool_scale[0].reshape(1, -1),
        wout=wout)
    zero_state = jnp.zeros((n_prompt, N_SSM_GROUPS, SSM_STATE), F32)
    zero_pool = jnp.zeros((n_prompt, POOL_HIST, D_POOL), F32)
    ffn_f32 = (w_ffn_in[0], w_ffn_out[0])
    (y_p, p_re, p_im, p_pool), ffn_bf16 = _trunk(x_prompt, mod_p, zero_state, zero_state, zero_pool, 0, p,
                                                  time_major=False, ffn_w=ffn_f32)
    (y_s, s_re, s_im, s_pool), _ = _trunk(x_sample, mod_s, state_ssm_re[0], state_ssm_im[0], state_pool[0],
                                          PAST_LEN, p, time_major=True, ffn_w=ffn_bf16)
    return (y_p, y_s, p_re, p_im, p_pool, s_re, s_im, s_pool)
```

```python
import functools

import jax
import jax.numpy as jnp
from jax import lax
from jax.experimental import pallas as pl
from jax.experimental.pallas import tpu as pltpu

D_MODEL = 1024
D_SSM = 512
SSM_GROUP = 16
N_SSM_GROUPS = 32
SSM_STATE = 64
N_STATE = N_SSM_GROUPS * SSM_STATE
D_POOL = 512
POOL_WINDOWS = (2, 4, 8, 16)
POOL_GROUP = 128
POOL_OUT = 256
POOL_HIST = 15
D_FF = 2816
N_MOD = 6
EPS = 1e-6
PAST_LEN = 16384

SUBLANES = 8
BF16_SUBLANES = 16
LANES = 128
TILE_ROWS = 512
MIXER_SUBTILES = 2
FFN_TILE_ROWS = 1024
HIST_PAD = 16
N_TAIL_SPLITS = 2
N_FFN_SPLITS = 4
ADA_ROW_BLOCKS = 4
MXU_DEPTH = 256
BLOCK_STEPS = 4
SET_GROUPS = MXU_DEPTH // (BLOCK_STEPS * SSM_GROUP)
N_SETS = N_SSM_GROUPS // SET_GROUPS
SET_IN = SET_GROUPS * SSM_GROUP
SET_STATES = SET_GROUPS * SSM_STATE
VMEM_LIMIT = 60 * 1024 * 1024

BF16 = jnp.bfloat16
F32 = jnp.float32


def _dot(a, b):
    return jnp.dot(a, b, preferred_element_type=F32)


def _sigmoid(x):
    return 0.5 * jnp.tanh(0.5 * x) + 0.5


def _two_sigmoid(half_x):
    return jnp.tanh(half_x) + 1.0


def _rmsnorm(x, g):
    return x * lax.rsqrt(jnp.mean(x * x, axis=-1, keepdims=True) + EPS) * g


def _tile3(v, tt, nb, time_major):
    lead, inner = (tt, nb) if time_major else (nb, tt)
    return v.reshape(lead, inner, v.shape[-1])


def _modulate(h, shift, scale, tt, nb, time_major):
    return (_tile3(h, tt, nb, time_major) * (1.0 + scale) + shift).reshape(h.shape)


def _rows_gate(v, gate, tt, nb, time_major):
    return (_tile3(v, tt, nb, time_major) * gate).reshape(v.shape)


def _cast_specs(arrays, n_steps):
    specs = []
    for a in arrays:
        n_blk = max(k for k in range(1, n_steps + 1)
                    if n_steps % k == 0 and a.shape[0] % (k * BF16_SUBLANES) == 0)
        specs.append(pl.BlockSpec((a.shape[0] // n_blk, a.shape[1]),
                                  lambda i, _hold=n_steps // n_blk: (i // _hold, 0)))
    return specs


def _cast_blocks(srcs, dsts, halved):
    for src, dst, (lo, hi) in zip(srcs, dsts, halved):
        if lo > 0:
            dst[:, :lo] = src[:, :lo].astype(BF16)
        if hi > lo:
            dst[:, lo:hi] = (src[:, lo:hi] * 0.5).astype(BF16)
        if hi < src.shape[-1]:
            dst[:, hi:] = src[:, hi:].astype(BF16)


def _ada_kernel(cp_ref, cs_ref, w_ref, b_ref, *rest, halved):
    n_cast = len(halved)
    cast_src, (op_ref, os_ref), cast_dst = rest[:n_cast], rest[n_cast:n_cast + 2], rest[n_cast + 2:]
    _cast_blocks(cast_src, cast_dst, halved)
    k = pl.program_id(0)
    w = w_ref[...].astype(BF16)
    for c_ref, o_ref in ((cp_ref, op_ref), (cs_ref, os_ref)):
        c = c_ref[...]
        part = _dot((c * _sigmoid(c)).astype(BF16), w)

        @pl.when(k == 0)
        def _(o_ref=o_ref, part=part):
            o_ref[...] = b_ref[...] + part

        @pl.when(k > 0)
        def _(o_ref=o_ref, part=part):
            o_ref[...] += part


def _ada(c_prompt, c_sample, w_ada, b_ada, cast):
    n_p, n_s = c_prompt.shape[0], c_sample.shape[0]
    kb = D_MODEL // ADA_ROW_BLOCKS
    arrays, halved = tuple(a for a, _ in cast), tuple(h for _, h in cast)
    cast_specs = _cast_specs(arrays, ADA_ROW_BLOCKS)
    n_out = N_MOD * D_MODEL
    return pl.pallas_call(
        functools.partial(_ada_kernel, halved=halved),
        grid=(ADA_ROW_BLOCKS,),
        in_specs=[pl.BlockSpec((n_p, kb), lambda k: (0, k)),
                  pl.BlockSpec((n_s, kb), lambda k: (0, k)),
                  pl.BlockSpec((kb, n_out), lambda k: (k, 0)),
                  pl.BlockSpec((1, n_out), lambda k: (0, 0))] + cast_specs,
        out_specs=[pl.BlockSpec((n_p, n_out), lambda k: (0, 0)), pl.BlockSpec((n_s, n_out), lambda k: (0, 0))]
        + cast_specs,
        out_shape=[jax.ShapeDtypeStruct((n_p, n_out), F32), jax.ShapeDtypeStruct((n_s, n_out), F32)]
        + [jax.ShapeDtypeStruct(a.shape, BF16) for a in arrays],
        compiler_params=pltpu.CompilerParams(dimension_semantics=("arbitrary",),
                                             vmem_limit_bytes=VMEM_LIMIT),
        name="ada_mod",
    )(c_prompt, c_sample, w_ada, b_ada.reshape(1, -1), *arrays)


def _cmul(ar, ai, br, bi):
    return ar * br - ai * bi, ar * bi + ai * br


def _ssm_prep_kernel(lr_ref, li_ref, ldt_ref, br_ref, bi_ref, cr_ref, ci_ref,
                     alr_ref, ali_ref, wsi_ref, wso_ref, wt_ref):
    lr = lr_ref[...]
    li = li_ref[...]
    dt = jnp.exp(ldt_ref[...])
    mag = jnp.exp(lr * dt)
    a_r = mag * jnp.cos(li * dt)
    a_i = mag * jnp.sin(li * dt)
    den = lr * lr + li * li
    nr = a_r - 1.0
    ni = a_i
    f_r = (nr * lr + ni * li) / den
    f_i = (ni * lr - nr * li) / den
    powers = [(a_r, a_i)]
    for _ in range(BLOCK_STEPS - 1):
        powers.append(_cmul(a_r, a_i, *powers[-1]))
    for g in range(N_SSM_GROUPS):
        states = slice(g * SSM_STATE, (g + 1) * SSM_STATE)
        alr_ref[:, states] = powers[-1][0][g:g + 1, :]
        ali_ref[:, states] = powers[-1][1][g:g + 1, :]
    f_cols = (jnp.transpose(f_r), jnp.transpose(f_i))
    power_cols = [(jnp.transpose(pr), jnp.transpose(pi)) for pr, pi in powers]
    wsi_ref[...] = jnp.zeros(wsi_ref.shape, BF16)
    wso_ref[...] = jnp.zeros(wso_ref.shape, BF16)
    wt_ref[...] = jnp.zeros(wt_ref.shape, BF16)
    row_group = lax.broadcasted_iota(jnp.int32, (SET_IN, SET_IN), 0) // SSM_GROUP
    col_group = lax.broadcasted_iota(jnp.int32, (SET_IN, SET_IN), 1) // SSM_GROUP
    same_group = row_group == col_group
    for s in range(N_SETS):
        gs = range(s * SET_GROUPS, (s + 1) * SET_GROUPS)

        def per_group(fn, gs=gs):
            return jnp.concatenate([fn(g) for g in gs], axis=1)

        def expand(col_pair):
            return tuple(per_group(lambda g, c=c: jnp.broadcast_to(c[:, g:g + 1], (SSM_STATE, SSM_GROUP)))
                         for c in col_pair)

        b_r = per_group(lambda g: br_ref[g])
        b_i = per_group(lambda g: bi_ref[g])
        bb_lags = [_cmul(*expand(f_cols), b_r, b_i)]
        a_exp = expand(power_cols[0])
        for _ in range(BLOCK_STEPS - 1):
            bb_lags.append(_cmul(*a_exp, *bb_lags[-1]))
        c_set = slice(s * SET_GROUPS, (s + 1) * SET_GROUPS)
        c_rt = jnp.transpose(cr_ref[c_set].reshape(SET_IN, SSM_STATE))
        c_it = jnp.transpose(ci_ref[c_set].reshape(SET_IN, SSM_STATE))
        for j in range(BLOCK_STEPS):
            lag = BLOCK_STEPS - 1 - j
            bb_r, bb_i = bb_lags[lag]
            bb_rt = jnp.transpose(bb_r)
            bb_it = jnp.transpose(bb_i)
            k_t = (jnp.dot(bb_rt, c_rt, precision=lax.Precision.HIGHEST, preferred_element_type=F32)
                   - jnp.dot(bb_it, c_it, precision=lax.Precision.HIGHEST, preferred_element_type=F32))
            k_t = jnp.where(same_group, k_t, 0.0).astype(BF16)
            for t in range(lag, BLOCK_STEPS):
                wt_ref[s, (t - lag) * SET_IN:(t - lag + 1) * SET_IN, t * SET_IN:(t + 1) * SET_IN] = k_t
            for gj in range(SET_GROUPS):
                src = slice(gj * SSM_GROUP, (gj + 1) * SSM_GROUP)
                rows = slice(j * SET_IN + gj * SSM_GROUP, j * SET_IN + (gj + 1) * SSM_GROUP)
                wsi_ref[s, rows, gj * SSM_STATE:(gj + 1) * SSM_STATE] = bb_rt[src, :].astype(BF16)
                wsi_ref[s, rows, SET_STATES + gj * SSM_STATE:SET_STATES + (gj + 1) * SSM_STATE] = (
                    bb_it[src, :].astype(BF16))
        for t in range(BLOCK_STEPS):
            p_r, p_i = expand(power_cols[t])
            from_sr = (c_rt * p_r - c_it * p_i).astype(BF16)
            from_si = (-c_rt * p_i - c_it * p_r).astype(BF16)
            for gj in range(SET_GROUPS):
                src = slice(gj * SSM_GROUP, (gj + 1) * SSM_GROUP)
                out = slice(t * SET_IN + gj * SSM_GROUP, t * SET_IN + (gj + 1) * SSM_GROUP)
                wso_ref[s, gj * SSM_STATE:(gj + 1) * SSM_STATE, out] = from_sr[:, src]
                wso_ref[s, SET_STATES + gj * SSM_STATE:SET_STATES + (gj + 1) * SSM_STATE, out] = from_si[:, src]


def _ssm_prep(lam_re, lam_im, log_dt, b_re, b_im, c_re, c_im):
    return pl.pallas_call(
        _ssm_prep_kernel,
        out_shape=(jax.ShapeDtypeStruct((1, N_STATE), F32), jax.ShapeDtypeStruct((1, N_STATE), F32),
                   jax.ShapeDtypeStruct((N_SETS, BLOCK_STEPS * SET_IN, 2 * SET_STATES), BF16),
                   jax.ShapeDtypeStruct((N_SETS, 2 * SET_STATES, BLOCK_STEPS * SET_IN), BF16),
                   jax.ShapeDtypeStruct((N_SETS, BLOCK_STEPS * SET_IN, BLOCK_STEPS * SET_IN), BF16)),
        name="ssm_prep",
    )(lam_re, lam_im, log_dt.reshape(N_SSM_GROUPS, 1), b_re, b_im, c_re, c_im)


def _block_scan(z_ref, st_r, st_i, alr_ref, ali_ref, nb, n_blocks):
    for s in range(N_SETS):
        lanes = slice(s * SET_STATES, (s + 1) * SET_STATES)
        re_lanes = slice(2 * s * SET_STATES, (2 * s + 1) * SET_STATES)
        im_lanes = slice((2 * s + 1) * SET_STATES, (2 * s + 2) * SET_STATES)
        ar = jnp.broadcast_to(alr_ref[:, lanes], (SUBLANES, SET_STATES))
        ai = jnp.broadcast_to(ali_ref[:, lanes], (SUBLANES, SET_STATES))

        def seq_block(rb, carry, lanes=lanes, re_lanes=re_lanes, im_lanes=im_lanes, ar=ar, ai=ai):
            r0 = pl.multiple_of(rb * SUBLANES, SUBLANES)
            sr = st_r[pl.ds(r0, SUBLANES), lanes]
            si = st_i[pl.ds(r0, SUBLANES), lanes]

            def step(k, state):
                sr, si = state
                rows = pl.ds(pl.multiple_of(k * nb + r0, SUBLANES), SUBLANES)
                zr = z_ref[rows, re_lanes]
                zi = z_ref[rows, im_lanes]
                z_ref[rows, re_lanes] = sr
                z_ref[rows, im_lanes] = si
                return ar * sr - ai * si + zr, ar * si + ai * sr + zi

            sr, si = lax.fori_loop(0, n_blocks, step, (sr, si), unroll=True)
            st_r[pl.ds(r0, SUBLANES), lanes] = sr
            st_i[pl.ds(r0, SUBLANES), lanes] = si
            return carry

        if nb == SUBLANES:
            seq_block(0, 0)
        else:
            lax.fori_loop(0, nb // SUBLANES, seq_block, 0)


def _to_time_major(v, perm_ref, nb, tt):
    for c in range(v.shape[-1] // LANES):
        for b in range(nb):
            perm_ref.at[c][pl.ds(b, tt, stride=nb), :] = v[b * tt:(b + 1) * tt, c * LANES:(c + 1) * LANES]
    return jnp.concatenate([perm_ref[c] for c in range(v.shape[-1] // LANES)], axis=-1)


def _to_seq_major(v, perm_ref, nb, tt):
    n_slabs = v.shape[-1] // LANES
    for c in range(n_slabs):
        perm_ref[c] = v[:, c * LANES:(c + 1) * LANES]
    return jnp.concatenate(
        [jnp.concatenate([perm_ref.at[c][pl.ds(b, tt, stride=nb), :] for b in range(nb)], axis=0)
         for c in range(n_slabs)], axis=-1)


def _pool_time_major(u_pool, hist_ref, i, nb, tt, pos0):
    m = tt * nb
    hist_rows = POOL_HIST * nb
    hist_ref[hist_rows:hist_rows + m, :] = u_pool
    t_abs = pos0 + i * tt + lax.broadcasted_iota(jnp.int32, (tt, nb, POOL_GROUP), 0).reshape(m, POOL_GROUP)
    pooled = []
    for k, w in enumerate(POOL_WINDOWS):
        cols = slice(k * POOL_GROUP, (k + 1) * POOL_GROUP)
        acc = u_pool[:, cols]
        for j in range(1, w):
            acc = acc + hist_ref[hist_rows - j * nb:hist_rows - j * nb + m, cols]
        count = jnp.minimum(t_abs + 1, w).astype(F32)
        pooled.append(acc / count - u_pool[:, cols])
    return pooled


def _pool_seq_major(u_pool, hist_ref, i, nb, tt, pos0):
    m = tt * nb
    t_abs = pos0 + i * tt + lax.broadcasted_iota(jnp.int32, (nb, tt, POOL_GROUP), 1)
    pooled = []
    for k, w in enumerate(POOL_WINDOWS):
        cur = u_pool[:, k * POOL_GROUP:(k + 1) * POOL_GROUP].reshape(nb, tt, POOL_GROUP)
        hist_ref[k, :, HIST_PAD:HIST_PAD + tt, :] = cur
        acc = cur
        for j in range(1, w):
            acc = acc + hist_ref[k, :, HIST_PAD - j:HIST_PAD - j + tt, :]
        count = jnp.minimum(t_abs + 1, w).astype(F32)
        pooled.append((acc / count - cur).reshape(m, POOL_GROUP))
    return pooled


def _mixer_kernel(x_ref, mod_ref, h0r_ref, h0i_ref, pool0_ref, n1_ref, win_ref, wsi_ref, wso_ref, wt_ref,
                  alr_ref, ali_ref, d_ref, wglu_ref, wpool_ref, pscale_ref, wout_ref, *rest,
                  nb, tt, n_sub, n_steps, pos0, time_major, cast_halved):
    n_cast = len(cast_halved)
    cast_src, rest = rest[:n_cast], rest[n_cast:]
    xo_ref, sr_out_ref, si_out_ref, pool_out_ref = rest[:4]
    cast_dst, rest = rest[4:4 + n_cast], rest[4 + n_cast:]
    z_ref, st_r, st_i, hist_ref, perm_ref = rest
    n_blocks = tt // BLOCK_STEPS
    mb = n_blocks * nb
    i = pl.program_id(0)
    _cast_blocks(cast_src, cast_dst, cast_halved)
    m = tt * nb
    hist_rows = POOL_HIST * nb

    @pl.when(i == 0)
    def _():
        st_r[...] = h0r_ref[...]
        st_i[...] = h0i_ref[...]
        if time_major:
            hist_ref[0:hist_rows, :] = pool0_ref[...]
        else:
            for k in range(len(POOL_WINDOWS)):
                hist_ref[k, :, 0:HIST_PAD, :] = pool0_ref[:, :, k * POOL_GROUP:(k + 1) * POOL_GROUP]

    for sub in range(n_sub):
        tile = i * n_sub + sub
        t_rows = slice(sub * tt, (sub + 1) * tt)
        x = (x_ref[...] if time_major else x_ref[:, t_rows, :]).reshape(m, D_MODEL)
        shift1 = mod_ref[:, :, 0 * D_MODEL:1 * D_MODEL]
        scale1 = mod_ref[:, :, 1 * D_MODEL:2 * D_MODEL]
        gate1 = mod_ref[:, :, 2 * D_MODEL:3 * D_MODEL]
        hb = _modulate(_rmsnorm(x, n1_ref[...]), shift1, scale1, tt, nb, time_major).astype(BF16)

        u_ssm = _dot(hb, win_ref[:, 0:D_SSM])
        u_pool = _dot(hb, win_ref[:, D_SSM:D_SSM + D_POOL])
        u_tm = u_ssm if time_major else _to_time_major(u_ssm, perm_ref, nb, tt)
        u_blocks = u_tm.reshape(n_blocks, BLOCK_STEPS, nb, D_SSM)
        u_steps = [u_blocks[:, j].reshape(mb, D_SSM).astype(BF16) for j in range(BLOCK_STEPS)]
        y_sets = []
        for s in range(N_SETS):
            chans = slice(s * SET_IN, (s + 1) * SET_IN)
            lhs = jnp.concatenate([u[:, chans] for u in u_steps], axis=1)
            z_ref[:, 2 * s * SET_STATES:2 * (s + 1) * SET_STATES] = _dot(lhs, wsi_ref[s])
            y_sets.append(_dot(lhs, wt_ref[s]))
        g_ssm = _dot(hb, win_ref[:, D_SSM + D_POOL:D_SSM + D_POOL + D_MODEL])
        g_pool = _dot(hb, win_ref[:, D_SSM + D_POOL + D_MODEL:])

        _block_scan(z_ref, st_r, st_i, alr_ref, ali_ref, nb, n_blocks)
        if time_major:
            pooled = _pool_time_major(u_pool, hist_ref, tile, nb, tt, pos0)
            new_buf_rows = (slice(m, m + hist_rows), slice(None))
            if n_steps * n_sub > 1:
                hist_ref[0:hist_rows, :] = hist_ref[new_buf_rows]
        else:
            pooled = _pool_seq_major(u_pool, hist_ref, tile, nb, tt, pos0)
            if n_steps * n_sub > 1:
                hist_ref[:, :, 0:HIST_PAD, :] = hist_ref[:, :, tt:tt + HIST_PAD, :]
        z_pieces = [_dot(pooled[k].astype(BF16), wpool_ref[k]) for k in range(len(POOL_WINDOWS))]
        br_pool = jnp.concatenate(z_pieces, axis=-1) * pscale_ref[...]

        for s in range(N_SETS):
            s_in = z_ref[:, 2 * s * SET_STATES:2 * (s + 1) * SET_STATES].astype(BF16)
            y_sets[s] = y_sets[s] + _dot(s_in, wso_ref[s])
        y_steps = [jnp.concatenate([y[:, j * SET_IN:(j + 1) * SET_IN] for y in y_sets], axis=1).reshape(n_blocks, nb, D_SSM)
                   for j in range(BLOCK_STEPS)]
        y_cs = jnp.stack(y_steps, axis=1).reshape(m, D_SSM)
        if not time_major:
            y_cs = _to_seq_major(y_cs, perm_ref, nb, tt)

        for h in range(N_TAIL_SPLITS):
            rows = slice(h * m // N_TAIL_SPLITS, (h + 1) * m // N_TAIL_SPLITS)
            y_ssm = y_cs[rows] + d_ref[...] * u_ssm[rows]
            glu = _dot(jax.nn.gelu(y_ssm).astype(BF16), wglu_ref[...])
            br_ssm = glu[:, :D_MODEL] * _two_sigmoid(glu[:, D_MODEL:])
            merged2 = _two_sigmoid(g_ssm[rows]) * br_ssm + _two_sigmoid(g_pool[rows]) * br_pool[rows]
            o = _dot(merged2.astype(BF16), wout_ref[...])
            if time_major:
                xo_ref[rows, :] = x[rows] + _rows_gate(o, gate1, tt // N_TAIL_SPLITS, nb, True)
            else:
                nb_h = nb // N_TAIL_SPLITS
                seqs = slice(h * nb_h, (h + 1) * nb_h)
                x1 = x[rows] + _rows_gate(o, gate1[seqs], tt, nb_h, False)
                xo_ref[seqs, t_rows, :] = x1.reshape(nb_h, tt, D_MODEL)

    @pl.when(i == n_steps - 1)
    def _():
        sr_out_ref[...] = st_r[...]
        si_out_ref[...] = st_i[...]
        if time_major:
            pool_out_ref[...] = hist_ref[new_buf_rows]
        else:
            pool_out_ref[...] = jnp.concatenate(
                [hist_ref[k, :, tt + HIST_PAD - POOL_HIST:tt + HIST_PAD, :] for k in range(len(POOL_WINDOWS))],
                axis=-1)


def _const_spec(shape):
    nd = len(shape)
    return pl.BlockSpec(shape, lambda i, _nd=nd: (0,) * _nd, pipeline_mode=pl.Buffered(1))


def _row_spec(nb, tt, time_major):
    if time_major:
        return pl.BlockSpec((tt * nb, D_MODEL), lambda i: (i, 0))
    return pl.BlockSpec((nb, tt, D_MODEL), lambda i: (0, i, 0))


def _mixer(x, mod, h0r, h0i, pool0, n1, win, wsi, wso, wt, alr, ali, d, wglu, wpool, pscale, wout,
           *, nb, tt, n_sub, n_steps, pos0, time_major, cast=()):
    m = tt * nb
    consts = (mod, h0r, h0i, pool0, n1, win, wsi, wso, wt, alr, ali, d, wglu, wpool, pscale, wout)
    cast, cast_halved = tuple(a for a, _ in cast), tuple(n for _, n in cast)
    cast_specs = _cast_specs(cast, n_steps)
    if time_major:
        hist_shape = (POOL_HIST * nb + m, D_POOL)
        pool_out_shape = (POOL_HIST * nb, D_POOL)
    else:
        hist_shape = (len(POOL_WINDOWS), nb, HIST_PAD + tt, POOL_GROUP)
        pool_out_shape = (nb, POOL_HIST, D_POOL)
    kern = functools.partial(_mixer_kernel, nb=nb, tt=tt, n_sub=n_sub, n_steps=n_steps, pos0=pos0,
                             time_major=time_major, cast_halved=cast_halved)
    return pl.pallas_call(
        kern,
        grid=(n_steps,),
        in_specs=[_row_spec(nb, n_sub * tt, time_major)] + [_const_spec(a.shape) for a in consts] + cast_specs,
        out_specs=(_row_spec(nb, n_sub * tt, time_major),
                   pl.BlockSpec((nb, N_STATE), lambda i: (0, 0)),
                   pl.BlockSpec((nb, N_STATE), lambda i: (0, 0)),
                   pl.BlockSpec(pool_out_shape, lambda i: (0,) * len(pool_out_shape))) + tuple(cast_specs),
        out_shape=(jax.ShapeDtypeStruct(x.shape, F32),
                   jax.ShapeDtypeStruct((nb, N_STATE), F32),
                   jax.ShapeDtypeStruct((nb, N_STATE), F32),
                   jax.ShapeDtypeStruct(pool_out_shape, F32)) + tuple(jax.ShapeDtypeStruct(a.shape, BF16) for a in cast),
        scratch_shapes=[pltpu.VMEM((m // BLOCK_STEPS, 2 * N_STATE), F32),
                        pltpu.VMEM((nb, N_STATE), F32), pltpu.VMEM((nb, N_STATE), F32),
                        pltpu.VMEM(hist_shape, F32),
                        pltpu.VMEM((D_SSM // LANES, m, LANES), F32)],
        compiler_params=pltpu.CompilerParams(dimension_semantics=("arbitrary",),
                                             vmem_limit_bytes=VMEM_LIMIT),
        name=f"mixer_nb{nb}",
    )(x, *consts, *cast)


def _ffn_kernel(x_ref, mod_ref, n2_ref, nf_ref, wfi_ref, wfo_ref, o_ref, *, nb, tt, time_major):
    for h in range(N_FFN_SPLITS):
        if time_major:
            tt_h, nb_h = tt // N_FFN_SPLITS, nb
            piece = (slice(h * tt_h * nb, (h + 1) * tt_h * nb), slice(None))
            mod = mod_ref
        else:
            tt_h, nb_h = tt, nb // N_FFN_SPLITS
            piece = (slice(h * nb_h, (h + 1) * nb_h), slice(None), slice(None))
            mod = mod_ref.at[h * nb_h:(h + 1) * nb_h]
        x = x_ref[piece].reshape(tt_h * nb_h, D_MODEL)
        shift2 = mod[:, :, 3 * D_MODEL:4 * D_MODEL]
        scale2 = mod[:, :, 4 * D_MODEL:5 * D_MODEL]
        gate2 = mod[:, :, 5 * D_MODEL:6 * D_MODEL]
        hb = _modulate(_rmsnorm(x, n2_ref[...]), shift2, scale2, tt_h, nb_h, time_major).astype(BF16)
        f_a = _dot(hb, wfi_ref[:, :D_FF])
        f_b = _dot(hb, wfi_ref[:, D_FF:])
        act = (f_a * _two_sigmoid(f_a) * f_b).astype(BF16)
        x2 = x + _rows_gate(_dot(act, wfo_ref[...]), gate2, tt_h, nb_h, time_major)
        y = _rmsnorm(x2, nf_ref[...])
        o_ref[piece] = y if time_major else y.reshape(nb_h, tt_h, D_MODEL)


def _ffn(x, mod, n2, nf, wfi, wfo, *, nb, tt, n_steps, time_major):
    consts = (mod, n2, nf, wfi, wfo)
    return pl.pallas_call(
        functools.partial(_ffn_kernel, nb=nb, tt=tt, time_major=time_major),
        grid=(n_steps,),
        in_specs=[_row_spec(nb, tt, time_major)] + [_const_spec(a.shape) for a in consts],
        out_specs=_row_spec(nb, tt, time_major),
        out_shape=jax.ShapeDtypeStruct(x.shape, F32),
        compiler_params=pltpu.CompilerParams(dimension_semantics=("arbitrary",),
                                             vmem_limit_bytes=VMEM_LIMIT),
        name=f"ffn_nb{nb}",
    )(x, *consts)


def _trunk(x, mod, h0r, h0i, pool0, pos0, p, *, time_major, ffn_w):
    nb, t_len, _ = x.shape
    tt = min(t_len, TILE_ROWS // nb)
    n_sub = 1 if time_major else MIXER_SUBTILES
    tt_ffn = min(t_len, FFN_TILE_ROWS // nb)
    if time_major:
        rows = lambda a: jnp.transpose(a, (1, 0, 2)).reshape(a.shape[1] * nb, a.shape[2])
        unrows = lambda a, t: jnp.transpose(a.reshape(t, nb, a.shape[-1]), (1, 0, 2))
        x_in, pool_in, mod3 = rows(x), rows(pool0), mod[None]
    else:
        assert nb == SUBLANES
        x_in, mod3 = x, mod[:, None, :]
        pool_in = jnp.pad(pool0, ((0, 0), (HIST_PAD - POOL_HIST, 0), (0, 0)))
    cast = ((ffn_w[0], (0, D_FF)), (ffn_w[1], (0, 0))) if ffn_w[0].dtype == F32 else ()
    x1, s_r, s_i, new_buf, *cast_out = _mixer(
        x_in, mod3, h0r.reshape(nb, N_STATE), h0i.reshape(nb, N_STATE), pool_in,
        p["n1"], p["win"], p["wsi"], p["wso"], p["wt"], p["alr"], p["ali"], p["d"], p["wglu"], p["wpool"],
        p["pscale"], p["wout"], nb=nb, tt=tt, n_sub=n_sub, n_steps=t_len // (tt * n_sub), pos0=pos0,
        time_major=time_major, cast=cast)
    wfi, wfo = cast_out if cast else ffn_w
    y = _ffn(x1, mod3, p["n2"], p["nf"], wfi, wfo, nb=nb, tt=tt_ffn, n_steps=t_len // tt_ffn,
             time_major=time_major)
    if time_major:
        y, new_buf = unrows(y, t_len), unrows(new_buf, POOL_HIST)
    shape_s = (1, nb, N_SSM_GROUPS, SSM_STATE)
    return (y, s_r.reshape(shape_s), s_i.reshape(shape_s), new_buf[None]), (wfi, wfo)


def kernel(x_prompt, x_sample, c_prompt, c_sample, state_ssm_re, state_ssm_im, state_pool, norm1_g, norm2_g, normf_g, w_ada, b_ada, w_in, ssm_lam_re, ssm_lam_im, ssm_log_dt, ssm_b_re, ssm_b_im, ssm_c_re, ssm_c_im, ssm_d, w_glu, w_pool, pool_scale, w_out, w_ffn_in, w_ffn_out):
    n_prompt = x_prompt.shape[0]
    mod_p, mod_s, win, wglu, wout, wpool = _ada(
        c_prompt, c_sample, w_ada[0], b_ada[0],
        cast=((w_in[0], (D_SSM + D_POOL, w_in.shape[-1])), (w_glu[0], (0, 2 * D_MODEL)),
              (w_out[0], (0, D_MODEL)), (w_pool[0].reshape(D_POOL, POOL_OUT), (0, 0))))
    alr, ali, wsi, wso, wt = _ssm_prep(ssm_lam_re[0], ssm_lam_im[0], ssm_log_dt[0], ssm_b_re[0], ssm_b_im[0],
                                       ssm_c_re[0], ssm_c_im[0])
    p = dict(
        n1=norm1_g[0].reshape(1, -1), n2=norm2_g[0].reshape(1, -1), nf=normf_g.reshape(1, -1),
        win=win, wsi=wsi, wso=wso, wt=wt, alr=alr, ali=ali, d=ssm_d[0].reshape(1, -1),
        wglu=wglu, wpool=wpool.reshape(len(POOL_WINDOWS), POOL_GROUP, POOL_OUT), pscale=pool_scale[0].reshape(1, -1),
        wout=wout)
    zero_state = jnp.zeros((n_prompt, N_SSM_GROUPS, SSM_STATE), F32)
    zero_pool = jnp.zeros((n_prompt, POOL_HIST, D_POOL), F32)
    ffn_f32 = (w_ffn_in[0], w_ffn_out[0])
    (y_p, p_re, p_im, p_pool), ffn_bf16 = _trunk(x_prompt, mod_p, zero_state, zero_state, zero_pool, 0, p,
                                                  time_major=False, ffn_w=ffn_f32)
    (y_s, s_re, s_im, s_pool), _ = _trunk(x_sample, mod_s, state_ssm_re[0], state_ssm_im[0], state_pool[0],
                                          PAST_LEN, p, time_major=True, ffn_w=ffn_bf16)
    return (y_p, y_s, p_re, p_im, p_pool, s_re, s_im, s_pool)
```

```python
import functools

import jax
import jax.numpy as jnp
from jax import lax
from jax.experimental import pallas as pl
from jax.experimental.pallas import tpu as pltpu

D_MODEL = 1024
D_SSM = 512
SSM_GROUP = 16
N_SSM_GROUPS = 32
SSM_STATE = 64
N_STATE = N_SSM_GROUPS * SSM_STATE
D_POOL = 512
POOL_WINDOWS = (2, 4, 8, 16)
POOL_GROUP = 128
POOL_OUT = 256
POOL_HIST = 15
D_FF = 2816
N_MOD = 6
EPS = 1e-6
PAST_LEN = 16384

SUBLANES = 8
BF16_SUBLANES = 16
LANES = 128
TILE_ROWS = 512
MIXER_SUBTILES = 2
FFN_TILE_ROWS = 1024
HIST_PAD = 16
N_TAIL_SPLITS = 2
N_FFN_SPLITS = 4
ADA_ROW_BLOCKS = 4
MXU_DEPTH = 256
BLOCK_STEPS = 4
SET_GROUPS = MXU_DEPTH // (BLOCK_STEPS * SSM_GROUP)
N_SETS = N_SSM_GROUPS // SET_GROUPS
SET_IN = SET_GROUPS * SSM_GROUP
SET_STATES = SET_GROUPS * SSM_STATE
VMEM_LIMIT = 60 * 1024 * 1024

BF16 = jnp.bfloat16
F32 = jnp.float32


def _dot(a, b):
    return jnp.dot(a, b, preferred_element_type=F32)


def _sigmoid(x):
    return 0.5 * jnp.tanh(0.5 * x) + 0.5


def _two_sigmoid(half_x):
    return jnp.tanh(half_x) + 1.0


def _rmsnorm(x, g):
    return x * lax.rsqrt(jnp.mean(x * x, axis=-1, keepdims=True) + EPS) * g


def _tile3(v, tt, nb, time_major):
    lead, inner = (tt, nb) if time_major else (nb, tt)
    return v.reshape(lead, inner, v.shape[-1])


def _modulate(h, shift, scale, tt, nb, time_major):
    return (_tile3(h, tt, nb, time_major) * (1.0 + scale) + shift).reshape(h.shape)


def _rows_gate(v, gate, tt, nb, time_major):
    return (_tile3(v, tt, nb, time_major) * gate).reshape(v.shape)


def _cast_specs(arrays, n_steps):
    specs = []
    for a in arrays:
        n_blk = max(k for k in range(1, n_steps + 1)
                    if n_steps % k == 0 and a.shape[0] % (k * BF16_SUBLANES) == 0)
        specs.append(pl.BlockSpec((a.shape[0] // n_blk, a.shape[1]),
                                  lambda i, _hold=n_steps // n_blk: (i // _hold, 0)))
    return specs


def _cast_blocks(srcs, dsts, halved):
    for src, dst, (lo, hi) in zip(srcs, dsts, halved):
        if lo > 0:
            dst[:, :lo] = src[:, :lo].astype(BF16)
        if hi > lo:
            dst[:, lo:hi] = (src[:, lo:hi] * 0.5).astype(BF16)
        if hi < src.shape[-1]:
            dst[:, hi:] = src[:, hi:].astype(BF16)


def _ada_kernel(cp_ref, cs_ref, w_ref, b_ref, *rest, halved):
    n_cast = len(halved)
    cast_src, (op_ref, os_ref), cast_dst = rest[:n_cast], rest[n_cast:n_cast + 2], rest[n_cast + 2:]
    _cast_blocks(cast_src, cast_dst, halved)
    k = pl.program_id(0)
    w = w_ref[...].astype(BF16)
    for c_ref, o_ref in ((cp_ref, op_ref), (cs_ref, os_ref)):
        c = c_ref[...]
        part = _dot((c * _sigmoid(c)).astype(BF16), w)

        @pl.when(k == 0)
        def _(o_ref=o_ref, part=part):
            o_ref[...] = b_ref[...] + part

        @pl.when(k > 0)
        def _(o_ref=o_ref, part=part):
            o_ref[...] += part


def _ada(c_prompt, c_sample, w_ada, b_ada, cast):
    n_p, n_s = c_prompt.shape[0], c_sample.shape[0]
    kb = D_MODEL // ADA_ROW_BLOCKS
    arrays, halved = tuple(a for a, _ in cast), tuple(h for _, h in cast)
    cast_specs = _cast_specs(arrays, ADA_ROW_BLOCKS)
    n_out = N_MOD * D_MODEL
    return pl.pallas_call(
        functools.partial(_ada_kernel, halved=halved),
        grid=(ADA_ROW_BLOCKS,),
        in_specs=[pl.BlockSpec((n_p, kb), lambda k: (0, k)),
                  pl.BlockSpec((n_s, kb), lambda k: (0, k)),
                  pl.BlockSpec((kb, n_out), lambda k: (k, 0)),
                  pl.BlockSpec((1, n_out), lambda k: (0, 0))] + cast_specs,
        out_specs=[pl.BlockSpec((n_p, n_out), lambda k: (0, 0)), pl.BlockSpec((n_s, n_out), lambda k: (0, 0))]
        + cast_specs,
        out_shape=[jax.ShapeDtypeStruct((n_p, n_out), F32), jax.ShapeDtypeStruct((n_s, n_out), F32)]
        + [jax.ShapeDtypeStruct(a.shape, BF16) for a in arrays],
        compiler_params=pltpu.CompilerParams(dimension_semantics=("arbitrary",),
                                             vmem_limit_bytes=VMEM_LIMIT),
        name="ada_mod",
    )(c_prompt, c_sample, w_ada, b_ada.reshape(1, -1), *arrays)


def _cmul(ar, ai, br, bi):
    return ar * br - ai * bi, ar * bi + ai * br


def _ssm_prep_kernel(lr_ref, li_ref, ldt_ref, bt_r_ref, bt_i_ref, cr_ref, ci_ref,
                     alr_ref, ali_ref, wsi_ref, wso_ref, wt_ref):
    lr = lr_ref[...]
    li = li_ref[...]
    dt = jnp.exp(ldt_ref[...])
    mag = jnp.exp(lr * dt)
    a_r = mag * jnp.cos(li * dt)
    a_i = mag * jnp.sin(li * dt)
    den = lr * lr + li * li
    nr = a_r - 1.0
    ni = a_i
    f = ((nr * lr + ni * li) / den, (ni * lr - nr * li) / den)
    powers = [(a_r, a_i)]
    for _ in range(BLOCK_STEPS - 1):
        powers.append(_cmul(a_r, a_i, *powers[-1]))
    for g in range(N_SSM_GROUPS):
        states = slice(g * SSM_STATE, (g + 1) * SSM_STATE)
        alr_ref[:, states] = powers[-1][0][g:g + 1, :]
        ali_ref[:, states] = powers[-1][1][g:g + 1, :]
    wsi_ref[...] = jnp.zeros(wsi_ref.shape, BF16)
    wso_ref[...] = jnp.zeros(wso_ref.shape, BF16)
    wt_ref[...] = jnp.zeros(wt_ref.shape, BF16)
    row_group = lax.broadcasted_iota(jnp.int32, (SET_IN, SET_IN), 0) // SSM_GROUP
    col_group = lax.broadcasted_iota(jnp.int32, (SET_IN, SET_IN), 1) // SSM_GROUP
    same_group = row_group == col_group
    nt_dims = (((1,), (1,)), ((), ()))
    for s in range(N_SETS):
        groups = slice(s * SET_GROUPS, (s + 1) * SET_GROUPS)

        def rows_of(pair, groups=groups):
            return tuple(jnp.broadcast_to(v[groups][:, None, :], (SET_GROUPS, SSM_GROUP, SSM_STATE))
                         .reshape(SET_IN, SSM_STATE) for v in pair)

        bt = (bt_r_ref[groups].reshape(SET_IN, SSM_STATE), bt_i_ref[groups].reshape(SET_IN, SSM_STATE))
        c = (cr_ref[groups].reshape(SET_IN, SSM_STATE), ci_ref[groups].reshape(SET_IN, SSM_STATE))
        a_rows = rows_of(powers[0])
        bb_lags = [_cmul(*rows_of(f), *bt)]
        for _ in range(BLOCK_STEPS - 1):
            bb_lags.append(_cmul(*a_rows, *bb_lags[-1]))
        for j in range(BLOCK_STEPS):
            lag = BLOCK_STEPS - 1 - j
            bb_r, bb_i = bb_lags[lag]
            k_t = (lax.dot_general(bb_r, c[0], nt_dims, precision=lax.Precision.HIGHEST, preferred_element_type=F32)
                   - lax.dot_general(bb_i, c[1], nt_dims, precision=lax.Precision.HIGHEST, preferred_element_type=F32))
            k_t = jnp.where(same_group, k_t, 0.0).astype(BF16)
            for t in range(lag, BLOCK_STEPS):
                wt_ref[s, (t - lag) * SET_IN:(t - lag + 1) * SET_IN, t * SET_IN:(t + 1) * SET_IN] = k_t
            for gj in range(SET_GROUPS):
                src = slice(gj * SSM_GROUP, (gj + 1) * SSM_GROUP)
                rows = slice(j * SET_IN + gj * SSM_GROUP, j * SET_IN + (gj + 1) * SSM_GROUP)
                wsi_ref[s, rows, gj * SSM_STATE:(gj + 1) * SSM_STATE] = bb_r[src, :].astype(BF16)
                wsi_ref[s, rows, SET_STATES + gj * SSM_STATE:SET_STATES + (gj + 1) * SSM_STATE] = (
                    bb_i[src, :].astype(BF16))
        for t in range(BLOCK_STEPS):
            p_r, p_i = rows_of(powers[t])
            from_sr = jnp.transpose(c[0] * p_r - c[1] * p_i).astype(BF16)
            from_si = jnp.transpose(-c[0] * p_i - c[1] * p_r).astype(BF16)
            for gj in range(SET_GROUPS):
                src = slice(gj * SSM_GROUP, (gj + 1) * SSM_GROUP)
                out = slice(t * SET_IN + gj * SSM_GROUP, t * SET_IN + (gj + 1) * SSM_GROUP)
                wso_ref[s, gj * SSM_STATE:(gj + 1) * SSM_STATE, out] = from_sr[:, src]
                wso_ref[s, SET_STATES + gj * SSM_STATE:SET_STATES + (gj + 1) * SSM_STATE, out] = from_si[:, src]


def _ssm_prep(lam_re, lam_im, log_dt, b_re, b_im, c_re, c_im):
    return pl.pallas_call(
        _ssm_prep_kernel,
        out_shape=(jax.ShapeDtypeStruct((1, N_STATE), F32), jax.ShapeDtypeStruct((1, N_STATE), F32),
                   jax.ShapeDtypeStruct((N_SETS, BLOCK_STEPS * SET_IN, 2 * SET_STATES), BF16),
                   jax.ShapeDtypeStruct((N_SETS, 2 * SET_STATES, BLOCK_STEPS * SET_IN), BF16),
                   jax.ShapeDtypeStruct((N_SETS, BLOCK_STEPS * SET_IN, BLOCK_STEPS * SET_IN), BF16)),
        name="ssm_prep",
    )(lam_re, lam_im, log_dt.reshape(N_SSM_GROUPS, 1), jnp.transpose(b_re, (0, 2, 1)), jnp.transpose(b_im, (0, 2, 1)),
      c_re, c_im)


def _block_scan(z_ref, st_r, st_i, alr_ref, ali_ref, nb, n_blocks):
    for s in range(N_SETS):
        lanes = slice(s * SET_STATES, (s + 1) * SET_STATES)
        re_lanes = slice(2 * s * SET_STATES, (2 * s + 1) * SET_STATES)
        im_lanes = slice((2 * s + 1) * SET_STATES, (2 * s + 2) * SET_STATES)
        ar = jnp.broadcast_to(alr_ref[:, lanes], (SUBLANES, SET_STATES))
        ai = jnp.broadcast_to(ali_ref[:, lanes], (SUBLANES, SET_STATES))

        def seq_block(rb, carry, lanes=lanes, re_lanes=re_lanes, im_lanes=im_lanes, ar=ar, ai=ai):
            r0 = pl.multiple_of(rb * SUBLANES, SUBLANES)
            sr = st_r[pl.ds(r0, SUBLANES), lanes]
            si = st_i[pl.ds(r0, SUBLANES), lanes]

            def step(k, state):
                sr, si = state
                rows = pl.ds(pl.multiple_of(k * nb + r0, SUBLANES), SUBLANES)
                zr = z_ref[rows, re_lanes]
                zi = z_ref[rows, im_lanes]
                z_ref[rows, re_lanes] = sr
                z_ref[rows, im_lanes] = si
                return ar * sr - ai * si + zr, ar * si + ai * sr + zi

            sr, si = lax.fori_loop(0, n_blocks, step, (sr, si), unroll=True)
            st_r[pl.ds(r0, SUBLANES), lanes] = sr
            st_i[pl.ds(r0, SUBLANES), lanes] = si
            return carry

        if nb == SUBLANES:
            seq_block(0, 0)
        else:
            lax.fori_loop(0, nb // SUBLANES, seq_block, 0)


def _to_time_major(v, perm_ref, nb, tt):
    for c in range(v.shape[-1] // LANES):
        for b in range(nb):
            perm_ref.at[c][pl.ds(b, tt, stride=nb), :] = v[b * tt:(b + 1) * tt, c * LANES:(c + 1) * LANES]
    return jnp.concatenate([perm_ref[c] for c in range(v.shape[-1] // LANES)], axis=-1)


def _to_seq_major(v, perm_ref, nb, tt):
    n_slabs = v.shape[-1] // LANES
    for c in range(n_slabs):
        perm_ref[c] = v[:, c * LANES:(c + 1) * LANES]
    return jnp.concatenate(
        [jnp.concatenate([perm_ref.at[c][pl.ds(b, tt, stride=nb), :] for b in range(nb)], axis=0)
         for c in range(n_slabs)], axis=-1)


def _pool_time_major(u_pool, hist_ref, i, nb, tt, pos0):
    m = tt * nb
    hist_rows = POOL_HIST * nb
    hist_ref[hist_rows:hist_rows + m, :] = u_pool
    t_abs = pos0 + i * tt + lax.broadcasted_iota(jnp.int32, (tt, nb, POOL_GROUP), 0).reshape(m, POOL_GROUP)
    pooled = []
    for k, w in enumerate(POOL_WINDOWS):
        cols = slice(k * POOL_GROUP, (k + 1) * POOL_GROUP)
        acc = u_pool[:, cols]
        for j in range(1, w):
            acc = acc + hist_ref[hist_rows - j * nb:hist_rows - j * nb + m, cols]
        count = jnp.minimum(t_abs + 1, w).astype(F32)
        pooled.append(acc / count - u_pool[:, cols])
    return pooled


def _pool_seq_major(u_pool, hist_ref, i, nb, tt, pos0):
    m = tt * nb
    t_abs = pos0 + i * tt + lax.broadcasted_iota(jnp.int32, (nb, tt, POOL_GROUP), 1)
    pooled = []
    for k, w in enumerate(POOL_WINDOWS):
        cur = u_pool[:, k * POOL_GROUP:(k + 1) * POOL_GROUP].reshape(nb, tt, POOL_GROUP)
        hist_ref[k, :, HIST_PAD:HIST_PAD + tt, :] = cur
        acc = cur
        for j in range(1, w):
            acc = acc + hist_ref[k, :, HIST_PAD - j:HIST_PAD - j + tt, :]
        count = jnp.minimum(t_abs + 1, w).astype(F32)
        pooled.append((acc / count - cur).reshape(m, POOL_GROUP))
    return pooled


def _mixer_kernel(x_ref, mod_ref, h0r_ref, h0i_ref, pool0_ref, n1_ref, win_ref, wsi_ref, wso_ref, wt_ref,
                  alr_ref, ali_ref, d_ref, wglu_ref, wpool_ref, pscale_ref, wout_ref, *rest,
                  nb, tt, n_sub, n_steps, pos0, time_major, cast_halved):
    n_cast = len(cast_halved)
    cast_src, rest = rest[:n_cast], rest[n_cast:]
    xo_ref, sr_out_ref, si_out_ref, pool_out_ref = rest[:4]
    cast_dst, rest = rest[4:4 + n_cast], rest[4 + n_cast:]
    z_ref, st_r, st_i, hist_ref, perm_ref = rest
    n_blocks = tt // BLOCK_STEPS
    mb = n_blocks * nb
    i = pl.program_id(0)
    _cast_blocks(cast_src, cast_dst, cast_halved)
    m = tt * nb
    hist_rows = POOL_HIST * nb

    @pl.when(i == 0)
    def _():
        st_r[...] = h0r_ref[...]
        st_i[...] = h0i_ref[...]
        if time_major:
            hist_ref[0:hist_rows, :] = pool0_ref[...]
        else:
            for k in range(len(POOL_WINDOWS)):
                hist_ref[k, :, 0:HIST_PAD, :] = pool0_ref[:, :, k * POOL_GROUP:(k + 1) * POOL_GROUP]

    for sub in range(n_sub):
        tile = i * n_sub + sub
        t_rows = slice(sub * tt, (sub + 1) * tt)
        x = (x_ref[...] if time_major else x_ref[:, t_rows, :]).reshape(m, D_MODEL)
        shift1 = mod_ref[:, :, 0 * D_MODEL:1 * D_MODEL]
        scale1 = mod_ref[:, :, 1 * D_MODEL:2 * D_MODEL]
        gate1 = mod_ref[:, :, 2 * D_MODEL:3 * D_MODEL]
        hb = _modulate(_rmsnorm(x, n1_ref[...]), shift1, scale1, tt, nb, time_major).astype(BF16)

        u_ssm = _dot(hb, win_ref[:, 0:D_SSM])
        u_pool = _dot(hb, win_ref[:, D_SSM:D_SSM + D_POOL])
        u_tm = u_ssm if time_major else _to_time_major(u_ssm, perm_ref, nb, tt)
        u_blocks = u_tm.reshape(n_blocks, BLOCK_STEPS, nb, D_SSM)
        u_steps = [u_blocks[:, j].reshape(mb, D_SSM).astype(BF16) for j in range(BLOCK_STEPS)]
        y_sets = []
        for s in range(N_SETS):
            chans = slice(s * SET_IN, (s + 1) * SET_IN)
            lhs = jnp.concatenate([u[:, chans] for u in u_steps], axis=1)
            z_ref[:, 2 * s * SET_STATES:2 * (s + 1) * SET_STATES] = _dot(lhs, wsi_ref[s])
            y_sets.append(_dot(lhs, wt_ref[s]))
        g_ssm = _dot(hb, win_ref[:, D_SSM + D_POOL:D_SSM + D_POOL + D_MODEL])
        g_pool = _dot(hb, win_ref[:, D_SSM + D_POOL + D_MODEL:])

        _block_scan(z_ref, st_r, st_i, alr_ref, ali_ref, nb, n_blocks)
        if time_major:
            pooled = _pool_time_major(u_pool, hist_ref, tile, nb, tt, pos0)
            new_buf_rows = (slice(m, m + hist_rows), slice(None))
            if n_steps * n_sub > 1:
                hist_ref[0:hist_rows, :] = hist_ref[new_buf_rows]
        else:
            pooled = _pool_seq_major(u_pool, hist_ref, tile, nb, tt, pos0)
            if n_steps * n_sub > 1:
                hist_ref[:, :, 0:HIST_PAD, :] = hist_ref[:, :, tt:tt + HIST_PAD, :]
        z_pieces = [_dot(pooled[k].astype(BF16), wpool_ref[k]) for k in range(len(POOL_WINDOWS))]
        br_pool = jnp.concatenate(z_pieces, axis=-1) * pscale_ref[...]

        for s in range(N_SETS):
            s_in = z_ref[:, 2 * s * SET_STATES:2 * (s + 1) * SET_STATES].astype(BF16)
            y_sets[s] = y_sets[s] + _dot(s_in, wso_ref[s])
        y_steps = [jnp.concatenate([y[:, j * SET_IN:(j + 1) * SET_IN] for y in y_sets], axis=1).reshape(n_blocks, nb, D_SSM)
                   for j in range(BLOCK_STEPS)]
        y_cs = jnp.stack(y_steps, axis=1).reshape(m, D_SSM)
        if not time_major:
            y_cs = _to_seq_major(y_cs, perm_ref, nb, tt)

        for h in range(N_TAIL_SPLITS):
            rows = slice(h * m // N_TAIL_SPLITS, (h + 1) * m // N_TAIL_SPLITS)
            y_ssm = y_cs[rows] + d_ref[...] * u_ssm[rows]
            glu = _dot(jax.nn.gelu(y_ssm).astype(BF16), wglu_ref[...])
            br_ssm = glu[:, :D_MODEL] * _two_sigmoid(glu[:, D_MODEL:])
            merged2 = _two_sigmoid(g_ssm[rows]) * br_ssm + _two_sigmoid(g_pool[rows]) * br_pool[rows]
            o = _dot(merged2.astype(BF16), wout_ref[...])
            if time_major:
                xo_ref[rows, :] = x[rows] + _rows_gate(o, gate1, tt // N_TAIL_SPLITS, nb, True)
            else:
                nb_h = nb // N_TAIL_SPLITS
                seqs = slice(h * nb_h, (h + 1) * nb_h)
                x1 = x[rows] + _rows_gate(o, gate1[seqs], tt, nb_h, False)
                xo_ref[seqs, t_rows, :] = x1.reshape(nb_h, tt, D_MODEL)

    @pl.when(i == n_steps - 1)
    def _():
        sr_out_ref[...] = st_r[...]
        si_out_ref[...] = st_i[...]
        if time_major:
            pool_out_ref[...] = hist_ref[new_buf_rows]
        else:
            pool_out_ref[...] = jnp.concatenate(
                [hist_ref[k, :, tt + HIST_PAD - POOL_HIST:tt + HIST_PAD, :] for k in range(len(POOL_WINDOWS))],
                axis=-1)


def _const_spec(shape):
    nd = len(shape)
    return pl.BlockSpec(shape, lambda i, _nd=nd: (0,) * _nd, pipeline_mode=pl.Buffered(1))


def _row_spec(nb, tt, time_major):
    if time_major:
        return pl.BlockSpec((tt * nb, D_MODEL), lambda i: (i, 0))
    return pl.BlockSpec((nb, tt, D_MODEL), lambda i: (0, i, 0))


def _mixer(x, mod, h0r, h0i, pool0, n1, win, wsi, wso, wt, alr, ali, d, wglu, wpool, pscale, wout,
           *, nb, tt, n_sub, n_steps, pos0, time_major, cast=()):
    m = tt * nb
    consts = (mod, h0r, h0i, pool0, n1, win, wsi, wso, wt, alr, ali, d, wglu, wpool, pscale, wout)
    cast, cast_halved = tuple(a for a, _ in cast), tuple(n for _, n in cast)
    cast_specs = _cast_specs(cast, n_steps)
    if time_major:
        hist_shape = (POOL_HIST * nb + m, D_POOL)
        pool_out_shape = (POOL_HIST * nb, D_POOL)
    else:
        hist_shape = (len(POOL_WINDOWS), nb, HIST_PAD + tt, POOL_GROUP)
        pool_out_shape = (nb, POOL_HIST, D_POOL)
    kern = functools.partial(_mixer_kernel, nb=nb, tt=tt, n_sub=n_sub, n_steps=n_steps, pos0=pos0,
                             time_major=time_major, cast_halved=cast_halved)
    return pl.pallas_call(
        kern,
        grid=(n_steps,),
        in_specs=[_row_spec(nb, n_sub * tt, time_major)] + [_const_spec(a.shape) for a in consts] + cast_specs,
        out_specs=(_row_spec(nb, n_sub * tt, time_major),
                   pl.BlockSpec((nb, N_STATE), lambda i: (0, 0)),
                   pl.BlockSpec((nb, N_STATE), lambda i: (0, 0)),
                   pl.BlockSpec(pool_out_shape, lambda i: (0,) * len(pool_out_shape))) + tuple(cast_specs),
        out_shape=(jax.ShapeDtypeStruct(x.shape, F32),
                   jax.ShapeDtypeStruct((nb, N_STATE), F32),
                   jax.ShapeDtypeStruct((nb, N_STATE), F32),
                   jax.ShapeDtypeStruct(pool_out_shape, F32)) + tuple(jax.ShapeDtypeStruct(a.shape, BF16) for a in cast),
        scratch_shapes=[pltpu.VMEM((m // BLOCK_STEPS, 2 * N_STATE), F32),
                        pltpu.VMEM((nb, N_STATE), F32), pltpu.VMEM((nb, N_STATE), F32),
                        pltpu.VMEM(hist_shape, F32),
                        pltpu.VMEM((D_SSM // LANES, m, LANES), F32)],
        compiler_params=pltpu.CompilerParams(dimension_semantics=("arbitrary",),
                                             vmem_limit_bytes=VMEM_LIMIT),
        name=f"mixer_nb{nb}",
    )(x, *consts, *cast)


def _ffn_kernel(x_ref, mod_ref, n2_ref, nf_ref, wfi_ref, wfo_ref, o_ref, *, nb, tt, time_major):
    for h in range(N_FFN_SPLITS):
        if time_major:
            tt_h, nb_h = tt // N_FFN_SPLITS, nb
            piece = (slice(h * tt_h * nb, (h + 1) * tt_h * nb), slice(None))
            mod = mod_ref
        else:
            tt_h, nb_h = tt, nb // N_FFN_SPLITS
            piece = (slice(h * nb_h, (h + 1) * nb_h), slice(None), slice(None))
            mod = mod_ref.at[h * nb_h:(h + 1) * nb_h]
        x = x_ref[piece].reshape(tt_h * nb_h, D_MODEL)
        shift2 = mod[:, :, 3 * D_MODEL:4 * D_MODEL]
        scale2 = mod[:, :, 4 * D_MODEL:5 * D_MODEL]
        gate2 = mod[:, :, 5 * D_MODEL:6 * D_MODEL]
        hb = _modulate(_rmsnorm(x, n2_ref[...]), shift2, scale2, tt_h, nb_h, time_major).astype(BF16)
        f_a = _dot(hb, wfi_ref[:, :D_FF])
        f_b = _dot(hb, wfi_ref[:, D_FF:])
        act = (f_a * _two_sigmoid(f_a) * f_b).astype(BF16)
        x2 = x + _rows_gate(_dot(act, wfo_ref[...]), gate2, tt_h, nb_h, time_major)
        y = _rmsnorm(x2, nf_ref[...])
        o_ref[piece] = y if time_major else y.reshape(nb_h, tt_h, D_MODEL)


def _ffn(x, mod, n2, nf, wfi, wfo, *, nb, tt, n_steps, time_major):
    consts = (mod, n2, nf, wfi, wfo)
    return pl.pallas_call(
        functools.partial(_ffn_kernel, nb=nb, tt=tt, time_major=time_major),
        grid=(n_steps,),
        in_specs=[_row_spec(nb, tt, time_major)] + [_const_spec(a.shape) for a in consts],
        out_specs=_row_spec(nb, tt, time_major),
        out_shape=jax.ShapeDtypeStruct(x.shape, F32),
        compiler_params=pltpu.CompilerParams(dimension_semantics=("arbitrary",),
                                             vmem_limit_bytes=VMEM_LIMIT),
        name=f"ffn_nb{nb}",
    )(x, *consts)


def _trunk(x, mod, h0r, h0i, pool0, pos0, p, *, time_major, ffn_w):
    nb, t_len, _ = x.shape
    tt = min(t_len, TILE_ROWS // nb)
    n_sub = 1 if time_major else MIXER_SUBTILES
    tt_ffn = min(t_len, FFN_TILE_ROWS // nb)
    if time_major:
        rows = lambda a: jnp.transpose(a, (1, 0, 2)).reshape(a.shape[1] * nb, a.shape[2])
        unrows = lambda a, t: jnp.transpose(a.reshape(t, nb, a.shape[-1]), (1, 0, 2))
        x_in, pool_in, mod3 = rows(x), rows(pool0), mod[None]
    else:
        assert nb == SUBLANES
        x_in, mod3 = x, mod[:, None, :]
        pool_in = jnp.pad(pool0, ((0, 0), (HIST_PAD - POOL_HIST, 0), (0, 0)))
    cast = ((ffn_w[0], (0, D_FF)), (ffn_w[1], (0, 0))) if ffn_w[0].dtype == F32 else ()
    x1, s_r, s_i, new_buf, *cast_out = _mixer(
        x_in, mod3, h0r.reshape(nb, N_STATE), h0i.reshape(nb, N_STATE), pool_in,
        p["n1"], p["win"], p["wsi"], p["wso"], p["wt"], p["alr"], p["ali"], p["d"], p["wglu"], p["wpool"],
        p["pscale"], p["wout"], nb=nb, tt=tt, n_sub=n_sub, n_steps=t_len // (tt * n_sub), pos0=pos0,
        time_major=time_major, cast=cast)
    wfi, wfo = cast_out if cast else ffn_w
    y = _ffn(x1, mod3, p["n2"], p["nf"], wfi, wfo, nb=nb, tt=tt_ffn, n_steps=t_len // tt_ffn,
             time_major=time_major)
    if time_major:
        y, new_buf = unrows(y, t_len), unrows(new_buf, POOL_HIST)
    shape_s = (1, nb, N_SSM_GROUPS, SSM_STATE)
    return (y, s_r.reshape(shape_s), s_i.reshape(shape_s), new_buf[None]), (wfi, wfo)


def kernel(x_prompt, x_sample, c_prompt, c_sample, state_ssm_re, state_ssm_im, state_pool, norm1_g, norm2_g, normf_g, w_ada, b_ada, w_in, ssm_lam_re, ssm_lam_im, ssm_log_dt, ssm_b_re, ssm_b_im, ssm_c_re, ssm_c_im, ssm_d, w_glu, w_pool, pool_scale, w_out, w_ffn_in, w_ffn_out):
    n_prompt = x_prompt.shape[0]
    mod_p, mod_s, win, wglu, wout, wpool = _ada(
        c_prompt, c_sample, w_ada[0], b_ada[0],
        cast=((w_in[0], (D_SSM + D_POOL, w_in.shape[-1])), (w_glu[0], (0, 2 * D_MODEL)),
              (w_out[0], (0, D_MODEL)), (w_pool[0].reshape(D_POOL, POOL_OUT), (0, 0))))
    alr, ali, wsi, wso, wt = _ssm_prep(ssm_lam_re[0], ssm_lam_im[0], ssm_log_dt[0], ssm_b_re[0], ssm_b_im[0],
                                       ssm_c_re[0], ssm_c_im[0])
    p = dict(
        n1=norm1_g[0].reshape(1, -1), n2=norm2_g[0].reshape(1, -1), nf=normf_g.reshape(1, -1),
        win=win, wsi=wsi, wso=wso, wt=wt, alr=alr, ali=ali, d=ssm_d[0].reshape(1, -1),
        wglu=wglu, wpool=wpool.reshape(len(POOL_WINDOWS), POOL_GROUP, POOL_OUT), pscale=pool_scale[0].reshape(1, -1),
        wout=wout)
    zero_state = jnp.zeros((n_prompt, N_SSM_GROUPS, SSM_STATE), F32)
    zero_pool = jnp.zeros((n_prompt, POOL_HIST, D_POOL), F32)
    ffn_f32 = (w_ffn_in[0], w_ffn_out[0])
    (y_p, p_re, p_im, p_pool), ffn_bf16 = _trunk(x_prompt, mod_p, zero_state, zero_state, zero_pool, 0, p,
                                                  time_major=False, ffn_w=ffn_f32)
    (y_s, s_re, s_im, s_pool), _ = _trunk(x_sample, mod_s, state_ssm_re[0], state_ssm_im[0], state_pool[0],
                                          PAST_LEN, p, time_major=True, ffn_w=ffn_bf16)
    return (y_p, y_s, p_re, p_im, p_pool, s_re, s_im, s_pool)
```

```python
import functools

import jax
import jax.numpy as jnp
from jax import lax
from jax.experimental import pallas as pl
from jax.experimental.pallas import tpu as pltpu

D_MODEL = 1024
D_SSM = 512
SSM_GROUP = 16
N_SSM_GROUPS = 32
SSM_STATE = 64
N_STATE = N_SSM_GROUPS * SSM_STATE
D_POOL = 512
POOL_WINDOWS = (2, 4, 8, 16)
POOL_GROUP = 128
POOL_OUT = 256
POOL_HIST = 15
D_FF = 2816
N_MOD = 6
EPS = 1e-6
PAST_LEN = 16384

SUBLANES = 8
BF16_SUBLANES = 16
LANES = 128
TILE_ROWS = 512
MIXER_SUBTILES = 2
FFN_TILE_ROWS = 1024
HIST_PAD = 16
N_TAIL_SPLITS = 2
N_FFN_SPLITS = 4
ADA_ROW_BLOCKS = 4
MXU_DEPTH = 256
BLOCK_STEPS = 4
SET_GROUPS = MXU_DEPTH // (BLOCK_STEPS * SSM_GROUP)
N_SETS = N_SSM_GROUPS // SET_GROUPS
SET_IN = SET_GROUPS * SSM_GROUP
SET_STATES = SET_GROUPS * SSM_STATE
VMEM_LIMIT = 60 * 1024 * 1024

BF16 = jnp.bfloat16
F32 = jnp.float32


def _dot(a, b):
    return jnp.dot(a, b, preferred_element_type=F32)


def _sigmoid(x):
    return 0.5 * jnp.tanh(0.5 * x) + 0.5


def _two_sigmoid(half_x):
    return jnp.tanh(half_x) + 1.0


def _rmsnorm(x, g):
    return x * lax.rsqrt(jnp.mean(x * x, axis=-1, keepdims=True) + EPS) * g


def _tile3(v, tt, nb, time_major):
    lead, inner = (tt, nb) if time_major else (nb, tt)
    return v.reshape(lead, inner, v.shape[-1])


def _modulate(h, shift, scale, tt, nb, time_major):
    return (_tile3(h, tt, nb, time_major) * (1.0 + scale) + shift).reshape(h.shape)


def _rows_gate(v, gate, tt, nb, time_major):
    return (_tile3(v, tt, nb, time_major) * gate).reshape(v.shape)


def _cast_specs(arrays, n_steps):
    specs = []
    for a in arrays:
        n_blk = max(k for k in range(1, n_steps + 1)
                    if n_steps % k == 0 and a.shape[0] % (k * BF16_SUBLANES) == 0)
        specs.append(pl.BlockSpec((a.shape[0] // n_blk, a.shape[1]),
                                  lambda i, _hold=n_steps // n_blk: (i // _hold, 0)))
    return specs


def _cast_blocks(srcs, dsts, halved):
    for src, dst, (lo, hi) in zip(srcs, dsts, halved):
        if lo > 0:
            dst[:, :lo] = src[:, :lo].astype(BF16)
        if hi > lo:
            dst[:, lo:hi] = (src[:, lo:hi] * 0.5).astype(BF16)
        if hi < src.shape[-1]:
            dst[:, hi:] = src[:, hi:].astype(BF16)


def _ada_kernel(cp_ref, cs_ref, w_ref, b_ref, *rest, halved):
    n_cast = len(halved)
    cast_src, (op_ref, os_ref), cast_dst = rest[:n_cast], rest[n_cast:n_cast + 2], rest[n_cast + 2:]
    _cast_blocks(cast_src, cast_dst, halved)
    k = pl.program_id(0)
    w = w_ref[...].astype(BF16)
    for c_ref, o_ref in ((cp_ref, op_ref), (cs_ref, os_ref)):
        c = c_ref[...]
        part = _dot((c * _sigmoid(c)).astype(BF16), w)

        @pl.when(k == 0)
        def _(o_ref=o_ref, part=part):
            o_ref[...] = b_ref[...] + part

        @pl.when(k > 0)
        def _(o_ref=o_ref, part=part):
            o_ref[...] += part


def _ada(c_prompt, c_sample, w_ada, b_ada, cast):
    n_p, n_s = c_prompt.shape[0], c_sample.shape[0]
    kb = D_MODEL // ADA_ROW_BLOCKS
    arrays, halved = tuple(a for a, _ in cast), tuple(h for _, h in cast)
    cast_specs = _cast_specs(arrays, ADA_ROW_BLOCKS)
    n_out = N_MOD * D_MODEL
    return pl.pallas_call(
        functools.partial(_ada_kernel, halved=halved),
        grid=(ADA_ROW_BLOCKS,),
        in_specs=[pl.BlockSpec((n_p, kb), lambda k: (0, k)),
                  pl.BlockSpec((n_s, kb), lambda k: (0, k)),
                  pl.BlockSpec((kb, n_out), lambda k: (k, 0)),
                  pl.BlockSpec((1, n_out), lambda k: (0, 0))] + cast_specs,
        out_specs=[pl.BlockSpec((n_p, n_out), lambda k: (0, 0)), pl.BlockSpec((n_s, n_out), lambda k: (0, 0))]
        + cast_specs,
        out_shape=[jax.ShapeDtypeStruct((n_p, n_out), F32), jax.ShapeDtypeStruct((n_s, n_out), F32)]
        + [jax.ShapeDtypeStruct(a.shape, BF16) for a in arrays],
        compiler_params=pltpu.CompilerParams(dimension_semantics=("arbitrary",),
                                             vmem_limit_bytes=VMEM_LIMIT),
        name="ada_mod",
    )(c_prompt, c_sample, w_ada, b_ada.reshape(1, -1), *arrays)


def _cmul(ar, ai, br, bi):
    return ar * br - ai * bi, ar * bi + ai * br


def _ssm_prep_kernel(lr_ref, li_ref, ldt_ref, bt_r_ref, bt_i_ref, cr_ref, ci_ref,
                     alr_ref, ali_ref, wsi_ref, wso_ref, wt_ref):
    lr = lr_ref[...]
    li = li_ref[...]
    dt = jnp.exp(ldt_ref[...])
    mag = jnp.exp(lr * dt)
    a_r = mag * jnp.cos(li * dt)
    a_i = mag * jnp.sin(li * dt)
    den = lr * lr + li * li
    nr = a_r - 1.0
    ni = a_i
    f = ((nr * lr + ni * li) / den, (ni * lr - nr * li) / den)
    powers = [(a_r, a_i)]
    for _ in range(BLOCK_STEPS - 1):
        powers.append(_cmul(a_r, a_i, *powers[-1]))
    for g in range(N_SSM_GROUPS):
        states = slice(g * SSM_STATE, (g + 1) * SSM_STATE)
        alr_ref[:, states] = powers[-1][0][g:g + 1, :]
        ali_ref[:, states] = powers[-1][1][g:g + 1, :]
    wsi_ref[...] = jnp.zeros(wsi_ref.shape, BF16)
    wso_ref[...] = jnp.zeros(wso_ref.shape, BF16)
    wt_ref[...] = jnp.zeros(wt_ref.shape, BF16)
    row_group = lax.broadcasted_iota(jnp.int32, (SET_IN, SET_IN), 0) // SSM_GROUP
    col_group = lax.broadcasted_iota(jnp.int32, (SET_IN, SET_IN), 1) // SSM_GROUP
    same_group = row_group == col_group
    nt_dims = (((1,), (1,)), ((), ()))
    for s in range(N_SETS):
        groups = slice(s * SET_GROUPS, (s + 1) * SET_GROUPS)

        def rows_of(pair, groups=groups):
            return tuple(jnp.broadcast_to(v[groups][:, None, :], (SET_GROUPS, SSM_GROUP, SSM_STATE))
                         .reshape(SET_IN, SSM_STATE) for v in pair)

        bt = (bt_r_ref[groups].reshape(SET_IN, SSM_STATE), bt_i_ref[groups].reshape(SET_IN, SSM_STATE))
        c = (cr_ref[groups].reshape(SET_IN, SSM_STATE), ci_ref[groups].reshape(SET_IN, SSM_STATE))
        a_rows = rows_of(powers[0])
        bb_lags = [_cmul(*rows_of(f), *bt)]
        for _ in range(BLOCK_STEPS - 1):
            bb_lags.append(_cmul(*a_rows, *bb_lags[-1]))
        for j in range(BLOCK_STEPS):
            lag = BLOCK_STEPS - 1 - j
            bb_r, bb_i = bb_lags[lag]
            k_t = (lax.dot_general(bb_r, c[0], nt_dims, precision=lax.Precision.HIGHEST, preferred_element_type=F32)
                   - lax.dot_general(bb_i, c[1], nt_dims, precision=lax.Precision.HIGHEST, preferred_element_type=F32))
            k_t = jnp.where(same_group, k_t, 0.0).astype(BF16)
            for t in range(lag, BLOCK_STEPS):
                wt_ref[s, (t - lag) * SET_IN:(t - lag + 1) * SET_IN, t * SET_IN:(t + 1) * SET_IN] = k_t
            for gj in range(SET_GROUPS):
                src = slice(gj * SSM_GROUP, (gj + 1) * SSM_GROUP)
                rows = slice(j * SET_IN + gj * SSM_GROUP, j * SET_IN + (gj + 1) * SSM_GROUP)
                wsi_ref[s, rows, gj * SSM_STATE:(gj + 1) * SSM_STATE] = bb_r[src, :].astype(BF16)
                wsi_ref[s, rows, SET_STATES + gj * SSM_STATE:SET_STATES + (gj + 1) * SSM_STATE] = (
                    bb_i[src, :].astype(BF16))
        for t in range(BLOCK_STEPS):
            p_r, p_i = rows_of(powers[t])
            from_sr = jnp.transpose(c[0] * p_r - c[1] * p_i).astype(BF16)
            from_si = jnp.transpose(-c[0] * p_i - c[1] * p_r).astype(BF16)
            for gj in range(SET_GROUPS):
                src = slice(gj * SSM_GROUP, (gj + 1) * SSM_GROUP)
                out = slice(t * SET_IN + gj * SSM_GROUP, t * SET_IN + (gj + 1) * SSM_GROUP)
                wso_ref[s, gj * SSM_STATE:(gj + 1) * SSM_STATE, out] = from_sr[:, src]
                wso_ref[s, SET_STATES + gj * SSM_STATE:SET_STATES + (gj + 1) * SSM_STATE, out] = from_si[:, src]


def _ssm_prep(lam_re, lam_im, log_dt, b_re, b_im, c_re, c_im):
    return pl.pallas_call(
        _ssm_prep_kernel,
        out_shape=(jax.ShapeDtypeStruct((1, N_STATE), F32), jax.ShapeDtypeStruct((1, N_STATE), F32),
                   jax.ShapeDtypeStruct((N_SETS, BLOCK_STEPS * SET_IN, 2 * SET_STATES), BF16),
                   jax.ShapeDtypeStruct((N_SETS, 2 * SET_STATES, BLOCK_STEPS * SET_IN), BF16),
                   jax.ShapeDtypeStruct((N_SETS, BLOCK_STEPS * SET_IN, BLOCK_STEPS * SET_IN), BF16)),
        name="ssm_prep",
    )(lam_re, lam_im, log_dt.reshape(N_SSM_GROUPS, 1), jnp.transpose(b_re, (0, 2, 1)), jnp.transpose(b_im, (0, 2, 1)),
      c_re, c_im)


def _block_scan(z_ref, st_r, st_i, alr_ref, ali_ref, nb, n_blocks):
    for s in range(N_SETS):
        lanes = slice(s * SET_STATES, (s + 1) * SET_STATES)
        re_lanes = slice(2 * s * SET_STATES, (2 * s + 1) * SET_STATES)
        im_lanes = slice((2 * s + 1) * SET_STATES, (2 * s + 2) * SET_STATES)
        ar = jnp.broadcast_to(alr_ref[:, lanes], (SUBLANES, SET_STATES))
        ai = jnp.broadcast_to(ali_ref[:, lanes], (SUBLANES, SET_STATES))

        def seq_block(rb, carry, lanes=lanes, re_lanes=re_lanes, im_lanes=im_lanes, ar=ar, ai=ai):
            r0 = pl.multiple_of(rb * SUBLANES, SUBLANES)
            sr = st_r[pl.ds(r0, SUBLANES), lanes]
            si = st_i[pl.ds(r0, SUBLANES), lanes]

            def step(k, state):
                sr, si = state
                rows = pl.ds(pl.multiple_of(k * nb + r0, SUBLANES), SUBLANES)
                zr = z_ref[rows, re_lanes]
                zi = z_ref[rows, im_lanes]
                z_ref[rows, re_lanes] = sr
                z_ref[rows, im_lanes] = si
                return ar * sr - ai * si + zr, ar * si + ai * sr + zi

            sr, si = lax.fori_loop(0, n_blocks, step, (sr, si), unroll=True)
            st_r[pl.ds(r0, SUBLANES), lanes] = sr
            st_i[pl.ds(r0, SUBLANES), lanes] = si
            return carry

        if nb == SUBLANES:
            seq_block(0, 0)
        else:
            lax.fori_loop(0, nb // SUBLANES, seq_block, 0)


def _to_time_major(v, perm_ref, nb, tt):
    for c in range(v.shape[-1] // LANES):
        for b in range(nb):
            perm_ref.at[c][pl.ds(b, tt, stride=nb), :] = v[b * tt:(b + 1) * tt, c * LANES:(c + 1) * LANES]
    return jnp.concatenate([perm_ref[c] for c in range(v.shape[-1] // LANES)], axis=-1)


def _to_seq_major(v, perm_ref, nb, tt):
    n_slabs = v.shape[-1] // LANES
    for c in range(n_slabs):
        perm_ref[c] = v[:, c * LANES:(c + 1) * LANES]
    return jnp.concatenate(
        [jnp.concatenate([perm_ref.at[c][pl.ds(b, tt, stride=nb), :] for b in range(nb)], axis=0)
         for c in range(n_slabs)], axis=-1)


def _pool_time_major(u_pool, hist_ref, i, nb, tt, pos0):
    m = tt * nb
    hist_rows = POOL_HIST * nb
    hist_ref[hist_rows:hist_rows + m, :] = u_pool
    t_abs = pos0 + i * tt + lax.broadcasted_iota(jnp.int32, (tt, nb, POOL_GROUP), 0).reshape(m, POOL_GROUP)
    pooled = []
    for k, w in enumerate(POOL_WINDOWS):
        cols = slice(k * POOL_GROUP, (k + 1) * POOL_GROUP)
        acc = u_pool[:, cols]
        for j in range(1, w):
            acc = acc + hist_ref[hist_rows - j * nb:hist_rows - j * nb + m, cols]
        count = jnp.minimum(t_abs + 1, w).astype(F32)
        pooled.append(acc / count - u_pool[:, cols])
    return pooled


def _pool_seq_major(u_pool, hist_ref, i, nb, tt, pos0):
    m = tt * nb
    t_abs = pos0 + i * tt + lax.broadcasted_iota(jnp.int32, (nb, tt, POOL_GROUP), 1)
    pooled = []
    for k, w in enumerate(POOL_WINDOWS):
        cur = u_pool[:, k * POOL_GROUP:(k + 1) * POOL_GROUP].reshape(nb, tt, POOL_GROUP)
        hist_ref[k, :, HIST_PAD:HIST_PAD + tt, :] = cur
        acc = cur
        for j in range(1, w):
            acc = acc + hist_ref[k, :, HIST_PAD - j:HIST_PAD - j + tt, :]
        count = jnp.minimum(t_abs + 1, w).astype(F32)
        pooled.append((acc / count - cur).reshape(m, POOL_GROUP))
    return pooled


def _mixer_kernel(x_ref, mod_ref, h0r_ref, h0i_ref, pool0_ref, n1_ref, win_ref, wsi_ref, wso_ref, wt_ref,
                  alr_ref, ali_ref, d_ref, wglu_ref, wpool_ref, pscale_ref, wout_ref, *rest,
                  nb, tt, n_sub, n_steps, pos0, time_major, cast_halved):
    n_cast = len(cast_halved)
    cast_src, rest = rest[:n_cast], rest[n_cast:]
    xo_ref, sr_out_ref, si_out_ref, pool_out_ref = rest[:4]
    cast_dst, rest = rest[4:4 + n_cast], rest[4 + n_cast:]
    z_ref, st_r, st_i, hist_ref, perm_ref = rest
    n_blocks = tt // BLOCK_STEPS
    mb = n_blocks * nb
    i = pl.program_id(0)
    _cast_blocks(cast_src, cast_dst, cast_halved)
    m = tt * nb
    hist_rows = POOL_HIST * nb

    @pl.when(i == 0)
    def _():
        st_r[...] = h0r_ref[...].T if time_major else h0r_ref[...]
        st_i[...] = h0i_ref[...].T if time_major else h0i_ref[...]
        if time_major:
            hist_ref[0:hist_rows, :] = pool0_ref[...]
        else:
            for k in range(len(POOL_WINDOWS)):
                hist_ref[k, :, 0:HIST_PAD, :] = pool0_ref[:, :, k * POOL_GROUP:(k + 1) * POOL_GROUP]

    for sub in range(n_sub):
        tile = i * n_sub + sub
        t_rows = slice(sub * tt, (sub + 1) * tt)
        x = (x_ref[...] if time_major else x_ref[:, t_rows, :]).reshape(m, D_MODEL)
        shift1 = mod_ref[:, :, 0 * D_MODEL:1 * D_MODEL]
        scale1 = mod_ref[:, :, 1 * D_MODEL:2 * D_MODEL]
        gate1 = mod_ref[:, :, 2 * D_MODEL:3 * D_MODEL]
        hb = _modulate(_rmsnorm(x, n1_ref[...]), shift1, scale1, tt, nb, time_major).astype(BF16)

        u_ssm = _dot(hb, win_ref[:, 0:D_SSM])
        u_pool = _dot(hb, win_ref[:, D_SSM:D_SSM + D_POOL])
        u_tm = u_ssm if time_major else _to_time_major(u_ssm, perm_ref, nb, tt)
        u_blocks = u_tm.reshape(n_blocks, BLOCK_STEPS, nb, D_SSM)
        u_steps = [u_blocks[:, j].reshape(mb, D_SSM).astype(BF16) for j in range(BLOCK_STEPS)]
        y_sets = []
        for s in range(N_SETS):
            chans = slice(s * SET_IN, (s + 1) * SET_IN)
            lhs = jnp.concatenate([u[:, chans] for u in u_steps], axis=1)
            z_ref[:, 2 * s * SET_STATES:2 * (s + 1) * SET_STATES] = _dot(lhs, wsi_ref[s])
            y_sets.append(_dot(lhs, wt_ref[s]))
        g_ssm = _dot(hb, win_ref[:, D_SSM + D_POOL:D_SSM + D_POOL + D_MODEL])
        g_pool = _dot(hb, win_ref[:, D_SSM + D_POOL + D_MODEL:])

        _block_scan(z_ref, st_r, st_i, alr_ref, ali_ref, nb, n_blocks)
        if time_major:
            pooled = _pool_time_major(u_pool, hist_ref, tile, nb, tt, pos0)
            new_buf_rows = (slice(m, m + hist_rows), slice(None))
            if n_steps * n_sub > 1:
                hist_ref[0:hist_rows, :] = hist_ref[new_buf_rows]
        else:
            pooled = _pool_seq_major(u_pool, hist_ref, tile, nb, tt, pos0)
            if n_steps * n_sub > 1:
                hist_ref[:, :, 0:HIST_PAD, :] = hist_ref[:, :, tt:tt + HIST_PAD, :]
        z_pieces = [_dot(pooled[k].astype(BF16), wpool_ref[k]) for k in range(len(POOL_WINDOWS))]
        br_pool = jnp.concatenate(z_pieces, axis=-1) * pscale_ref[...]

        for s in range(N_SETS):
            s_in = z_ref[:, 2 * s * SET_STATES:2 * (s + 1) * SET_STATES].astype(BF16)
            y_sets[s] = y_sets[s] + _dot(s_in, wso_ref[s])
        y_steps = [jnp.concatenate([y[:, j * SET_IN:(j + 1) * SET_IN] for y in y_sets], axis=1).reshape(n_blocks, nb, D_SSM)
                   for j in range(BLOCK_STEPS)]
        y_cs = jnp.stack(y_steps, axis=1).reshape(m, D_SSM)
        if not time_major:
            y_cs = _to_seq_major(y_cs, perm_ref, nb, tt)

        for h in range(N_TAIL_SPLITS):
            rows = slice(h * m // N_TAIL_SPLITS, (h + 1) * m // N_TAIL_SPLITS)
            y_ssm = y_cs[rows] + d_ref[...] * u_ssm[rows]
            glu = _dot(jax.nn.gelu(y_ssm).astype(BF16), wglu_ref[...])
            br_ssm = glu[:, :D_MODEL] * _two_sigmoid(glu[:, D_MODEL:])
            merged2 = _two_sigmoid(g_ssm[rows]) * br_ssm + _two_sigmoid(g_pool[rows]) * br_pool[rows]
            o = _dot(merged2.astype(BF16), wout_ref[...])
            if time_major:
                xo_ref[rows, :] = x[rows] + _rows_gate(o, gate1, tt // N_TAIL_SPLITS, nb, True)
            else:
                nb_h = nb // N_TAIL_SPLITS
                seqs = slice(h * nb_h, (h + 1) * nb_h)
                x1 = x[rows] + _rows_gate(o, gate1[seqs], tt, nb_h, False)
                xo_ref[seqs, t_rows, :] = x1.reshape(nb_h, tt, D_MODEL)

    @pl.when(i == n_steps - 1)
    def _():
        sr_out_ref[...] = st_r[...].T if time_major else st_r[...]
        si_out_ref[...] = st_i[...].T if time_major else st_i[...]
        if time_major:
            pool_out_ref[...] = hist_ref[new_buf_rows]
        else:
            pool_out_ref[...] = jnp.concatenate(
                [hist_ref[k, :, tt + HIST_PAD - POOL_HIST:tt + HIST_PAD, :] for k in range(len(POOL_WINDOWS))],
                axis=-1)


def _const_spec(shape):
    nd = len(shape)
    return pl.BlockSpec(shape, lambda i, _nd=nd: (0,) * _nd, pipeline_mode=pl.Buffered(1))


def _row_spec(nb, tt, time_major):
    if time_major:
        return pl.BlockSpec((tt * nb, D_MODEL), lambda i: (i, 0))
    return pl.BlockSpec((nb, tt, D_MODEL), lambda i: (0, i, 0))


def _mixer(x, mod, h0r, h0i, pool0, n1, win, wsi, wso, wt, alr, ali, d, wglu, wpool, pscale, wout,
           *, nb, tt, n_sub, n_steps, pos0, time_major, cast=()):
    m = tt * nb
    consts = (mod, h0r, h0i, pool0, n1, win, wsi, wso, wt, alr, ali, d, wglu, wpool, pscale, wout)
    cast, cast_halved = tuple(a for a, _ in cast), tuple(n for _, n in cast)
    cast_specs = _cast_specs(cast, n_steps)
    if time_major:
        hist_shape = (POOL_HIST * nb + m, D_POOL)
        pool_out_shape = (POOL_HIST * nb, D_POOL)
    else:
        hist_shape = (len(POOL_WINDOWS), nb, HIST_PAD + tt, POOL_GROUP)
        pool_out_shape = (nb, POOL_HIST, D_POOL)
    kern = functools.partial(_mixer_kernel, nb=nb, tt=tt, n_sub=n_sub, n_steps=n_steps, pos0=pos0,
                             time_major=time_major, cast_halved=cast_halved)
    return pl.pallas_call(
        kern,
        grid=(n_steps,),
        in_specs=[_row_spec(nb, n_sub * tt, time_major)] + [_const_spec(a.shape) for a in consts] + cast_specs,
        out_specs=(_row_spec(nb, n_sub * tt, time_major),
                   pl.BlockSpec(h0r.shape, lambda i: (0, 0)),
                   pl.BlockSpec(h0r.shape, lambda i: (0, 0)),
                   pl.BlockSpec(pool_out_shape, lambda i: (0,) * len(pool_out_shape))) + tuple(cast_specs),
        out_shape=(jax.ShapeDtypeStruct(x.shape, F32),
                   jax.ShapeDtypeStruct(h0r.shape, F32),
                   jax.ShapeDtypeStruct(h0r.shape, F32),
                   jax.ShapeDtypeStruct(pool_out_shape, F32)) + tuple(jax.ShapeDtypeStruct(a.shape, BF16) for a in cast),
        scratch_shapes=[pltpu.VMEM((m // BLOCK_STEPS, 2 * N_STATE), F32),
                        pltpu.VMEM((nb, N_STATE), F32), pltpu.VMEM((nb, N_STATE), F32),
                        pltpu.VMEM(hist_shape, F32),
                        pltpu.VMEM((D_SSM // LANES, m, LANES), F32)],
        compiler_params=pltpu.CompilerParams(dimension_semantics=("arbitrary",),
                                             vmem_limit_bytes=VMEM_LIMIT),
        name=f"mixer_nb{nb}",
    )(x, *consts, *cast)


def _ffn_kernel(x_ref, mod_ref, n2_ref, nf_ref, wfi_ref, wfo_ref, o_ref, *, nb, tt, time_major):
    for h in range(N_FFN_SPLITS):
        if time_major:
            tt_h, nb_h = tt // N_FFN_SPLITS, nb
            piece = (slice(h * tt_h * nb, (h + 1) * tt_h * nb), slice(None))
            mod = mod_ref
        else:
            tt_h, nb_h = tt, nb // N_FFN_SPLITS
            piece = (slice(h * nb_h, (h + 1) * nb_h), slice(None), slice(None))
            mod = mod_ref.at[h * nb_h:(h + 1) * nb_h]
        x = x_ref[piece].reshape(tt_h * nb_h, D_MODEL)
        shift2 = mod[:, :, 3 * D_MODEL:4 * D_MODEL]
        scale2 = mod[:, :, 4 * D_MODEL:5 * D_MODEL]
        gate2 = mod[:, :, 5 * D_MODEL:6 * D_MODEL]
        hb = _modulate(_rmsnorm(x, n2_ref[...]), shift2, scale2, tt_h, nb_h, time_major).astype(BF16)
        f_a = _dot(hb, wfi_ref[:, :D_FF])
        f_b = _dot(hb, wfi_ref[:, D_FF:])
        act = (f_a * _two_sigmoid(f_a) * f_b).astype(BF16)
        x2 = x + _rows_gate(_dot(act, wfo_ref[...]), gate2, tt_h, nb_h, time_major)
        y = _rmsnorm(x2, nf_ref[...])
        o_ref[piece] = y if time_major else y.reshape(nb_h, tt_h, D_MODEL)


def _ffn(x, mod, n2, nf, wfi, wfo, *, nb, tt, n_steps, time_major):
    consts = (mod, n2, nf, wfi, wfo)
    return pl.pallas_call(
        functools.partial(_ffn_kernel, nb=nb, tt=tt, time_major=time_major),
        grid=(n_steps,),
        in_specs=[_row_spec(nb, tt, time_major)] + [_const_spec(a.shape) for a in consts],
        out_specs=_row_spec(nb, tt, time_major),
        out_shape=jax.ShapeDtypeStruct(x.shape, F32),
        compiler_params=pltpu.CompilerParams(dimension_semantics=("arbitrary",),
                                             vmem_limit_bytes=VMEM_LIMIT),
        name=f"ffn_nb{nb}",
    )(x, *consts)


def _trunk(x, mod, h0r, h0i, pool0, pos0, p, *, time_major, ffn_w):
    nb, t_len, _ = x.shape
    tt = min(t_len, TILE_ROWS // nb)
    n_sub = 1 if time_major else MIXER_SUBTILES
    tt_ffn = min(t_len, FFN_TILE_ROWS // nb)
    if time_major:
        rows = lambda a: jnp.transpose(a, (1, 0, 2)).reshape(a.shape[1] * nb, a.shape[2])
        unrows = lambda a, t: jnp.transpose(a.reshape(t, nb, a.shape[-1]), (1, 0, 2))
        x_in, pool_in, mod3 = rows(x), rows(pool0), mod[None]
        flat_state = lambda a: jnp.transpose(a, (1, 2, 0)).reshape(N_STATE, nb)
        unflat_state = lambda a: jnp.transpose(a.reshape(N_SSM_GROUPS, SSM_STATE, nb), (2, 0, 1))[None]
    else:
        assert nb == SUBLANES
        x_in, mod3 = x, mod[:, None, :]
        pool_in = jnp.pad(pool0, ((0, 0), (HIST_PAD - POOL_HIST, 0), (0, 0)))
        flat_state = lambda a: a.reshape(nb, N_STATE)
        unflat_state = lambda a: a.reshape(1, nb, N_SSM_GROUPS, SSM_STATE)
    cast = ((ffn_w[0], (0, D_FF)), (ffn_w[1], (0, 0))) if ffn_w[0].dtype == F32 else ()
    x1, s_r, s_i, new_buf, *cast_out = _mixer(
        x_in, mod3, flat_state(h0r), flat_state(h0i), pool_in,
        p["n1"], p["win"], p["wsi"], p["wso"], p["wt"], p["alr"], p["ali"], p["d"], p["wglu"], p["wpool"],
        p["pscale"], p["wout"], nb=nb, tt=tt, n_sub=n_sub, n_steps=t_len // (tt * n_sub), pos0=pos0,
        time_major=time_major, cast=cast)
    wfi, wfo = cast_out if cast else ffn_w
    y = _ffn(x1, mod3, p["n2"], p["nf"], wfi, wfo, nb=nb, tt=tt_ffn, n_steps=t_len // tt_ffn,
             time_major=time_major)
    if time_major:
        y, new_buf = unrows(y, t_len), unrows(new_buf, POOL_HIST)
    return (y, unflat_state(s_r), unflat_state(s_i), new_buf[None]), (wfi, wfo)


def kernel(x_prompt, x_sample, c_prompt, c_sample, state_ssm_re, state_ssm_im, state_pool, norm1_g, norm2_g, normf_g, w_ada, b_ada, w_in, ssm_lam_re, ssm_lam_im, ssm_log_dt, ssm_b_re, ssm_b_im, ssm_c_re, ssm_c_im, ssm_d, w_glu, w_pool, pool_scale, w_out, w_ffn_in, w_ffn_out):
    n_prompt = x_prompt.shape[0]
    mod_p, mod_s, win, wglu, wout, wpool = _ada(
        c_prompt, c_sample, w_ada[0], b_ada[0],
        cast=((w_in[0], (D_SSM + D_POOL, w_in.shape[-1])), (w_glu[0], (0, 2 * D_MODEL)),
              (w_out[0], (0, D_MODEL)), (w_pool[0].reshape(D_POOL, POOL_OUT), (0, 0))))
    alr, ali, wsi, wso, wt = _ssm_prep(ssm_lam_re[0], ssm_lam_im[0], ssm_log_dt[0], ssm_b_re[0], ssm_b_im[0],
                                       ssm_c_re[0], ssm_c_im[0])
    p = dict(
        n1=norm1_g[0].reshape(1, -1), n2=norm2_g[0].reshape(1, -1), nf=normf_g.reshape(1, -1),
        win=win, wsi=wsi, wso=wso, wt=wt, alr=alr, ali=ali, d=ssm_d[0].reshape(1, -1),
        wglu=wglu, wpool=wpool.reshape(len(POOL_WINDOWS), POOL_GROUP, POOL_OUT), pscale=pool_scale[0].reshape(1, -1),
        wout=wout)
    zero_state = jnp.zeros((n_prompt, N_SSM_GROUPS, SSM_STATE), F32)
    zero_pool = jnp.zeros((n_prompt, POOL_HIST, D_POOL), F32)
    ffn_f32 = (w_ffn_in[0], w_ffn_out[0])
    (y_p, p_re, p_im, p_pool), ffn_bf16 = _trunk(x_prompt, mod_p, zero_state, zero_state, zero_pool, 0, p,
                                                  time_major=False, ffn_w=ffn_f32)
    (y_s, s_re, s_im, s_pool), _ = _trunk(x_sample, mod_s, state_ssm_re[0], state_ssm_im[0], state_pool[0],
                                          PAST_LEN, p, time_major=True, ffn_w=ffn_bf16)
    return (y_p, y_s, p_re, p_im, p_pool, s_re, s_im, s_pool)
```

```python
import functools

import jax
import jax.numpy as jnp
from jax import lax
from jax.experimental import pallas as pl
from jax.experimental.pallas import tpu as pltpu

D_MODEL = 1024
D_SSM = 512
SSM_GROUP = 16
N_SSM_GROUPS = 32
SSM_STATE = 64
N_STATE = N_SSM_GROUPS * SSM_STATE
D_POOL = 512
POOL_WINDOWS = (2, 4, 8, 16)
POOL_GROUP = 128
POOL_OUT = 256
POOL_HIST = 15
D_FF = 2816
N_MOD = 6
EPS = 1e-6
PAST_LEN = 16384

SUBLANES = 8
BF16_SUBLANES = 16
LANES = 128
TILE_ROWS = 512
MIXER_SUBTILES = 2
FFN_TILE_ROWS = 1024
HIST_PAD = 16
N_TAIL_SPLITS = 2
N_FFN_SPLITS = 4
ADA_ROW_BLOCKS = 4
MXU_DEPTH = 256
BLOCK_STEPS = 4
SET_GROUPS = MXU_DEPTH // (BLOCK_STEPS * SSM_GROUP)
N_SETS = N_SSM_GROUPS // SET_GROUPS
SET_IN = SET_GROUPS * SSM_GROUP
SET_STATES = SET_GROUPS * SSM_STATE
VMEM_LIMIT = 60 * 1024 * 1024

BF16 = jnp.bfloat16
F32 = jnp.float32


def _dot(a, b):
    return jnp.dot(a, b, preferred_element_type=F32)


def _sigmoid(x):
    return 0.5 * jnp.tanh(0.5 * x) + 0.5


def _two_sigmoid(half_x):
    return jnp.tanh(half_x) + 1.0


def _rmsnorm(x, g):
    return x * lax.rsqrt(jnp.mean(x * x, axis=-1, keepdims=True) + EPS) * g


def _tile3(v, tt, nb, time_major):
    lead, inner = (tt, nb) if time_major else (nb, tt)
    return v.reshape(lead, inner, v.shape[-1])


def _modulate(h, shift, scale, tt, nb, time_major):
    return (_tile3(h, tt, nb, time_major) * (1.0 + scale) + shift).reshape(h.shape)


def _rows_gate(v, gate, tt, nb, time_major):
    return (_tile3(v, tt, nb, time_major) * gate).reshape(v.shape)


def _cast_specs(arrays, n_steps):
    specs = []
    for a in arrays:
        n_blk = max(k for k in range(1, n_steps + 1)
                    if n_steps % k == 0 and a.shape[0] % (k * BF16_SUBLANES) == 0)
        specs.append(pl.BlockSpec((a.shape[0] // n_blk, a.shape[1]),
                                  lambda i, _hold=n_steps // n_blk: (i // _hold, 0)))
    return specs


def _cast_blocks(srcs, dsts, halved):
    for src, dst, (lo, hi) in zip(srcs, dsts, halved):
        if lo > 0:
            dst[:, :lo] = src[:, :lo].astype(BF16)
        if hi > lo:
            dst[:, lo:hi] = (src[:, lo:hi] * 0.5).astype(BF16)
        if hi < src.shape[-1]:
            dst[:, hi:] = src[:, hi:].astype(BF16)


def _ada_kernel(cp_ref, cs_ref, w_ref, b_ref, *rest, halved):
    n_cast = len(halved)
    cast_src, (op_ref, os_ref), cast_dst = rest[:n_cast], rest[n_cast:n_cast + 2], rest[n_cast + 2:]
    _cast_blocks(cast_src, cast_dst, halved)
    k = pl.program_id(0)
    w = w_ref[...].astype(BF16)
    for c_ref, o_ref in ((cp_ref, op_ref), (cs_ref, os_ref)):
        c = c_ref[...]
        part = _dot((c * _sigmoid(c)).astype(BF16), w)

        @pl.when(k == 0)
        def _(o_ref=o_ref, part=part):
            o_ref[...] = b_ref[...] + part

        @pl.when(k > 0)
        def _(o_ref=o_ref, part=part):
            o_ref[...] += part


def _ada(c_prompt, c_sample, w_ada, b_ada, cast):
    n_p, n_s = c_prompt.shape[0], c_sample.shape[0]
    kb = D_MODEL // ADA_ROW_BLOCKS
    arrays, halved = tuple(a for a, _ in cast), tuple(h for _, h in cast)
    cast_specs = _cast_specs(arrays, ADA_ROW_BLOCKS)
    n_out = N_MOD * D_MODEL
    return pl.pallas_call(
        functools.partial(_ada_kernel, halved=halved),
        grid=(ADA_ROW_BLOCKS,),
        in_specs=[pl.BlockSpec((n_p, kb), lambda k: (0, k)),
                  pl.BlockSpec((n_s, kb), lambda k: (0, k)),
                  pl.BlockSpec((kb, n_out), lambda k: (k, 0)),
                  pl.BlockSpec((1, n_out), lambda k: (0, 0))] + cast_specs,
        out_specs=[pl.BlockSpec((n_p, n_out), lambda k: (0, 0)), pl.BlockSpec((n_s, n_out), lambda k: (0, 0))]
        + cast_specs,
        out_shape=[jax.ShapeDtypeStruct((n_p, n_out), F32), jax.ShapeDtypeStruct((n_s, n_out), F32)]
        + [jax.ShapeDtypeStruct(a.shape, BF16) for a in arrays],
        compiler_params=pltpu.CompilerParams(dimension_semantics=("arbitrary",),
                                             vmem_limit_bytes=VMEM_LIMIT),
        name="ada_mod",
    )(c_prompt, c_sample, w_ada, b_ada.reshape(1, -1), *arrays)


def _cmul(ar, ai, br, bi):
    return ar * br - ai * bi, ar * bi + ai * br


def _ssm_prep_kernel(lr_ref, li_ref, ldt_ref, bt_r_ref, bt_i_ref, cr_ref, ci_ref,
                     alr_ref, ali_ref, wsi_ref, wso_ref, wt_ref):
    lr = lr_ref[...]
    li = li_ref[...]
    dt = jnp.exp(ldt_ref[...])
    mag = jnp.exp(lr * dt)
    a_r = mag * jnp.cos(li * dt)
    a_i = mag * jnp.sin(li * dt)
    den = lr * lr + li * li
    nr = a_r - 1.0
    ni = a_i
    f = ((nr * lr + ni * li) / den, (ni * lr - nr * li) / den)
    powers = [(a_r, a_i)]
    for _ in range(BLOCK_STEPS - 1):
        powers.append(_cmul(a_r, a_i, *powers[-1]))
    for g in range(N_SSM_GROUPS):
        states = slice(g * SSM_STATE, (g + 1) * SSM_STATE)
        alr_ref[:, states] = powers[-1][0][g:g + 1, :]
        ali_ref[:, states] = powers[-1][1][g:g + 1, :]
    wsi_ref[...] = jnp.zeros(wsi_ref.shape, BF16)
    wso_ref[...] = jnp.zeros(wso_ref.shape, BF16)
    wt_ref[...] = jnp.zeros(wt_ref.shape, BF16)
    row_group = lax.broadcasted_iota(jnp.int32, (SET_IN, SET_IN), 0) // SSM_GROUP
    col_group = lax.broadcasted_iota(jnp.int32, (SET_IN, SET_IN), 1) // SSM_GROUP
    same_group = row_group == col_group
    nt_dims = (((1,), (1,)), ((), ()))
    for s in range(N_SETS):
        groups = slice(s * SET_GROUPS, (s + 1) * SET_GROUPS)

        def rows_of(pair, groups=groups):
            return tuple(jnp.broadcast_to(v[groups][:, None, :], (SET_GROUPS, SSM_GROUP, SSM_STATE))
                         .reshape(SET_IN, SSM_STATE) for v in pair)

        bt = (bt_r_ref[groups].reshape(SET_IN, SSM_STATE), bt_i_ref[groups].reshape(SET_IN, SSM_STATE))
        c = (cr_ref[groups].reshape(SET_IN, SSM_STATE), ci_ref[groups].reshape(SET_IN, SSM_STATE))
        a_rows = rows_of(powers[0])
        bb_lags = [_cmul(*rows_of(f), *bt)]
        for _ in range(BLOCK_STEPS - 1):
            bb_lags.append(_cmul(*a_rows, *bb_lags[-1]))
        for j in range(BLOCK_STEPS):
            lag = BLOCK_STEPS - 1 - j
            bb_r, bb_i = bb_lags[lag]
            k_t = (lax.dot_general(bb_r, c[0], nt_dims, precision=lax.Precision.HIGHEST, preferred_element_type=F32)
                   - lax.dot_general(bb_i, c[1], nt_dims, precision=lax.Precision.HIGHEST, preferred_element_type=F32))
            k_t = jnp.where(same_group, k_t, 0.0).astype(BF16)
            for t in range(lag, BLOCK_STEPS):
                wt_ref[s, (t - lag) * SET_IN:(t - lag + 1) * SET_IN, t * SET_IN:(t + 1) * SET_IN] = k_t
            for gj in range(SET_GROUPS):
                src = slice(gj * SSM_GROUP, (gj + 1) * SSM_GROUP)
                rows = slice(j * SET_IN + gj * SSM_GROUP, j * SET_IN + (gj + 1) * SSM_GROUP)
                wsi_ref[s, rows, gj * SSM_STATE:(gj + 1) * SSM_STATE] = bb_r[src, :].astype(BF16)
                wsi_ref[s, rows, SET_STATES + gj * SSM_STATE:SET_STATES + (gj + 1) * SSM_STATE] = (
                    bb_i[src, :].astype(BF16))
        for t in range(BLOCK_STEPS):
            p_r, p_i = rows_of(powers[t])
            from_sr = jnp.transpose(c[0] * p_r - c[1] * p_i).astype(BF16)
            from_si = jnp.transpose(-c[0] * p_i - c[1] * p_r).astype(BF16)
            for gj in range(SET_GROUPS):
                src = slice(gj * SSM_GROUP, (gj + 1) * SSM_GROUP)
                out = slice(t * SET_IN + gj * SSM_GROUP, t * SET_IN + (gj + 1) * SSM_GROUP)
                wso_ref[s, gj * SSM_STATE:(gj + 1) * SSM_STATE, out] = from_sr[:, src]
                wso_ref[s, SET_STATES + gj * SSM_STATE:SET_STATES + (gj + 1) * SSM_STATE, out] = from_si[:, src]


def _ssm_prep(lam_re, lam_im, log_dt, b_re, b_im, c_re, c_im):
    return pl.pallas_call(
        _ssm_prep_kernel,
        out_shape=(jax.ShapeDtypeStruct((1, N_STATE), F32), jax.ShapeDtypeStruct((1, N_STATE), F32),
                   jax.ShapeDtypeStruct((N_SETS, BLOCK_STEPS * SET_IN, 2 * SET_STATES), BF16),
                   jax.ShapeDtypeStruct((N_SETS, 2 * SET_STATES, BLOCK_STEPS * SET_IN), BF16),
                   jax.ShapeDtypeStruct((N_SETS, BLOCK_STEPS * SET_IN, BLOCK_STEPS * SET_IN), BF16)),
        name="ssm_prep",
    )(lam_re, lam_im, log_dt.reshape(N_SSM_GROUPS, 1), jnp.transpose(b_re, (0, 2, 1)), jnp.transpose(b_im, (0, 2, 1)),
      c_re, c_im)


def _block_scan(z_ref, st_r, st_i, alr_ref, ali_ref, nb, n_blocks):
    for s in range(N_SETS):
        lanes = slice(s * SET_STATES, (s + 1) * SET_STATES)
        re_lanes = slice(2 * s * SET_STATES, (2 * s + 1) * SET_STATES)
        im_lanes = slice((2 * s + 1) * SET_STATES, (2 * s + 2) * SET_STATES)
        ar = jnp.broadcast_to(alr_ref[:, lanes], (SUBLANES, SET_STATES))
        ai = jnp.broadcast_to(ali_ref[:, lanes], (SUBLANES, SET_STATES))

        def seq_block(rb, carry, lanes=lanes, re_lanes=re_lanes, im_lanes=im_lanes, ar=ar, ai=ai):
            r0 = pl.multiple_of(rb * SUBLANES, SUBLANES)
            sr = st_r[pl.ds(r0, SUBLANES), lanes]
            si = st_i[pl.ds(r0, SUBLANES), lanes]

            def step(k, state):
                sr, si = state
                rows = pl.ds(pl.multiple_of(k * nb + r0, SUBLANES), SUBLANES)
                zr = z_ref[rows, re_lanes]
                zi = z_ref[rows, im_lanes]
                z_ref[rows, re_lanes] = sr
                z_ref[rows, im_lanes] = si
                return ar * sr - ai * si + zr, ar * si + ai * sr + zi

            sr, si = lax.fori_loop(0, n_blocks, step, (sr, si), unroll=True)
            st_r[pl.ds(r0, SUBLANES), lanes] = sr
            st_i[pl.ds(r0, SUBLANES), lanes] = si
            return carry

        if nb == SUBLANES:
            seq_block(0, 0)
        else:
            lax.fori_loop(0, nb // SUBLANES, seq_block, 0)


def _to_time_major(v, perm_ref, nb, tt):
    for c in range(v.shape[-1] // LANES):
        for b in range(nb):
            perm_ref.at[c][pl.ds(b, tt, stride=nb), :] = v[b * tt:(b + 1) * tt, c * LANES:(c + 1) * LANES]
    return jnp.concatenate([perm_ref[c] for c in range(v.shape[-1] // LANES)], axis=-1)


def _to_seq_major(v, perm_ref, nb, tt):
    n_slabs = v.shape[-1] // LANES
    for c in range(n_slabs):
        perm_ref[c] = v[:, c * LANES:(c + 1) * LANES]
    return jnp.concatenate(
        [jnp.concatenate([perm_ref.at[c][pl.ds(b, tt, stride=nb), :] for b in range(nb)], axis=0)
         for c in range(n_slabs)], axis=-1)


def _pool_time_major(u_pool, hist_ref, i, nb, tt, pos0):
    m = tt * nb
    hist_rows = POOL_HIST * nb
    hist_ref[hist_rows:hist_rows + m, :] = u_pool
    t_abs = pos0 + i * tt + lax.broadcasted_iota(jnp.int32, (tt, nb, POOL_GROUP), 0).reshape(m, POOL_GROUP)
    pooled = []
    for k, w in enumerate(POOL_WINDOWS):
        cols = slice(k * POOL_GROUP, (k + 1) * POOL_GROUP)
        acc = u_pool[:, cols]
        for j in range(1, w):
            acc = acc + hist_ref[hist_rows - j * nb:hist_rows - j * nb + m, cols]
        count = jnp.minimum(t_abs + 1, w).astype(F32)
        pooled.append(acc / count - u_pool[:, cols])
    return pooled


def _pool_seq_major(u_pool, hist_ref, i, nb, tt, pos0):
    m = tt * nb
    t_abs = pos0 + i * tt + lax.broadcasted_iota(jnp.int32, (nb, tt, POOL_GROUP), 1)
    pooled = []
    for k, w in enumerate(POOL_WINDOWS):
        cur = u_pool[:, k * POOL_GROUP:(k + 1) * POOL_GROUP].reshape(nb, tt, POOL_GROUP)
        hist_ref[k, :, HIST_PAD:HIST_PAD + tt, :] = cur
        acc = cur
        for j in range(1, w):
            acc = acc + hist_ref[k, :, HIST_PAD - j:HIST_PAD - j + tt, :]
        count = jnp.minimum(t_abs + 1, w).astype(F32)
        pooled.append((acc / count - cur).reshape(m, POOL_GROUP))
    return pooled


def _mixer_kernel(x_ref, mod_ref, h0r_ref, h0i_ref, pool0_ref, n1_ref, win_ref, wsi_ref, wso_ref, wt_ref,
                  alr_ref, ali_ref, d_ref, wglu_ref, wpool_ref, pscale_ref, wout_ref, *rest,
                  nb, tt, n_sub, n_steps, pos0, time_major, cast_halved):
    n_cast = len(cast_halved)
    cast_src, rest = rest[:n_cast], rest[n_cast:]
    xo_ref, sr_out_ref, si_out_ref, pool_out_ref = rest[:4]
    cast_dst, rest = rest[4:4 + n_cast], rest[4 + n_cast:]
    z_ref, st_r, st_i, hist_ref, perm_ref = rest
    n_blocks = tt // BLOCK_STEPS
    mb = n_blocks * nb
    i = pl.program_id(0)
    _cast_blocks(cast_src, cast_dst, cast_halved)
    m = tt * nb
    hist_rows = POOL_HIST * nb

    @pl.when(i == 0)
    def _():
        st_r[...] = h0r_ref[...].T if time_major else h0r_ref[...]
        st_i[...] = h0i_ref[...].T if time_major else h0i_ref[...]
        if time_major:
            hist_ref[0:hist_rows, :] = pool0_ref[...]
        else:
            for k in range(len(POOL_WINDOWS)):
                hist_ref[k, :, 0:HIST_PAD, :] = pool0_ref[:, :, k * POOL_GROUP:(k + 1) * POOL_GROUP]

    for sub in range(n_sub):
        tile = i * n_sub + sub
        t_rows = slice(sub * tt, (sub + 1) * tt)
        x = (x_ref[...] if time_major else x_ref[:, t_rows, :]).reshape(m, D_MODEL)
        shift1 = mod_ref[:, :, 0 * D_MODEL:1 * D_MODEL]
        scale1 = mod_ref[:, :, 1 * D_MODEL:2 * D_MODEL]
        gate1 = mod_ref[:, :, 2 * D_MODEL:3 * D_MODEL]
        hb = _modulate(_rmsnorm(x, n1_ref[...]), shift1, scale1, tt, nb, time_major).astype(BF16)

        u_ssm = _dot(hb, win_ref[:, 0:D_SSM])
        u_pool = _dot(hb, win_ref[:, D_SSM:D_SSM + D_POOL])
        u_tm = u_ssm if time_major else _to_time_major(u_ssm, perm_ref, nb, tt)
        u_blocks = u_tm.reshape(n_blocks, BLOCK_STEPS, nb, D_SSM)
        u_steps = [u_blocks[:, j].reshape(mb, D_SSM).astype(BF16) for j in range(BLOCK_STEPS)]
        y_sets = []
        for s in range(N_SETS):
            chans = slice(s * SET_IN, (s + 1) * SET_IN)
            lhs = jnp.concatenate([u[:, chans] for u in u_steps], axis=1)
            z_ref[:, 2 * s * SET_STATES:2 * (s + 1) * SET_STATES] = _dot(lhs, wsi_ref[s])
            y_sets.append(_dot(lhs, wt_ref[s]))
        g_ssm = _dot(hb, win_ref[:, D_SSM + D_POOL:D_SSM + D_POOL + D_MODEL])
        g_pool = _dot(hb, win_ref[:, D_SSM + D_POOL + D_MODEL:])

        _block_scan(z_ref, st_r, st_i, alr_ref, ali_ref, nb, n_blocks)
        if time_major:
            pooled = _pool_time_major(u_pool, hist_ref, tile, nb, tt, pos0)
            new_buf_rows = (slice(m, m + hist_rows), slice(None))
            if n_steps * n_sub > 1:
                hist_ref[0:hist_rows, :] = hist_ref[new_buf_rows]
        else:
            pooled = _pool_seq_major(u_pool, hist_ref, tile, nb, tt, pos0)
            if n_steps * n_sub > 1:
                hist_ref[:, :, 0:HIST_PAD, :] = hist_ref[:, :, tt:tt + HIST_PAD, :]
        z_pieces = [_dot(pooled[k].astype(BF16), wpool_ref[k]) for k in range(len(POOL_WINDOWS))]
        br_pool = jnp.concatenate(z_pieces, axis=-1) * pscale_ref[...]

        for s in range(N_SETS):
            s_in = z_ref[:, 2 * s * SET_STATES:2 * (s + 1) * SET_STATES].astype(BF16)
            y_sets[s] = y_sets[s] + _dot(s_in, wso_ref[s])
        y_steps = [jnp.concatenate([y[:, j * SET_IN:(j + 1) * SET_IN] for y in y_sets], axis=1).reshape(n_blocks, nb, D_SSM)
                   for j in range(BLOCK_STEPS)]
        y_cs = jnp.stack(y_steps, axis=1).reshape(m, D_SSM)
        if not time_major:
            y_cs = _to_seq_major(y_cs, perm_ref, nb, tt)

        for h in range(N_TAIL_SPLITS):
            rows = slice(h * m // N_TAIL_SPLITS, (h + 1) * m // N_TAIL_SPLITS)
            y_ssm = y_cs[rows] + d_ref[...] * u_ssm[rows]
            glu = _dot(jax.nn.gelu(y_ssm).astype(BF16), wglu_ref[...])
            br_ssm = glu[:, :D_MODEL] * _two_sigmoid(glu[:, D_MODEL:])
            merged2 = _two_sigmoid(g_ssm[rows]) * br_ssm + _two_sigmoid(g_pool[rows]) * br_pool[rows]
            o = _dot(merged2.astype(BF16), wout_ref[...])
            if time_major:
                xo_ref[rows, :] = x[rows] + _rows_gate(o, gate1, tt // N_TAIL_SPLITS, nb, True)
            else:
                nb_h = nb // N_TAIL_SPLITS
                seqs = slice(h * nb_h, (h + 1) * nb_h)
                x1 = x[rows] + _rows_gate(o, gate1[seqs], tt, nb_h, False)
                xo_ref[seqs, t_rows, :] = x1.reshape(nb_h, tt, D_MODEL)

    @pl.when(i == n_steps - 1)
    def _():
        sr_out_ref[...] = st_r[...].T if time_major else st_r[...]
        si_out_ref[...] = st_i[...].T if time_major else st_i[...]
        if time_major:
            pool_out_ref[...] = hist_ref[new_buf_rows]
        else:
            pool_out_ref[...] = jnp.concatenate(
                [hist_ref[k, :, tt + HIST_PAD - POOL_HIST:tt + HIST_PAD, :] for k in range(len(POOL_WINDOWS))],
                axis=-1)


def _const_spec(shape):
    nd = len(shape)
    return pl.BlockSpec(shape, lambda i, _nd=nd: (0,) * _nd, pipeline_mode=pl.Buffered(1))


def _row_spec(nb, tt, time_major):
    if time_major:
        return pl.BlockSpec((tt * nb, D_MODEL), lambda i: (i, 0))
    return pl.BlockSpec((nb, tt, D_MODEL), lambda i: (0, i, 0))


def _mixer(x, mod, h0r, h0i, pool0, n1, win, wsi, wso, wt, alr, ali, d, wglu, wpool, pscale, wout,
           *, nb, tt, n_sub, n_steps, pos0, time_major, cast=()):
    m = tt * nb
    consts = (mod, h0r, h0i, pool0, n1, win, wsi, wso, wt, alr, ali, d, wglu, wpool, pscale, wout)
    cast, cast_halved = tuple(a for a, _ in cast), tuple(n for _, n in cast)
    cast_specs = _cast_specs(cast, n_steps)
    if time_major:
        hist_shape = (POOL_HIST * nb + m, D_POOL)
        pool_out_shape = (POOL_HIST * nb, D_POOL)
    else:
        hist_shape = (len(POOL_WINDOWS), nb, HIST_PAD + tt, POOL_GROUP)
        pool_out_shape = (nb, POOL_HIST, D_POOL)
    kern = functools.partial(_mixer_kernel, nb=nb, tt=tt, n_sub=n_sub, n_steps=n_steps, pos0=pos0,
                             time_major=time_major, cast_halved=cast_halved)
    return pl.pallas_call(
        kern,
        grid=(n_steps,),
        in_specs=[_row_spec(nb, n_sub * tt, time_major)] + [_const_spec(a.shape) for a in consts] + cast_specs,
        out_specs=(_row_spec(nb, n_sub * tt, time_major),
                   pl.BlockSpec(h0r.shape, lambda i: (0, 0)),
                   pl.BlockSpec(h0r.shape, lambda i: (0, 0)),
                   pl.BlockSpec(pool_out_shape, lambda i: (0,) * len(pool_out_shape))) + tuple(cast_specs),
        out_shape=(jax.ShapeDtypeStruct(x.shape, F32),
                   jax.ShapeDtypeStruct(h0r.shape, F32),
                   jax.ShapeDtypeStruct(h0r.shape, F32),
                   jax.ShapeDtypeStruct(pool_out_shape, F32)) + tuple(jax.ShapeDtypeStruct(a.shape, BF16) for a in cast),
        scratch_shapes=[pltpu.VMEM((m // BLOCK_STEPS, 2 * N_STATE), F32),
                        pltpu.VMEM((nb, N_STATE), F32), pltpu.VMEM((nb, N_STATE), F32),
                        pltpu.VMEM(hist_shape, F32),
                        pltpu.VMEM((D_SSM // LANES, m, LANES), F32)],
        compiler_params=pltpu.CompilerParams(dimension_semantics=("arbitrary",),
                                             vmem_limit_bytes=VMEM_LIMIT),
        name=f"mixer_nb{nb}",
    )(x, *consts, *cast)


def _ffn_tile(x_ref, mod_ref, n2_ref, nf_ref, wfi_ref, wfo_ref, o_ref, nb, tt, time_major):
    for h in range(N_FFN_SPLITS):
        if time_major:
            tt_h, nb_h = tt // N_FFN_SPLITS, nb
            piece = (slice(h * tt_h * nb, (h + 1) * tt_h * nb), slice(None))
            mod = mod_ref
        else:
            tt_h, nb_h = tt, nb // N_FFN_SPLITS
            piece = (slice(h * nb_h, (h + 1) * nb_h), slice(None), slice(None))
            mod = mod_ref.at[h * nb_h:(h + 1) * nb_h]
        x = x_ref[piece].reshape(tt_h * nb_h, D_MODEL)
        shift2 = mod[:, :, 3 * D_MODEL:4 * D_MODEL]
        scale2 = mod[:, :, 4 * D_MODEL:5 * D_MODEL]
        gate2 = mod[:, :, 5 * D_MODEL:6 * D_MODEL]
        hb = _modulate(_rmsnorm(x, n2_ref[...]), shift2, scale2, tt_h, nb_h, time_major).astype(BF16)
        f_a = _dot(hb, wfi_ref[:, :D_FF])
        f_b = _dot(hb, wfi_ref[:, D_FF:])
        act = (f_a * _two_sigmoid(f_a) * f_b).astype(BF16)
        x2 = x + _rows_gate(_dot(act, wfo_ref[...]), gate2, tt_h, nb_h, time_major)
        y = _rmsnorm(x2, nf_ref[...])
        o_ref[piece] = y if time_major else y.reshape(nb_h, tt_h, D_MODEL)


def _ffn_kernel(x_ref, mod_ref, xs_ref, mods_ref, n2_ref, nf_ref, wfi_ref, wfo_ref, o_ref, os_ref,
                *, nb, tt, n_steps, nb_s, tt_s):
    i = pl.program_id(0)
    weights = (n2_ref, nf_ref, wfi_ref, wfo_ref)

    @pl.when(i < n_steps)
    def _():
        _ffn_tile(x_ref, mod_ref, *weights, o_ref, nb, tt, False)

    @pl.when(i == n_steps)
    def _():
        _ffn_tile(xs_ref, mods_ref, *weights, os_ref, nb_s, tt_s, True)


def _ffn(x, mod, x_s, mod_s, n2, nf, wfi, wfo, *, nb, tt, nb_s, tt_s):
    n_steps = x.shape[1] // tt
    row_spec = pl.BlockSpec((nb, tt, D_MODEL), lambda i: (0, jnp.minimum(i, n_steps - 1), 0))
    consts = (mod, x_s, mod_s, n2, nf, wfi, wfo)
    return pl.pallas_call(
        functools.partial(_ffn_kernel, nb=nb, tt=tt, n_steps=n_steps, nb_s=nb_s, tt_s=tt_s),
        grid=(n_steps + 1,),
        in_specs=[row_spec] + [_const_spec(a.shape) for a in consts],
        out_specs=(row_spec, pl.BlockSpec(x_s.shape, lambda i: (0, 0))),
        out_shape=(jax.ShapeDtypeStruct(x.shape, F32), jax.ShapeDtypeStruct(x_s.shape, F32)),
        compiler_params=pltpu.CompilerParams(dimension_semantics=("arbitrary",),
                                             vmem_limit_bytes=VMEM_LIMIT),
        name="ffn",
    )(x, *consts)


def _mix(x, mod, h0r, h0i, pool0, pos0, p, *, time_major, cast):
    nb, t_len, _ = x.shape
    tt = min(t_len, TILE_ROWS // nb)
    n_sub = 1 if time_major else MIXER_SUBTILES
    if time_major:
        rows = lambda a: jnp.transpose(a, (1, 0, 2)).reshape(a.shape[1] * nb, a.shape[2])
        x_in, pool_in, mod3 = rows(x), rows(pool0), mod[None]
        flat_state = lambda a: jnp.transpose(a, (1, 2, 0)).reshape(N_STATE, nb)
        unflat_state = lambda a: jnp.transpose(a.reshape(N_SSM_GROUPS, SSM_STATE, nb), (2, 0, 1))[None]
    else:
        assert nb == SUBLANES
        x_in, mod3 = x, mod[:, None, :]
        pool_in = jnp.pad(pool0, ((0, 0), (HIST_PAD - POOL_HIST, 0), (0, 0)))
        flat_state = lambda a: a.reshape(nb, N_STATE)
        unflat_state = lambda a: a.reshape(1, nb, N_SSM_GROUPS, SSM_STATE)
    x1, s_r, s_i, new_buf, *cast_out = _mixer(
        x_in, mod3, flat_state(h0r), flat_state(h0i), pool_in,
        p["n1"], p["win"], p["wsi"], p["wso"], p["wt"], p["alr"], p["ali"], p["d"], p["wglu"], p["wpool"],
        p["pscale"], p["wout"], nb=nb, tt=tt, n_sub=n_sub, n_steps=t_len // (tt * n_sub), pos0=pos0,
        time_major=time_major, cast=cast)
    if time_major:
        new_buf = _seq_major(new_buf, nb)
    return x1, unflat_state(s_r), unflat_state(s_i), new_buf[None], mod3, cast_out


def _seq_major(rows, nb):
    return jnp.transpose(rows.reshape(rows.shape[0] // nb, nb, rows.shape[-1]), (1, 0, 2))


def kernel(x_prompt, x_sample, c_prompt, c_sample, state_ssm_re, state_ssm_im, state_pool, norm1_g, norm2_g, normf_g, w_ada, b_ada, w_in, ssm_lam_re, ssm_lam_im, ssm_log_dt, ssm_b_re, ssm_b_im, ssm_c_re, ssm_c_im, ssm_d, w_glu, w_pool, pool_scale, w_out, w_ffn_in, w_ffn_out):
    n_prompt, n_sample, t_sample = x_prompt.shape[0], x_sample.shape[0], x_sample.shape[1]
    mod_p, mod_s, win, wglu, wout, wpool = _ada(
        c_prompt, c_sample, w_ada[0], b_ada[0],
        cast=((w_in[0], (D_SSM + D_POOL, w_in.shape[-1])), (w_glu[0], (0, 2 * D_MODEL)),
              (w_out[0], (0, D_MODEL)), (w_pool[0].reshape(D_POOL, POOL_OUT), (0, 0))))
    alr, ali, wsi, wso, wt = _ssm_prep(ssm_lam_re[0], ssm_lam_im[0], ssm_log_dt[0], ssm_b_re[0], ssm_b_im[0],
                                       ssm_c_re[0], ssm_c_im[0])
    p = dict(
        n1=norm1_g[0].reshape(1, -1), win=win, wsi=wsi, wso=wso, wt=wt, alr=alr, ali=ali, d=ssm_d[0].reshape(1, -1),
        wglu=wglu, wpool=wpool.reshape(len(POOL_WINDOWS), POOL_GROUP, POOL_OUT), pscale=pool_scale[0].reshape(1, -1),
        wout=wout)
    zero_state = jnp.zeros((n_prompt, N_SSM_GROUPS, SSM_STATE), F32)
    zero_pool = jnp.zeros((n_prompt, POOL_HIST, D_POOL), F32)
    x1_p, p_re, p_im, p_pool, mod3_p, (wfi, wfo) = _mix(
        x_prompt, mod_p, zero_state, zero_state, zero_pool, 0, p, time_major=False,
        cast=((w_ffn_in[0], (0, D_FF)), (w_ffn_out[0], (0, 0))))
    x1_s, s_re, s_im, s_pool, mod3_s, _ = _mix(
        x_sample, mod_s, state_ssm_re[0], state_ssm_im[0], state_pool[0], PAST_LEN, p, time_major=True, cast=())
    y_p, y_s = _ffn(x1_p, mod3_p, x1_s, mod3_s, norm2_g[0].reshape(1, -1), normf_g.reshape(1, -1), wfi, wfo,
                    nb=n_prompt, tt=FFN_TILE_ROWS // n_prompt, nb_s=n_sample, tt_s=t_sample)
    return (y_p, _seq_major(y_s, n_sample), p_re, p_im, p_pool, s_re, s_im, s_pool)
```

```python
import functools

import jax
import jax.numpy as jnp
from jax import lax
from jax.experimental import pallas as pl
from jax.experimental.pallas import tpu as pltpu

D_MODEL = 1024
D_SSM = 512
SSM_GROUP = 16
N_SSM_GROUPS = 32
SSM_STATE = 64
N_STATE = N_SSM_GROUPS * SSM_STATE
D_POOL = 512
POOL_WINDOWS = (2, 4, 8, 16)
POOL_GROUP = 128
POOL_OUT = 256
POOL_HIST = 15
D_FF = 2816
N_MOD = 6
EPS = 1e-6
PAST_LEN = 16384

SUBLANES = 8
BF16_SUBLANES = 16
LANES = 128
TILE_ROWS = 512
MIXER_SUBTILES = 2
FFN_TILE_ROWS = 1024
HIST_PAD = 16
N_TAIL_SPLITS = 2
N_FFN_SPLITS = 4
ADA_ROW_BLOCKS = 4
MXU_DEPTH = 256
BLOCK_STEPS = 4
SET_GROUPS = MXU_DEPTH // (BLOCK_STEPS * SSM_GROUP)
N_SETS = N_SSM_GROUPS // SET_GROUPS
SET_IN = SET_GROUPS * SSM_GROUP
SET_STATES = SET_GROUPS * SSM_STATE
VMEM_LIMIT = 60 * 1024 * 1024

BF16 = jnp.bfloat16
F32 = jnp.float32


def _dot(a, b):
    return jnp.dot(a, b, preferred_element_type=F32)


def _sigmoid(x):
    return 0.5 * jnp.tanh(0.5 * x) + 0.5


def _two_sigmoid(half_x):
    return jnp.tanh(half_x) + 1.0


def _rmsnorm(x, g):
    return x * lax.rsqrt(jnp.mean(x * x, axis=-1, keepdims=True) + EPS) * g


def _tile3(v, tt, nb, time_major):
    lead, inner = (tt, nb) if time_major else (nb, tt)
    return v.reshape(lead, inner, v.shape[-1])


def _modulate(h, shift, scale, tt, nb, time_major):
    return (_tile3(h, tt, nb, time_major) * (1.0 + scale) + shift).reshape(h.shape)


def _rows_gate(v, gate, tt, nb, time_major):
    return (_tile3(v, tt, nb, time_major) * gate).reshape(v.shape)


def _cast_specs(arrays, n_steps):
    specs = []
    for a in arrays:
        n_blk = max(k for k in range(1, n_steps + 1)
                    if n_steps % k == 0 and a.shape[0] % (k * BF16_SUBLANES) == 0)
        specs.append(pl.BlockSpec((a.shape[0] // n_blk, a.shape[1]),
                                  lambda i, _hold=n_steps // n_blk: (i // _hold, 0)))
    return specs


def _cast_blocks(srcs, dsts, halved):
    for src, dst, (lo, hi) in zip(srcs, dsts, halved):
        if lo > 0:
            dst[:, :lo] = src[:, :lo].astype(BF16)
        if hi > lo:
            dst[:, lo:hi] = (src[:, lo:hi] * 0.5).astype(BF16)
        if hi < src.shape[-1]:
            dst[:, hi:] = src[:, hi:].astype(BF16)


def _ada_kernel(cp_ref, cs_ref, w_ref, b_ref, *rest, halved):
    n_cast = len(halved)
    cast_src, (op_ref, os_ref), cast_dst = rest[:n_cast], rest[n_cast:n_cast + 2], rest[n_cast + 2:]
    _cast_blocks(cast_src, cast_dst, halved)
    k = pl.program_id(0)
    w = w_ref[...].astype(BF16)
    for c_ref, o_ref in ((cp_ref, op_ref), (cs_ref, os_ref)):
        c = c_ref[...]
        part = _dot((c * _sigmoid(c)).astype(BF16), w)

        @pl.when(k == 0)
        def _(o_ref=o_ref, part=part):
            o_ref[...] = b_ref[...] + part

        @pl.when(k > 0)
        def _(o_ref=o_ref, part=part):
            o_ref[...] += part


def _ada(c_prompt, c_sample, w_ada, b_ada, cast):
    n_p, n_s = c_prompt.shape[0], c_sample.shape[0]
    kb = D_MODEL // ADA_ROW_BLOCKS
    arrays, halved = tuple(a for a, _ in cast), tuple(h for _, h in cast)
    cast_specs = _cast_specs(arrays, ADA_ROW_BLOCKS)
    n_out = N_MOD * D_MODEL
    return pl.pallas_call(
        functools.partial(_ada_kernel, halved=halved),
        grid=(ADA_ROW_BLOCKS,),
        in_specs=[pl.BlockSpec((n_p, kb), lambda k: (0, k)),
                  pl.BlockSpec((n_s, kb), lambda k: (0, k)),
                  pl.BlockSpec((kb, n_out), lambda k: (k, 0)),
                  pl.BlockSpec((1, n_out), lambda k: (0, 0))] + cast_specs,
        out_specs=[pl.BlockSpec((n_p, n_out), lambda k: (0, 0)), pl.BlockSpec((n_s, n_out), lambda k: (0, 0))]
        + cast_specs,
        out_shape=[jax.ShapeDtypeStruct((n_p, n_out), F32), jax.ShapeDtypeStruct((n_s, n_out), F32)]
        + [jax.ShapeDtypeStruct(a.shape, BF16) for a in arrays],
        compiler_params=pltpu.CompilerParams(dimension_semantics=("arbitrary",),
                                             vmem_limit_bytes=VMEM_LIMIT),
        name="ada_mod",
    )(c_prompt, c_sample, w_ada, b_ada.reshape(1, -1), *arrays)


def _cmul(ar, ai, br, bi):
    return ar * br - ai * bi, ar * bi + ai * br


def _ssm_prep_kernel(lr_ref, li_ref, ldt_ref, bt_r_ref, bt_i_ref, cr_ref, ci_ref,
                     alr_ref, ali_ref, wsi_ref, wso_ref, wt_ref):
    lr = lr_ref[...]
    li = li_ref[...]
    dt = jnp.exp(ldt_ref[...])
    mag = jnp.exp(lr * dt)
    a_r = mag * jnp.cos(li * dt)
    a_i = mag * jnp.sin(li * dt)
    den = lr * lr + li * li
    nr = a_r - 1.0
    ni = a_i
    f = ((nr * lr + ni * li) / den, (ni * lr - nr * li) / den)
    powers = [(a_r, a_i)]
    for _ in range(BLOCK_STEPS - 1):
        powers.append(_cmul(a_r, a_i, *powers[-1]))
    for g in range(N_SSM_GROUPS):
        states = slice(g * SSM_STATE, (g + 1) * SSM_STATE)
        alr_ref[:, states] = powers[-1][0][g:g + 1, :]
        ali_ref[:, states] = powers[-1][1][g:g + 1, :]
    wsi_ref[...] = jnp.zeros(wsi_ref.shape, BF16)
    wso_ref[...] = jnp.zeros(wso_ref.shape, BF16)
    wt_ref[...] = jnp.zeros(wt_ref.shape, BF16)
    row_group = lax.broadcasted_iota(jnp.int32, (SET_IN, SET_IN), 0) // SSM_GROUP
    col_group = lax.broadcasted_iota(jnp.int32, (SET_IN, SET_IN), 1) // SSM_GROUP
    same_group = row_group == col_group
    nt_dims = (((1,), (1,)), ((), ()))
    for s in range(N_SETS):
        groups = slice(s * SET_GROUPS, (s + 1) * SET_GROUPS)

        def rows_of(pair, groups=groups):
            return tuple(jnp.broadcast_to(v[groups][:, None, :], (SET_GROUPS, SSM_GROUP, SSM_STATE))
                         .reshape(SET_IN, SSM_STATE) for v in pair)

        bt = (bt_r_ref[groups].reshape(SET_IN, SSM_STATE), bt_i_ref[groups].reshape(SET_IN, SSM_STATE))
        c = (cr_ref[groups].reshape(SET_IN, SSM_STATE), ci_ref[groups].reshape(SET_IN, SSM_STATE))
        a_rows = rows_of(powers[0])
        bb_lags = [_cmul(*rows_of(f), *bt)]
        for _ in range(BLOCK_STEPS - 1):
            bb_lags.append(_cmul(*a_rows, *bb_lags[-1]))
        for j in range(BLOCK_STEPS):
            lag = BLOCK_STEPS - 1 - j
            bb_r, bb_i = bb_lags[lag]
            k_t = (lax.dot_general(bb_r, c[0], nt_dims, precision=lax.Precision.HIGHEST, preferred_element_type=F32)
                   - lax.dot_general(bb_i, c[1], nt_dims, precision=lax.Precision.HIGHEST, preferred_element_type=F32))
            k_t = jnp.where(same_group, k_t, 0.0).astype(BF16)
            for t in range(lag, BLOCK_STEPS):
                wt_ref[s, (t - lag) * SET_IN:(t - lag + 1) * SET_IN, t * SET_IN:(t + 1) * SET_IN] = k_t
            for gj in range(SET_GROUPS):
                src = slice(gj * SSM_GROUP, (gj + 1) * SSM_GROUP)
                rows = slice(j * SET_IN + gj * SSM_GROUP, j * SET_IN + (gj + 1) * SSM_GROUP)
                wsi_ref[s, rows, gj * SSM_STATE:(gj + 1) * SSM_STATE] = bb_r[src, :].astype(BF16)
                wsi_ref[s, rows, SET_STATES + gj * SSM_STATE:SET_STATES + (gj + 1) * SSM_STATE] = (
                    bb_i[src, :].astype(BF16))
        for t in range(BLOCK_STEPS):
            p_r, p_i = rows_of(powers[t])
            from_sr = jnp.transpose(c[0] * p_r - c[1] * p_i).astype(BF16)
            from_si = jnp.transpose(-c[0] * p_i - c[1] * p_r).astype(BF16)
            for gj in range(SET_GROUPS):
                src = slice(gj * SSM_GROUP, (gj + 1) * SSM_GROUP)
                out = slice(t * SET_IN + gj * SSM_GROUP, t * SET_IN + (gj + 1) * SSM_GROUP)
                wso_ref[s, gj * SSM_STATE:(gj + 1) * SSM_STATE, out] = from_sr[:, src]
                wso_ref[s, SET_STATES + gj * SSM_STATE:SET_STATES + (gj + 1) * SSM_STATE, out] = from_si[:, src]


def _ssm_prep(lam_re, lam_im, log_dt, b_re, b_im, c_re, c_im):
    return pl.pallas_call(
        _ssm_prep_kernel,
        out_shape=(jax.ShapeDtypeStruct((1, N_STATE), F32), jax.ShapeDtypeStruct((1, N_STATE), F32),
                   jax.ShapeDtypeStruct((N_SETS, BLOCK_STEPS * SET_IN, 2 * SET_STATES), BF16),
                   jax.ShapeDtypeStruct((N_SETS, 2 * SET_STATES, BLOCK_STEPS * SET_IN), BF16),
                   jax.ShapeDtypeStruct((N_SETS, BLOCK_STEPS * SET_IN, BLOCK_STEPS * SET_IN), BF16)),
        name="ssm_prep",
    )(lam_re, lam_im, log_dt.reshape(N_SSM_GROUPS, 1), jnp.transpose(b_re, (0, 2, 1)), jnp.transpose(b_im, (0, 2, 1)),
      c_re, c_im)


def _block_scan(z_ref, st_r, st_i, alr_ref, ali_ref, nb, n_blocks):
    for s in range(N_SETS):
        lanes = slice(s * SET_STATES, (s + 1) * SET_STATES)
        re_lanes = slice(2 * s * SET_STATES, (2 * s + 1) * SET_STATES)
        im_lanes = slice((2 * s + 1) * SET_STATES, (2 * s + 2) * SET_STATES)
        ar = jnp.broadcast_to(alr_ref[:, lanes], (SUBLANES, SET_STATES))
        ai = jnp.broadcast_to(ali_ref[:, lanes], (SUBLANES, SET_STATES))

        def seq_block(rb, carry, lanes=lanes, re_lanes=re_lanes, im_lanes=im_lanes, ar=ar, ai=ai):
            r0 = pl.multiple_of(rb * SUBLANES, SUBLANES)
            sr = st_r[pl.ds(r0, SUBLANES), lanes]
            si = st_i[pl.ds(r0, SUBLANES), lanes]

            def step(k, state):
                sr, si = state
                rows = pl.ds(pl.multiple_of(k * nb + r0, SUBLANES), SUBLANES)
                zr = z_ref[rows, re_lanes]
                zi = z_ref[rows, im_lanes]
                z_ref[rows, re_lanes] = sr
                z_ref[rows, im_lanes] = si
                return ar * sr - ai * si + zr, ar * si + ai * sr + zi

            sr, si = lax.fori_loop(0, n_blocks, step, (sr, si), unroll=True)
            st_r[pl.ds(r0, SUBLANES), lanes] = sr
            st_i[pl.ds(r0, SUBLANES), lanes] = si
            return carry

        if nb == SUBLANES:
            seq_block(0, 0)
        else:
            lax.fori_loop(0, nb // SUBLANES, seq_block, 0)


def _to_time_major(v, perm_ref, nb, tt):
    for c in range(v.shape[-1] // LANES):
        for b in range(nb):
            perm_ref.at[c][pl.ds(b, tt, stride=nb), :] = v[b * tt:(b + 1) * tt, c * LANES:(c + 1) * LANES]
    return jnp.concatenate([perm_ref[c] for c in range(v.shape[-1] // LANES)], axis=-1)


def _to_seq_major(v, perm_ref, nb, tt):
    n_slabs = v.shape[-1] // LANES
    for c in range(n_slabs):
        perm_ref[c] = v[:, c * LANES:(c + 1) * LANES]
    return jnp.concatenate(
        [jnp.concatenate([perm_ref.at[c][pl.ds(b, tt, stride=nb), :] for b in range(nb)], axis=0)
         for c in range(n_slabs)], axis=-1)


def _pool_time_major(u_pool, hist_ref, i, nb, tt, pos0):
    m = tt * nb
    hist_rows = POOL_HIST * nb
    hist_ref[hist_rows:hist_rows + m, :] = u_pool
    t_abs = pos0 + i * tt + lax.broadcasted_iota(jnp.int32, (tt, nb, POOL_GROUP), 0).reshape(m, POOL_GROUP)
    pooled = []
    for k, w in enumerate(POOL_WINDOWS):
        cols = slice(k * POOL_GROUP, (k + 1) * POOL_GROUP)
        acc = u_pool[:, cols]
        for j in range(1, w):
            acc = acc + hist_ref[hist_rows - j * nb:hist_rows - j * nb + m, cols]
        count = jnp.minimum(t_abs + 1, w).astype(F32)
        pooled.append(acc / count - u_pool[:, cols])
    return pooled


def _pool_seq_major(u_pool, hist_ref, i, nb, tt, pos0):
    m = tt * nb
    t_abs = pos0 + i * tt + lax.broadcasted_iota(jnp.int32, (nb, tt, POOL_GROUP), 1)
    pooled = []
    for k, w in enumerate(POOL_WINDOWS):
        cur = u_pool[:, k * POOL_GROUP:(k + 1) * POOL_GROUP].reshape(nb, tt, POOL_GROUP)
        hist_ref[k, :, HIST_PAD:HIST_PAD + tt, :] = cur
        acc = cur
        for j in range(1, w):
            acc = acc + hist_ref[k, :, HIST_PAD - j:HIST_PAD - j + tt, :]
        count = jnp.minimum(t_abs + 1, w).astype(F32)
        pooled.append((acc / count - cur).reshape(m, POOL_GROUP))
    return pooled


def _mixer_kernel(x_ref, mod_ref, h0r_ref, h0i_ref, pool0_ref, n1_ref, win_ref, wsi_ref, wso_ref, wt_ref,
                  alr_ref, ali_ref, d_ref, wglu_ref, wpool_ref, pscale_ref, wout_ref, *rest,
                  nb, tt, n_sub, n_steps, pos0, time_major, cast_halved):
    n_cast = len(cast_halved)
    cast_src, rest = rest[:n_cast], rest[n_cast:]
    xo_ref, sr_out_ref, si_out_ref, pool_out_ref = rest[:4]
    cast_dst, rest = rest[4:4 + n_cast], rest[4 + n_cast:]
    z_ref, st_r, st_i, hist_ref, perm_ref = rest
    n_blocks = tt // BLOCK_STEPS
    mb = n_blocks * nb
    i = pl.program_id(0)
    _cast_blocks(cast_src, cast_dst, cast_halved)
    m = tt * nb
    hist_rows = POOL_HIST * nb

    @pl.when(i == 0)
    def _():
        st_r[...] = h0r_ref[...].T if time_major else h0r_ref[...]
        st_i[...] = h0i_ref[...].T if time_major else h0i_ref[...]
        if time_major:
            hist_ref[0:hist_rows, :] = pool0_ref[...]
        else:
            for k in range(len(POOL_WINDOWS)):
                hist_ref[k, :, 0:HIST_PAD, :] = pool0_ref[:, :, k * POOL_GROUP:(k + 1) * POOL_GROUP]

    for sub in range(n_sub):
        tile = i * n_sub + sub
        t_rows = slice(sub * tt, (sub + 1) * tt)
        x = (x_ref[...] if time_major else x_ref[:, t_rows, :]).reshape(m, D_MODEL)
        shift1 = mod_ref[:, :, 0 * D_MODEL:1 * D_MODEL]
        scale1 = mod_ref[:, :, 1 * D_MODEL:2 * D_MODEL]
        gate1 = mod_ref[:, :, 2 * D_MODEL:3 * D_MODEL]
        hb = _modulate(_rmsnorm(x, n1_ref[...]), shift1, scale1, tt, nb, time_major).astype(BF16)

        u_ssm = _dot(hb, win_ref[:, 0:D_SSM])
        u_pool = _dot(hb, win_ref[:, D_SSM:D_SSM + D_POOL])
        u_tm = u_ssm if time_major else _to_time_major(u_ssm, perm_ref, nb, tt)
        u_blocks = u_tm.reshape(n_blocks, BLOCK_STEPS, nb, D_SSM)
        u_steps = [u_blocks[:, j].reshape(mb, D_SSM).astype(BF16) for j in range(BLOCK_STEPS)]
        y_sets = []
        for s in range(N_SETS):
            chans = slice(s * SET_IN, (s + 1) * SET_IN)
            lhs = jnp.concatenate([u[:, chans] for u in u_steps], axis=1)
            z_ref[:, 2 * s * SET_STATES:2 * (s + 1) * SET_STATES] = _dot(lhs, wsi_ref[s])
            y_sets.append(_dot(lhs, wt_ref[s]))
        g_ssm = _dot(hb, win_ref[:, D_SSM + D_POOL:D_SSM + D_POOL + D_MODEL])
        g_pool = _dot(hb, win_ref[:, D_SSM + D_POOL + D_MODEL:])

        _block_scan(z_ref, st_r, st_i, alr_ref, ali_ref, nb, n_blocks)
        if time_major:
            pooled = _pool_time_major(u_pool, hist_ref, tile, nb, tt, pos0)
            new_buf_rows = (slice(m, m + hist_rows), slice(None))
            if n_steps * n_sub > 1:
                hist_ref[0:hist_rows, :] = hist_ref[new_buf_rows]
        else:
            pooled = _pool_seq_major(u_pool, hist_ref, tile, nb, tt, pos0)
            if n_steps * n_sub > 1:
                hist_ref[:, :, 0:HIST_PAD, :] = hist_ref[:, :, tt:tt + HIST_PAD, :]
        z_pieces = [_dot(pooled[k].astype(BF16), wpool_ref[k]) for k in range(len(POOL_WINDOWS))]
        br_pool = jnp.concatenate(z_pieces, axis=-1) * pscale_ref[...]

        for s in range(N_SETS):
            s_in = z_ref[:, 2 * s * SET_STATES:2 * (s + 1) * SET_STATES].astype(BF16)
            y_sets[s] = y_sets[s] + _dot(s_in, wso_ref[s])
        y_steps = [jnp.concatenate([y[:, j * SET_IN:(j + 1) * SET_IN] for y in y_sets], axis=1).reshape(n_blocks, nb, D_SSM)
                   for j in range(BLOCK_STEPS)]
        y_cs = jnp.stack(y_steps, axis=1).reshape(m, D_SSM)
        if not time_major:
            y_cs = _to_seq_major(y_cs, perm_ref, nb, tt)

        for h in range(N_TAIL_SPLITS):
            rows = slice(h * m // N_TAIL_SPLITS, (h + 1) * m // N_TAIL_SPLITS)
            y_ssm = y_cs[rows] + d_ref[...] * u_ssm[rows]
            glu = _dot(jax.nn.gelu(y_ssm).astype(BF16), wglu_ref[...])
            br_ssm = glu[:, :D_MODEL] * _two_sigmoid(glu[:, D_MODEL:])
            merged2 = _two_sigmoid(g_ssm[rows]) * br_ssm + _two_sigmoid(g_pool[rows]) * br_pool[rows]
            o = _dot(merged2.astype(BF16), wout_ref[...])
            if time_major:
                xo_ref[rows, :] = x[rows] + _rows_gate(o, gate1, tt // N_TAIL_SPLITS, nb, True)
            else:
                nb_h = nb // N_TAIL_SPLITS
                seqs = slice(h * nb_h, (h + 1) * nb_h)
                x1 = x[rows] + _rows_gate(o, gate1[seqs], tt, nb_h, False)
                xo_ref[seqs, t_rows, :] = x1.reshape(nb_h, tt, D_MODEL)

    @pl.when(i == n_steps - 1)
    def _():
        sr_out_ref[...] = st_r[...].T if time_major else st_r[...]
        si_out_ref[...] = st_i[...].T if time_major else st_i[...]
        if time_major:
            pool_out_ref[...] = hist_ref[new_buf_rows]
        else:
            pool_out_ref[...] = jnp.concatenate(
                [hist_ref[k, :, tt + HIST_PAD - POOL_HIST:tt + HIST_PAD, :] for k in range(len(POOL_WINDOWS))],
                axis=-1)


def _const_spec(shape):
    nd = len(shape)
    return pl.BlockSpec(shape, lambda i, _nd=nd: (0,) * _nd, pipeline_mode=pl.Buffered(1))


def _row_spec(nb, tt, time_major):
    if time_major:
        return pl.BlockSpec((tt * nb, D_MODEL), lambda i: (i, 0))
    return pl.BlockSpec((nb, tt, D_MODEL), lambda i: (0, i, 0))


def _mixer(x, mod, h0r, h0i, pool0, n1, win, wsi, wso, wt, alr, ali, d, wglu, wpool, pscale, wout,
           *, nb, tt, n_sub, n_steps, pos0, time_major, cast=()):
    m = tt * nb
    consts = (mod, h0r, h0i, pool0, n1, win, wsi, wso, wt, alr, ali, d, wglu, wpool, pscale, wout)
    cast, cast_halved = tuple(a for a, _ in cast), tuple(n for _, n in cast)
    cast_specs = _cast_specs(cast, n_steps)
    if time_major:
        hist_shape = (POOL_HIST * nb + m, D_POOL)
        pool_out_shape = (POOL_HIST * nb, D_POOL)
    else:
        hist_shape = (len(POOL_WINDOWS), nb, HIST_PAD + tt, POOL_GROUP)
        pool_out_shape = (nb, POOL_HIST, D_POOL)
    kern = functools.partial(_mixer_kernel, nb=nb, tt=tt, n_sub=n_sub, n_steps=n_steps, pos0=pos0,
                             time_major=time_major, cast_halved=cast_halved)
    return pl.pallas_call(
        kern,
        grid=(n_steps,),
        in_specs=[_row_spec(nb, n_sub * tt, time_major)] + [_const_spec(a.shape) for a in consts] + cast_specs,
        out_specs=(_row_spec(nb, n_sub * tt, time_major),
                   pl.BlockSpec(h0r.shape, lambda i: (0, 0)),
                   pl.BlockSpec(h0r.shape, lambda i: (0, 0)),
                   pl.BlockSpec(pool_out_shape, lambda i: (0,) * len(pool_out_shape))) + tuple(cast_specs),
        out_shape=(jax.ShapeDtypeStruct(x.shape, F32),
                   jax.ShapeDtypeStruct(h0r.shape, F32),
                   jax.ShapeDtypeStruct(h0r.shape, F32),
                   jax.ShapeDtypeStruct(pool_out_shape, F32)) + tuple(jax.ShapeDtypeStruct(a.shape, BF16) for a in cast),
        scratch_shapes=[pltpu.VMEM((m // BLOCK_STEPS, 2 * N_STATE), F32),
                        pltpu.VMEM((nb, N_STATE), F32), pltpu.VMEM((nb, N_STATE), F32),
                        pltpu.VMEM(hist_shape, F32),
                        pltpu.VMEM((D_SSM // LANES, m, LANES), F32)],
        compiler_params=pltpu.CompilerParams(dimension_semantics=("arbitrary",),
                                             vmem_limit_bytes=VMEM_LIMIT),
        name=f"mixer_nb{nb}",
    )(x, *consts, *cast)


def _ffn_kernel(x_ref, mod_ref, n2_ref, nf_ref, wfi_ref, wfo_ref, o_ref, *, nb, tt, time_major):
    for h in range(N_FFN_SPLITS):
        if time_major:
            tt_h, nb_h = tt // N_FFN_SPLITS, nb
            piece = (slice(h * tt_h * nb, (h + 1) * tt_h * nb), slice(None))
            mod = mod_ref
        else:
            tt_h, nb_h = tt, nb // N_FFN_SPLITS
            piece = (slice(h * nb_h, (h + 1) * nb_h), slice(None), slice(None))
            mod = mod_ref.at[h * nb_h:(h + 1) * nb_h]
        x = x_ref[piece].reshape(tt_h * nb_h, D_MODEL)
        shift2 = mod[:, :, 3 * D_MODEL:4 * D_MODEL]
        scale2 = mod[:, :, 4 * D_MODEL:5 * D_MODEL]
        gate2 = mod[:, :, 5 * D_MODEL:6 * D_MODEL]
        hb = _modulate(_rmsnorm(x, n2_ref[...]), shift2, scale2, tt_h, nb_h, time_major).astype(BF16)
        f_a = _dot(hb, wfi_ref[:, :D_FF])
        f_b = _dot(hb, wfi_ref[:, D_FF:])
        act = (f_a * _two_sigmoid(f_a) * f_b).astype(BF16)
        x2 = x + _rows_gate(_dot(act, wfo_ref[...]), gate2, tt_h, nb_h, time_major)
        y = _rmsnorm(x2, nf_ref[...])
        o_ref[piece] = y if time_major else y.reshape(nb_h, tt_h, D_MODEL)


def _ffn(x, mod, n2, nf, wfi, wfo, *, nb, tt, n_steps, time_major):
    consts = (mod, n2, nf, wfi, wfo)
    return pl.pallas_call(
        functools.partial(_ffn_kernel, nb=nb, tt=tt, time_major=time_major),
        grid=(n_steps,),
        in_specs=[_row_spec(nb, tt, time_major)] + [_const_spec(a.shape) for a in consts],
        out_specs=_row_spec(nb, tt, time_major),
        out_shape=jax.ShapeDtypeStruct(x.shape, F32),
        compiler_params=pltpu.CompilerParams(dimension_semantics=("arbitrary",),
                                             vmem_limit_bytes=VMEM_LIMIT),
        name=f"ffn_nb{nb}",
    )(x, *consts)


def _trunk(x, mod, h0r, h0i, pool0, pos0, p, *, time_major, ffn_w):
    nb, t_len, _ = x.shape
    tt = min(t_len, TILE_ROWS // nb)
    n_sub = 1 if time_major else MIXER_SUBTILES
    tt_ffn = min(t_len, FFN_TILE_ROWS // nb)
    if time_major:
        rows = lambda a: jnp.transpose(a, (1, 0, 2)).reshape(a.shape[1] * nb, a.shape[2])
        unrows = lambda a, t: jnp.transpose(a.reshape(t, nb, a.shape[-1]), (1, 0, 2))
        x_in, pool_in, mod3 = rows(x), rows(pool0), mod[None]
        flat_state = lambda a: jnp.transpose(a, (1, 2, 0)).reshape(N_STATE, nb)
        unflat_state = lambda a: jnp.transpose(a.reshape(N_SSM_GROUPS, SSM_STATE, nb), (2, 0, 1))[None]
    else:
        assert nb == SUBLANES
        x_in, mod3 = x, mod[:, None, :]
        pool_in = jnp.pad(pool0, ((0, 0), (HIST_PAD - POOL_HIST, 0), (0, 0)))
        flat_state = lambda a: a.reshape(nb, N_STATE)
        unflat_state = lambda a: a.reshape(1, nb, N_SSM_GROUPS, SSM_STATE)
    cast = ((ffn_w[0], (0, D_FF)), (ffn_w[1], (0, 0))) if ffn_w[0].dtype == F32 else ()
    x1, s_r, s_i, new_buf, *cast_out = _mixer(
        x_in, mod3, flat_state(h0r), flat_state(h0i), pool_in,
        p["n1"], p["win"], p["wsi"], p["wso"], p["wt"], p["alr"], p["ali"], p["d"], p["wglu"], p["wpool"],
        p["pscale"], p["wout"], nb=nb, tt=tt, n_sub=n_sub, n_steps=t_len // (tt * n_sub), pos0=pos0,
        time_major=time_major, cast=cast)
    wfi, wfo = cast_out if cast else ffn_w
    y = _ffn(x1, mod3, p["n2"], p["nf"], wfi, wfo, nb=nb, tt=tt_ffn, n_steps=t_len // tt_ffn,
             time_major=time_major)
    if time_major:
        y, new_buf = unrows(y, t_len), unrows(new_buf, POOL_HIST)
    return (y, unflat_state(s_r), unflat_state(s_i), new_buf[None]), (wfi, wfo)


def kernel(x_prompt, x_sample, c_prompt, c_sample, state_ssm_re, state_ssm_im, state_pool, norm1_g, norm2_g, normf_g, w_ada, b_ada, w_in, ssm_lam_re, ssm_lam_im, ssm_log_dt, ssm_b_re, ssm_b_im, ssm_c_re, ssm_c_im, ssm_d, w_glu, w_pool, pool_scale, w_out, w_ffn_in, w_ffn_out):
    n_prompt = x_prompt.shape[0]
    mod_p, mod_s, win, wglu, wout, wpool = _ada(
        c_prompt, c_sample, w_ada[0], b_ada[0],
        cast=((w_in[0], (D_SSM + D_POOL, w_in.shape[-1])), (w_glu[0], (0, 2 * D_MODEL)),
              (w_out[0], (0, D_MODEL)), (w_pool[0].reshape(D_POOL, POOL_OUT), (0, 0))))
    alr, ali, wsi, wso, wt = _ssm_prep(ssm_lam_re[0], ssm_lam_im[0], ssm_log_dt[0], ssm_b_re[0], ssm_b_im[0],
                                       ssm_c_re[0], ssm_c_im[0])
    p = dict(
        n1=norm1_g[0].reshape(1, -1), n2=norm2_g[0].reshape(1, -1), nf=normf_g.reshape(1, -1),
        win=win, wsi=wsi, wso=wso, wt=wt, alr=alr, ali=ali, d=ssm_d[0].reshape(1, -1),
        wglu=wglu, wpool=wpool.reshape(len(POOL_WINDOWS), POOL_GROUP, POOL_OUT), pscale=pool_scale[0].reshape(1, -1),
        wout=wout)
    zero_state = jnp.zeros((n_prompt, N_SSM_GROUPS, SSM_STATE), F32)
    zero_pool = jnp.zeros((n_prompt, POOL_HIST, D_POOL), F32)
    ffn_f32 = (w_ffn_in[0], w_ffn_out[0])
    (y_p, p_re, p_im, p_pool), ffn_bf16 = _trunk(x_prompt, mod_p, zero_state, zero_state, zero_pool, 0, p,
                                                  time_major=False, ffn_w=ffn_f32)
    (y_s, s_re, s_im, s_pool), _ = _trunk(x_sample, mod_s, state_ssm_re[0], state_ssm_im[0], state_pool[0],
                                          PAST_LEN, p, time_major=True, ffn_w=ffn_bf16)
    return (y_p, y_s, p_re, p_im, p_pool, s_re, s_im, s_pool)
```

```python
import functools

import jax
import jax.numpy as jnp
from jax import lax
from jax.experimental import pallas as pl
from jax.experimental.pallas import tpu as pltpu

D_MODEL = 1024
D_SSM = 512
SSM_GROUP = 16
N_SSM_GROUPS = 32
SSM_STATE = 64
N_STATE = N_SSM_GROUPS * SSM_STATE
D_POOL = 512
POOL_WINDOWS = (2, 4, 8, 16)
POOL_GROUP = 128
POOL_OUT = 256
POOL_HIST = 15
D_FF = 2816
N_MOD = 6
EPS = 1e-6
PAST_LEN = 16384

SUBLANES = 8
BF16_SUBLANES = 16
LANES = 128
TILE_ROWS = 512
MIXER_SUBTILES = 2
FFN_TILE_ROWS = 1024
HIST_PAD = 16
N_TAIL_SPLITS = 2
N_FFN_SPLITS = 4
ADA_ROW_BLOCKS = 4
MXU_DEPTH = 256
BLOCK_STEPS = 4
SET_GROUPS = MXU_DEPTH // (BLOCK_STEPS * SSM_GROUP)
N_SETS = N_SSM_GROUPS // SET_GROUPS
SET_IN = SET_GROUPS * SSM_GROUP
SET_STATES = SET_GROUPS * SSM_STATE
VMEM_LIMIT = 60 * 1024 * 1024

BF16 = jnp.bfloat16
F32 = jnp.float32


def _dot(a, b):
    return jnp.dot(a, b, preferred_element_type=F32)


def _sigmoid(x):
    return 0.5 * jnp.tanh(0.5 * x) + 0.5


def _two_sigmoid(half_x):
    return jnp.tanh(half_x) + 1.0


def _rmsnorm(x, g):
    return x * lax.rsqrt(jnp.mean(x * x, axis=-1, keepdims=True) + EPS) * g


def _tile3(v, tt, nb, time_major):
    lead, inner = (tt, nb) if time_major else (nb, tt)
    return v.reshape(lead, inner, v.shape[-1])


def _modulate(h, shift, scale, tt, nb, time_major):
    return (_tile3(h, tt, nb, time_major) * (1.0 + scale) + shift).reshape(h.shape)


def _rows_gate(v, gate, tt, nb, time_major):
    return (_tile3(v, tt, nb, time_major) * gate).reshape(v.shape)


def _cast_specs(arrays, n_steps):
    specs = []
    for a in arrays:
        n_blk = max(k for k in range(1, n_steps + 1)
                    if n_steps % k == 0 and a.shape[0] % (k * BF16_SUBLANES) == 0)
        specs.append(pl.BlockSpec((a.shape[0] // n_blk, a.shape[1]),
                                  lambda i, _hold=n_steps // n_blk: (i // _hold, 0)))
    return specs


def _cast_blocks(srcs, dsts, halved):
    for src, dst, (lo, hi) in zip(srcs, dsts, halved):
        if lo > 0:
            dst[:, :lo] = src[:, :lo].astype(BF16)
        if hi > lo:
            dst[:, lo:hi] = (src[:, lo:hi] * 0.5).astype(BF16)
        if hi < src.shape[-1]:
            dst[:, hi:] = src[:, hi:].astype(BF16)


def _ada_kernel(cp_ref, cs_ref, w_ref, b_ref, *rest, halved):
    n_cast = len(halved)
    cast_src, (op_ref, os_ref), cast_dst = rest[:n_cast], rest[n_cast:n_cast + 2], rest[n_cast + 2:]
    _cast_blocks(cast_src, cast_dst, halved)
    k = pl.program_id(0)
    w = w_ref[...].astype(BF16)
    for c_ref, o_ref in ((cp_ref, op_ref), (cs_ref, os_ref)):
        c = c_ref[...]
        part = _dot((c * _sigmoid(c)).astype(BF16), w)

        @pl.when(k == 0)
        def _(o_ref=o_ref, part=part):
            o_ref[...] = b_ref[...] + part

        @pl.when(k > 0)
        def _(o_ref=o_ref, part=part):
            o_ref[...] += part


def _ada(c_prompt, c_sample, w_ada, b_ada, cast):
    n_p, n_s = c_prompt.shape[0], c_sample.shape[0]
    kb = D_MODEL // ADA_ROW_BLOCKS
    arrays, halved = tuple(a for a, _ in cast), tuple(h for _, h in cast)
    cast_specs = _cast_specs(arrays, ADA_ROW_BLOCKS)
    n_out = N_MOD * D_MODEL
    return pl.pallas_call(
        functools.partial(_ada_kernel, halved=halved),
        grid=(ADA_ROW_BLOCKS,),
        in_specs=[pl.BlockSpec((n_p, kb), lambda k: (0, k)),
                  pl.BlockSpec((n_s, kb), lambda k: (0, k)),
                  pl.BlockSpec((kb, n_out), lambda k: (k, 0)),
                  pl.BlockSpec((1, n_out), lambda k: (0, 0))] + cast_specs,
        out_specs=[pl.BlockSpec((n_p, n_out), lambda k: (0, 0)), pl.BlockSpec((n_s, n_out), lambda k: (0, 0))]
        + cast_specs,
        out_shape=[jax.ShapeDtypeStruct((n_p, n_out), F32), jax.ShapeDtypeStruct((n_s, n_out), F32)]
        + [jax.ShapeDtypeStruct(a.shape, BF16) for a in arrays],
        compiler_params=pltpu.CompilerParams(dimension_semantics=("arbitrary",),
                                             vmem_limit_bytes=VMEM_LIMIT),
        name="ada_mod",
    )(c_prompt, c_sample, w_ada, b_ada.reshape(1, -1), *arrays)


def _cmul(ar, ai, br, bi):
    return ar * br - ai * bi, ar * bi + ai * br


def _ssm_prep_kernel(lr_ref, li_ref, ldt_ref, bt_r_ref, bt_i_ref, cr_ref, ci_ref,
                     alr_ref, ali_ref, wsi_ref, wso_ref, wt_ref):
    lr = lr_ref[...]
    li = li_ref[...]
    dt = jnp.exp(ldt_ref[...])
    mag = jnp.exp(lr * dt)
    a_r = mag * jnp.cos(li * dt)
    a_i = mag * jnp.sin(li * dt)
    den = lr * lr + li * li
    nr = a_r - 1.0
    ni = a_i
    f = ((nr * lr + ni * li) / den, (ni * lr - nr * li) / den)
    powers = [(a_r, a_i)]
    for _ in range(BLOCK_STEPS - 1):
        powers.append(_cmul(a_r, a_i, *powers[-1]))
    for g in range(N_SSM_GROUPS):
        states = slice(g * SSM_STATE, (g + 1) * SSM_STATE)
        alr_ref[:, states] = powers[-1][0][g:g + 1, :]
        ali_ref[:, states] = powers[-1][1][g:g + 1, :]
    wsi_ref[...] = jnp.zeros(wsi_ref.shape, BF16)
    wso_ref[...] = jnp.zeros(wso_ref.shape, BF16)
    wt_ref[...] = jnp.zeros(wt_ref.shape, BF16)
    k_shape = (BLOCK_STEPS * SET_IN, SET_IN)
    row_group = lax.broadcasted_iota(jnp.int32, k_shape, 0) % SET_IN // SSM_GROUP
    col_group = lax.broadcasted_iota(jnp.int32, k_shape, 1) // SSM_GROUP
    same_group = row_group == col_group
    nt_dims = (((1,), (1,)), ((), ()))
    for s in range(N_SETS):
        groups = slice(s * SET_GROUPS, (s + 1) * SET_GROUPS)

        def rows_of(pair, groups=groups):
            return tuple(jnp.broadcast_to(v[groups][:, None, :], (SET_GROUPS, SSM_GROUP, SSM_STATE))
                         .reshape(SET_IN, SSM_STATE) for v in pair)

        bt = (bt_r_ref[groups].reshape(SET_IN, SSM_STATE), bt_i_ref[groups].reshape(SET_IN, SSM_STATE))
        c = (cr_ref[groups].reshape(SET_IN, SSM_STATE), ci_ref[groups].reshape(SET_IN, SSM_STATE))
        a_rows = rows_of(powers[0])
        bb_lags = [_cmul(*rows_of(f), *bt)]
        for _ in range(BLOCK_STEPS - 1):
            bb_lags.append(_cmul(*a_rows, *bb_lags[-1]))
        bb_r = jnp.concatenate([bb_lags[BLOCK_STEPS - 1 - j][0] for j in range(BLOCK_STEPS)], axis=0)
        bb_i = jnp.concatenate([bb_lags[BLOCK_STEPS - 1 - j][1] for j in range(BLOCK_STEPS)], axis=0)
        k_all = (lax.dot_general(bb_r, c[0], nt_dims, precision=lax.Precision.HIGHEST, preferred_element_type=F32)
                 - lax.dot_general(bb_i, c[1], nt_dims, precision=lax.Precision.HIGHEST, preferred_element_type=F32))
        k_all = jnp.where(same_group, k_all, 0.0).astype(BF16)
        for j in range(BLOCK_STEPS):
            lag = BLOCK_STEPS - 1 - j
            for t in range(lag, BLOCK_STEPS):
                wt_ref[s, (t - lag) * SET_IN:(t - lag + 1) * SET_IN, t * SET_IN:(t + 1) * SET_IN] = (
                    k_all[j * SET_IN:(j + 1) * SET_IN])
            for gj in range(SET_GROUPS):
                rows = slice(j * SET_IN + gj * SSM_GROUP, j * SET_IN + (gj + 1) * SSM_GROUP)
                wsi_ref[s, rows, gj * SSM_STATE:(gj + 1) * SSM_STATE] = bb_r[rows, :].astype(BF16)
                wsi_ref[s, rows, SET_STATES + gj * SSM_STATE:SET_STATES + (gj + 1) * SSM_STATE] = (
                    bb_i[rows, :].astype(BF16))
        for t in range(BLOCK_STEPS):
            p_r, p_i = rows_of(powers[t])
            from_sr = jnp.transpose(c[0] * p_r - c[1] * p_i).astype(BF16)
            from_si = jnp.transpose(-c[0] * p_i - c[1] * p_r).astype(BF16)
            for gj in range(SET_GROUPS):
                src = slice(gj * SSM_GROUP, (gj + 1) * SSM_GROUP)
                out = slice(t * SET_IN + gj * SSM_GROUP, t * SET_IN + (gj + 1) * SSM_GROUP)
                wso_ref[s, gj * SSM_STATE:(gj + 1) * SSM_STATE, out] = from_sr[:, src]
                wso_ref[s, SET_STATES + gj * SSM_STATE:SET_STATES + (gj + 1) * SSM_STATE, out] = from_si[:, src]


def _ssm_prep(lam_re, lam_im, log_dt, b_re, b_im, c_re, c_im):
    return pl.pallas_call(
        _ssm_prep_kernel,
        out_shape=(jax.ShapeDtypeStruct((1, N_STATE), F32), jax.ShapeDtypeStruct((1, N_STATE), F32),
                   jax.ShapeDtypeStruct((N_SETS, BLOCK_STEPS * SET_IN, 2 * SET_STATES), BF16),
                   jax.ShapeDtypeStruct((N_SETS, 2 * SET_STATES, BLOCK_STEPS * SET_IN), BF16),
                   jax.ShapeDtypeStruct((N_SETS, BLOCK_STEPS * SET_IN, BLOCK_STEPS * SET_IN), BF16)),
        name="ssm_prep",
    )(lam_re, lam_im, log_dt.reshape(N_SSM_GROUPS, 1), jnp.transpose(b_re, (0, 2, 1)), jnp.transpose(b_im, (0, 2, 1)),
      c_re, c_im)


def _block_scan(z_ref, st_r, st_i, alr_ref, ali_ref, nb, n_blocks):
    for s in range(N_SETS):
        lanes = slice(s * SET_STATES, (s + 1) * SET_STATES)
        re_lanes = slice(2 * s * SET_STATES, (2 * s + 1) * SET_STATES)
        im_lanes = slice((2 * s + 1) * SET_STATES, (2 * s + 2) * SET_STATES)
        ar = jnp.broadcast_to(alr_ref[:, lanes], (SUBLANES, SET_STATES))
        ai = jnp.broadcast_to(ali_ref[:, lanes], (SUBLANES, SET_STATES))

        def seq_block(rb, carry, lanes=lanes, re_lanes=re_lanes, im_lanes=im_lanes, ar=ar, ai=ai):
            r0 = pl.multiple_of(rb * SUBLANES, SUBLANES)
            sr = st_r[pl.ds(r0, SUBLANES), lanes]
            si = st_i[pl.ds(r0, SUBLANES), lanes]

            def step(k, state):
                sr, si = state
                rows = pl.ds(pl.multiple_of(k * nb + r0, SUBLANES), SUBLANES)
                zr = z_ref[rows, re_lanes]
                zi = z_ref[rows, im_lanes]
                z_ref[rows, re_lanes] = sr
                z_ref[rows, im_lanes] = si
                return ar * sr - ai * si + zr, ar * si + ai * sr + zi

            sr, si = lax.fori_loop(0, n_blocks, step, (sr, si), unroll=True)
            st_r[pl.ds(r0, SUBLANES), lanes] = sr
            st_i[pl.ds(r0, SUBLANES), lanes] = si
            return carry

        if nb == SUBLANES:
            seq_block(0, 0)
        else:
            lax.fori_loop(0, nb // SUBLANES, seq_block, 0)


def _to_time_major(v, perm_ref, nb, tt):
    for c in range(v.shape[-1] // LANES):
        for b in range(nb):
            perm_ref.at[c][pl.ds(b, tt, stride=nb), :] = v[b * tt:(b + 1) * tt, c * LANES:(c + 1) * LANES]
    return jnp.concatenate([perm_ref[c] for c in range(v.shape[-1] // LANES)], axis=-1)


def _to_seq_major(v, perm_ref, nb, tt):
    n_slabs = v.shape[-1] // LANES
    for c in range(n_slabs):
        perm_ref[c] = v[:, c * LANES:(c + 1) * LANES]
    return jnp.concatenate(
        [jnp.concatenate([perm_ref.at[c][pl.ds(b, tt, stride=nb), :] for b in range(nb)], axis=0)
         for c in range(n_slabs)], axis=-1)


def _pool_time_major(u_pool, hist_ref, i, nb, tt, pos0):
    m = tt * nb
    hist_rows = POOL_HIST * nb
    hist_ref[hist_rows:hist_rows + m, :] = u_pool
    t_abs = pos0 + i * tt + lax.broadcasted_iota(jnp.int32, (tt, nb, POOL_GROUP), 0).reshape(m, POOL_GROUP)
    pooled = []
    for k, w in enumerate(POOL_WINDOWS):
        cols = slice(k * POOL_GROUP, (k + 1) * POOL_GROUP)
        acc = u_pool[:, cols]
        for j in range(1, w):
            acc = acc + hist_ref[hist_rows - j * nb:hist_rows - j * nb + m, cols]
        count = jnp.minimum(t_abs + 1, w).astype(F32)
        pooled.append(acc / count - u_pool[:, cols])
    return pooled


def _pool_seq_major(u_pool, hist_ref, i, nb, tt, pos0):
    m = tt * nb
    t_abs = pos0 + i * tt + lax.broadcasted_iota(jnp.int32, (nb, tt, POOL_GROUP), 1)
    pooled = []
    for k, w in enumerate(POOL_WINDOWS):
        cur = u_pool[:, k * POOL_GROUP:(k + 1) * POOL_GROUP].reshape(nb, tt, POOL_GROUP)
        hist_ref[k, :, HIST_PAD:HIST_PAD + tt, :] = cur
        acc = cur
        for j in range(1, w):
            acc = acc + hist_ref[k, :, HIST_PAD - j:HIST_PAD - j + tt, :]
        count = jnp.minimum(t_abs + 1, w).astype(F32)
        pooled.append((acc / count - cur).reshape(m, POOL_GROUP))
    return pooled


def _mixer_kernel(x_ref, mod_ref, h0r_ref, h0i_ref, pool0_ref, n1_ref, win_ref, wsi_ref, wso_ref, wt_ref,
                  alr_ref, ali_ref, d_ref, wglu_ref, wpool_ref, pscale_ref, wout_ref, *rest,
                  nb, tt, n_sub, n_steps, pos0, time_major, cast_halved):
    n_cast = len(cast_halved)
    cast_src, rest = rest[:n_cast], rest[n_cast:]
    xo_ref, sr_out_ref, si_out_ref, pool_out_ref = rest[:4]
    cast_dst, rest = rest[4:4 + n_cast], rest[4 + n_cast:]
    z_ref, st_r, st_i, hist_ref, perm_ref = rest
    n_blocks = tt // BLOCK_STEPS
    mb = n_blocks * nb
    i = pl.program_id(0)
    _cast_blocks(cast_src, cast_dst, cast_halved)
    m = tt * nb
    hist_rows = POOL_HIST * nb

    @pl.when(i == 0)
    def _():
        st_r[...] = h0r_ref[...].T if time_major else h0r_ref[...]
        st_i[...] = h0i_ref[...].T if time_major else h0i_ref[...]
        if time_major:
            hist_ref[0:hist_rows, :] = pool0_ref[...]
        else:
            for k in range(len(POOL_WINDOWS)):
                hist_ref[k, :, 0:HIST_PAD, :] = pool0_ref[:, :, k * POOL_GROUP:(k + 1) * POOL_GROUP]

    for sub in range(n_sub):
        tile = i * n_sub + sub
        t_rows = slice(sub * tt, (sub + 1) * tt)
        x = (x_ref[...] if time_major else x_ref[:, t_rows, :]).reshape(m, D_MODEL)
        shift1 = mod_ref[:, :, 0 * D_MODEL:1 * D_MODEL]
        scale1 = mod_ref[:, :, 1 * D_MODEL:2 * D_MODEL]
        gate1 = mod_ref[:, :, 2 * D_MODEL:3 * D_MODEL]
        hb = _modulate(_rmsnorm(x, n1_ref[...]), shift1, scale1, tt, nb, time_major).astype(BF16)

        u_ssm = _dot(hb, win_ref[:, 0:D_SSM])
        u_pool = _dot(hb, win_ref[:, D_SSM:D_SSM + D_POOL])
        u_tm = u_ssm if time_major else _to_time_major(u_ssm, perm_ref, nb, tt)
        u_blocks = u_tm.reshape(n_blocks, BLOCK_STEPS, nb, D_SSM)
        u_steps = [u_blocks[:, j].reshape(mb, D_SSM).astype(BF16) for j in range(BLOCK_STEPS)]
        y_sets = []
        for s in range(N_SETS):
            chans = slice(s * SET_IN, (s + 1) * SET_IN)
            lhs = jnp.concatenate([u[:, chans] for u in u_steps], axis=1)
            z_ref[:, 2 * s * SET_STATES:2 * (s + 1) * SET_STATES] = _dot(lhs, wsi_ref[s])
            y_sets.append(_dot(lhs, wt_ref[s]))
        g_ssm = _dot(hb, win_ref[:, D_SSM + D_POOL:D_SSM + D_POOL + D_MODEL])
        g_pool = _dot(hb, win_ref[:, D_SSM + D_POOL + D_MODEL:])

        _block_scan(z_ref, st_r, st_i, alr_ref, ali_ref, nb, n_blocks)
        if time_major:
            pooled = _pool_time_major(u_pool, hist_ref, tile, nb, tt, pos0)
            new_buf_rows = (slice(m, m + hist_rows), slice(None))
            if n_steps * n_sub > 1:
                hist_ref[0:hist_rows, :] = hist_ref[new_buf_rows]
        else:
            pooled = _pool_seq_major(u_pool, hist_ref, tile, nb, tt, pos0)
            if n_steps * n_sub > 1:
                hist_ref[:, :, 0:HIST_PAD, :] = hist_ref[:, :, tt:tt + HIST_PAD, :]
        z_pieces = [_dot(pooled[k].astype(BF16), wpool_ref[k]) for k in range(len(POOL_WINDOWS))]
        br_pool = jnp.concatenate(z_pieces, axis=-1) * pscale_ref[...]

        for s in range(N_SETS):
            s_in = z_ref[:, 2 * s * SET_STATES:2 * (s + 1) * SET_STATES].astype(BF16)
            y_sets[s] = y_sets[s] + _dot(s_in, wso_ref[s])
        y_steps = [jnp.concatenate([y[:, j * SET_IN:(j + 1) * SET_IN] for y in y_sets], axis=1).reshape(n_blocks, nb, D_SSM)
                   for j in range(BLOCK_STEPS)]
        y_cs = jnp.stack(y_steps, axis=1).reshape(m, D_SSM)
        if not time_major:
            y_cs = _to_seq_major(y_cs, perm_ref, nb, tt)

        for h in range(N_TAIL_SPLITS):
            rows = slice(h * m // N_TAIL_SPLITS, (h + 1) * m // N_TAIL_SPLITS)
            y_ssm = y_cs[rows] + d_ref[...] * u_ssm[rows]
            glu = _dot(jax.nn.gelu(y_ssm).astype(BF16), wglu_ref[...])
            br_ssm = glu[:, :D_MODEL] * _two_sigmoid(glu[:, D_MODEL:])
            merged2 = _two_sigmoid(g_ssm[rows]) * br_ssm + _two_sigmoid(g_pool[rows]) * br_pool[rows]
            o = _dot(merged2.astype(BF16), wout_ref[...])
            if time_major:
                xo_ref[rows, :] = x[rows] + _rows_gate(o, gate1, tt // N_TAIL_SPLITS, nb, True)
            else:
                nb_h = nb // N_TAIL_SPLITS
                seqs = slice(h * nb_h, (h + 1) * nb_h)
                x1 = x[rows] + _rows_gate(o, gate1[seqs], tt, nb_h, False)
                xo_ref[seqs, t_rows, :] = x1.reshape(nb_h, tt, D_MODEL)

    @pl.when(i == n_steps - 1)
    def _():
        sr_out_ref[...] = st_r[...].T if time_major else st_r[...]
        si_out_ref[...] = st_i[...].T if time_major else st_i[...]
        if time_major:
            pool_out_ref[...] = hist_ref[new_buf_rows]
        else:
            pool_out_ref[...] = jnp.concatenate(
                [hist_ref[k, :, tt + HIST_PAD - POOL_HIST:tt + HIST_PAD, :] for k in range(len(POOL_WINDOWS))],
                axis=-1)


def _const_spec(shape):
    nd = len(shape)
    return pl.BlockSpec(shape, lambda i, _nd=nd: (0,) * _nd, pipeline_mode=pl.Buffered(1))


def _row_spec(nb, tt, time_major):
    if time_major:
        return pl.BlockSpec((tt * nb, D_MODEL), lambda i: (i, 0))
    return pl.BlockSpec((nb, tt, D_MODEL), lambda i: (0, i, 0))


def _mixer(x, mod, h0r, h0i, pool0, n1, win, wsi, wso, wt, alr, ali, d, wglu, wpool, pscale, wout,
           *, nb, tt, n_sub, n_steps, pos0, time_major, cast=()):
    m = tt * nb
    consts = (mod, h0r, h0i, pool0, n1, win, wsi, wso, wt, alr, ali, d, wglu, wpool, pscale, wout)
    cast, cast_halved = tuple(a for a, _ in cast), tuple(n for _, n in cast)
    cast_specs = _cast_specs(cast, n_steps)
    if time_major:
        hist_shape = (POOL_HIST * nb + m, D_POOL)
        pool_out_shape = (POOL_HIST * nb, D_POOL)
    else:
        hist_shape = (len(POOL_WINDOWS), nb, HIST_PAD + tt, POOL_GROUP)
        pool_out_shape = (nb, POOL_HIST, D_POOL)
    kern = functools.partial(_mixer_kernel, nb=nb, tt=tt, n_sub=n_sub, n_steps=n_steps, pos0=pos0,
                             time_major=time_major, cast_halved=cast_halved)
    return pl.pallas_call(
        kern,
        grid=(n_steps,),
        in_specs=[_row_spec(nb, n_sub * tt, time_major)] + [_const_spec(a.shape) for a in consts] + cast_specs,
        out_specs=(_row_spec(nb, n_sub * tt, time_major),
                   pl.BlockSpec(h0r.shape, lambda i: (0, 0)),
                   pl.BlockSpec(h0r.shape, lambda i: (0, 0)),
                   pl.BlockSpec(pool_out_shape, lambda i: (0,) * len(pool_out_shape))) + tuple(cast_specs),
        out_shape=(jax.ShapeDtypeStruct(x.shape, F32),
                   jax.ShapeDtypeStruct(h0r.shape, F32),
                   jax.ShapeDtypeStruct(h0r.shape, F32),
                   jax.ShapeDtypeStruct(pool_out_shape, F32)) + tuple(jax.ShapeDtypeStruct(a.shape, BF16) for a in cast),
        scratch_shapes=[pltpu.VMEM((m // BLOCK_STEPS, 2 * N_STATE), F32),
                        pltpu.VMEM((nb, N_STATE), F32), pltpu.VMEM((nb, N_STATE), F32),
                        pltpu.VMEM(hist_shape, F32),
                        pltpu.VMEM((D_SSM // LANES, m, LANES), F32)],
        compiler_params=pltpu.CompilerParams(dimension_semantics=("arbitrary",),
                                             vmem_limit_bytes=VMEM_LIMIT),
        name=f"mixer_nb{nb}",
    )(x, *consts, *cast)


def _ffn_kernel(x_ref, mod_ref, n2_ref, nf_ref, wfi_ref, wfo_ref, o_ref, *, nb, tt, time_major):
    for h in range(N_FFN_SPLITS):
        if time_major:
            tt_h, nb_h = tt // N_FFN_SPLITS, nb
            piece = (slice(h * tt_h * nb, (h + 1) * tt_h * nb), slice(None))
            mod = mod_ref
        else:
            tt_h, nb_h = tt, nb // N_FFN_SPLITS
            piece = (slice(h * nb_h, (h + 1) * nb_h), slice(None), slice(None))
            mod = mod_ref.at[h * nb_h:(h + 1) * nb_h]
        x = x_ref[piece].reshape(tt_h * nb_h, D_MODEL)
        shift2 = mod[:, :, 3 * D_MODEL:4 * D_MODEL]
        scale2 = mod[:, :, 4 * D_MODEL:5 * D_MODEL]
        gate2 = mod[:, :, 5 * D_MODEL:6 * D_MODEL]
        hb = _modulate(_rmsnorm(x, n2_ref[...]), shift2, scale2, tt_h, nb_h, time_major).astype(BF16)
        f_a = _dot(hb, wfi_ref[:, :D_FF])
        f_b = _dot(hb, wfi_ref[:, D_FF:])
        act = (f_a * _two_sigmoid(f_a) * f_b).astype(BF16)
        x2 = x + _rows_gate(_dot(act, wfo_ref[...]), gate2, tt_h, nb_h, time_major)
        y = _rmsnorm(x2, nf_ref[...])
        o_ref[piece] = y if time_major else y.reshape(nb_h, tt_h, D_MODEL)


def _ffn(x, mod, n2, nf, wfi, wfo, *, nb, tt, n_steps, time_major):
    consts = (mod, n2, nf, wfi, wfo)
    return pl.pallas_call(
        functools.partial(_ffn_kernel, nb=nb, tt=tt, time_major=time_major),
        grid=(n_steps,),
        in_specs=[_row_spec(nb, tt, time_major)] + [_const_spec(a.shape) for a in consts],
        out_specs=_row_spec(nb, tt, time_major),
        out_shape=jax.ShapeDtypeStruct(x.shape, F32),
        compiler_params=pltpu.CompilerParams(dimension_semantics=("arbitrary",),
                                             vmem_limit_bytes=VMEM_LIMIT),
        name=f"ffn_nb{nb}",
    )(x, *consts)


def _trunk(x, mod, h0r, h0i, pool0, pos0, p, *, time_major, ffn_w):
    nb, t_len, _ = x.shape
    tt = min(t_len, TILE_ROWS // nb)
    n_sub = 1 if time_major else MIXER_SUBTILES
    tt_ffn = min(t_len, FFN_TILE_ROWS // nb)
    if time_major:
        rows = lambda a: jnp.transpose(a, (1, 0, 2)).reshape(a.shape[1] * nb, a.shape[2])
        unrows = lambda a, t: jnp.transpose(a.reshape(t, nb, a.shape[-1]), (1, 0, 2))
        x_in, pool_in, mod3 = rows(x), rows(pool0), mod[None]
        flat_state = lambda a: jnp.transpose(a, (1, 2, 0)).reshape(N_STATE, nb)
        unflat_state = lambda a: jnp.transpose(a.reshape(N_SSM_GROUPS, SSM_STATE, nb), (2, 0, 1))[None]
    else:
        assert nb == SUBLANES
        x_in, mod3 = x, mod[:, None, :]
        pool_in = jnp.pad(pool0, ((0, 0), (HIST_PAD - POOL_HIST, 0), (0, 0)))
        flat_state = lambda a: a.reshape(nb, N_STATE)
        unflat_state = lambda a: a.reshape(1, nb, N_SSM_GROUPS, SSM_STATE)
    cast = ((ffn_w[0], (0, D_FF)), (ffn_w[1], (0, 0))) if ffn_w[0].dtype == F32 else ()
    x1, s_r, s_i, new_buf, *cast_out = _mixer(
        x_in, mod3, flat_state(h0r), flat_state(h0i), pool_in,
        p["n1"], p["win"], p["wsi"], p["wso"], p["wt"], p["alr"], p["ali"], p["d"], p["wglu"], p["wpool"],
        p["pscale"], p["wout"], nb=nb, tt=tt, n_sub=n_sub, n_steps=t_len // (tt * n_sub), pos0=pos0,
        time_major=time_major, cast=cast)
    wfi, wfo = cast_out if cast else ffn_w
    y = _ffn(x1, mod3, p["n2"], p["nf"], wfi, wfo, nb=nb, tt=tt_ffn, n_steps=t_len // tt_ffn,
             time_major=time_major)
    if time_major:
        y, new_buf = unrows(y, t_len), unrows(new_buf, POOL_HIST)
    return (y, unflat_state(s_r), unflat_state(s_i), new_buf[None]), (wfi, wfo)


def kernel(x_prompt, x_sample, c_prompt, c_sample, state_ssm_re, state_ssm_im, state_pool, norm1_g, norm2_g, normf_g, w_ada, b_ada, w_in, ssm_lam_re, ssm_lam_im, ssm_log_dt, ssm_b_re, ssm_b_im, ssm_c_re, ssm_c_im, ssm_d, w_glu, w_pool, pool_scale, w_out, w_ffn_in, w_ffn_out):
    n_prompt = x_prompt.shape[0]
    mod_p, mod_s, win, wglu, wout, wpool = _ada(
        c_prompt, c_sample, w_ada[0], b_ada[0],
        cast=((w_in[0], (D_SSM + D_POOL, w_in.shape[-1])), (w_glu[0], (0, 2 * D_MODEL)),
              (w_out[0], (0, D_MODEL)), (w_pool[0].reshape(D_POOL, POOL_OUT), (0, 0))))
    alr, ali, wsi, wso, wt = _ssm_prep(ssm_lam_re[0], ssm_lam_im[0], ssm_log_dt[0], ssm_b_re[0], ssm_b_im[0],
                                       ssm_c_re[0], ssm_c_im[0])
    p = dict(
        n1=norm1_g[0].reshape(1, -1), n2=norm2_g[0].reshape(1, -1), nf=normf_g.reshape(1, -1),
        win=win, wsi=wsi, wso=wso, wt=wt, alr=alr, ali=ali, d=ssm_d[0].reshape(1, -1),
        wglu=wglu, wpool=wpool.reshape(len(POOL_WINDOWS), POOL_GROUP, POOL_OUT), pscale=pool_scale[0].reshape(1, -1),
        wout=wout)
    zero_state = jnp.zeros((n_prompt, N_SSM_GROUPS, SSM_STATE), F32)
    zero_pool = jnp.zeros((n_prompt, POOL_HIST, D_POOL), F32)
    ffn_f32 = (w_ffn_in[0], w_ffn_out[0])
    (y_p, p_re, p_im, p_pool), ffn_bf16 = _trunk(x_prompt, mod_p, zero_state, zero_state, zero_pool, 0, p,
                                                  time_major=False, ffn_w=ffn_f32)
    (y_s, s_re, s_im, s_pool), _ = _trunk(x_sample, mod_s, state_ssm_re[0], state_ssm_im[0], state_pool[0],
                                          PAST_LEN, p, time_major=True, ffn_w=ffn_bf16)
    return (y_p, y_s, p_re, p_im, p_pool, s_re, s_im, s_pool)
```

```python
import functools

import jax
import jax.numpy as jnp
from jax import lax
from jax.experimental import pallas as pl
from jax.experimental.pallas import tpu as pltpu

D_MODEL = 1024
D_SSM = 512
SSM_GROUP = 16
N_SSM_GROUPS = 32
SSM_STATE = 64
N_STATE = N_SSM_GROUPS * SSM_STATE
D_POOL = 512
POOL_WINDOWS = (2, 4, 8, 16)
POOL_GROUP = 128
POOL_OUT = 256
POOL_HIST = 15
D_FF = 2816
N_MOD = 6
EPS = 1e-6
PAST_LEN = 16384

SUBLANES = 8
BF16_SUBLANES = 16
LANES = 128
TILE_ROWS = 512
MIXER_SUBTILES = 2
FFN_TILE_ROWS = 1024
HIST_PAD = 16
N_TAIL_SPLITS = 2
N_FFN_SPLITS = 4
ADA_ROW_BLOCKS = 4
MXU_DEPTH = 256
BLOCK_STEPS = 4
SET_GROUPS = MXU_DEPTH // (BLOCK_STEPS * SSM_GROUP)
N_SETS = N_SSM_GROUPS // SET_GROUPS
SET_IN = SET_GROUPS * SSM_GROUP
SET_STATES = SET_GROUPS * SSM_STATE
VMEM_LIMIT = 60 * 1024 * 1024

BF16 = jnp.bfloat16
F32 = jnp.float32


def _dot(a, b):
    return jnp.dot(a, b, preferred_element_type=F32)


def _sigmoid(x):
    return 0.5 * jnp.tanh(0.5 * x) + 0.5


def _two_sigmoid(half_x):
    return jnp.tanh(half_x) + 1.0


def _rmsnorm(x, g):
    return x * lax.rsqrt(jnp.mean(x * x, axis=-1, keepdims=True) + EPS) * g


def _tile3(v, tt, nb, time_major):
    lead, inner = (tt, nb) if time_major else (nb, tt)
    return v.reshape(lead, inner, v.shape[-1])


def _modulate(h, shift, scale, tt, nb, time_major):
    return (_tile3(h, tt, nb, time_major) * (1.0 + scale) + shift).reshape(h.shape)


def _rows_gate(v, gate, tt, nb, time_major):
    return (_tile3(v, tt, nb, time_major) * gate).reshape(v.shape)


def _cast_specs(arrays, n_steps):
    specs = []
    for a in arrays:
        n_blk = max(k for k in range(1, n_steps + 1)
                    if n_steps % k == 0 and a.shape[0] % (k * BF16_SUBLANES) == 0)
        specs.append(pl.BlockSpec((a.shape[0] // n_blk, a.shape[1]),
                                  lambda i, _hold=n_steps // n_blk: (i // _hold, 0)))
    return specs


def _cast_blocks(srcs, dsts, halved):
    for src, dst, (lo, hi) in zip(srcs, dsts, halved):
        if lo > 0:
            dst[:, :lo] = src[:, :lo].astype(BF16)
        if hi > lo:
            dst[:, lo:hi] = (src[:, lo:hi] * 0.5).astype(BF16)
        if hi < src.shape[-1]:
            dst[:, hi:] = src[:, hi:].astype(BF16)


def _ada_kernel(cp_ref, cs_ref, w_ref, b_ref, *rest, halved, n_prep_in):
    n_cast = len(halved)
    cast_src, prep_in, rest = rest[:n_cast], rest[n_cast:n_cast + n_prep_in], rest[n_cast + n_prep_in:]
    (op_ref, os_ref), cast_dst, prep_out = rest[:2], rest[2:2 + n_cast], rest[2 + n_cast:]
    _cast_blocks(cast_src, cast_dst, halved)
    k = pl.program_id(0)

    @pl.when(k == 0)
    def _():
        _ssm_prep_kernel(*prep_in, *prep_out)

    w = w_ref[...].astype(BF16)
    for c_ref, o_ref in ((cp_ref, op_ref), (cs_ref, os_ref)):
        c = c_ref[...]
        part = _dot((c * _sigmoid(c)).astype(BF16), w)

        @pl.when(k == 0)
        def _(o_ref=o_ref, part=part):
            o_ref[...] = b_ref[...] + part

        @pl.when(k > 0)
        def _(o_ref=o_ref, part=part):
            o_ref[...] += part


def _ada(c_prompt, c_sample, w_ada, b_ada, cast, prep_in):
    n_p, n_s = c_prompt.shape[0], c_sample.shape[0]
    kb = D_MODEL // ADA_ROW_BLOCKS
    arrays, halved = tuple(a for a, _ in cast), tuple(h for _, h in cast)
    cast_specs = _cast_specs(arrays, ADA_ROW_BLOCKS)
    n_out = N_MOD * D_MODEL
    prep_out = _ssm_prep_shapes()
    whole = lambda a: pl.BlockSpec(a.shape, lambda k, _nd=len(a.shape): (0,) * _nd)
    return pl.pallas_call(
        functools.partial(_ada_kernel, halved=halved, n_prep_in=len(prep_in)),
        grid=(ADA_ROW_BLOCKS,),
        in_specs=[pl.BlockSpec((n_p, kb), lambda k: (0, k)),
                  pl.BlockSpec((n_s, kb), lambda k: (0, k)),
                  pl.BlockSpec((kb, n_out), lambda k: (k, 0)),
                  pl.BlockSpec((1, n_out), lambda k: (0, 0))] + cast_specs + [whole(a) for a in prep_in],
        out_specs=[pl.BlockSpec((n_p, n_out), lambda k: (0, 0)), pl.BlockSpec((n_s, n_out), lambda k: (0, 0))]
        + cast_specs + [whole(a) for a in prep_out],
        out_shape=[jax.ShapeDtypeStruct((n_p, n_out), F32), jax.ShapeDtypeStruct((n_s, n_out), F32)]
        + [jax.ShapeDtypeStruct(a.shape, BF16) for a in arrays] + list(prep_out),
        compiler_params=pltpu.CompilerParams(dimension_semantics=("arbitrary",),
                                             vmem_limit_bytes=VMEM_LIMIT),
        name="ada_mod",
    )(c_prompt, c_sample, w_ada, b_ada.reshape(1, -1), *arrays, *prep_in)


def _cmul(ar, ai, br, bi):
    return ar * br - ai * bi, ar * bi + ai * br


def _ssm_prep_kernel(lr_ref, li_ref, ldt_ref, bt_r_ref, bt_i_ref, cr_ref, ci_ref,
                     alr_ref, ali_ref, wsi_ref, wso_ref, wt_ref):
    lr = lr_ref[...]
    li = li_ref[...]
    dt = jnp.exp(ldt_ref[...])
    mag = jnp.exp(lr * dt)
    a_r = mag * jnp.cos(li * dt)
    a_i = mag * jnp.sin(li * dt)
    den = lr * lr + li * li
    nr = a_r - 1.0
    ni = a_i
    f = ((nr * lr + ni * li) / den, (ni * lr - nr * li) / den)
    powers = [(a_r, a_i)]
    for _ in range(BLOCK_STEPS - 1):
        powers.append(_cmul(a_r, a_i, *powers[-1]))
    for g in range(N_SSM_GROUPS):
        states = slice(g * SSM_STATE, (g + 1) * SSM_STATE)
        alr_ref[:, states] = powers[-1][0][g:g + 1, :]
        ali_ref[:, states] = powers[-1][1][g:g + 1, :]
    wsi_ref[...] = jnp.zeros(wsi_ref.shape, BF16)
    wso_ref[...] = jnp.zeros(wso_ref.shape, BF16)
    wt_ref[...] = jnp.zeros(wt_ref.shape, BF16)
    k_shape = (BLOCK_STEPS * SET_IN, SET_IN)
    row_group = lax.broadcasted_iota(jnp.int32, k_shape, 0) % SET_IN // SSM_GROUP
    col_group = lax.broadcasted_iota(jnp.int32, k_shape, 1) // SSM_GROUP
    same_group = row_group == col_group
    nt_dims = (((1,), (1,)), ((), ()))
    for s in range(N_SETS):
        groups = slice(s * SET_GROUPS, (s + 1) * SET_GROUPS)

        def rows_of(pair, groups=groups):
            return tuple(jnp.broadcast_to(v[groups][:, None, :], (SET_GROUPS, SSM_GROUP, SSM_STATE))
                         .reshape(SET_IN, SSM_STATE) for v in pair)

        bt = (bt_r_ref[groups].reshape(SET_IN, SSM_STATE), bt_i_ref[groups].reshape(SET_IN, SSM_STATE))
        c = (cr_ref[groups].reshape(SET_IN, SSM_STATE), ci_ref[groups].reshape(SET_IN, SSM_STATE))
        a_rows = rows_of(powers[0])
        bb_lags = [_cmul(*rows_of(f), *bt)]
        for _ in range(BLOCK_STEPS - 1):
            bb_lags.append(_cmul(*a_rows, *bb_lags[-1]))
        bb_r = jnp.concatenate([bb_lags[BLOCK_STEPS - 1 - j][0] for j in range(BLOCK_STEPS)], axis=0)
        bb_i = jnp.concatenate([bb_lags[BLOCK_STEPS - 1 - j][1] for j in range(BLOCK_STEPS)], axis=0)
        k_all = (lax.dot_general(bb_r, c[0], nt_dims, precision=lax.Precision.HIGHEST, preferred_element_type=F32)
                 - lax.dot_general(bb_i, c[1], nt_dims, precision=lax.Precision.HIGHEST, preferred_element_type=F32))
        k_all = jnp.where(same_group, k_all, 0.0).astype(BF16)
        for j in range(BLOCK_STEPS):
            lag = BLOCK_STEPS - 1 - j
            for t in range(lag, BLOCK_STEPS):
                wt_ref[s, (t - lag) * SET_IN:(t - lag + 1) * SET_IN, t * SET_IN:(t + 1) * SET_IN] = (
                    k_all[j * SET_IN:(j + 1) * SET_IN])
            for gj in range(SET_GROUPS):
                rows = slice(j * SET_IN + gj * SSM_GROUP, j * SET_IN + (gj + 1) * SSM_GROUP)
                wsi_ref[s, rows, gj * SSM_STATE:(gj + 1) * SSM_STATE] = bb_r[rows, :].astype(BF16)
                wsi_ref[s, rows, SET_STATES + gj * SSM_STATE:SET_STATES + (gj + 1) * SSM_STATE] = (
                    bb_i[rows, :].astype(BF16))
        for t in range(BLOCK_STEPS):
            p_r, p_i = rows_of(powers[t])
            from_sr = jnp.transpose(c[0] * p_r - c[1] * p_i).astype(BF16)
            from_si = jnp.transpose(-c[0] * p_i - c[1] * p_r).astype(BF16)
            for gj in range(SET_GROUPS):
                src = slice(gj * SSM_GROUP, (gj + 1) * SSM_GROUP)
                out = slice(t * SET_IN + gj * SSM_GROUP, t * SET_IN + (gj + 1) * SSM_GROUP)
                wso_ref[s, gj * SSM_STATE:(gj + 1) * SSM_STATE, out] = from_sr[:, src]
                wso_ref[s, SET_STATES + gj * SSM_STATE:SET_STATES + (gj + 1) * SSM_STATE, out] = from_si[:, src]


def _ssm_prep_shapes():
    return (jax.ShapeDtypeStruct((1, N_STATE), F32), jax.ShapeDtypeStruct((1, N_STATE), F32),
            jax.ShapeDtypeStruct((N_SETS, BLOCK_STEPS * SET_IN, 2 * SET_STATES), BF16),
            jax.ShapeDtypeStruct((N_SETS, 2 * SET_STATES, BLOCK_STEPS * SET_IN), BF16),
            jax.ShapeDtypeStruct((N_SETS, BLOCK_STEPS * SET_IN, BLOCK_STEPS * SET_IN), BF16))


def _block_scan(z_ref, st_r, st_i, alr_ref, ali_ref, nb, n_blocks):
    for s in range(N_SETS):
        lanes = slice(s * SET_STATES, (s + 1) * SET_STATES)
        re_lanes = slice(2 * s * SET_STATES, (2 * s + 1) * SET_STATES)
        im_lanes = slice((2 * s + 1) * SET_STATES, (2 * s + 2) * SET_STATES)
        ar = jnp.broadcast_to(alr_ref[:, lanes], (SUBLANES, SET_STATES))
        ai = jnp.broadcast_to(ali_ref[:, lanes], (SUBLANES, SET_STATES))

        def seq_block(rb, carry, lanes=lanes, re_lanes=re_lanes, im_lanes=im_lanes, ar=ar, ai=ai):
            r0 = pl.multiple_of(rb * SUBLANES, SUBLANES)
            sr = st_r[pl.ds(r0, SUBLANES), lanes]
            si = st_i[pl.ds(r0, SUBLANES), lanes]

            def step(k, state):
                sr, si = state
                rows = pl.ds(pl.multiple_of(k * nb + r0, SUBLANES), SUBLANES)
                zr = z_ref[rows, re_lanes]
                zi = z_ref[rows, im_lanes]
                z_ref[rows, re_lanes] = sr
                z_ref[rows, im_lanes] = si
                return ar * sr - ai * si + zr, ar * si + ai * sr + zi

            sr, si = lax.fori_loop(0, n_blocks, step, (sr, si), unroll=True)
            st_r[pl.ds(r0, SUBLANES), lanes] = sr
            st_i[pl.ds(r0, SUBLANES), lanes] = si
            return carry

        if nb == SUBLANES:
            seq_block(0, 0)
        else:
            lax.fori_loop(0, nb // SUBLANES, seq_block, 0)


def _to_time_major(v, perm_ref, nb, tt):
    for c in range(v.shape[-1] // LANES):
        for b in range(nb):
            perm_ref.at[c][pl.ds(b, tt, stride=nb), :] = v[b * tt:(b + 1) * tt, c * LANES:(c + 1) * LANES]
    return jnp.concatenate([perm_ref[c] for c in range(v.shape[-1] // LANES)], axis=-1)


def _to_seq_major(v, perm_ref, nb, tt):
    n_slabs = v.shape[-1] // LANES
    for c in range(n_slabs):
        perm_ref[c] = v[:, c * LANES:(c + 1) * LANES]
    return jnp.concatenate(
        [jnp.concatenate([perm_ref.at[c][pl.ds(b, tt, stride=nb), :] for b in range(nb)], axis=0)
         for c in range(n_slabs)], axis=-1)


def _pool_time_major(u_pool, hist_ref, i, nb, tt, pos0):
    m = tt * nb
    hist_rows = POOL_HIST * nb
    hist_ref[hist_rows:hist_rows + m, :] = u_pool
    t_abs = pos0 + i * tt + lax.broadcasted_iota(jnp.int32, (tt, nb, POOL_GROUP), 0).reshape(m, POOL_GROUP)
    pooled = []
    for k, w in enumerate(POOL_WINDOWS):
        cols = slice(k * POOL_GROUP, (k + 1) * POOL_GROUP)
        acc = u_pool[:, cols]
        for j in range(1, w):
            acc = acc + hist_ref[hist_rows - j * nb:hist_rows - j * nb + m, cols]
        count = jnp.minimum(t_abs + 1, w).astype(F32)
        pooled.append(acc / count - u_pool[:, cols])
    return pooled


def _pool_seq_major(u_pool, hist_ref, i, nb, tt, pos0):
    m = tt * nb
    t_abs = pos0 + i * tt + lax.broadcasted_iota(jnp.int32, (nb, tt, POOL_GROUP), 1)
    pooled = []
    for k, w in enumerate(POOL_WINDOWS):
        cur = u_pool[:, k * POOL_GROUP:(k + 1) * POOL_GROUP].reshape(nb, tt, POOL_GROUP)
        hist_ref[k, :, HIST_PAD:HIST_PAD + tt, :] = cur
        acc = cur
        for j in range(1, w):
            acc = acc + hist_ref[k, :, HIST_PAD - j:HIST_PAD - j + tt, :]
        count = jnp.minimum(t_abs + 1, w).astype(F32)
        pooled.append((acc / count - cur).reshape(m, POOL_GROUP))
    return pooled


def _mixer_kernel(x_ref, mod_ref, h0r_ref, h0i_ref, pool0_ref, n1_ref, win_ref, wsi_ref, wso_ref, wt_ref,
                  alr_ref, ali_ref, d_ref, wglu_ref, wpool_ref, pscale_ref, wout_ref, *rest,
                  nb, tt, n_sub, n_steps, pos0, time_major, cast_halved):
    n_cast = len(cast_halved)
    cast_src, rest = rest[:n_cast], rest[n_cast:]
    xo_ref, sr_out_ref, si_out_ref, pool_out_ref = rest[:4]
    cast_dst, rest = rest[4:4 + n_cast], rest[4 + n_cast:]
    z_ref, st_r, st_i, hist_ref, perm_ref = rest
    n_blocks = tt // BLOCK_STEPS
    mb = n_blocks * nb
    i = pl.program_id(0)
    _cast_blocks(cast_src, cast_dst, cast_halved)
    m = tt * nb
    hist_rows = POOL_HIST * nb

    @pl.when(i == 0)
    def _():
        st_r[...] = h0r_ref[...].T if time_major else h0r_ref[...]
        st_i[...] = h0i_ref[...].T if time_major else h0i_ref[...]
        if time_major:
            hist_ref[0:hist_rows, :] = pool0_ref[...]
        else:
            for k in range(len(POOL_WINDOWS)):
                hist_ref[k, :, 0:HIST_PAD, :] = pool0_ref[:, :, k * POOL_GROUP:(k + 1) * POOL_GROUP]

    for sub in range(n_sub):
        tile = i * n_sub + sub
        t_rows = slice(sub * tt, (sub + 1) * tt)
        x = (x_ref[...] if time_major else x_ref[:, t_rows, :]).reshape(m, D_MODEL)
        shift1 = mod_ref[:, :, 0 * D_MODEL:1 * D_MODEL]
        scale1 = mod_ref[:, :, 1 * D_MODEL:2 * D_MODEL]
        gate1 = mod_ref[:, :, 2 * D_MODEL:3 * D_MODEL]
        hb = _modulate(_rmsnorm(x, n1_ref[...]), shift1, scale1, tt, nb, time_major).astype(BF16)

        u_ssm = _dot(hb, win_ref[:, 0:D_SSM])
        u_pool = _dot(hb, win_ref[:, D_SSM:D_SSM + D_POOL])
        u_tm = u_ssm if time_major else _to_time_major(u_ssm, perm_ref, nb, tt)
        u_blocks = u_tm.reshape(n_blocks, BLOCK_STEPS, nb, D_SSM)
        u_steps = [u_blocks[:, j].reshape(mb, D_SSM).astype(BF16) for j in range(BLOCK_STEPS)]
        y_sets = []
        for s in range(N_SETS):
            chans = slice(s * SET_IN, (s + 1) * SET_IN)
            lhs = jnp.concatenate([u[:, chans] for u in u_steps], axis=1)
            z_ref[:, 2 * s * SET_STATES:2 * (s + 1) * SET_STATES] = _dot(lhs, wsi_ref[s])
            y_sets.append(_dot(lhs, wt_ref[s]))
        g_ssm = _dot(hb, win_ref[:, D_SSM + D_POOL:D_SSM + D_POOL + D_MODEL])
        g_pool = _dot(hb, win_ref[:, D_SSM + D_POOL + D_MODEL:])

        _block_scan(z_ref, st_r, st_i, alr_ref, ali_ref, nb, n_blocks)
        if time_major:
            pooled = _pool_time_major(u_pool, hist_ref, tile, nb, tt, pos0)
            new_buf_rows = (slice(m, m + hist_rows), slice(None))
            if n_steps * n_sub > 1:
                hist_ref[0:hist_rows, :] = hist_ref[new_buf_rows]
        else:
            pooled = _pool_seq_major(u_pool, hist_ref, tile, nb, tt, pos0)
            if n_steps * n_sub > 1:
                hist_ref[:, :, 0:HIST_PAD, :] = hist_ref[:, :, tt:tt + HIST_PAD, :]
        z_pieces = [_dot(pooled[k].astype(BF16), wpool_ref[k]) for k in range(len(POOL_WINDOWS))]
        br_pool = jnp.concatenate(z_pieces, axis=-1) * pscale_ref[...]

        for s in range(N_SETS):
            s_in = z_ref[:, 2 * s * SET_STATES:2 * (s + 1) * SET_STATES].astype(BF16)
            y_sets[s] = y_sets[s] + _dot(s_in, wso_ref[s])
        y_steps = [jnp.concatenate([y[:, j * SET_IN:(j + 1) * SET_IN] for y in y_sets], axis=1).reshape(n_blocks, nb, D_SSM)
                   for j in range(BLOCK_STEPS)]
        y_cs = jnp.stack(y_steps, axis=1).reshape(m, D_SSM)
        if not time_major:
            y_cs = _to_seq_major(y_cs, perm_ref, nb, tt)

        for h in range(N_TAIL_SPLITS):
            rows = slice(h * m // N_TAIL_SPLITS, (h + 1) * m // N_TAIL_SPLITS)
            y_ssm = y_cs[rows] + d_ref[...] * u_ssm[rows]
            glu = _dot(jax.nn.gelu(y_ssm).astype(BF16), wglu_ref[...])
            br_ssm = glu[:, :D_MODEL] * _two_sigmoid(glu[:, D_MODEL:])
            merged2 = _two_sigmoid(g_ssm[rows]) * br_ssm + _two_sigmoid(g_pool[rows]) * br_pool[rows]
            o = _dot(merged2.astype(BF16), wout_ref[...])
            if time_major:
                xo_ref[rows, :] = x[rows] + _rows_gate(o, gate1, tt // N_TAIL_SPLITS, nb, True)
            else:
                nb_h = nb // N_TAIL_SPLITS
                seqs = slice(h * nb_h, (h + 1) * nb_h)
                x1 = x[rows] + _rows_gate(o, gate1[seqs], tt, nb_h, False)
                xo_ref[seqs, t_rows, :] = x1.reshape(nb_h, tt, D_MODEL)

    @pl.when(i == n_steps - 1)
    def _():
        sr_out_ref[...] = st_r[...].T if time_major else st_r[...]
        si_out_ref[...] = st_i[...].T if time_major else st_i[...]
        if time_major:
            pool_out_ref[...] = hist_ref[new_buf_rows]
        else:
            pool_out_ref[...] = jnp.concatenate(
                [hist_ref[k, :, tt + HIST_PAD - POOL_HIST:tt + HIST_PAD, :] for k in range(len(POOL_WINDOWS))],
                axis=-1)


def _const_spec(shape):
    nd = len(shape)
    return pl.BlockSpec(shape, lambda i, _nd=nd: (0,) * _nd, pipeline_mode=pl.Buffered(1))


def _row_spec(nb, tt, time_major):
    if time_major:
        return pl.BlockSpec((tt * nb, D_MODEL), lambda i: (i, 0))
    return pl.BlockSpec((nb, tt, D_MODEL), lambda i: (0, i, 0))


def _mixer(x, mod, h0r, h0i, pool0, n1, win, wsi, wso, wt, alr, ali, d, wglu, wpool, pscale, wout,
           *, nb, tt, n_sub, n_steps, pos0, time_major, cast=()):
    m = tt * nb
    consts = (mod, h0r, h0i, pool0, n1, win, wsi, wso, wt, alr, ali, d, wglu, wpool, pscale, wout)
    cast, cast_halved = tuple(a for a, _ in cast), tuple(n for _, n in cast)
    cast_specs = _cast_specs(cast, n_steps)
    if time_major:
        hist_shape = (POOL_HIST * nb + m, D_POOL)
        pool_out_shape = (POOL_HIST * nb, D_POOL)
    else:
        hist_shape = (len(POOL_WINDOWS), nb, HIST_PAD + tt, POOL_GROUP)
        pool_out_shape = (nb, POOL_HIST, D_POOL)
    kern = functools.partial(_mixer_kernel, nb=nb, tt=tt, n_sub=n_sub, n_steps=n_steps, pos0=pos0,
                             time_major=time_major, cast_halved=cast_halved)
    return pl.pallas_call(
        kern,
        grid=(n_steps,),
        in_specs=[_row_spec(nb, n_sub * tt, time_major)] + [_const_spec(a.shape) for a in consts] + cast_specs,
        out_specs=(_row_spec(nb, n_sub * tt, time_major),
                   pl.BlockSpec(h0r.shape, lambda i: (0, 0)),
                   pl.BlockSpec(h0r.shape, lambda i: (0, 0)),
                   pl.BlockSpec(pool_out_shape, lambda i: (0,) * len(pool_out_shape))) + tuple(cast_specs),
        out_shape=(jax.ShapeDtypeStruct(x.shape, F32),
                   jax.ShapeDtypeStruct(h0r.shape, F32),
                   jax.ShapeDtypeStruct(h0r.shape, F32),
                   jax.ShapeDtypeStruct(pool_out_shape, F32)) + tuple(jax.ShapeDtypeStruct(a.shape, BF16) for a in cast),
        scratch_shapes=[pltpu.VMEM((m // BLOCK_STEPS, 2 * N_STATE), F32),
                        pltpu.VMEM((nb, N_STATE), F32), pltpu.VMEM((nb, N_STATE), F32),
                        pltpu.VMEM(hist_shape, F32),
                        pltpu.VMEM((D_SSM // LANES, m, LANES), F32)],
        compiler_params=pltpu.CompilerParams(dimension_semantics=("arbitrary",),
                                             vmem_limit_bytes=VMEM_LIMIT),
        name=f"mixer_nb{nb}",
    )(x, *consts, *cast)


def _ffn_kernel(x_ref, mod_ref, n2_ref, nf_ref, wfi_ref, wfo_ref, o_ref, *, nb, tt, time_major):
    for h in range(N_FFN_SPLITS):
        if time_major:
            tt_h, nb_h = tt // N_FFN_SPLITS, nb
            piece = (slice(h * tt_h * nb, (h + 1) * tt_h * nb), slice(None))
            mod = mod_ref
        else:
            tt_h, nb_h = tt, nb // N_FFN_SPLITS
            piece = (slice(h * nb_h, (h + 1) * nb_h), slice(None), slice(None))
            mod = mod_ref.at[h * nb_h:(h + 1) * nb_h]
        x = x_ref[piece].reshape(tt_h * nb_h, D_MODEL)
        shift2 = mod[:, :, 3 * D_MODEL:4 * D_MODEL]
        scale2 = mod[:, :, 4 * D_MODEL:5 * D_MODEL]
        gate2 = mod[:, :, 5 * D_MODEL:6 * D_MODEL]
        hb = _modulate(_rmsnorm(x, n2_ref[...]), shift2, scale2, tt_h, nb_h, time_major).astype(BF16)
        f_a = _dot(hb, wfi_ref[:, :D_FF])
        f_b = _dot(hb, wfi_ref[:, D_FF:])
        act = (f_a * _two_sigmoid(f_a) * f_b).astype(BF16)
        x2 = x + _rows_gate(_dot(act, wfo_ref[...]), gate2, tt_h, nb_h, time_major)
        y = _rmsnorm(x2, nf_ref[...])
        o_ref[piece] = y if time_major else y.reshape(nb_h, tt_h, D_MODEL)


def _ffn(x, mod, n2, nf, wfi, wfo, *, nb, tt, n_steps, time_major):
    consts = (mod, n2, nf, wfi, wfo)
    return pl.pallas_call(
        functools.partial(_ffn_kernel, nb=nb, tt=tt, time_major=time_major),
        grid=(n_steps,),
        in_specs=[_row_spec(nb, tt, time_major)] + [_const_spec(a.shape) for a in consts],
        out_specs=_row_spec(nb, tt, time_major),
        out_shape=jax.ShapeDtypeStruct(x.shape, F32),
        compiler_params=pltpu.CompilerParams(dimension_semantics=("arbitrary",),
                                             vmem_limit_bytes=VMEM_LIMIT),
        name=f"ffn_nb{nb}",
    )(x, *consts)


def _trunk(x, mod, h0r, h0i, pool0, pos0, p, *, time_major, ffn_w):
    nb, t_len, _ = x.shape
    tt = min(t_len, TILE_ROWS // nb)
    n_sub = 1 if time_major else MIXER_SUBTILES
    tt_ffn = min(t_len, FFN_TILE_ROWS // nb)
    if time_major:
        rows = lambda a: jnp.transpose(a, (1, 0, 2)).reshape(a.shape[1] * nb, a.shape[2])
        unrows = lambda a, t: jnp.transpose(a.reshape(t, nb, a.shape[-1]), (1, 0, 2))
        x_in, pool_in, mod3 = rows(x), rows(pool0), mod[None]
        flat_state = lambda a: jnp.transpose(a, (1, 2, 0)).reshape(N_STATE, nb)
        unflat_state = lambda a: jnp.transpose(a.reshape(N_SSM_GROUPS, SSM_STATE, nb), (2, 0, 1))[None]
    else:
        assert nb == SUBLANES
        x_in, mod3 = x, mod[:, None, :]
        pool_in = jnp.pad(pool0, ((0, 0), (HIST_PAD - POOL_HIST, 0), (0, 0)))
        flat_state = lambda a: a.reshape(nb, N_STATE)
        unflat_state = lambda a: a.reshape(1, nb, N_SSM_GROUPS, SSM_STATE)
    cast = ((ffn_w[0], (0, D_FF)), (ffn_w[1], (0, 0))) if ffn_w[0].dtype == F32 else ()
    x1, s_r, s_i, new_buf, *cast_out = _mixer(
        x_in, mod3, flat_state(h0r), flat_state(h0i), pool_in,
        p["n1"], p["win"], p["wsi"], p["wso"], p["wt"], p["alr"], p["ali"], p["d"], p["wglu"], p["wpool"],
        p["pscale"], p["wout"], nb=nb, tt=tt, n_sub=n_sub, n_steps=t_len // (tt * n_sub), pos0=pos0,
        time_major=time_major, cast=cast)
    wfi, wfo = cast_out if cast else ffn_w
    y = _ffn(x1, mod3, p["n2"], p["nf"], wfi, wfo, nb=nb, tt=tt_ffn, n_steps=t_len // tt_ffn,
             time_major=time_major)
    if time_major:
        y, new_buf = unrows(y, t_len), unrows(new_buf, POOL_HIST)
    return (y, unflat_state(s_r), unflat_state(s_i), new_buf[None]), (wfi, wfo)


def kernel(x_prompt, x_sample, c_prompt, c_sample, state_ssm_re, state_ssm_im, state_pool, norm1_g, norm2_g, normf_g, w_ada, b_ada, w_in, ssm_lam_re, ssm_lam_im, ssm_log_dt, ssm_b_re, ssm_b_im, ssm_c_re, ssm_c_im, ssm_d, w_glu, w_pool, pool_scale, w_out, w_ffn_in, w_ffn_out):
    n_prompt = x_prompt.shape[0]
    mod_p, mod_s, win, wglu, wout, wpool, alr, ali, wsi, wso, wt = _ada(
        c_prompt, c_sample, w_ada[0], b_ada[0],
        cast=((w_in[0], (D_SSM + D_POOL, w_in.shape[-1])), (w_glu[0], (0, 2 * D_MODEL)),
              (w_out[0], (0, D_MODEL)), (w_pool[0].reshape(D_POOL, POOL_OUT), (0, 0))),
        prep_in=(ssm_lam_re[0], ssm_lam_im[0], ssm_log_dt[0].reshape(N_SSM_GROUPS, 1),
                 jnp.transpose(ssm_b_re[0], (0, 2, 1)), jnp.transpose(ssm_b_im[0], (0, 2, 1)),
                 ssm_c_re[0], ssm_c_im[0]))
    p = dict(
        n1=norm1_g[0].reshape(1, -1), n2=norm2_g[0].reshape(1, -1), nf=normf_g.reshape(1, -1),
        win=win, wsi=wsi, wso=wso, wt=wt, alr=alr, ali=ali, d=ssm_d[0].reshape(1, -1),
        wglu=wglu, wpool=wpool.reshape(len(POOL_WINDOWS), POOL_GROUP, POOL_OUT), pscale=pool_scale[0].reshape(1, -1),
        wout=wout)
    zero_state = jnp.zeros((n_prompt, N_SSM_GROUPS, SSM_STATE), F32)
    zero_pool = jnp.zeros((n_prompt, POOL_HIST, D_POOL), F32)
    ffn_f32 = (w_ffn_in[0], w_ffn_out[0])
    (y_p, p_re, p_im, p_pool), ffn_bf16 = _trunk(x_prompt, mod_p, zero_state, zero_state, zero_pool, 0, p,
                                                  time_major=False, ffn_w=ffn_f32)
    (y_s, s_re, s_im, s_pool), _ = _trunk(x_sample, mod_s, state_ssm_re[0], state_ssm_im[0], state_pool[0],
                                          PAST_LEN, p, time_major=True, ffn_w=ffn_bf16)
    return (y_p, y_s, p_re, p_im, p_pool, s_re, s_im, s_pool)
```

```python
import functools

import jax
import jax.numpy as jnp
from jax import lax
from jax.experimental import pallas as pl
from jax.experimental.pallas import tpu as pltpu

D_MODEL = 1024
D_SSM = 512
SSM_GROUP = 16
N_SSM_GROUPS = 32
SSM_STATE = 64
N_STATE = N_SSM_GROUPS * SSM_STATE
D_POOL = 512
POOL_WINDOWS = (2, 4, 8, 16)
POOL_GROUP = 128
POOL_OUT = 256
POOL_HIST = 15
D_FF = 2816
N_MOD = 6
EPS = 1e-6
PAST_LEN = 16384

SUBLANES = 8
BF16_SUBLANES = 16
LANES = 128
TILE_ROWS = 512
MIXER_SUBTILES = 2
FFN_TILE_ROWS = 1024
HIST_PAD = 16
N_TAIL_SPLITS = 2
N_FFN_SPLITS = 4
ADA_ROW_BLOCKS = 4
MXU_DEPTH = 256
BLOCK_STEPS = 4
SET_GROUPS = MXU_DEPTH // (BLOCK_STEPS * SSM_GROUP)
N_SETS = N_SSM_GROUPS // SET_GROUPS
SET_IN = SET_GROUPS * SSM_GROUP
SET_STATES = SET_GROUPS * SSM_STATE
VMEM_LIMIT = 60 * 1024 * 1024
ADA_VMEM_LIMIT = 40 * 1024 * 1024
FFN_VMEM_LIMIT = 48 * 1024 * 1024

BF16 = jnp.bfloat16
F32 = jnp.float32


def _dot(a, b):
    return jnp.dot(a, b, preferred_element_type=F32)


def _sigmoid(x):
    return 0.5 * jnp.tanh(0.5 * x) + 0.5


def _two_sigmoid(half_x):
    return jnp.tanh(half_x) + 1.0


def _rmsnorm(x, g):
    return x * lax.rsqrt(jnp.mean(x * x, axis=-1, keepdims=True) + EPS) * g


def _tile3(v, tt, nb, time_major):
    lead, inner = (tt, nb) if time_major else (nb, tt)
    return v.reshape(lead, inner, v.shape[-1])


def _modulate(h, shift, scale, tt, nb, time_major):
    return (_tile3(h, tt, nb, time_major) * (1.0 + scale) + shift).reshape(h.shape)


def _rows_gate(v, gate, tt, nb, time_major):
    return (_tile3(v, tt, nb, time_major) * gate).reshape(v.shape)


def _cast_specs(arrays, n_steps):
    specs = []
    for a in arrays:
        n_blk = max(k for k in range(1, n_steps + 1)
                    if n_steps % k == 0 and a.shape[0] % (k * BF16_SUBLANES) == 0)
        specs.append(pl.BlockSpec((a.shape[0] // n_blk, a.shape[1]),
                                  lambda i, _hold=n_steps // n_blk: (i // _hold, 0)))
    return specs


def _cast_blocks(srcs, dsts, halved):
    for src, dst, (lo, hi) in zip(srcs, dsts, halved):
        if lo > 0:
            dst[:, :lo] = src[:, :lo].astype(BF16)
        if hi > lo:
            dst[:, lo:hi] = (src[:, lo:hi] * 0.5).astype(BF16)
        if hi < src.shape[-1]:
            dst[:, hi:] = src[:, hi:].astype(BF16)


def _ada_kernel(cp_ref, cs_ref, w_ref, b_ref, *rest, halved):
    n_cast = len(halved)
    cast_src, (op_ref, os_ref), cast_dst = rest[:n_cast], rest[n_cast:n_cast + 2], rest[n_cast + 2:]
    _cast_blocks(cast_src, cast_dst, halved)
    k = pl.program_id(0)
    w = w_ref[...].astype(BF16)
    for c_ref, o_ref in ((cp_ref, op_ref), (cs_ref, os_ref)):
        c = c_ref[...]
        part = _dot((c * _sigmoid(c)).astype(BF16), w)

        @pl.when(k == 0)
        def _(o_ref=o_ref, part=part):
            o_ref[...] = b_ref[...] + part

        @pl.when(k > 0)
        def _(o_ref=o_ref, part=part):
            o_ref[...] += part


def _ada(c_prompt, c_sample, w_ada, b_ada, cast):
    n_p, n_s = c_prompt.shape[0], c_sample.shape[0]
    kb = D_MODEL // ADA_ROW_BLOCKS
    arrays, halved = tuple(a for a, _ in cast), tuple(h for _, h in cast)
    cast_specs = _cast_specs(arrays, ADA_ROW_BLOCKS)
    n_out = N_MOD * D_MODEL
    return pl.pallas_call(
        functools.partial(_ada_kernel, halved=halved),
        grid=(ADA_ROW_BLOCKS,),
        in_specs=[pl.BlockSpec((n_p, kb), lambda k: (0, k)),
                  pl.BlockSpec((n_s, kb), lambda k: (0, k)),
                  pl.BlockSpec((kb, n_out), lambda k: (k, 0)),
                  pl.BlockSpec((1, n_out), lambda k: (0, 0))] + cast_specs,
        out_specs=[pl.BlockSpec((n_p, n_out), lambda k: (0, 0)), pl.BlockSpec((n_s, n_out), lambda k: (0, 0))]
        + cast_specs,
        out_shape=[jax.ShapeDtypeStruct((n_p, n_out), F32), jax.ShapeDtypeStruct((n_s, n_out), F32)]
        + [jax.ShapeDtypeStruct(a.shape, BF16) for a in arrays],
        compiler_params=pltpu.CompilerParams(dimension_semantics=("arbitrary",),
                                             vmem_limit_bytes=ADA_VMEM_LIMIT),
        name="ada_mod",
    )(c_prompt, c_sample, w_ada, b_ada.reshape(1, -1), *arrays)


def _cmul(ar, ai, br, bi):
    return ar * br - ai * bi, ar * bi + ai * br


def _ssm_prep_kernel(lr_ref, li_ref, ldt_ref, bt_r_ref, bt_i_ref, cr_ref, ci_ref,
                     alr_ref, ali_ref, wsi_ref, wso_ref, wt_ref):
    lr = lr_ref[...]
    li = li_ref[...]
    dt = jnp.exp(ldt_ref[...])
    mag = jnp.exp(lr * dt)
    a_r = mag * jnp.cos(li * dt)
    a_i = mag * jnp.sin(li * dt)
    den = lr * lr + li * li
    nr = a_r - 1.0
    ni = a_i
    f = ((nr * lr + ni * li) / den, (ni * lr - nr * li) / den)
    powers = [(a_r, a_i)]
    for _ in range(BLOCK_STEPS - 1):
        powers.append(_cmul(a_r, a_i, *powers[-1]))
    for g in range(N_SSM_GROUPS):
        states = slice(g * SSM_STATE, (g + 1) * SSM_STATE)
        alr_ref[:, states] = powers[-1][0][g:g + 1, :]
        ali_ref[:, states] = powers[-1][1][g:g + 1, :]
    wsi_ref[...] = jnp.zeros(wsi_ref.shape, BF16)
    wso_ref[...] = jnp.zeros(wso_ref.shape, BF16)
    wt_ref[...] = jnp.zeros(wt_ref.shape, BF16)
    k_shape = (BLOCK_STEPS * SET_IN, SET_IN)
    row_group = lax.broadcasted_iota(jnp.int32, k_shape, 0) % SET_IN // SSM_GROUP
    col_group = lax.broadcasted_iota(jnp.int32, k_shape, 1) // SSM_GROUP
    same_group = row_group == col_group
    nt_dims = (((1,), (1,)), ((), ()))
    for s in range(N_SETS):
        groups = slice(s * SET_GROUPS, (s + 1) * SET_GROUPS)

        def rows_of(pair, groups=groups):
            return tuple(jnp.broadcast_to(v[groups][:, None, :], (SET_GROUPS, SSM_GROUP, SSM_STATE))
                         .reshape(SET_IN, SSM_STATE) for v in pair)

        bt = (bt_r_ref[groups].reshape(SET_IN, SSM_STATE), bt_i_ref[groups].reshape(SET_IN, SSM_STATE))
        c = (cr_ref[groups].reshape(SET_IN, SSM_STATE), ci_ref[groups].reshape(SET_IN, SSM_STATE))
        a_rows = rows_of(powers[0])
        bb_lags = [_cmul(*rows_of(f), *bt)]
        for _ in range(BLOCK_STEPS - 1):
            bb_lags.append(_cmul(*a_rows, *bb_lags[-1]))
        bb_r = jnp.concatenate([bb_lags[BLOCK_STEPS - 1 - j][0] for j in range(BLOCK_STEPS)], axis=0)
        bb_i = jnp.concatenate([bb_lags[BLOCK_STEPS - 1 - j][1] for j in range(BLOCK_STEPS)], axis=0)
        k_all = (lax.dot_general(bb_r, c[0], nt_dims, precision=lax.Precision.HIGHEST, preferred_element_type=F32)
                 - lax.dot_general(bb_i, c[1], nt_dims, precision=lax.Precision.HIGHEST, preferred_element_type=F32))
        k_all = jnp.where(same_group, k_all, 0.0).astype(BF16)
        for j in range(BLOCK_STEPS):
            lag = BLOCK_STEPS - 1 - j
            for t in range(lag, BLOCK_STEPS):
                wt_ref[s, (t - lag) * SET_IN:(t - lag + 1) * SET_IN, t * SET_IN:(t + 1) * SET_IN] = (
                    k_all[j * SET_IN:(j + 1) * SET_IN])
            for gj in range(SET_GROUPS):
                rows = slice(j * SET_IN + gj * SSM_GROUP, j * SET_IN + (gj + 1) * SSM_GROUP)
                wsi_ref[s, rows, gj * SSM_STATE:(gj + 1) * SSM_STATE] = bb_r[rows, :].astype(BF16)
                wsi_ref[s, rows, SET_STATES + gj * SSM_STATE:SET_STATES + (gj + 1) * SSM_STATE] = (
                    bb_i[rows, :].astype(BF16))
        for t in range(BLOCK_STEPS):
            p_r, p_i = rows_of(powers[t])
            from_sr = jnp.transpose(c[0] * p_r - c[1] * p_i).astype(BF16)
            from_si = jnp.transpose(-c[0] * p_i - c[1] * p_r).astype(BF16)
            for gj in range(SET_GROUPS):
                src = slice(gj * SSM_GROUP, (gj + 1) * SSM_GROUP)
                out = slice(t * SET_IN + gj * SSM_GROUP, t * SET_IN + (gj + 1) * SSM_GROUP)
                wso_ref[s, gj * SSM_STATE:(gj + 1) * SSM_STATE, out] = from_sr[:, src]
                wso_ref[s, SET_STATES + gj * SSM_STATE:SET_STATES + (gj + 1) * SSM_STATE, out] = from_si[:, src]


def _ssm_prep(lam_re, lam_im, log_dt, b_re, b_im, c_re, c_im):
    return pl.pallas_call(
        _ssm_prep_kernel,
        out_shape=(jax.ShapeDtypeStruct((1, N_STATE), F32), jax.ShapeDtypeStruct((1, N_STATE), F32),
                   jax.ShapeDtypeStruct((N_SETS, BLOCK_STEPS * SET_IN, 2 * SET_STATES), BF16),
                   jax.ShapeDtypeStruct((N_SETS, 2 * SET_STATES, BLOCK_STEPS * SET_IN), BF16),
                   jax.ShapeDtypeStruct((N_SETS, BLOCK_STEPS * SET_IN, BLOCK_STEPS * SET_IN), BF16)),
        name="ssm_prep",
    )(lam_re, lam_im, log_dt.reshape(N_SSM_GROUPS, 1), jnp.transpose(b_re, (0, 2, 1)), jnp.transpose(b_im, (0, 2, 1)),
      c_re, c_im)


def _block_scan(z_ref, st_r, st_i, alr_ref, ali_ref, nb, n_blocks):
    for s in range(N_SETS):
        lanes = slice(s * SET_STATES, (s + 1) * SET_STATES)
        re_lanes = slice(2 * s * SET_STATES, (2 * s + 1) * SET_STATES)
        im_lanes = slice((2 * s + 1) * SET_STATES, (2 * s + 2) * SET_STATES)
        ar = jnp.broadcast_to(alr_ref[:, lanes], (SUBLANES, SET_STATES))
        ai = jnp.broadcast_to(ali_ref[:, lanes], (SUBLANES, SET_STATES))

        def seq_block(rb, carry, lanes=lanes, re_lanes=re_lanes, im_lanes=im_lanes, ar=ar, ai=ai):
            r0 = pl.multiple_of(rb * SUBLANES, SUBLANES)
            sr = st_r[pl.ds(r0, SUBLANES), lanes]
            si = st_i[pl.ds(r0, SUBLANES), lanes]

            def step(k, state):
                sr, si = state
                rows = pl.ds(pl.multiple_of(k * nb + r0, SUBLANES), SUBLANES)
                zr = z_ref[rows, re_lanes]
                zi = z_ref[rows, im_lanes]
                z_ref[rows, re_lanes] = sr
                z_ref[rows, im_lanes] = si
                return ar * sr - ai * si + zr, ar * si + ai * sr + zi

            sr, si = lax.fori_loop(0, n_blocks, step, (sr, si), unroll=True)
            st_r[pl.ds(r0, SUBLANES), lanes] = sr
            st_i[pl.ds(r0, SUBLANES), lanes] = si
            return carry

        if nb == SUBLANES:
            seq_block(0, 0)
        else:
            lax.fori_loop(0, nb // SUBLANES, seq_block, 0)


def _to_time_major(v, perm_ref, nb, tt):
    for c in range(v.shape[-1] // LANES):
        for b in range(nb):
            perm_ref.at[c][pl.ds(b, tt, stride=nb), :] = v[b * tt:(b + 1) * tt, c * LANES:(c + 1) * LANES]
    return jnp.concatenate([perm_ref[c] for c in range(v.shape[-1] // LANES)], axis=-1)


def _to_seq_major(v, perm_ref, nb, tt):
    n_slabs = v.shape[-1] // LANES
    for c in range(n_slabs):
        perm_ref[c] = v[:, c * LANES:(c + 1) * LANES]
    return jnp.concatenate(
        [jnp.concatenate([perm_ref.at[c][pl.ds(b, tt, stride=nb), :] for b in range(nb)], axis=0)
         for c in range(n_slabs)], axis=-1)


def _pool_time_major(u_pool, hist_ref, i, nb, tt, pos0):
    m = tt * nb
    hist_rows = POOL_HIST * nb
    hist_ref[hist_rows:hist_rows + m, :] = u_pool
    t_abs = pos0 + i * tt + lax.broadcasted_iota(jnp.int32, (tt, nb, POOL_GROUP), 0).reshape(m, POOL_GROUP)
    pooled = []
    for k, w in enumerate(POOL_WINDOWS):
        cols = slice(k * POOL_GROUP, (k + 1) * POOL_GROUP)
        acc = u_pool[:, cols]
        for j in range(1, w):
            acc = acc + hist_ref[hist_rows - j * nb:hist_rows - j * nb + m, cols]
        count = jnp.minimum(t_abs + 1, w).astype(F32)
        pooled.append(acc / count - u_pool[:, cols])
    return pooled


def _pool_seq_major(u_pool, hist_ref, i, nb, tt, pos0):
    m = tt * nb
    t_abs = pos0 + i * tt + lax.broadcasted_iota(jnp.int32, (nb, tt, POOL_GROUP), 1)
    pooled = []
    for k, w in enumerate(POOL_WINDOWS):
        cur = u_pool[:, k * POOL_GROUP:(k + 1) * POOL_GROUP].reshape(nb, tt, POOL_GROUP)
        hist_ref[k, :, HIST_PAD:HIST_PAD + tt, :] = cur
        acc = cur
        for j in range(1, w):
            acc = acc + hist_ref[k, :, HIST_PAD - j:HIST_PAD - j + tt, :]
        count = jnp.minimum(t_abs + 1, w).astype(F32)
        pooled.append((acc / count - cur).reshape(m, POOL_GROUP))
    return pooled


def _mixer_kernel(x_ref, mod_ref, h0r_ref, h0i_ref, pool0_ref, n1_ref, win_ref, wsi_ref, wso_ref, wt_ref,
                  alr_ref, ali_ref, d_ref, wglu_ref, wpool_ref, pscale_ref, wout_ref, *rest,
                  nb, tt, n_sub, n_steps, pos0, time_major, cast_halved):
    n_cast = len(cast_halved)
    cast_src, rest = rest[:n_cast], rest[n_cast:]
    xo_ref, sr_out_ref, si_out_ref, pool_out_ref = rest[:4]
    cast_dst, rest = rest[4:4 + n_cast], rest[4 + n_cast:]
    z_ref, st_r, st_i, hist_ref, perm_ref = rest
    n_blocks = tt // BLOCK_STEPS
    mb = n_blocks * nb
    i = pl.program_id(0)
    _cast_blocks(cast_src, cast_dst, cast_halved)
    m = tt * nb
    hist_rows = POOL_HIST * nb

    @pl.when(i == 0)
    def _():
        st_r[...] = h0r_ref[...].T if time_major else h0r_ref[...]
        st_i[...] = h0i_ref[...].T if time_major else h0i_ref[...]
        if time_major:
            hist_ref[0:hist_rows, :] = pool0_ref[...]
        else:
            for k in range(len(POOL_WINDOWS)):
                hist_ref[k, :, 0:HIST_PAD, :] = pool0_ref[:, :, k * POOL_GROUP:(k + 1) * POOL_GROUP]

    for sub in range(n_sub):
        tile = i * n_sub + sub
        t_rows = slice(sub * tt, (sub + 1) * tt)
        x = (x_ref[...] if time_major else x_ref[:, t_rows, :]).reshape(m, D_MODEL)
        shift1 = mod_ref[:, :, 0 * D_MODEL:1 * D_MODEL]
        scale1 = mod_ref[:, :, 1 * D_MODEL:2 * D_MODEL]
        gate1 = mod_ref[:, :, 2 * D_MODEL:3 * D_MODEL]
        hb = _modulate(_rmsnorm(x, n1_ref[...]), shift1, scale1, tt, nb, time_major).astype(BF16)

        u_ssm = _dot(hb, win_ref[:, 0:D_SSM])
        u_pool = _dot(hb, win_ref[:, D_SSM:D_SSM + D_POOL])
        u_tm = u_ssm if time_major else _to_time_major(u_ssm, perm_ref, nb, tt)
        u_blocks = u_tm.reshape(n_blocks, BLOCK_STEPS, nb, D_SSM)
        u_steps = [u_blocks[:, j].reshape(mb, D_SSM).astype(BF16) for j in range(BLOCK_STEPS)]
        y_sets = []
        for s in range(N_SETS):
            chans = slice(s * SET_IN, (s + 1) * SET_IN)
            lhs = jnp.concatenate([u[:, chans] for u in u_steps], axis=1)
            z_ref[:, 2 * s * SET_STATES:2 * (s + 1) * SET_STATES] = _dot(lhs, wsi_ref[s])
            y_sets.append(_dot(lhs, wt_ref[s]))
        g_ssm = _dot(hb, win_ref[:, D_SSM + D_POOL:D_SSM + D_POOL + D_MODEL])
        g_pool = _dot(hb, win_ref[:, D_SSM + D_POOL + D_MODEL:])

        _block_scan(z_ref, st_r, st_i, alr_ref, ali_ref, nb, n_blocks)
        if time_major:
            pooled = _pool_time_major(u_pool, hist_ref, tile, nb, tt, pos0)
            new_buf_rows = (slice(m, m + hist_rows), slice(None))
            if n_steps * n_sub > 1:
                hist_ref[0:hist_rows, :] = hist_ref[new_buf_rows]
        else:
            pooled = _pool_seq_major(u_pool, hist_ref, tile, nb, tt, pos0)
            if n_steps * n_sub > 1:
                hist_ref[:, :, 0:HIST_PAD, :] = hist_ref[:, :, tt:tt + HIST_PAD, :]
        z_pieces = [_dot(pooled[k].astype(BF16), wpool_ref[k]) for k in range(len(POOL_WINDOWS))]
        br_pool = jnp.concatenate(z_pieces, axis=-1) * pscale_ref[...]

        for s in range(N_SETS):
            s_in = z_ref[:, 2 * s * SET_STATES:2 * (s + 1) * SET_STATES].astype(BF16)
            y_sets[s] = y_sets[s] + _dot(s_in, wso_ref[s])
        y_steps = [jnp.concatenate([y[:, j * SET_IN:(j + 1) * SET_IN] for y in y_sets], axis=1).reshape(n_blocks, nb, D_SSM)
                   for j in range(BLOCK_STEPS)]
        y_cs = jnp.stack(y_steps, axis=1).reshape(m, D_SSM)
        if not time_major:
            y_cs = _to_seq_major(y_cs, perm_ref, nb, tt)

        for h in range(N_TAIL_SPLITS):
            rows = slice(h * m // N_TAIL_SPLITS, (h + 1) * m // N_TAIL_SPLITS)
            y_ssm = y_cs[rows] + d_ref[...] * u_ssm[rows]
            glu = _dot(jax.nn.gelu(y_ssm).astype(BF16), wglu_ref[...])
            br_ssm = glu[:, :D_MODEL] * _two_sigmoid(glu[:, D_MODEL:])
            merged2 = _two_sigmoid(g_ssm[rows]) * br_ssm + _two_sigmoid(g_pool[rows]) * br_pool[rows]
            o = _dot(merged2.astype(BF16), wout_ref[...])
            if time_major:
                xo_ref[rows, :] = x[rows] + _rows_gate(o, gate1, tt // N_TAIL_SPLITS, nb, True)
            else:
                nb_h = nb // N_TAIL_SPLITS
                seqs = slice(h * nb_h, (h + 1) * nb_h)
                x1 = x[rows] + _rows_gate(o, gate1[seqs], tt, nb_h, False)
                xo_ref[seqs, t_rows, :] = x1.reshape(nb_h, tt, D_MODEL)

    @pl.when(i == n_steps - 1)
    def _():
        sr_out_ref[...] = st_r[...].T if time_major else st_r[...]
        si_out_ref[...] = st_i[...].T if time_major else st_i[...]
        if time_major:
            pool_out_ref[...] = hist_ref[new_buf_rows]
        else:
            pool_out_ref[...] = jnp.concatenate(
                [hist_ref[k, :, tt + HIST_PAD - POOL_HIST:tt + HIST_PAD, :] for k in range(len(POOL_WINDOWS))],
                axis=-1)


def _const_spec(shape):
    nd = len(shape)
    return pl.BlockSpec(shape, lambda i, _nd=nd: (0,) * _nd, pipeline_mode=pl.Buffered(1))


def _row_spec(nb, tt, time_major):
    if time_major:
        return pl.BlockSpec((tt * nb, D_MODEL), lambda i: (i, 0))
    return pl.BlockSpec((nb, tt, D_MODEL), lambda i: (0, i, 0))


def _mixer(x, mod, h0r, h0i, pool0, n1, win, wsi, wso, wt, alr, ali, d, wglu, wpool, pscale, wout,
           *, nb, tt, n_sub, n_steps, pos0, time_major, cast=()):
    m = tt * nb
    consts = (mod, h0r, h0i, pool0, n1, win, wsi, wso, wt, alr, ali, d, wglu, wpool, pscale, wout)
    cast, cast_halved = tuple(a for a, _ in cast), tuple(n for _, n in cast)
    cast_specs = _cast_specs(cast, n_steps)
    if time_major:
        hist_shape = (POOL_HIST * nb + m, D_POOL)
        pool_out_shape = (POOL_HIST * nb, D_POOL)
    else:
        hist_shape = (len(POOL_WINDOWS), nb, HIST_PAD + tt, POOL_GROUP)
        pool_out_shape = (nb, POOL_HIST, D_POOL)
    kern = functools.partial(_mixer_kernel, nb=nb, tt=tt, n_sub=n_sub, n_steps=n_steps, pos0=pos0,
                             time_major=time_major, cast_halved=cast_halved)
    return pl.pallas_call(
        kern,
        grid=(n_steps,),
        in_specs=[_row_spec(nb, n_sub * tt, time_major)] + [_const_spec(a.shape) for a in consts] + cast_specs,
        out_specs=(_row_spec(nb, n_sub * tt, time_major),
                   pl.BlockSpec(h0r.shape, lambda i: (0, 0)),
                   pl.BlockSpec(h0r.shape, lambda i: (0, 0)),
                   pl.BlockSpec(pool_out_shape, lambda i: (0,) * len(pool_out_shape))) + tuple(cast_specs),
        out_shape=(jax.ShapeDtypeStruct(x.shape, F32),
                   jax.ShapeDtypeStruct(h0r.shape, F32),
                   jax.ShapeDtypeStruct(h0r.shape, F32),
                   jax.ShapeDtypeStruct(pool_out_shape, F32)) + tuple(jax.ShapeDtypeStruct(a.shape, BF16) for a in cast),
        scratch_shapes=[pltpu.VMEM((m // BLOCK_STEPS, 2 * N_STATE), F32),
                        pltpu.VMEM((nb, N_STATE), F32), pltpu.VMEM((nb, N_STATE), F32),
                        pltpu.VMEM(hist_shape, F32),
                        pltpu.VMEM((D_SSM // LANES, m, LANES), F32)],
        compiler_params=pltpu.CompilerParams(dimension_semantics=("arbitrary",),
                                             vmem_limit_bytes=VMEM_LIMIT),
        name=f"mixer_nb{nb}",
    )(x, *consts, *cast)


def _ffn_kernel(x_ref, mod_ref, n2_ref, nf_ref, wfi_ref, wfo_ref, o_ref, *, nb, tt, time_major):
    for h in range(N_FFN_SPLITS):
        if time_major:
            tt_h, nb_h = tt // N_FFN_SPLITS, nb
            piece = (slice(h * tt_h * nb, (h + 1) * tt_h * nb), slice(None))
            mod = mod_ref
        else:
            tt_h, nb_h = tt, nb // N_FFN_SPLITS
            piece = (slice(h * nb_h, (h + 1) * nb_h), slice(None), slice(None))
            mod = mod_ref.at[h * nb_h:(h + 1) * nb_h]
        x = x_ref[piece].reshape(tt_h * nb_h, D_MODEL)
        shift2 = mod[:, :, 3 * D_MODEL:4 * D_MODEL]
        scale2 = mod[:, :, 4 * D_MODEL:5 * D_MODEL]
        gate2 = mod[:, :, 5 * D_MODEL:6 * D_MODEL]
        hb = _modulate(_rmsnorm(x, n2_ref[...]), shift2, scale2, tt_h, nb_h, time_major).astype(BF16)
        f_a = _dot(hb, wfi_ref[:, :D_FF])
        f_b = _dot(hb, wfi_ref[:, D_FF:])
        act = (f_a * _two_sigmoid(f_a) * f_b).astype(BF16)
        x2 = x + _rows_gate(_dot(act, wfo_ref[...]), gate2, tt_h, nb_h, time_major)
        y = _rmsnorm(x2, nf_ref[...])
        o_ref[piece] = y if time_major else y.reshape(nb_h, tt_h, D_MODEL)


def _ffn(x, mod, n2, nf, wfi, wfo, *, nb, tt, n_steps, time_major):
    consts = (mod, n2, nf, wfi, wfo)
    return pl.pallas_call(
        functools.partial(_ffn_kernel, nb=nb, tt=tt, time_major=time_major),
        grid=(n_steps,),
        in_specs=[_row_spec(nb, tt, time_major)] + [_const_spec(a.shape) for a in consts],
        out_specs=_row_spec(nb, tt, time_major),
        out_shape=jax.ShapeDtypeStruct(x.shape, F32),
        compiler_params=pltpu.CompilerParams(dimension_semantics=("arbitrary",),
                                             vmem_limit_bytes=FFN_VMEM_LIMIT),
        name=f"ffn_nb{nb}",
    )(x, *consts)


def _trunk(x, mod, h0r, h0i, pool0, pos0, p, *, time_major, ffn_w):
    nb, t_len, _ = x.shape
    tt = min(t_len, TILE_ROWS // nb)
    n_sub = 1 if time_major else MIXER_SUBTILES
    tt_ffn = min(t_len, FFN_TILE_ROWS // nb)
    if time_major:
        rows = lambda a: jnp.transpose(a, (1, 0, 2)).reshape(a.shape[1] * nb, a.shape[2])
        unrows = lambda a, t: jnp.transpose(a.reshape(t, nb, a.shape[-1]), (1, 0, 2))
        x_in, pool_in, mod3 = rows(x), rows(pool0), mod[None]
        flat_state = lambda a: jnp.transpose(a, (1, 2, 0)).reshape(N_STATE, nb)
        unflat_state = lambda a: jnp.transpose(a.reshape(N_SSM_GROUPS, SSM_STATE, nb), (2, 0, 1))[None]
    else:
        assert nb == SUBLANES
        x_in, mod3 = x, mod[:, None, :]
        pool_in = jnp.pad(pool0, ((0, 0), (HIST_PAD - POOL_HIST, 0), (0, 0)))
        flat_state = lambda a: a.reshape(nb, N_STATE)
        unflat_state = lambda a: a.reshape(1, nb, N_SSM_GROUPS, SSM_STATE)
    cast = ((ffn_w[0], (0, D_FF)), (ffn_w[1], (0, 0))) if ffn_w[0].dtype == F32 else ()
    x1, s_r, s_i, new_buf, *cast_out = _mixer(
        x_in, mod3, flat_state(h0r), flat_state(h0i), pool_in,
        p["n1"], p["win"], p["wsi"], p["wso"], p["wt"], p["alr"], p["ali"], p["d"], p["wglu"], p["wpool"],
        p["pscale"], p["wout"], nb=nb, tt=tt, n_sub=n_sub, n_steps=t_len // (tt * n_sub), pos0=pos0,
        time_major=time_major, cast=cast)
    wfi, wfo = cast_out if cast else ffn_w
    y = _ffn(x1, mod3, p["n2"], p["nf"], wfi, wfo, nb=nb, tt=tt_ffn, n_steps=t_len // tt_ffn,
             time_major=time_major)
    if time_major:
        y, new_buf = unrows(y, t_len), unrows(new_buf, POOL_HIST)
    return (y, unflat_state(s_r), unflat_state(s_i), new_buf[None]), (wfi, wfo)


def kernel(x_prompt, x_sample, c_prompt, c_sample, state_ssm_re, state_ssm_im, state_pool, norm1_g, norm2_g, normf_g, w_ada, b_ada, w_in, ssm_lam_re, ssm_lam_im, ssm_log_dt, ssm_b_re, ssm_b_im, ssm_c_re, ssm_c_im, ssm_d, w_glu, w_pool, pool_scale, w_out, w_ffn_in, w_ffn_out):
    n_prompt = x_prompt.shape[0]
    mod_p, mod_s, win, wglu, wout, wpool = _ada(
        c_prompt, c_sample, w_ada[0], b_ada[0],
        cast=((w_in[0], (D_SSM + D_POOL, w_in.shape[-1])), (w_glu[0], (0, 2 * D_MODEL)),
              (w_out[0], (0, D_MODEL)), (w_pool[0].reshape(D_POOL, POOL_OUT), (0, 0))))
    alr, ali, wsi, wso, wt = _ssm_prep(ssm_lam_re[0], ssm_lam_im[0], ssm_log_dt[0], ssm_b_re[0], ssm_b_im[0],
                                       ssm_c_re[0], ssm_c_im[0])
    p = dict(
        n1=norm1_g[0].reshape(1, -1), n2=norm2_g[0].reshape(1, -1), nf=normf_g.reshape(1, -1),
        win=win, wsi=wsi, wso=wso, wt=wt, alr=alr, ali=ali, d=ssm_d[0].reshape(1, -1),
        wglu=wglu, wpool=wpool.reshape(len(POOL_WINDOWS), POOL_GROUP, POOL_OUT), pscale=pool_scale[0].reshape(1, -1),
        wout=wout)
    zero_state = jnp.zeros((n_prompt, N_SSM_GROUPS, SSM_STATE), F32)
    zero_pool = jnp.zeros((n_prompt, POOL_HIST, D_POOL), F32)
    ffn_f32 = (w_ffn_in[0], w_ffn_out[0])
    (y_p, p_re, p_im, p_pool), ffn_bf16 = _trunk(x_prompt, mod_p, zero_state, zero_state, zero_pool, 0, p,
                                                  time_major=False, ffn_w=ffn_f32)
    (y_s, s_re, s_im, s_pool), _ = _trunk(x_sample, mod_s, state_ssm_re[0], state_ssm_im[0], state_pool[0],
                                          PAST_LEN, p, time_major=True, ffn_w=ffn_bf16)
    return (y_p, y_s, p_re, p_im, p_pool, s_re, s_im, s_pool)
```

```python
import functools

import jax
import jax.numpy as jnp
from jax import lax
from jax.experimental import pallas as pl
from jax.experimental.pallas import tpu as pltpu

D_MODEL = 1024
D_SSM = 512
SSM_GROUP = 16
N_SSM_GROUPS = 32
SSM_STATE = 64
N_STATE = N_SSM_GROUPS * SSM_STATE
D_POOL = 512
POOL_WINDOWS = (2, 4, 8, 16)
POOL_GROUP = 128
POOL_OUT = 256
POOL_HIST = 15
D_FF = 2816
N_MOD = 6
EPS = 1e-6
PAST_LEN = 16384

SUBLANES = 8
BF16_SUBLANES = 16
LANES = 128
TILE_ROWS = 512
MIXER_SUBTILES = 2
FFN_TILE_ROWS = 1024
HIST_PAD = 16
N_TAIL_SPLITS = 2
N_FFN_SPLITS = 4
ADA_ROW_BLOCKS = 4
MXU_DEPTH = 256
BLOCK_STEPS = 4
SET_GROUPS = MXU_DEPTH // (BLOCK_STEPS * SSM_GROUP)
N_SETS = N_SSM_GROUPS // SET_GROUPS
SET_IN = SET_GROUPS * SSM_GROUP
SET_STATES = SET_GROUPS * SSM_STATE
VMEM_LIMIT = 60 * 1024 * 1024
MIXER_VMEM_LIMIT = 52 * 1024 * 1024
PREP_VMEM_LIMIT = 16 * 1024 * 1024
ADA_VMEM_LIMIT = 40 * 1024 * 1024
FFN_VMEM_LIMIT = 42 * 1024 * 1024

BF16 = jnp.bfloat16
F32 = jnp.float32


def _dot(a, b):
    return jnp.dot(a, b, preferred_element_type=F32)


def _sigmoid(x):
    return 0.5 * jnp.tanh(0.5 * x) + 0.5


def _two_sigmoid(half_x):
    return jnp.tanh(half_x) + 1.0


def _rmsnorm(x, g):
    return x * lax.rsqrt(jnp.mean(x * x, axis=-1, keepdims=True) + EPS) * g


def _tile3(v, tt, nb, time_major):
    lead, inner = (tt, nb) if time_major else (nb, tt)
    return v.reshape(lead, inner, v.shape[-1])


def _modulate(h, shift, scale, tt, nb, time_major):
    return (_tile3(h, tt, nb, time_major) * (1.0 + scale) + shift).reshape(h.shape)


def _rows_gate(v, gate, tt, nb, time_major):
    return (_tile3(v, tt, nb, time_major) * gate).reshape(v.shape)


def _cast_specs(arrays, n_steps):
    specs = []
    for a in arrays:
        n_blk = max(k for k in range(1, n_steps + 1)
                    if n_steps % k == 0 and a.shape[0] % (k * BF16_SUBLANES) == 0)
        specs.append(pl.BlockSpec((a.shape[0] // n_blk, a.shape[1]),
                                  lambda i, _hold=n_steps // n_blk: (i // _hold, 0)))
    return specs


def _cast_blocks(srcs, dsts, halved):
    for src, dst, (lo, hi) in zip(srcs, dsts, halved):
        if lo > 0:
            dst[:, :lo] = src[:, :lo].astype(BF16)
        if hi > lo:
            dst[:, lo:hi] = (src[:, lo:hi] * 0.5).astype(BF16)
        if hi < src.shape[-1]:
            dst[:, hi:] = src[:, hi:].astype(BF16)


def _ada_kernel(cp_ref, cs_ref, w_ref, b_ref, *rest, halved):
    n_cast = len(halved)
    cast_src, (op_ref, os_ref), cast_dst = rest[:n_cast], rest[n_cast:n_cast + 2], rest[n_cast + 2:]
    _cast_blocks(cast_src, cast_dst, halved)
    k = pl.program_id(0)
    w = w_ref[...].astype(BF16)
    for c_ref, o_ref in ((cp_ref, op_ref), (cs_ref, os_ref)):
        c = c_ref[...]
        part = _dot((c * _sigmoid(c)).astype(BF16), w)

        @pl.when(k == 0)
        def _(o_ref=o_ref, part=part):
            o_ref[...] = b_ref[...] + part

        @pl.when(k > 0)
        def _(o_ref=o_ref, part=part):
            o_ref[...] += part


def _ada(c_prompt, c_sample, w_ada, b_ada, cast):
    n_p, n_s = c_prompt.shape[0], c_sample.shape[0]
    kb = D_MODEL // ADA_ROW_BLOCKS
    arrays, halved = tuple(a for a, _ in cast), tuple(h for _, h in cast)
    cast_specs = _cast_specs(arrays, ADA_ROW_BLOCKS)
    n_out = N_MOD * D_MODEL
    return pl.pallas_call(
        functools.partial(_ada_kernel, halved=halved),
        grid=(ADA_ROW_BLOCKS,),
        in_specs=[pl.BlockSpec((n_p, kb), lambda k: (0, k)),
                  pl.BlockSpec((n_s, kb), lambda k: (0, k)),
                  pl.BlockSpec((kb, n_out), lambda k: (k, 0)),
                  pl.BlockSpec((1, n_out), lambda k: (0, 0))] + cast_specs,
        out_specs=[pl.BlockSpec((n_p, n_out), lambda k: (0, 0)), pl.BlockSpec((n_s, n_out), lambda k: (0, 0))]
        + cast_specs,
        out_shape=[jax.ShapeDtypeStruct((n_p, n_out), F32), jax.ShapeDtypeStruct((n_s, n_out), F32)]
        + [jax.ShapeDtypeStruct(a.shape, BF16) for a in arrays],
        compiler_params=pltpu.CompilerParams(dimension_semantics=("arbitrary",),
                                             vmem_limit_bytes=ADA_VMEM_LIMIT),
        name="ada_mod",
    )(c_prompt, c_sample, w_ada, b_ada.reshape(1, -1), *arrays)


def _cmul(ar, ai, br, bi):
    return ar * br - ai * bi, ar * bi + ai * br


def _ssm_prep_kernel(lr_ref, li_ref, ldt_ref, bt_r_ref, bt_i_ref, cr_ref, ci_ref,
                     alr_ref, ali_ref, wsi_ref, wso_ref, wt_ref):
    lr = lr_ref[...]
    li = li_ref[...]
    dt = jnp.exp(ldt_ref[...])
    mag = jnp.exp(lr * dt)
    a_r = mag * jnp.cos(li * dt)
    a_i = mag * jnp.sin(li * dt)
    den = lr * lr + li * li
    nr = a_r - 1.0
    ni = a_i
    f = ((nr * lr + ni * li) / den, (ni * lr - nr * li) / den)
    powers = [(a_r, a_i)]
    for _ in range(BLOCK_STEPS - 1):
        powers.append(_cmul(a_r, a_i, *powers[-1]))
    for g in range(N_SSM_GROUPS):
        states = slice(g * SSM_STATE, (g + 1) * SSM_STATE)
        alr_ref[:, states] = powers[-1][0][g:g + 1, :]
        ali_ref[:, states] = powers[-1][1][g:g + 1, :]
    wsi_ref[...] = jnp.zeros(wsi_ref.shape, BF16)
    wso_ref[...] = jnp.zeros(wso_ref.shape, BF16)
    wt_ref[...] = jnp.zeros(wt_ref.shape, BF16)
    k_shape = (BLOCK_STEPS * SET_IN, SET_IN)
    row_group = lax.broadcasted_iota(jnp.int32, k_shape, 0) % SET_IN // SSM_GROUP
    col_group = lax.broadcasted_iota(jnp.int32, k_shape, 1) // SSM_GROUP
    same_group = row_group == col_group
    nt_dims = (((1,), (1,)), ((), ()))
    for s in range(N_SETS):
        groups = slice(s * SET_GROUPS, (s + 1) * SET_GROUPS)

        def rows_of(pair, groups=groups):
            return tuple(jnp.broadcast_to(v[groups][:, None, :], (SET_GROUPS, SSM_GROUP, SSM_STATE))
                         .reshape(SET_IN, SSM_STATE) for v in pair)

        bt = (bt_r_ref[groups].reshape(SET_IN, SSM_STATE), bt_i_ref[groups].reshape(SET_IN, SSM_STATE))
        c = (cr_ref[groups].reshape(SET_IN, SSM_STATE), ci_ref[groups].reshape(SET_IN, SSM_STATE))
        a_rows = rows_of(powers[0])
        bb_lags = [_cmul(*rows_of(f), *bt)]
        for _ in range(BLOCK_STEPS - 1):
            bb_lags.append(_cmul(*a_rows, *bb_lags[-1]))
        bb_r = jnp.concatenate([bb_lags[BLOCK_STEPS - 1 - j][0] for j in range(BLOCK_STEPS)], axis=0)
        bb_i = jnp.concatenate([bb_lags[BLOCK_STEPS - 1 - j][1] for j in range(BLOCK_STEPS)], axis=0)
        k_all = (lax.dot_general(bb_r, c[0], nt_dims, precision=lax.Precision.HIGHEST, preferred_element_type=F32)
                 - lax.dot_general(bb_i, c[1], nt_dims, precision=lax.Precision.HIGHEST, preferred_element_type=F32))
        k_all = jnp.where(same_group, k_all, 0.0).astype(BF16)
        for j in range(BLOCK_STEPS):
            lag = BLOCK_STEPS - 1 - j
            for t in range(lag, BLOCK_STEPS):
                wt_ref[s, (t - lag) * SET_IN:(t - lag + 1) * SET_IN, t * SET_IN:(t + 1) * SET_IN] = (
                    k_all[j * SET_IN:(j + 1) * SET_IN])
            for gj in range(SET_GROUPS):
                rows = slice(j * SET_IN + gj * SSM_GROUP, j * SET_IN + (gj + 1) * SSM_GROUP)
                wsi_ref[s, rows, gj * SSM_STATE:(gj + 1) * SSM_STATE] = bb_r[rows, :].astype(BF16)
                wsi_ref[s, rows, SET_STATES + gj * SSM_STATE:SET_STATES + (gj + 1) * SSM_STATE] = (
                    bb_i[rows, :].astype(BF16))
        for t in range(BLOCK_STEPS):
            p_r, p_i = rows_of(powers[t])
            from_sr = jnp.transpose(c[0] * p_r - c[1] * p_i).astype(BF16)
            from_si = jnp.transpose(-c[0] * p_i - c[1] * p_r).astype(BF16)
            for gj in range(SET_GROUPS):
                src = slice(gj * SSM_GROUP, (gj + 1) * SSM_GROUP)
                out = slice(t * SET_IN + gj * SSM_GROUP, t * SET_IN + (gj + 1) * SSM_GROUP)
                wso_ref[s, gj * SSM_STATE:(gj + 1) * SSM_STATE, out] = from_sr[:, src]
                wso_ref[s, SET_STATES + gj * SSM_STATE:SET_STATES + (gj + 1) * SSM_STATE, out] = from_si[:, src]


def _ssm_prep(lam_re, lam_im, log_dt, b_re, b_im, c_re, c_im):
    return pl.pallas_call(
        _ssm_prep_kernel,
        out_shape=(jax.ShapeDtypeStruct((1, N_STATE), F32), jax.ShapeDtypeStruct((1, N_STATE), F32),
                   jax.ShapeDtypeStruct((N_SETS, BLOCK_STEPS * SET_IN, 2 * SET_STATES), BF16),
                   jax.ShapeDtypeStruct((N_SETS, 2 * SET_STATES, BLOCK_STEPS * SET_IN), BF16),
                   jax.ShapeDtypeStruct((N_SETS, BLOCK_STEPS * SET_IN, BLOCK_STEPS * SET_IN), BF16)),
        compiler_params=pltpu.CompilerParams(vmem_limit_bytes=PREP_VMEM_LIMIT),
        name="ssm_prep",
    )(lam_re, lam_im, log_dt.reshape(N_SSM_GROUPS, 1), jnp.transpose(b_re, (0, 2, 1)), jnp.transpose(b_im, (0, 2, 1)),
      c_re, c_im)


def _block_scan(z_ref, st_r, st_i, alr_ref, ali_ref, nb, n_blocks):
    for s in range(N_SETS):
        lanes = slice(s * SET_STATES, (s + 1) * SET_STATES)
        re_lanes = slice(2 * s * SET_STATES, (2 * s + 1) * SET_STATES)
        im_lanes = slice((2 * s + 1) * SET_STATES, (2 * s + 2) * SET_STATES)
        ar = jnp.broadcast_to(alr_ref[:, lanes], (SUBLANES, SET_STATES))
        ai = jnp.broadcast_to(ali_ref[:, lanes], (SUBLANES, SET_STATES))

        def seq_block(rb, carry, lanes=lanes, re_lanes=re_lanes, im_lanes=im_lanes, ar=ar, ai=ai):
            r0 = pl.multiple_of(rb * SUBLANES, SUBLANES)
            sr = st_r[pl.ds(r0, SUBLANES), lanes]
            si = st_i[pl.ds(r0, SUBLANES), lanes]

            def step(k, state):
                sr, si = state
                rows = pl.ds(pl.multiple_of(k * nb + r0, SUBLANES), SUBLANES)
                zr = z_ref[rows, re_lanes]
                zi = z_ref[rows, im_lanes]
                z_ref[rows, re_lanes] = sr
                z_ref[rows, im_lanes] = si
                return ar * sr - ai * si + zr, ar * si + ai * sr + zi

            sr, si = lax.fori_loop(0, n_blocks, step, (sr, si), unroll=True)
            st_r[pl.ds(r0, SUBLANES), lanes] = sr
            st_i[pl.ds(r0, SUBLANES), lanes] = si
            return carry

        if nb == SUBLANES:
            seq_block(0, 0)
        else:
            lax.fori_loop(0, nb // SUBLANES, seq_block, 0)


def _to_time_major(v, perm_ref, nb, tt):
    for c in range(v.shape[-1] // LANES):
        for b in range(nb):
            perm_ref.at[c][pl.ds(b, tt, stride=nb), :] = v[b * tt:(b + 1) * tt, c * LANES:(c + 1) * LANES]
    return jnp.concatenate([perm_ref[c] for c in range(v.shape[-1] // LANES)], axis=-1)


def _to_seq_major(v, perm_ref, nb, tt):
    n_slabs = v.shape[-1] // LANES
    for c in range(n_slabs):
        perm_ref[c] = v[:, c * LANES:(c + 1) * LANES]
    return jnp.concatenate(
        [jnp.concatenate([perm_ref.at[c][pl.ds(b, tt, stride=nb), :] for b in range(nb)], axis=0)
         for c in range(n_slabs)], axis=-1)


def _pool_time_major(u_pool, hist_ref, i, nb, tt, pos0):
    m = tt * nb
    hist_rows = POOL_HIST * nb
    hist_ref[hist_rows:hist_rows + m, :] = u_pool
    t_abs = pos0 + i * tt + lax.broadcasted_iota(jnp.int32, (tt, nb, POOL_GROUP), 0).reshape(m, POOL_GROUP)
    pooled = []
    for k, w in enumerate(POOL_WINDOWS):
        cols = slice(k * POOL_GROUP, (k + 1) * POOL_GROUP)
        acc = u_pool[:, cols]
        for j in range(1, w):
            acc = acc + hist_ref[hist_rows - j * nb:hist_rows - j * nb + m, cols]
        count = jnp.minimum(t_abs + 1, w).astype(F32)
        pooled.append(acc / count - u_pool[:, cols])
    return pooled


def _pool_seq_major(u_pool, hist_ref, i, nb, tt, pos0):
    m = tt * nb
    t_abs = pos0 + i * tt + lax.broadcasted_iota(jnp.int32, (nb, tt, POOL_GROUP), 1)
    pooled = []
    for k, w in enumerate(POOL_WINDOWS):
        cur = u_pool[:, k * POOL_GROUP:(k + 1) * POOL_GROUP].reshape(nb, tt, POOL_GROUP)
        hist_ref[k, :, HIST_PAD:HIST_PAD + tt, :] = cur
        acc = cur
        for j in range(1, w):
            acc = acc + hist_ref[k, :, HIST_PAD - j:HIST_PAD - j + tt, :]
        count = jnp.minimum(t_abs + 1, w).astype(F32)
        pooled.append((acc / count - cur).reshape(m, POOL_GROUP))
    return pooled


def _mixer_kernel(x_ref, mod_ref, h0r_ref, h0i_ref, pool0_ref, n1_ref, win_ref, wsi_ref, wso_ref, wt_ref,
                  alr_ref, ali_ref, d_ref, wglu_ref, wpool_ref, pscale_ref, wout_ref, *rest,
                  nb, tt, n_sub, n_steps, pos0, time_major, cast_halved):
    n_cast = len(cast_halved)
    cast_src, rest = rest[:n_cast], rest[n_cast:]
    xo_ref, sr_out_ref, si_out_ref, pool_out_ref = rest[:4]
    cast_dst, rest = rest[4:4 + n_cast], rest[4 + n_cast:]
    z_ref, st_r, st_i, hist_ref, perm_ref = rest
    n_blocks = tt // BLOCK_STEPS
    mb = n_blocks * nb
    i = pl.program_id(0)
    _cast_blocks(cast_src, cast_dst, cast_halved)
    m = tt * nb
    hist_rows = POOL_HIST * nb

    @pl.when(i == 0)
    def _():
        st_r[...] = h0r_ref[...].T if time_major else h0r_ref[...]
        st_i[...] = h0i_ref[...].T if time_major else h0i_ref[...]
        if time_major:
            hist_ref[0:hist_rows, :] = pool0_ref[...]
        else:
            for k in range(len(POOL_WINDOWS)):
                hist_ref[k, :, 0:HIST_PAD, :] = pool0_ref[:, :, k * POOL_GROUP:(k + 1) * POOL_GROUP]

    for sub in range(n_sub):
        tile = i * n_sub + sub
        t_rows = slice(sub * tt, (sub + 1) * tt)
        x = (x_ref[...] if time_major else x_ref[:, t_rows, :]).reshape(m, D_MODEL)
        shift1 = mod_ref[:, :, 0 * D_MODEL:1 * D_MODEL]
        scale1 = mod_ref[:, :, 1 * D_MODEL:2 * D_MODEL]
        gate1 = mod_ref[:, :, 2 * D_MODEL:3 * D_MODEL]
        hb = _modulate(_rmsnorm(x, n1_ref[...]), shift1, scale1, tt, nb, time_major).astype(BF16)

        u_ssm = _dot(hb, win_ref[:, 0:D_SSM])
        u_pool = _dot(hb, win_ref[:, D_SSM:D_SSM + D_POOL])
        u_tm = u_ssm if time_major else _to_time_major(u_ssm, perm_ref, nb, tt)
        u_blocks = u_tm.reshape(n_blocks, BLOCK_STEPS, nb, D_SSM)
        u_steps = [u_blocks[:, j].reshape(mb, D_SSM).astype(BF16) for j in range(BLOCK_STEPS)]
        y_sets = []
        for s in range(N_SETS):
            chans = slice(s * SET_IN, (s + 1) * SET_IN)
            lhs = jnp.concatenate([u[:, chans] for u in u_steps], axis=1)
            z_ref[:, 2 * s * SET_STATES:2 * (s + 1) * SET_STATES] = _dot(lhs, wsi_ref[s])
            y_sets.append(_dot(lhs, wt_ref[s]))
        g_ssm = _dot(hb, win_ref[:, D_SSM + D_POOL:D_SSM + D_POOL + D_MODEL])
        g_pool = _dot(hb, win_ref[:, D_SSM + D_POOL + D_MODEL:])

        _block_scan(z_ref, st_r, st_i, alr_ref, ali_ref, nb, n_blocks)
        if time_major:
            pooled = _pool_time_major(u_pool, hist_ref, tile, nb, tt, pos0)
            new_buf_rows = (slice(m, m + hist_rows), slice(None))
            if n_steps * n_sub > 1:
                hist_ref[0:hist_rows, :] = hist_ref[new_buf_rows]
        else:
            pooled = _pool_seq_major(u_pool, hist_ref, tile, nb, tt, pos0)
            if n_steps * n_sub > 1:
                hist_ref[:, :, 0:HIST_PAD, :] = hist_ref[:, :, tt:tt + HIST_PAD, :]
        z_pieces = [_dot(pooled[k].astype(BF16), wpool_ref[k]) for k in range(len(POOL_WINDOWS))]
        br_pool = jnp.concatenate(z_pieces, axis=-1) * pscale_ref[...]

        for s in range(N_SETS):
            s_in = z_ref[:, 2 * s * SET_STATES:2 * (s + 1) * SET_STATES].astype(BF16)
            y_sets[s] = y_sets[s] + _dot(s_in, wso_ref[s])
        y_steps = [jnp.concatenate([y[:, j * SET_IN:(j + 1) * SET_IN] for y in y_sets], axis=1).reshape(n_blocks, nb, D_SSM)
                   for j in range(BLOCK_STEPS)]
        y_cs = jnp.stack(y_steps, axis=1).reshape(m, D_SSM)
        if not time_major:
            y_cs = _to_seq_major(y_cs, perm_ref, nb, tt)

        for h in range(N_TAIL_SPLITS):
            rows = slice(h * m // N_TAIL_SPLITS, (h + 1) * m // N_TAIL_SPLITS)
            y_ssm = y_cs[rows] + d_ref[...] * u_ssm[rows]
            glu = _dot(jax.nn.gelu(y_ssm).astype(BF16), wglu_ref[...])
            br_ssm = glu[:, :D_MODEL] * _two_sigmoid(glu[:, D_MODEL:])
            merged2 = _two_sigmoid(g_ssm[rows]) * br_ssm + _two_sigmoid(g_pool[rows]) * br_pool[rows]
            o = _dot(merged2.astype(BF16), wout_ref[...])
            if time_major:
                xo_ref[rows, :] = x[rows] + _rows_gate(o, gate1, tt // N_TAIL_SPLITS, nb, True)
            else:
                nb_h = nb // N_TAIL_SPLITS
                seqs = slice(h * nb_h, (h + 1) * nb_h)
                x1 = x[rows] + _rows_gate(o, gate1[seqs], tt, nb_h, False)
                xo_ref[seqs, t_rows, :] = x1.reshape(nb_h, tt, D_MODEL)

    @pl.when(i == n_steps - 1)
    def _():
        sr_out_ref[...] = st_r[...].T if time_major else st_r[...]
        si_out_ref[...] = st_i[...].T if time_major else st_i[...]
        if time_major:
            pool_out_ref[...] = hist_ref[new_buf_rows]
        else:
            pool_out_ref[...] = jnp.concatenate(
                [hist_ref[k, :, tt + HIST_PAD - POOL_HIST:tt + HIST_PAD, :] for k in range(len(POOL_WINDOWS))],
                axis=-1)


def _const_spec(shape):
    nd = len(shape)
    return pl.BlockSpec(shape, lambda i, _nd=nd: (0,) * _nd, pipeline_mode=pl.Buffered(1))


def _row_spec(nb, tt, time_major):
    if time_major:
        return pl.BlockSpec((tt * nb, D_MODEL), lambda i: (i, 0))
    return pl.BlockSpec((nb, tt, D_MODEL), lambda i: (0, i, 0))


def _mixer(x, mod, h0r, h0i, pool0, n1, win, wsi, wso, wt, alr, ali, d, wglu, wpool, pscale, wout,
           *, nb, tt, n_sub, n_steps, pos0, time_major, cast=()):
    m = tt * nb
    consts = (mod, h0r, h0i, pool0, n1, win, wsi, wso, wt, alr, ali, d, wglu, wpool, pscale, wout)
    cast, cast_halved = tuple(a for a, _ in cast), tuple(n for _, n in cast)
    cast_specs = _cast_specs(cast, n_steps)
    if time_major:
        hist_shape = (POOL_HIST * nb + m, D_POOL)
        pool_out_shape = (POOL_HIST * nb, D_POOL)
    else:
        hist_shape = (len(POOL_WINDOWS), nb, HIST_PAD + tt, POOL_GROUP)
        pool_out_shape = (nb, POOL_HIST, D_POOL)
    kern = functools.partial(_mixer_kernel, nb=nb, tt=tt, n_sub=n_sub, n_steps=n_steps, pos0=pos0,
                             time_major=time_major, cast_halved=cast_halved)
    return pl.pallas_call(
        kern,
        grid=(n_steps,),
        in_specs=[_row_spec(nb, n_sub * tt, time_major)] + [_const_spec(a.shape) for a in consts] + cast_specs,
        out_specs=(_row_spec(nb, n_sub * tt, time_major),
                   pl.BlockSpec(h0r.shape, lambda i: (0, 0)),
                   pl.BlockSpec(h0r.shape, lambda i: (0, 0)),
                   pl.BlockSpec(pool_out_shape, lambda i: (0,) * len(pool_out_shape))) + tuple(cast_specs),
        out_shape=(jax.ShapeDtypeStruct(x.shape, F32),
                   jax.ShapeDtypeStruct(h0r.shape, F32),
                   jax.ShapeDtypeStruct(h0r.shape, F32),
                   jax.ShapeDtypeStruct(pool_out_shape, F32)) + tuple(jax.ShapeDtypeStruct(a.shape, BF16) for a in cast),
        scratch_shapes=[pltpu.VMEM((m // BLOCK_STEPS, 2 * N_STATE), F32),
                        pltpu.VMEM((nb, N_STATE), F32), pltpu.VMEM((nb, N_STATE), F32),
                        pltpu.VMEM(hist_shape, F32),
                        pltpu.VMEM((D_SSM // LANES, m, LANES), F32)],
        compiler_params=pltpu.CompilerParams(dimension_semantics=("arbitrary",),
                                             vmem_limit_bytes=VMEM_LIMIT if time_major else MIXER_VMEM_LIMIT),
        name=f"mixer_nb{nb}",
    )(x, *consts, *cast)


def _ffn_kernel(x_ref, mod_ref, n2_ref, nf_ref, wfi_ref, wfo_ref, o_ref, *, nb, tt, time_major):
    for h in range(N_FFN_SPLITS):
        if time_major:
            tt_h, nb_h = tt // N_FFN_SPLITS, nb
            piece = (slice(h * tt_h * nb, (h + 1) * tt_h * nb), slice(None))
            mod = mod_ref
        else:
            tt_h, nb_h = tt, nb // N_FFN_SPLITS
            piece = (slice(h * nb_h, (h + 1) * nb_h), slice(None), slice(None))
            mod = mod_ref.at[h * nb_h:(h + 1) * nb_h]
        x = x_ref[piece].reshape(tt_h * nb_h, D_MODEL)
        shift2 = mod[:, :, 3 * D_MODEL:4 * D_MODEL]
        scale2 = mod[:, :, 4 * D_MODEL:5 * D_MODEL]
        gate2 = mod[:, :, 5 * D_MODEL:6 * D_MODEL]
        hb = _modulate(_rmsnorm(x, n2_ref[...]), shift2, scale2, tt_h, nb_h, time_major).astype(BF16)
        f_a = _dot(hb, wfi_ref[:, :D_FF])
        f_b = _dot(hb, wfi_ref[:, D_FF:])
        act = (f_a * _two_sigmoid(f_a) * f_b).astype(BF16)
        x2 = x + _rows_gate(_dot(act, wfo_ref[...]), gate2, tt_h, nb_h, time_major)
        y = _rmsnorm(x2, nf_ref[...])
        o_ref[piece] = y if time_major else y.reshape(nb_h, tt_h, D_MODEL)


def _ffn(x, mod, n2, nf, wfi, wfo, *, nb, tt, n_steps, time_major):
    consts = (mod, n2, nf, wfi, wfo)
    return pl.pallas_call(
        functools.partial(_ffn_kernel, nb=nb, tt=tt, time_major=time_major),
        grid=(n_steps,),
        in_specs=[_row_spec(nb, tt, time_major)] + [_const_spec(a.shape) for a in consts],
        out_specs=_row_spec(nb, tt, time_major),
        out_shape=jax.ShapeDtypeStruct(x.shape, F32),
        compiler_params=pltpu.CompilerParams(dimension_semantics=("arbitrary",),
                                             vmem_limit_bytes=FFN_VMEM_LIMIT),
        name=f"ffn_nb{nb}",
    )(x, *consts)


def _trunk(x, mod, h0r, h0i, pool0, pos0, p, *, time_major, ffn_w):
    nb, t_len, _ = x.shape
    tt = min(t_len, TILE_ROWS // nb)
    n_sub = 1 if time_major else MIXER_SUBTILES
    tt_ffn = min(t_len, FFN_TILE_ROWS // nb)
    if time_major:
        rows = lambda a: jnp.transpose(a, (1, 0, 2)).reshape(a.shape[1] * nb, a.shape[2])
        unrows = lambda a, t: jnp.transpose(a.reshape(t, nb, a.shape[-1]), (1, 0, 2))
        x_in, pool_in, mod3 = rows(x), rows(pool0), mod[None]
        flat_state = lambda a: jnp.transpose(a, (1, 2, 0)).reshape(N_STATE, nb)
        unflat_state = lambda a: jnp.transpose(a.reshape(N_SSM_GROUPS, SSM_STATE, nb), (2, 0, 1))[None]
    else:
        assert nb == SUBLANES
        x_in, mod3 = x, mod[:, None, :]
        pool_in = jnp.pad(pool0, ((0, 0), (HIST_PAD - POOL_HIST, 0), (0, 0)))
        flat_state = lambda a: a.reshape(nb, N_STATE)
        unflat_state = lambda a: a.reshape(1, nb, N_SSM_GROUPS, SSM_STATE)
    cast = ((ffn_w[0], (0, D_FF)), (ffn_w[1], (0, 0))) if ffn_w[0].dtype == F32 else ()
    x1, s_r, s_i, new_buf, *cast_out = _mixer(
        x_in, mod3, flat_state(h0r), flat_state(h0i), pool_in,
        p["n1"], p["win"], p["wsi"], p["wso"], p["wt"], p["alr"], p["ali"], p["d"], p["wglu"], p["wpool"],
        p["pscale"], p["wout"], nb=nb, tt=tt, n_sub=n_sub, n_steps=t_len // (tt * n_sub), pos0=pos0,
        time_major=time_major, cast=cast)
    wfi, wfo = cast_out if cast else ffn_w
    y = _ffn(x1, mod3, p["n2"], p["nf"], wfi, wfo, nb=nb, tt=tt_ffn, n_steps=t_len // tt_ffn,
             time_major=time_major)
    if time_major:
        y, new_buf = unrows(y, t_len), unrows(new_buf, POOL_HIST)
    return (y, unflat_state(s_r), unflat_state(s_i), new_buf[None]), (wfi, wfo)


def kernel(x_prompt, x_sample, c_prompt, c_sample, state_ssm_re, state_ssm_im, state_pool, norm1_g, norm2_g, normf_g, w_ada, b_ada, w_in, ssm_lam_re, ssm_lam_im, ssm_log_dt, ssm_b_re, ssm_b_im, ssm_c_re, ssm_c_im, ssm_d, w_glu, w_pool, pool_scale, w_out, w_ffn_in, w_ffn_out):
    n_prompt = x_prompt.shape[0]
    mod_p, mod_s, win, wglu, wout, wpool = _ada(
        c_prompt, c_sample, w_ada[0], b_ada[0],
        cast=((w_in[0], (D_SSM + D_POOL, w_in.shape[-1])), (w_glu[0], (0, 2 * D_MODEL)),
              (w_out[0], (0, D_MODEL)), (w_pool[0].reshape(D_POOL, POOL_OUT), (0, 0))))
    alr, ali, wsi, wso, wt = _ssm_prep(ssm_lam_re[0], ssm_lam_im[0], ssm_log_dt[0], ssm_b_re[0], ssm_b_im[0],
                                       ssm_c_re[0], ssm_c_im[0])
    p = dict(
        n1=norm1_g[0].reshape(1, -1), n2=norm2_g[0].reshape(1, -1), nf=normf_g.reshape(1, -1),
        win=win, wsi=wsi, wso=wso, wt=wt, alr=alr, ali=ali, d=ssm_d[0].reshape(1, -1),
        wglu=wglu, wpool=wpool.reshape(len(POOL_WINDOWS), POOL_GROUP, POOL_OUT), pscale=pool_scale[0].reshape(1, -1),
        wout=wout)
    zero_state = jnp.zeros((n_prompt, N_SSM_GROUPS, SSM_STATE), F32)
    zero_pool = jnp.zeros((n_prompt, POOL_HIST, D_POOL), F32)
    ffn_f32 = (w_ffn_in[0], w_ffn_out[0])
    (y_p, p_re, p_im, p_pool), ffn_bf16 = _trunk(x_prompt, mod_p, zero_state, zero_state, zero_pool, 0, p,
                                                  time_major=False, ffn_w=ffn_f32)
    (y_s, s_re, s_im, s_pool), _ = _trunk(x_sample, mod_s, state_ssm_re[0], state_ssm_im[0], state_pool[0],
                                          PAST_LEN, p, time_major=True, ffn_w=ffn_bf16)
    return (y_p, y_s, p_re, p_im, p_pool, s_re, s_im, s_pool)
```

```python
import functools

import jax
import jax.numpy as jnp
from jax import lax
from jax.experimental import pallas as pl
from jax.experimental.pallas import tpu as pltpu

D_MODEL = 1024
D_SSM = 512
SSM_GROUP = 16
N_SSM_GROUPS = 32
SSM_STATE = 64
N_STATE = N_SSM_GROUPS * SSM_STATE
D_POOL = 512
POOL_WINDOWS = (2, 4, 8, 16)
POOL_GROUP = 128
POOL_OUT = 256
POOL_HIST = 15
D_FF = 2816
N_MOD = 6
EPS = 1e-6
PAST_LEN = 16384

SUBLANES = 8
BF16_SUBLANES = 16
LANES = 128
TILE_ROWS = 512
MIXER_SUBTILES = 2
FFN_TILE_ROWS = 1024
HIST_PAD = 16
N_TAIL_SPLITS = 2
N_FFN_SPLITS = 4
ADA_ROW_BLOCKS = 4
MXU_DEPTH = 256
BLOCK_STEPS = 4
SET_GROUPS = MXU_DEPTH // (BLOCK_STEPS * SSM_GROUP)
N_SETS = N_SSM_GROUPS // SET_GROUPS
SET_IN = SET_GROUPS * SSM_GROUP
SET_STATES = SET_GROUPS * SSM_STATE
VMEM_LIMIT = 60 * 1024 * 1024
ADA_VMEM_LIMIT = 40 * 1024 * 1024
FFN_VMEM_LIMIT = 48 * 1024 * 1024

BF16 = jnp.bfloat16
F32 = jnp.float32


def _dot(a, b):
    return jnp.dot(a, b, preferred_element_type=F32)


def _sigmoid(x):
    return 0.5 * jnp.tanh(0.5 * x) + 0.5


def _two_sigmoid(half_x):
    return jnp.tanh(half_x) + 1.0


def _rmsnorm(x, g):
    return x * lax.rsqrt(jnp.mean(x * x, axis=-1, keepdims=True) + EPS) * g


def _tile3(v, tt, nb, time_major):
    lead, inner = (tt, nb) if time_major else (nb, tt)
    return v.reshape(lead, inner, v.shape[-1])


def _modulate(h, shift, scale, tt, nb, time_major):
    return (_tile3(h, tt, nb, time_major) * (1.0 + scale) + shift).reshape(h.shape)


def _rows_gate(v, gate, tt, nb, time_major):
    return (_tile3(v, tt, nb, time_major) * gate).reshape(v.shape)


def _cast_specs(arrays, n_steps):
    specs = []
    for a in arrays:
        n_blk = max(k for k in range(1, n_steps + 1)
                    if n_steps % k == 0 and a.shape[0] % (k * BF16_SUBLANES) == 0)
        specs.append(pl.BlockSpec((a.shape[0] // n_blk, a.shape[1]),
                                  lambda i, _hold=n_steps // n_blk: (i // _hold, 0)))
    return specs


def _cast_blocks(srcs, dsts, halved):
    for src, dst, (lo, hi) in zip(srcs, dsts, halved):
        if lo > 0:
            dst[:, :lo] = src[:, :lo].astype(BF16)
        if hi > lo:
            dst[:, lo:hi] = (src[:, lo:hi] * 0.5).astype(BF16)
        if hi < src.shape[-1]:
            dst[:, hi:] = src[:, hi:].astype(BF16)


def _ada_kernel(cp_ref, cs_ref, w_ref, b_ref, *rest, halved):
    n_cast = len(halved)
    cast_src, (op_ref, os_ref), cast_dst = rest[:n_cast], rest[n_cast:n_cast + 2], rest[n_cast + 2:]
    _cast_blocks(cast_src, cast_dst, halved)
    k = pl.program_id(0)
    w = w_ref[...].astype(BF16)
    for c_ref, o_ref in ((cp_ref, op_ref), (cs_ref, os_ref)):
        c = c_ref[...]
        part = _dot((c * _sigmoid(c)).astype(BF16), w)

        @pl.when(k == 0)
        def _(o_ref=o_ref, part=part):
            o_ref[...] = b_ref[...] + part

        @pl.when(k > 0)
        def _(o_ref=o_ref, part=part):
            o_ref[...] += part


def _ada(c_prompt, c_sample, w_ada, b_ada, cast):
    n_p, n_s = c_prompt.shape[0], c_sample.shape[0]
    kb = D_MODEL // ADA_ROW_BLOCKS
    arrays, halved = tuple(a for a, _ in cast), tuple(h for _, h in cast)
    cast_specs = _cast_specs(arrays, ADA_ROW_BLOCKS)
    n_out = N_MOD * D_MODEL
    return pl.pallas_call(
        functools.partial(_ada_kernel, halved=halved),
        grid=(ADA_ROW_BLOCKS,),
        in_specs=[pl.BlockSpec((n_p, kb), lambda k: (0, k)),
                  pl.BlockSpec((n_s, kb), lambda k: (0, k)),
                  pl.BlockSpec((kb, n_out), lambda k: (k, 0)),
                  pl.BlockSpec((1, n_out), lambda k: (0, 0))] + cast_specs,
        out_specs=[pl.BlockSpec((n_p, n_out), lambda k: (0, 0)), pl.BlockSpec((n_s, n_out), lambda k: (0, 0))]
        + cast_specs,
        out_shape=[jax.ShapeDtypeStruct((n_p, n_out), F32), jax.ShapeDtypeStruct((n_s, n_out), F32)]
        + [jax.ShapeDtypeStruct(a.shape, BF16) for a in arrays],
        compiler_params=pltpu.CompilerParams(dimension_semantics=("arbitrary",),
                                             vmem_limit_bytes=ADA_VMEM_LIMIT),
        name="ada_mod",
    )(c_prompt, c_sample, w_ada, b_ada.reshape(1, -1), *arrays)


def _cmul(ar, ai, br, bi):
    return ar * br - ai * bi, ar * bi + ai * br


def _ssm_prep_kernel(lr_ref, li_ref, ldt_ref, bt_r_ref, bt_i_ref, cr_ref, ci_ref,
                     alr_ref, ali_ref, wsi_ref, wso_ref, wt_ref):
    lr = lr_ref[...]
    li = li_ref[...]
    dt = jnp.exp(ldt_ref[...])
    mag = jnp.exp(lr * dt)
    a_r = mag * jnp.cos(li * dt)
    a_i = mag * jnp.sin(li * dt)
    den = lr * lr + li * li
    nr = a_r - 1.0
    ni = a_i
    f = ((nr * lr + ni * li) / den, (ni * lr - nr * li) / den)
    powers = [(a_r, a_i)]
    for _ in range(BLOCK_STEPS - 1):
        powers.append(_cmul(a_r, a_i, *powers[-1]))
    for g in range(N_SSM_GROUPS):
        states = slice(g * SSM_STATE, (g + 1) * SSM_STATE)
        alr_ref[:, states] = powers[-1][0][g:g + 1, :]
        ali_ref[:, states] = powers[-1][1][g:g + 1, :]
    wsi_ref[...] = jnp.zeros(wsi_ref.shape, BF16)
    wso_ref[...] = jnp.zeros(wso_ref.shape, BF16)
    wt_ref[...] = jnp.zeros(wt_ref.shape, BF16)
    k_shape = (BLOCK_STEPS * SET_IN, SET_IN)
    row_group = lax.broadcasted_iota(jnp.int32, k_shape, 0) % SET_IN // SSM_GROUP
    col_group = lax.broadcasted_iota(jnp.int32, k_shape, 1) // SSM_GROUP
    same_group = row_group == col_group
    nt_dims = (((1,), (1,)), ((), ()))
    for s in range(N_SETS):
        groups = slice(s * SET_GROUPS, (s + 1) * SET_GROUPS)

        def rows_of(pair, groups=groups):
            return tuple(jnp.broadcast_to(v[groups][:, None, :], (SET_GROUPS, SSM_GROUP, SSM_STATE))
                         .reshape(SET_IN, SSM_STATE) for v in pair)

        bt = (bt_r_ref[groups].reshape(SET_IN, SSM_STATE), bt_i_ref[groups].reshape(SET_IN, SSM_STATE))
        c = (cr_ref[groups].reshape(SET_IN, SSM_STATE), ci_ref[groups].reshape(SET_IN, SSM_STATE))
        a_rows = rows_of(powers[0])
        bb_lags = [_cmul(*rows_of(f), *bt)]
        for _ in range(BLOCK_STEPS - 1):
            bb_lags.append(_cmul(*a_rows, *bb_lags[-1]))
        bb_r = jnp.concatenate([bb_lags[BLOCK_STEPS - 1 - j][0] for j in range(BLOCK_STEPS)], axis=0)
        bb_i = jnp.concatenate([bb_lags[BLOCK_STEPS - 1 - j][1] for j in range(BLOCK_STEPS)], axis=0)
        k_all = (lax.dot_general(bb_r, c[0], nt_dims, precision=lax.Precision.HIGHEST, preferred_element_type=F32)
                 - lax.dot_general(bb_i, c[1], nt_dims, precision=lax.Precision.HIGHEST, preferred_element_type=F32))
        k_all = jnp.where(same_group, k_all, 0.0).astype(BF16)
        for j in range(BLOCK_STEPS):
            lag = BLOCK_STEPS - 1 - j
            for t in range(lag, BLOCK_STEPS):
                wt_ref[s, (t - lag) * SET_IN:(t - lag + 1) * SET_IN, t * SET_IN:(t + 1) * SET_IN] = (
                    k_all[j * SET_IN:(j + 1) * SET_IN])
            for gj in range(SET_GROUPS):
                rows = slice(j * SET_IN + gj * SSM_GROUP, j * SET_IN + (gj + 1) * SSM_GROUP)
                wsi_ref[s, rows, gj * SSM_STATE:(gj + 1) * SSM_STATE] = bb_r[rows, :].astype(BF16)
                wsi_ref[s, rows, SET_STATES + gj * SSM_STATE:SET_STATES + (gj + 1) * SSM_STATE] = (
                    bb_i[rows, :].astype(BF16))
        for t in range(BLOCK_STEPS):
            p_r, p_i = rows_of(powers[t])
            from_sr = jnp.transpose(c[0] * p_r - c[1] * p_i).astype(BF16)
            from_si = jnp.transpose(-c[0] * p_i - c[1] * p_r).astype(BF16)
            for gj in range(SET_GROUPS):
                src = slice(gj * SSM_GROUP, (gj + 1) * SSM_GROUP)
                out = slice(t * SET_IN + gj * SSM_GROUP, t * SET_IN + (gj + 1) * SSM_GROUP)
                wso_ref[s, gj * SSM_STATE:(gj + 1) * SSM_STATE, out] = from_sr[:, src]
                wso_ref[s, SET_STATES + gj * SSM_STATE:SET_STATES + (gj + 1) * SSM_STATE, out] = from_si[:, src]


def _ssm_prep(lam_re, lam_im, log_dt, b_re, b_im, c_re, c_im):
    return pl.pallas_call(
        _ssm_prep_kernel,
        out_shape=(jax.ShapeDtypeStruct((1, N_STATE), F32), jax.ShapeDtypeStruct((1, N_STATE), F32),
                   jax.ShapeDtypeStruct((N_SETS, BLOCK_STEPS * SET_IN, 2 * SET_STATES), BF16),
                   jax.ShapeDtypeStruct((N_SETS, 2 * SET_STATES, BLOCK_STEPS * SET_IN), BF16),
                   jax.ShapeDtypeStruct((N_SETS, BLOCK_STEPS * SET_IN, BLOCK_STEPS * SET_IN), BF16)),
        name="ssm_prep",
    )(lam_re, lam_im, log_dt.reshape(N_SSM_GROUPS, 1), jnp.transpose(b_re, (0, 2, 1)), jnp.transpose(b_im, (0, 2, 1)),
      c_re, c_im)


def _block_scan(z_ref, st_r, st_i, alr_ref, ali_ref, nb, n_blocks):
    for s in range(N_SETS):
        lanes = slice(s * SET_STATES, (s + 1) * SET_STATES)
        re_lanes = slice(2 * s * SET_STATES, (2 * s + 1) * SET_STATES)
        im_lanes = slice((2 * s + 1) * SET_STATES, (2 * s + 2) * SET_STATES)
        ar = jnp.broadcast_to(alr_ref[:, lanes], (SUBLANES, SET_STATES))
        ai = jnp.broadcast_to(ali_ref[:, lanes], (SUBLANES, SET_STATES))

        def seq_block(rb, carry, lanes=lanes, re_lanes=re_lanes, im_lanes=im_lanes, ar=ar, ai=ai):
            r0 = pl.multiple_of(rb * SUBLANES, SUBLANES)
            sr = st_r[pl.ds(r0, SUBLANES), lanes]
            si = st_i[pl.ds(r0, SUBLANES), lanes]

            def step(k, state):
                sr, si = state
                rows = pl.ds(pl.multiple_of(k * nb + r0, SUBLANES), SUBLANES)
                zr = z_ref[rows, re_lanes]
                zi = z_ref[rows, im_lanes]
                z_ref[rows, re_lanes] = sr
                z_ref[rows, im_lanes] = si
                return ar * sr - ai * si + zr, ar * si + ai * sr + zi

            sr, si = lax.fori_loop(0, n_blocks, step, (sr, si), unroll=True)
            st_r[pl.ds(r0, SUBLANES), lanes] = sr
            st_i[pl.ds(r0, SUBLANES), lanes] = si
            return carry

        if nb == SUBLANES:
            seq_block(0, 0)
        else:
            lax.fori_loop(0, nb // SUBLANES, seq_block, 0)


def _to_time_major(v, perm_ref, nb, tt):
    for c in range(v.shape[-1] // LANES):
        for b in range(nb):
            perm_ref.at[c][pl.ds(b, tt, stride=nb), :] = v[b * tt:(b + 1) * tt, c * LANES:(c + 1) * LANES]
    return jnp.concatenate([perm_ref[c] for c in range(v.shape[-1] // LANES)], axis=-1)


def _to_seq_major(v, perm_ref, nb, tt):
    n_slabs = v.shape[-1] // LANES
    for c in range(n_slabs):
        perm_ref[c] = v[:, c * LANES:(c + 1) * LANES]
    return jnp.concatenate(
        [jnp.concatenate([perm_ref.at[c][pl.ds(b, tt, stride=nb), :] for b in range(nb)], axis=0)
         for c in range(n_slabs)], axis=-1)


def _pool_time_major(u_pool, hist_ref, i, nb, tt, pos0):
    m = tt * nb
    hist_rows = POOL_HIST * nb
    hist_ref[hist_rows:hist_rows + m, :] = u_pool
    t_abs = pos0 + i * tt + lax.broadcasted_iota(jnp.int32, (tt, nb, POOL_GROUP), 0).reshape(m, POOL_GROUP)
    pooled = []
    for k, w in enumerate(POOL_WINDOWS):
        cols = slice(k * POOL_GROUP, (k + 1) * POOL_GROUP)
        acc = u_pool[:, cols]
        for j in range(1, w):
            acc = acc + hist_ref[hist_rows - j * nb:hist_rows - j * nb + m, cols]
        count = jnp.minimum(t_abs + 1, w).astype(F32)
        pooled.append(acc / count - u_pool[:, cols])
    return pooled


def _pool_seq_major(u_pool, hist_ref, i, nb, tt, pos0):
    m = tt * nb
    t_abs = pos0 + i * tt + lax.broadcasted_iota(jnp.int32, (nb, tt, POOL_GROUP), 1)
    pooled = []
    for k, w in enumerate(POOL_WINDOWS):
        cur = u_pool[:, k * POOL_GROUP:(k + 1) * POOL_GROUP].reshape(nb, tt, POOL_GROUP)
        hist_ref[k, :, HIST_PAD:HIST_PAD + tt, :] = cur
        acc = cur
        for j in range(1, w):
            acc = acc + hist_ref[k, :, HIST_PAD - j:HIST_PAD - j + tt, :]
        count = jnp.minimum(t_abs + 1, w).astype(F32)
        pooled.append((acc / count - cur).reshape(m, POOL_GROUP))
    return pooled


def _mixer_kernel(x_ref, mod_ref, h0r_ref, h0i_ref, pool0_ref, n1_ref, win_ref, wsi_ref, wso_ref, wt_ref,
                  alr_ref, ali_ref, d_ref, wglu_ref, wpool_ref, pscale_ref, wout_ref, *rest,
                  nb, tt, n_sub, n_steps, pos0, time_major, cast_halved):
    n_cast = len(cast_halved)
    cast_src, rest = rest[:n_cast], rest[n_cast:]
    xo_ref, sr_out_ref, si_out_ref, pool_out_ref = rest[:4]
    cast_dst, rest = rest[4:4 + n_cast], rest[4 + n_cast:]
    z_ref, st_r, st_i, hist_ref, perm_ref = rest
    n_blocks = tt // BLOCK_STEPS
    mb = n_blocks * nb
    i = pl.program_id(0)
    _cast_blocks(cast_src, cast_dst, cast_halved)
    m = tt * nb
    hist_rows = POOL_HIST * nb

    @pl.when(i == 0)
    def _():
        st_r[...] = h0r_ref[...].T if time_major else h0r_ref[...]
        st_i[...] = h0i_ref[...].T if time_major else h0i_ref[...]
        if time_major:
            hist_ref[0:hist_rows, :] = pool0_ref[...]
        else:
            for k in range(len(POOL_WINDOWS)):
                hist_ref[k, :, 0:HIST_PAD, :] = pool0_ref[:, :, k * POOL_GROUP:(k + 1) * POOL_GROUP]

    for sub in range(n_sub):
        tile = i * n_sub + sub
        t_rows = slice(sub * tt, (sub + 1) * tt)
        x = (x_ref[...] if time_major else x_ref[:, t_rows, :]).reshape(m, D_MODEL)
        shift1 = mod_ref[:, :, 0 * D_MODEL:1 * D_MODEL]
        scale1 = mod_ref[:, :, 1 * D_MODEL:2 * D_MODEL]
        gate1 = mod_ref[:, :, 2 * D_MODEL:3 * D_MODEL]
        hb = _modulate(_rmsnorm(x, n1_ref[...]), shift1, scale1, tt, nb, time_major).astype(BF16)

        u_ssm = _dot(hb, win_ref[:, 0:D_SSM])
        u_pool = _dot(hb, win_ref[:, D_SSM:D_SSM + D_POOL])
        u_tm = u_ssm if time_major else _to_time_major(u_ssm, perm_ref, nb, tt)
        u_blocks = u_tm.reshape(n_blocks, BLOCK_STEPS, nb, D_SSM)
        u_steps = [u_blocks[:, j].reshape(mb, D_SSM).astype(BF16) for j in range(BLOCK_STEPS)]
        y_sets = []
        for s in range(N_SETS):
            chans = slice(s * SET_IN, (s + 1) * SET_IN)
            lhs = jnp.concatenate([u[:, chans] for u in u_steps], axis=1)
            z_ref[:, 2 * s * SET_STATES:2 * (s + 1) * SET_STATES] = _dot(lhs, wsi_ref[s])
            y_sets.append(_dot(lhs, wt_ref[s]))
        g_ssm = _dot(hb, win_ref[:, D_SSM + D_POOL:D_SSM + D_POOL + D_MODEL])
        g_pool = _dot(hb, win_ref[:, D_SSM + D_POOL + D_MODEL:])

        _block_scan(z_ref, st_r, st_i, alr_ref, ali_ref, nb, n_blocks)
        if time_major:
            pooled = _pool_time_major(u_pool, hist_ref, tile, nb, tt, pos0)
            new_buf_rows = (slice(m, m + hist_rows), slice(None))
            if n_steps * n_sub > 1:
                hist_ref[0:hist_rows, :] = hist_ref[new_buf_rows]
        else:
            pooled = _pool_seq_major(u_pool, hist_ref, tile, nb, tt, pos0)
            if n_steps * n_sub > 1:
                hist_ref[:, :, 0:HIST_PAD, :] = hist_ref[:, :, tt:tt + HIST_PAD, :]
        z_pieces = [_dot(pooled[k].astype(BF16), wpool_ref[k]) for k in range(len(POOL_WINDOWS))]
        br_pool = jnp.concatenate(z_pieces, axis=-1) * pscale_ref[...]

        for s in range(N_SETS):
            s_in = z_ref[:, 2 * s * SET_STATES:2 * (s + 1) * SET_STATES].astype(BF16)
            y_sets[s] = y_sets[s] + _dot(s_in, wso_ref[s])
        y_steps = [jnp.concatenate([y[:, j * SET_IN:(j + 1) * SET_IN] for y in y_sets], axis=1).reshape(n_blocks, nb, D_SSM)
                   for j in range(BLOCK_STEPS)]
        y_cs = jnp.stack(y_steps, axis=1).reshape(m, D_SSM)
        if not time_major:
            y_cs = _to_seq_major(y_cs, perm_ref, nb, tt)

        for h in range(N_TAIL_SPLITS):
            rows = slice(h * m // N_TAIL_SPLITS, (h + 1) * m // N_TAIL_SPLITS)
            y_ssm = y_cs[rows] + d_ref[...] * u_ssm[rows]
            glu = _dot(jax.nn.gelu(y_ssm).astype(BF16), wglu_ref[...])
            br_ssm = glu[:, :D_MODEL] * _two_sigmoid(glu[:, D_MODEL:])
            merged2 = _two_sigmoid(g_ssm[rows]) * br_ssm + _two_sigmoid(g_pool[rows]) * br_pool[rows]
            o = _dot(merged2.astype(BF16), wout_ref[...])
            if time_major:
                xo_ref[rows, :] = x[rows] + _rows_gate(o, gate1, tt // N_TAIL_SPLITS, nb, True)
            else:
                nb_h = nb // N_TAIL_SPLITS
                seqs = slice(h * nb_h, (h + 1) * nb_h)
                x1 = x[rows] + _rows_gate(o, gate1[seqs], tt, nb_h, False)
                xo_ref[seqs, t_rows, :] = x1.reshape(nb_h, tt, D_MODEL)

    @pl.when(i == n_steps - 1)
    def _():
        sr_out_ref[...] = st_r[...].T if time_major else st_r[...]
        si_out_ref[...] = st_i[...].T if time_major else st_i[...]
        if time_major:
            pool_out_ref[...] = hist_ref[new_buf_rows]
        else:
            pool_out_ref[...] = jnp.concatenate(
                [hist_ref[k, :, tt + HIST_PAD - POOL_HIST:tt + HIST_PAD, :] for k in range(len(POOL_WINDOWS))],
                axis=-1)


def _const_spec(shape):
    nd = len(shape)
    return pl.BlockSpec(shape, lambda i, _nd=nd: (0,) * _nd, pipeline_mode=pl.Buffered(1))


def _row_spec(nb, tt, time_major):
    if time_major:
        return pl.BlockSpec((tt * nb, D_MODEL), lambda i: (i, 0))
    return pl.BlockSpec((nb, tt, D_MODEL), lambda i: (0, i, 0))


def _mixer(x, mod, h0r, h0i, pool0, n1, win, wsi, wso, wt, alr, ali, d, wglu, wpool, pscale, wout,
           *, nb, tt, n_sub, n_steps, pos0, time_major, cast=()):
    m = tt * nb
    consts = (mod, h0r, h0i, pool0, n1, win, wsi, wso, wt, alr, ali, d, wglu, wpool, pscale, wout)
    cast, cast_halved = tuple(a for a, _ in cast), tuple(n for _, n in cast)
    cast_specs = _cast_specs(cast, n_steps)
    if time_major:
        hist_shape = (POOL_HIST * nb + m, D_POOL)
        pool_out_shape = (POOL_HIST * nb, D_POOL)
    else:
        hist_shape = (len(POOL_WINDOWS), nb, HIST_PAD + tt, POOL_GROUP)
        pool_out_shape = (nb, POOL_HIST, D_POOL)
    kern = functools.partial(_mixer_kernel, nb=nb, tt=tt, n_sub=n_sub, n_steps=n_steps, pos0=pos0,
                             time_major=time_major, cast_halved=cast_halved)
    return pl.pallas_call(
        kern,
        grid=(n_steps,),
        in_specs=[_row_spec(nb, n_sub * tt, time_major)] + [_const_spec(a.shape) for a in consts] + cast_specs,
        out_specs=(_row_spec(nb, n_sub * tt, time_major),
                   pl.BlockSpec(h0r.shape, lambda i: (0, 0)),
                   pl.BlockSpec(h0r.shape, lambda i: (0, 0)),
                   pl.BlockSpec(pool_out_shape, lambda i: (0,) * len(pool_out_shape))) + tuple(cast_specs),
        out_shape=(jax.ShapeDtypeStruct(x.shape, F32),
                   jax.ShapeDtypeStruct(h0r.shape, F32),
                   jax.ShapeDtypeStruct(h0r.shape, F32),
                   jax.ShapeDtypeStruct(pool_out_shape, F32)) + tuple(jax.ShapeDtypeStruct(a.shape, BF16) for a in cast),
        scratch_shapes=[pltpu.VMEM((m // BLOCK_STEPS, 2 * N_STATE), F32),
                        pltpu.VMEM((nb, N_STATE), F32), pltpu.VMEM((nb, N_STATE), F32),
                        pltpu.VMEM(hist_shape, F32),
                        pltpu.VMEM((D_SSM // LANES, m, LANES), F32)],
        compiler_params=pltpu.CompilerParams(dimension_semantics=("arbitrary",),
                                             vmem_limit_bytes=VMEM_LIMIT),
        name=f"mixer_nb{nb}",
    )(x, *consts, *cast)


def _ffn_kernel(x_ref, mod_ref, n2_ref, nf_ref, wfi_ref, wfo_ref, o_ref, *, nb, tt, time_major):
    n_splits = min(N_FFN_SPLITS, tt) if time_major else N_FFN_SPLITS
    for h in range(n_splits):
        if time_major:
            tt_h, nb_h = tt // n_splits, nb
            piece = (slice(h * tt_h * nb, (h + 1) * tt_h * nb), slice(None))
            mod = mod_ref
        else:
            tt_h, nb_h = tt, nb // N_FFN_SPLITS
            piece = (slice(h * nb_h, (h + 1) * nb_h), slice(None), slice(None))
            mod = mod_ref.at[h * nb_h:(h + 1) * nb_h]
        x = x_ref[piece].reshape(tt_h * nb_h, D_MODEL)
        shift2 = mod[:, :, 3 * D_MODEL:4 * D_MODEL]
        scale2 = mod[:, :, 4 * D_MODEL:5 * D_MODEL]
        gate2 = mod[:, :, 5 * D_MODEL:6 * D_MODEL]
        hb = _modulate(_rmsnorm(x, n2_ref[...]), shift2, scale2, tt_h, nb_h, time_major).astype(BF16)
        f_a = _dot(hb, wfi_ref[:, :D_FF])
        f_b = _dot(hb, wfi_ref[:, D_FF:])
        act = (f_a * _two_sigmoid(f_a) * f_b).astype(BF16)
        x2 = x + _rows_gate(_dot(act, wfo_ref[...]), gate2, tt_h, nb_h, time_major)
        y = _rmsnorm(x2, nf_ref[...])
        o_ref[piece] = y if time_major else y.reshape(nb_h, tt_h, D_MODEL)


def _ffn(x, mod, n2, nf, wfi, wfo, *, nb, tt, n_steps, time_major):
    consts = (mod, n2, nf, wfi, wfo)
    return pl.pallas_call(
        functools.partial(_ffn_kernel, nb=nb, tt=tt, time_major=time_major),
        grid=(n_steps,),
        in_specs=[_row_spec(nb, tt, time_major)] + [_const_spec(a.shape) for a in consts],
        out_specs=_row_spec(nb, tt, time_major),
        out_shape=jax.ShapeDtypeStruct(x.shape, F32),
        compiler_params=pltpu.CompilerParams(dimension_semantics=("arbitrary",),
                                             vmem_limit_bytes=FFN_VMEM_LIMIT),
        name=f"ffn_nb{nb}",
    )(x, *consts)


def _trunk(x, mod, h0r, h0i, pool0, pos0, p, *, time_major, ffn_w):
    nb, t_len, _ = x.shape
    tt = min(t_len, TILE_ROWS // nb)
    n_sub = 1 if time_major else MIXER_SUBTILES
    tt_ffn = t_len // 2 if time_major else min(t_len, FFN_TILE_ROWS // nb)
    if time_major:
        rows = lambda a: jnp.transpose(a, (1, 0, 2)).reshape(a.shape[1] * nb, a.shape[2])
        unrows = lambda a, t: jnp.transpose(a.reshape(t, nb, a.shape[-1]), (1, 0, 2))
        x_in, pool_in, mod3 = rows(x), rows(pool0), mod[None]
        flat_state = lambda a: jnp.transpose(a, (1, 2, 0)).reshape(N_STATE, nb)
        unflat_state = lambda a: jnp.transpose(a.reshape(N_SSM_GROUPS, SSM_STATE, nb), (2, 0, 1))[None]
    else:
        assert nb == SUBLANES
        x_in, mod3 = x, mod[:, None, :]
        pool_in = jnp.pad(pool0, ((0, 0), (HIST_PAD - POOL_HIST, 0), (0, 0)))
        flat_state = lambda a: a.reshape(nb, N_STATE)
        unflat_state = lambda a: a.reshape(1, nb, N_SSM_GROUPS, SSM_STATE)
    cast = ((ffn_w[0], (0, D_FF)), (ffn_w[1], (0, 0))) if ffn_w[0].dtype == F32 else ()
    x1, s_r, s_i, new_buf, *cast_out = _mixer(
        x_in, mod3, flat_state(h0r), flat_state(h0i), pool_in,
        p["n1"], p["win"], p["wsi"], p["wso"], p["wt"], p["alr"], p["ali"], p["d"], p["wglu"], p["wpool"],
        p["pscale"], p["wout"], nb=nb, tt=tt, n_sub=n_sub, n_steps=t_len // (tt * n_sub), pos0=pos0,
        time_major=time_major, cast=cast)
    wfi, wfo = cast_out if cast else ffn_w
    y = _ffn(x1, mod3, p["n2"], p["nf"], wfi, wfo, nb=nb, tt=tt_ffn, n_steps=t_len // tt_ffn,
             time_major=time_major)
    if time_major:
        y, new_buf = unrows(y, t_len), unrows(new_buf, POOL_HIST)
    return (y, unflat_state(s_r), unflat_state(s_i), new_buf[None]), (wfi, wfo)


def kernel(x_prompt, x_sample, c_prompt, c_sample, state_ssm_re, state_ssm_im, state_pool, norm1_g, norm2_g, normf_g, w_ada, b_ada, w_in, ssm_lam_re, ssm_lam_im, ssm_log_dt, ssm_b_re, ssm_b_im, ssm_c_re, ssm_c_im, ssm_d, w_glu, w_pool, pool_scale, w_out, w_ffn_in, w_ffn_out):
    n_prompt = x_prompt.shape[0]
    mod_p, mod_s, win, wglu, wout, wpool = _ada(
        c_prompt, c_sample, w_ada[0], b_ada[0],
        cast=((w_in[0], (D_SSM + D_POOL, w_in.shape[-1])), (w_glu[0], (0, 2 * D_MODEL)),
              (w_out[0], (0, D_MODEL)), (w_pool[0].reshape(D_POOL, POOL_OUT), (0, 0))))
    alr, ali, wsi, wso, wt = _ssm_prep(ssm_lam_re[0], ssm_lam_im[0], ssm_log_dt[0], ssm_b_re[0], ssm_b_im[0],
                                       ssm_c_re[0], ssm_c_im[0])
    p = dict(
        n1=norm1_g[0].reshape(1, -1), n2=norm2_g[0].reshape(1, -1), nf=normf_g.reshape(1, -1),
        win=win, wsi=wsi, wso=wso, wt=wt, alr=alr, ali=ali, d=ssm_d[0].reshape(1, -1),
        wglu=wglu, wpool=wpool.reshape(len(POOL_WINDOWS), POOL_GROUP, POOL_OUT), pscale=pool_scale[0].reshape(1, -1),
        wout=wout)
    zero_state = jnp.zeros((n_prompt, N_SSM_GROUPS, SSM_STATE), F32)
    zero_pool = jnp.zeros((n_prompt, POOL_HIST, D_POOL), F32)
    ffn_f32 = (w_ffn_in[0], w_ffn_out[0])
    (y_p, p_re, p_im, p_pool), ffn_bf16 = _trunk(x_prompt, mod_p, zero_state, zero_state, zero_pool, 0, p,
                                                  time_major=False, ffn_w=ffn_f32)
    (y_s, s_re, s_im, s_pool), _ = _trunk(x_sample, mod_s, state_ssm_re[0], state_ssm_im[0], state_pool[0],
                                          PAST_LEN, p, time_major=True, ffn_w=ffn_bf16)
    return (y_p, y_s, p_re, p_im, p_pool, s_re, s_im, s_pool)
```
